```python
import math
import jax
import jax.numpy as jnp
from jax import lax
import numpy as np

D_MODEL = 1024
BATCH = 8
SEQ = 2048
DEPTH = 2
DEC_BATCH = 128
DEC_SEQ = 4
PAST_LEN = 2048
PAGE_SIZE = 128

HEAD_DIM = 64
ROT_DIM = HEAD_DIM // 4
ROPE_THETA = 500000.0
NORM_EPS = 1e-6
SCALE = HEAD_DIM ** -0.5
Q_BLOCK = 128
NEG_INF = -1e30
TINY = 1e-30

A_HEADS = 4
A_VDIM = 2 * HEAD_DIM
NSA_HEADS = 8
NSA_KV_HEADS = 2
NSA_REP = NSA_HEADS // NSA_KV_HEADS
CMP_LEN = 32
CMP_STRIDE = 16
CMP_HID = 4 * HEAD_DIM
SEL_BLOCK = 64
SEL_TOPK = 16
SEL_Q_BLOCK = 64
NSA_WINDOW = 512
FORCE_BONUS = 1e3
C_HEADS = 16
C_GROUPS = ((128, 1), (512, 4), (2048, 16))
N_C_GROUPS = len(C_GROUPS)

A_QK_W = A_HEADS * 2 * HEAD_DIM
A_V_W = A_HEADS * A_VDIM
NSA_Q_W = NSA_HEADS * HEAD_DIM
NSA_KV_W = 6 * NSA_KV_HEADS * HEAD_DIM
NSA_GATE_W = 3 * NSA_HEADS
IN0_W = 2 * A_QK_W + A_V_W + NSA_Q_W + NSA_KV_W + NSA_GATE_W
OUT0_W = A_V_W + NSA_Q_W
C_W = C_HEADS * HEAD_DIM
IN1_W = N_C_GROUPS * 3 * C_W

D_FF = 2816
N_EXPERTS = 8
TOP_K = 2
D_FF_EXPERT = 3584

kernel_name = 'hybrid_diff_nsa_dilated_decoder_step'


def rms_norm(x, g):
    xf = x.astype(jnp.float32)
    y = xf * lax.rsqrt(jnp.mean(xf * xf, axis=-1, keepdims=True) + NORM_EPS)
    return (y * g.astype(jnp.float32)).astype(x.dtype)


def partial_rope(x, pos):
    half = ROT_DIM // 2
    inv_freq = ROPE_THETA ** (-jnp.arange(half, dtype=jnp.float32) / half)
    ang = pos.astype(jnp.float32)[:, None] * inv_freq[None, :]
    shape = (1, pos.shape[0]) + (1,) * (x.ndim - 3) + (half,)
    cos = jnp.cos(ang).reshape(shape)
    sin = jnp.sin(ang).reshape(shape)
    xf = x.astype(jnp.float32)
    x1, x2 = xf[..., :half], xf[..., half:ROT_DIM]
    out = jnp.concatenate([x1 * cos - x2 * sin, x2 * cos + x1 * sin, xf[..., ROT_DIM:]], axis=-1)
    return out.astype(x.dtype)


def masked_softmax(s, mask):
    s = jnp.where(mask, s.astype(jnp.float32), NEG_INF)
    m = jnp.max(s, axis=-1, keepdims=True)
    e = jnp.where(mask, jnp.exp(s - m), 0.0)
    l = jnp.maximum(jnp.sum(e, axis=-1, keepdims=True), TINY)
    return e / l, (m + jnp.log(l))[..., 0]


def paged_rows(cache, page_table):
    g = cache[page_table]
    return g.reshape((g.shape[0], g.shape[1] * g.shape[2]) + g.shape[3:])


def roll_buffer(buf, new):
    n_buf, t = buf.shape[1], new.shape[1]
    if t >= n_buf:
        return new[:, t - n_buf:]
    return jnp.concatenate([buf[:, t:], new], axis=1)


def gather_rows(buf, new, idx):
    n_buf = buf.shape[1]
    from_buf = buf[:, np.clip(idx, 0, n_buf - 1)]
    from_new = new[:, np.clip(idx - n_buf, 0, new.shape[1] - 1)]
    sel = (idx < n_buf).reshape(idx.shape + (1,) * (buf.ndim - 2))
    return jnp.where(sel, from_buf, from_new)


def banded_attn(q, k, v, band):
    n, L, g, r, dh = q.shape
    blk = math.gcd(L, Q_BLOCK)
    nb = L // blk
    pad = ((0, 0), (band, 0), (0, 0), (0, 0))
    idx = np.arange(nb)[:, None] * blk + np.arange(blk + band)[None, :]
    kb = jnp.pad(k, pad)[:, idx]
    vb = jnp.pad(v, pad)[:, idx]
    qb = q.reshape(n, nb, blk, g, r, dh)
    s = jnp.einsum('nbqgrd,nbkgd->nbgrqk', qb, kb, preferred_element_type=jnp.float32) * SCALE
    qpos = np.arange(nb)[:, None] * blk + np.arange(blk)[None, :]
    kpos = idx - band
    dist = qpos[:, :, None] - kpos[:, None, :]
    mask = (dist >= 0) & (dist <= band) & (kpos[:, None, :] >= 0)
    p, lse = masked_softmax(s, mask[None, :, None, None])
    o = jnp.einsum('nbgrqk,nbkgd->nbqgrd', p, vb.astype(jnp.float32))
    return o.reshape(n, L, g, r, dh).astype(q.dtype), lse.transpose(0, 1, 4, 2, 3).reshape(n, L, g, r)


def diff_lambda(lq1, lk1, lq2, lk2, lam_init):
    f = lambda a: a.astype(jnp.float32)
    return jnp.exp(jnp.sum(f(lq1) * f(lk1))) - jnp.exp(jnp.sum(f(lq2) * f(lk2))) + lam_init


def diff_heads(qa, ka, va, pos, g_q, g_k):
    n, t = qa.shape[:2]
    q = partial_rope(rms_norm(qa.reshape(n, t, A_HEADS, 2, HEAD_DIM), g_q), pos)
    k = partial_rope(rms_norm(ka.reshape(n, t, A_HEADS, 2, HEAD_DIM), g_k), pos)
    return q, k, va.reshape(n, t, A_HEADS, A_VDIM)


def diff_core(q, k, v, qpos, kpos, lam):
    s = jnp.einsum('nqhmd,nkhmd->nhmqk', q, k, preferred_element_type=jnp.float32) * SCALE
    p, _ = masked_softmax(s, (kpos[None, :] <= qpos[:, None])[None, None, None])
    a = p[:, :, 0] - lam * p[:, :, 1]
    return jnp.einsum('nhqk,nkhe->nqhe', a, v.astype(jnp.float32)).astype(v.dtype)


def diff_attn_prompt(q, k, v, pos, lam):
    n, s = q.shape[:2]
    nb = s // Q_BLOCK
    qb = q.reshape((n, nb, Q_BLOCK) + q.shape[2:]).swapaxes(0, 1)
    ob = lax.map(lambda a: diff_core(a[0], k, v, a[1], pos, lam), (qb, pos.reshape(nb, Q_BLOCK)))
    return ob.swapaxes(0, 1).reshape(n, s, A_HEADS, A_VDIM)


def diff_output(o, g_sub, lam_init):
    n, t = o.shape[:2]
    return (rms_norm(o, g_sub) * (1.0 - lam_init)).reshape(n, t, A_V_W)


def nsa_heads(qb, kvb, gb, pos, g_q, g_k):
    n, t = qb.shape[:2]
    q = rms_norm(qb.reshape(n, t, NSA_KV_HEADS, NSA_REP, HEAD_DIM), g_q)
    q_rot = partial_rope(q, pos)
    kv = kvb.reshape(n, t, 6, NSA_KV_HEADS, HEAD_DIM)
    k_slc = partial_rope(rms_norm(kv[:, :, 2], g_k[1]), pos)
    k_win = partial_rope(rms_norm(kv[:, :, 4], g_k[2]), pos)
    long_rows = jnp.stack([kv[:, :, 0], kv[:, :, 1], k_slc, kv[:, :, 3]], axis=2)
    win_rows = jnp.stack([k_win, kv[:, :, 5]], axis=2)
    gates = jax.nn.sigmoid(gb.astype(jnp.float32)).reshape(n, t, NSA_KV_HEADS, NSA_REP, 3)
    return q, q_rot, long_rows, win_rows, gates


def nsa_compress(rows, pe, w1, w2):
    n, L, g, dh = rows.shape
    n_cmp = (L - CMP_LEN) // CMP_STRIDE + 1
    idx = np.arange(n_cmp)[:, None] * CMP_STRIDE + np.arange(CMP_LEN)[None, :]
    blocks = rows[:, idx] + pe[None, None, :, None, :]
    flat = blocks.transpose(0, 1, 3, 2, 4).reshape(n, n_cmp, g, CMP_LEN * dh)
    return jax.nn.silu(flat @ w1) @ w2


def nsa_cmp_attn(q, k_cmp, v_cmp, qpos):
    n_cmp = k_cmp.shape[1]
    end = jnp.asarray(np.arange(n_cmp) * CMP_STRIDE + CMP_LEN - 1)
    s = jnp.einsum('nqgrd,ncgd->nqgrc', q, k_cmp, preferred_element_type=jnp.float32) * SCALE
    visible = end[None, :] <= qpos[:, None]
    p, _ = masked_softmax(s, visible[None, :, None, None, :])
    o = jnp.einsum('nqgrc,ncgd->nqgrd', p, v_cmp.astype(jnp.float32)).astype(q.dtype)
    return o, p


def cmp_to_sel_overlap(n_cmp, n_sel):
    c0 = np.arange(n_cmp)[:, None] * CMP_STRIDE
    s0 = np.arange(n_sel)[None, :] * SEL_BLOCK
    ov = np.minimum(c0 + CMP_LEN, s0 + SEL_BLOCK) - np.maximum(c0, s0)
    return jnp.asarray(np.maximum(ov, 0) / CMP_LEN, dtype=jnp.float32)


def nsa_select(p_cmp, qpos, n_sel):
    imp = jnp.einsum('nqgrc,cj->nqgj', p_cmp, cmp_to_sel_overlap(p_cmp.shape[-1], n_sel))
    blk = jnp.arange(n_sel)[None, :]
    cur = (qpos // SEL_BLOCK)[:, None]
    valid = blk <= cur
    forced = (blk == 0) | (blk == cur) | (blk == cur - 1)
    score = jnp.where(valid[None, :, None], imp + jnp.where(forced, FORCE_BONUS, 0.0)[None, :, None], NEG_INF)
    _, sel = lax.top_k(score, min(SEL_TOPK, n_sel))
    return sel


def nsa_sel_attn(q, k_blk, v_blk, sel, qpos):
    n, qc, g, r, dh = q.shape
    kk = sel.shape[-1]
    n_i = jnp.arange(n)[:, None, None, None]
    g_i = jnp.arange(g)[None, None, :, None]
    kg = k_blk[n_i, g_i, sel]
    vg = v_blk[n_i, g_i, sel]
    kpos = sel[..., None] * SEL_BLOCK + jnp.arange(SEL_BLOCK)
    visible = (kpos <= qpos[None, :, None, None, None]).reshape(n, qc, g, 1, kk * SEL_BLOCK)
    s = jnp.einsum('nqgrd,nqgkbd->nqgrkb', q, kg, preferred_element_type=jnp.float32)
    p, _ = masked_softmax(s.reshape(n, qc, g, r, kk * SEL_BLOCK) * SCALE, visible)
    o = jnp.einsum('nqgrx,nqgxd->nqgrd', p, vg.reshape(n, qc, g, kk * SEL_BLOCK, dh).astype(jnp.float32))
    return o.astype(q.dtype)


def nsa_long_branches(q, q_rot, long_all, qpos, g_kc, pe_k, w_k1, w_k2, pe_v, w_v1, w_v2):
    n, L, _, g, dh = long_all.shape
    k_cmp = rms_norm(nsa_compress(long_all[:, :, 0], pe_k, w_k1, w_k2), g_kc)
    v_cmp = nsa_compress(long_all[:, :, 1], pe_v, w_v1, w_v2)
    o_cmp, p_cmp = nsa_cmp_attn(q, k_cmp, v_cmp, qpos)
    n_sel = -(-L // SEL_BLOCK)
    sel = nsa_select(p_cmp, qpos, n_sel)

    def to_blocks(x):
        x = jnp.pad(x, ((0, 0), (0, n_sel * SEL_BLOCK - L), (0, 0), (0, 0)))
        return x.reshape(n, n_sel, SEL_BLOCK, g, dh).transpose(0, 3, 1, 2, 4)

    k_blk, v_blk = to_blocks(long_all[:, :, 2]), to_blocks(long_all[:, :, 3])
    nq = q.shape[1]
    qc = math.gcd(nq, SEL_Q_BLOCK)
    nc = nq // qc

    def chunks(x):
        return x.reshape((n, nc, qc) + x.shape[2:]).swapaxes(0, 1)

    o_sel = lax.map(lambda a: nsa_sel_attn(a[0], k_blk, v_blk, a[1], a[2]),
                    (chunks(q_rot), chunks(sel), qpos.reshape(nc, qc)))
    return o_cmp, o_sel.swapaxes(0, 1).reshape(q.shape)


def window_attn_sample(q, k_all, v_all, n_buf, window):
    t = q.shape[1]
    dist = (n_buf + np.arange(t))[:, None] - np.arange(n_buf + t)[None, :]
    visible = (dist >= 0) & (dist <= window)
    s = jnp.einsum('ntgrd,nkgd->ntgrk', q, k_all, preferred_element_type=jnp.float32) * SCALE
    p, _ = masked_softmax(s, visible[None, :, None, None, :])
    return jnp.einsum('ntgrk,nkgd->ntgrd', p, v_all.astype(jnp.float32)).astype(q.dtype)


def nsa_merge(gates, o_cmp, o_sel, o_win):
    f = jnp.float32
    o = gates[..., 0:1] * o_cmp.astype(f) + gates[..., 1:2] * o_sel.astype(f) + gates[..., 2:3] * o_win.astype(f)
    n, t = o.shape[:2]
    return o.reshape(n, t, NSA_Q_W).astype(o_cmp.dtype)


def split_in0(proj):
    sizes = [A_QK_W, A_QK_W, A_V_W, NSA_Q_W, NSA_KV_W, NSA_GATE_W]
    return jnp.split(proj, [int(o) for o in np.cumsum(sizes)[:-1]], axis=-1)


def even_mixer_prompt(h, pos, mw):
    w_in0, g_qa, g_ka, lam, lam_init, g_subln, g_qb, g_kb, cmp_w, w_out0 = mw
    n, s = h.shape[:2]
    qa, ka, va, qb, kvb, gb = split_in0(h @ w_in0)
    q, k, v = diff_heads(qa, ka, va, pos, g_qa, g_ka)
    o_a = diff_output(diff_attn_prompt(q, k, v, pos, lam), g_subln, lam_init)
    qn, qr, long_rows, win_rows, gates = nsa_heads(qb, kvb, gb, pos, g_qb, g_kb)
    o_cmp, o_sel = nsa_long_branches(qn, qr, long_rows, pos, g_kb[0], *cmp_w)
    o_win, _ = banded_attn(qr, win_rows[:, :, 0], win_rows[:, :, 1], NSA_WINDOW)
    o_b = nsa_merge(gates, o_cmp, o_sel, o_win)
    y = jnp.concatenate([o_a, o_b], axis=-1) @ w_out0
    a_rows = jnp.stack([k.reshape(n, s, A_HEADS, A_VDIM), v], axis=2)
    return y, a_rows, long_rows, win_rows[:, s - min(NSA_WINDOW, s):]


def even_mixer_sample(h, pos, cache_a_kv, cache_nsa_kv, state_nsa_win, page_table, mw):
    w_in0, g_qa, g_ka, lam, lam_init, g_subln, g_qb, g_kb, cmp_w, w_out0 = mw
    n, t = h.shape[:2]
    qa, ka, va, qb, kvb, gb = split_in0(h @ w_in0)
    q, k, v = diff_heads(qa, ka, va, pos, g_qa, g_ka)
    a_rows = jnp.stack([k.reshape(n, t, A_HEADS, A_VDIM), v], axis=2)
    a_all = jnp.concatenate([paged_rows(cache_a_kv, page_table), a_rows], axis=1)
    L = a_all.shape[1]
    o = diff_core(q, a_all[:, :, 0].reshape(n, L, A_HEADS, 2, HEAD_DIM), a_all[:, :, 1], pos,
                  jnp.arange(L, dtype=jnp.int32), lam)
    o_a = diff_output(o, g_subln, lam_init)
    qn, qr, long_rows, win_rows, gates = nsa_heads(qb, kvb, gb, pos, g_qb, g_kb)
    long_all = jnp.concatenate([paged_rows(cache_nsa_kv, page_table), long_rows], axis=1)
    o_cmp, o_sel = nsa_long_branches(qn, qr, long_all, pos, g_kb[0], *cmp_w)
    n_buf = state_nsa_win.shape[1]
    win_all = jnp.concatenate([state_nsa_win, win_rows], axis=1)
    o_win = window_attn_sample(qr, win_all[:, :, 0], win_all[:, :, 1], n_buf, NSA_WINDOW)
    o_b = nsa_merge(gates, o_cmp, o_sel, o_win)
    y = jnp.concatenate([o_a, o_b], axis=-1) @ w_out0
    return y, a_rows, long_rows, roll_buffer(state_nsa_win, win_rows)


def dilated_heads(h, pos, w_in1, g_qc, g_kc):
    n, t = h.shape[:2]
    proj = (h @ w_in1).reshape(n, t, N_C_GROUPS, 3, C_HEADS, HEAD_DIM)
    return [(partial_rope(rms_norm(proj[:, :, gi, 0], g_qc[gi]), pos),
             partial_rope(rms_norm(proj[:, :, gi, 1], g_kc[gi]), pos),
             proj[:, :, gi, 2]) for gi in range(N_C_GROUPS)]


def dilated_attn_prompt(q, k, v, dil, band):
    n, S, h, dh = q.shape
    L = S // dil

    def sub(x):
        return x.reshape(n, L, dil, h, dh).transpose(0, 2, 1, 3, 4).reshape(n * dil, L, h, dh)

    o, lse = banded_attn(sub(q)[:, :, :, None], sub(k), sub(v), band)
    o = o.reshape(n, dil, L, h, dh).transpose(0, 2, 1, 3, 4).reshape(n, S, h, dh)
    lse = lse.reshape(n, dil, L, h).transpose(0, 2, 1, 3).reshape(n, S, h)
    return o, lse


def dilated_attn_sample(q, buf, new_rows, dil, window):
    n_buf, t = buf.shape[1], q.shape[1]
    n_keys = window // dil + 1
    idx = n_buf + np.arange(t)[:, None] - dil * np.arange(n_keys)[None, :]
    rows = gather_rows(buf, new_rows, idx)
    s = jnp.einsum('nthd,ntkhd->nthk', q, rows[:, :, :, 0], preferred_element_type=jnp.float32) * SCALE
    p, lse = masked_softmax(s, (idx >= 0)[None, :, None, :])
    o = jnp.einsum('nthk,ntkhd->nthd', p, rows[:, :, :, 1].astype(jnp.float32))
    return o.astype(q.dtype), lse


def merge_dilations(outs, lses):
    w = jax.nn.softmax(jnp.stack(lses, axis=0), axis=0)
    o = jnp.einsum('gnth,gnthd->nthd', w, jnp.stack(outs, axis=0).astype(jnp.float32))
    return o.astype(outs[0].dtype)


def odd_mixer_prompt(h, pos, cw):
    w_in1, g_qc, g_kc, w_out1 = cw
    n, s = h.shape[:2]
    outs, lses, bufs = [], [], []
    for (window, dil), (q, k, v) in zip(C_GROUPS, dilated_heads(h, pos, w_in1, g_qc, g_kc)):
        o, lse = dilated_attn_prompt(q, k, v, dil, window // dil)
        outs.append(o)
        lses.append(lse)
        bufs.append(jnp.stack([k, v], axis=2)[:, s - min(window, s):])
    y = merge_dilations(outs, lses).reshape(n, s, C_W) @ w_out1
    return y, bufs


def odd_mixer_sample(h, pos, states, cw):
    w_in1, g_qc, g_kc, w_out1 = cw
    n, t = h.shape[:2]
    outs, lses, bufs = [], [], []
    for (window, dil), (q, k, v), buf in zip(C_GROUPS, dilated_heads(h, pos, w_in1, g_qc, g_kc), states):
        new_rows = jnp.stack([k, v], axis=2)
        o, lse = dilated_attn_sample(q, buf, new_rows, dil, window)
        outs.append(o)
        lses.append(lse)
        bufs.append(roll_buffer(buf, new_rows))
    y = merge_dilations(outs, lses).reshape(n, t, C_W) @ w_out1
    return y, bufs


def swiglu(h, wg, wu, wd):
    return (jax.nn.silu(h @ wg) * (h @ wu)) @ wd


def moe_swiglu(h, w_router, wg, wu, wd):
    logits = jnp.einsum('ntd,de->nte', h, w_router, preferred_element_type=jnp.float32)
    top_v, top_i = lax.top_k(logits, TOP_K)
    gate = jax.nn.softmax(top_v, axis=-1)
    comb = jnp.einsum('ntk,ntke->nte', gate, jax.nn.one_hot(top_i, N_EXPERTS, dtype=jnp.float32))
    y = jnp.zeros(h.shape, jnp.float32)
    for e in range(N_EXPERTS):
        y = y + comb[..., e:e + 1] * swiglu(h, wg[e], wu[e], wd[e]).astype(jnp.float32)
    return y.astype(h.dtype)


def setup_inputs(seed: int = 0) -> dict:
    key = jax.random.key(seed)
    keys = iter(jax.random.split(key, 64))
    f32 = jnp.float32

    def normal(shape, scale):
        return jax.random.normal(next(keys), shape, f32) * scale

    def gain(shape):
        return 1.0 + normal(shape, 0.05)

    n_pages = PAST_LEN // PAGE_SIZE
    n_used = DEC_BATCH * n_pages
    n_pool = n_used + max(1, n_used // 4)
    page_table = jax.random.permutation(next(keys), n_pool)[:n_used].reshape(DEC_BATCH, n_pages).astype(jnp.int32)
    return {
        'x_prompt': normal((BATCH, SEQ, D_MODEL), 1.0),
        'x_sample': normal((DEC_BATCH, DEC_SEQ, D_MODEL), 1.0),
        'cache_a_kv': normal((n_pool, PAGE_SIZE, 2, A_HEADS, A_VDIM), 1.0),
        'cache_nsa_kv': normal((n_pool, PAGE_SIZE, 4, NSA_KV_HEADS, HEAD_DIM), 1.0),
        'state_nsa_win': normal((DEC_BATCH, min(NSA_WINDOW, PAST_LEN), 2, NSA_KV_HEADS, HEAD_DIM), 1.0),
        'state_c_w128': normal((DEC_BATCH, min(C_GROUPS[0][0], PAST_LEN), 2, C_HEADS, HEAD_DIM), 1.0),
        'state_c_w512': normal((DEC_BATCH, min(C_GROUPS[1][0], PAST_LEN), 2, C_HEADS, HEAD_DIM), 1.0),
        'state_c_w2048': normal((DEC_BATCH, min(C_GROUPS[2][0], PAST_LEN), 2, C_HEADS, HEAD_DIM), 1.0),
        'page_table': page_table,
        'norm0_mix': gain((D_MODEL,)),
        'w_in0': normal((D_MODEL, IN0_W), D_MODEL ** -0.5),
        'g_qa': gain((HEAD_DIM,)),
        'g_ka': gain((HEAD_DIM,)),
        'lam_q1': normal((HEAD_DIM,), 0.1),
        'lam_k1': normal((HEAD_DIM,), 0.1),
        'lam_q2': normal((HEAD_DIM,), 0.1),
        'lam_k2': normal((HEAD_DIM,), 0.1),
        'g_subln': gain((A_VDIM,)),
        'g_qb': gain((HEAD_DIM,)),
        'g_kb': gain((3, HEAD_DIM)),
        'pe_cmp_k': normal((CMP_LEN, HEAD_DIM), 0.1),
        'w_cmp_k1': normal((CMP_LEN * HEAD_DIM, CMP_HID), (CMP_LEN * HEAD_DIM) ** -0.5),
        'w_cmp_k2': normal((CMP_HID, HEAD_DIM), CMP_HID ** -0.5),
        'pe_cmp_v': normal((CMP_LEN, HEAD_DIM), 0.1),
        'w_cmp_v1': normal((CMP_LEN * HEAD_DIM, CMP_HID), (CMP_LEN * HEAD_DIM) ** -0.5),
        'w_cmp_v2': normal((CMP_HID, HEAD_DIM), CMP_HID ** -0.5),
        'w_out0': normal((OUT0_W, D_MODEL), OUT0_W ** -0.5),
        'norm0_ffn': gain((D_MODEL,)),
        'w_ffn_gate': normal((D_MODEL, D_FF), D_MODEL ** -0.5),
        'w_ffn_up': normal((D_MODEL, D_FF), D_MODEL ** -0.5),
        'w_ffn_down': normal((D_FF, D_MODEL), D_FF ** -0.5),
        'norm1_mix': gain((D_MODEL,)),
        'w_in1': normal((D_MODEL, IN1_W), D_MODEL ** -0.5),
        'g_qc': gain((N_C_GROUPS, HEAD_DIM)),
        'g_kc': gain((N_C_GROUPS, HEAD_DIM)),
        'w_out1': normal((C_W, D_MODEL), C_W ** -0.5),
        'norm1_ffn': gain((D_MODEL,)),
        'w_router': normal((D_MODEL, N_EXPERTS), D_MODEL ** -0.5),
        'w_moe_gate': normal((N_EXPERTS, D_MODEL, D_FF_EXPERT), D_MODEL ** -0.5),
        'w_moe_up': normal((N_EXPERTS, D_MODEL, D_FF_EXPERT), D_MODEL ** -0.5),
        'w_moe_down': normal((N_EXPERTS, D_FF_EXPERT, D_MODEL), D_FF_EXPERT ** -0.5),
    }


def reference(x_prompt, x_sample, cache_a_kv, cache_nsa_kv, state_nsa_win, state_c_w128, state_c_w512,
              state_c_w2048, page_table, norm0_mix, w_in0, g_qa, g_ka, lam_q1, lam_k1, lam_q2, lam_k2,
              g_subln, g_qb, g_kb, pe_cmp_k, w_cmp_k1, w_cmp_k2, pe_cmp_v, w_cmp_v1, w_cmp_v2, w_out0,
              norm0_ffn, w_ffn_gate, w_ffn_up, w_ffn_down, norm1_mix, w_in1, g_qc, g_kc, w_out1,
              norm1_ffn, w_router, w_moe_gate, w_moe_up, w_moe_down):
    pos_p = jnp.arange(SEQ, dtype=jnp.int32)
    pos_s = PAST_LEN + jnp.arange(DEC_SEQ, dtype=jnp.int32)
    cmp_w = (pe_cmp_k, w_cmp_k1, w_cmp_k2, pe_cmp_v, w_cmp_v1, w_cmp_v2)
    c_states = (state_c_w128, state_c_w512, state_c_w2048)
    hp, hs = x_prompt, x_sample
    for layer in range(DEPTH):
        if layer % 2 == 0:
            lam_init = 0.8 - 0.6 * math.exp(-0.3 * layer)
            lam = diff_lambda(lam_q1, lam_k1, lam_q2, lam_k2, lam_init)
            mw = (w_in0, g_qa, g_ka, lam, lam_init, g_subln, g_qb, g_kb, cmp_w, w_out0)
            y_p, a_kv_p, nsa_kv_p, nsa_win_p = even_mixer_prompt(rms_norm(hp, norm0_mix), pos_p, mw)
            y_s, a_kv_s, nsa_kv_s, nsa_win_s = even_mixer_sample(rms_norm(hs, norm0_mix), pos_s, cache_a_kv,
                                                                 cache_nsa_kv, state_nsa_win, page_table, mw)
            hp = hp + y_p
            hs = hs + y_s
            hp = hp + swiglu(rms_norm(hp, norm0_ffn), w_ffn_gate, w_ffn_up, w_ffn_down)
            hs = hs + swiglu(rms_norm(hs, norm0_ffn), w_ffn_gate, w_ffn_up, w_ffn_down)
        else:
            cw = (w_in1, g_qc, g_kc, w_out1)
            y_p, c_p = odd_mixer_prompt(rms_norm(hp, norm1_mix), pos_p, cw)
            y_s, c_s = odd_mixer_sample(rms_norm(hs, norm1_mix), pos_s, c_states, cw)
            hp = hp + y_p
            hs = hs + y_s
            hp = hp + moe_swiglu(rms_norm(hp, norm1_ffn), w_router, w_moe_gate, w_moe_up, w_moe_down)
            hs = hs + moe_swiglu(rms_norm(hs, norm1_ffn), w_router, w_moe_gate, w_moe_up, w_moe_down)
    c128_p, c512_p, c2048_p = c_p
    c128_s, c512_s, c2048_s = c_s
    return (hp, hs, a_kv_p, a_kv_s, nsa_kv_p, nsa_kv_s, nsa_win_p, nsa_win_s,
            c128_p, c128_s, c512_p, c512_s, c2048_p, c2048_s)
```

```python
import functools
import math

import jax
import jax.numpy as jnp
import numpy as np
from jax import lax
from jax.experimental import pallas as pl
from jax.experimental.pallas import tpu as pltpu

F32 = jnp.float32
BF16 = jnp.bfloat16

D_MODEL = 1024
HEAD_DIM = 64
ROT_DIM = HEAD_DIM // 4
ROPE_THETA = 500000.0
NORM_EPS = 1e-6
SCALE = HEAD_DIM ** -0.5
Q_BLOCK = 128
NEG_INF = -1e30
TINY = 1e-30
A_HEADS = 4
A_VDIM = 2 * HEAD_DIM
NSA_HEADS = 8
NSA_KV_HEADS = 2
NSA_REP = NSA_HEADS // NSA_KV_HEADS
CMP_LEN = 32
CMP_STRIDE = 16
SEL_BLOCK = 64
SEL_TOPK = 16
SEL_Q_BLOCK = 64
NSA_WINDOW = 512
FORCE_BONUS = 1e3
C_HEADS = 16
C_GROUPS = ((128, 1), (512, 4), (2048, 16))
N_C_GROUPS = len(C_GROUPS)
A_QK_W = A_HEADS * 2 * HEAD_DIM
A_V_W = A_HEADS * A_VDIM
NSA_Q_W = NSA_HEADS * HEAD_DIM
NSA_KV_W = 6 * NSA_KV_HEADS * HEAD_DIM
NSA_GATE_W = 3 * NSA_HEADS
C_W = C_HEADS * HEAD_DIM
N_EXPERTS = 8
TOP_K = 2

VMEM_LIMIT_BYTES = 56 * 1024 * 1024
TOKEN_TILE = 512


def _cparams(*sem):
    return pltpu.CompilerParams(dimension_semantics=sem, vmem_limit_bytes=VMEM_LIMIT_BYTES)


def _rmsnorm_body(x_ref, g_ref, o_ref):
    x = x_ref[...]
    ms = jnp.mean(x * x, axis=-1, keepdims=True)
    o_ref[...] = (x * lax.rsqrt(ms + NORM_EPS) * g_ref[...]).astype(o_ref.dtype)


def rmsnorm_cast(x, g):
    m, d = x.shape
    tm = TOKEN_TILE
    return pl.pallas_call(
        _rmsnorm_body,
        grid=(m // tm,),
        in_specs=[pl.BlockSpec((tm, d), lambda i: (i, 0)), pl.BlockSpec((1, d), lambda i: (0, 0))],
        out_specs=pl.BlockSpec((tm, d), lambda i: (i, 0)),
        out_shape=jax.ShapeDtypeStruct((m, d), BF16),
        compiler_params=_cparams("parallel"),
        name="rmsnorm",
    )(x, g.reshape(1, d))


def _mm_body(x_ref, w_ref, o_ref):
    o_ref[...] = jnp.dot(x_ref[...], w_ref[...], preferred_element_type=F32).astype(o_ref.dtype)


def _mm_res_body(x_ref, w_ref, r_ref, o_ref):
    acc = jnp.dot(x_ref[...], w_ref[...], preferred_element_type=F32)
    o_ref[...] = (acc + r_ref[...]).astype(o_ref.dtype)


def matmul(x, w, *, tn, res=None, out_dtype=F32, name="matmul"):
    m, k = x.shape
    n = w.shape[1]
    tm = TOKEN_TILE
    in_specs = [pl.BlockSpec((tm, k), lambda j, i: (i, 0)), pl.BlockSpec((k, tn), lambda j, i: (0, j))]
    args = [x, w]
    body = _mm_body
    if res is not None:
        in_specs.append(pl.BlockSpec((tm, tn), lambda j, i: (i, j)))
        args.append(res)
        body = _mm_res_body
    return pl.pallas_call(
        body,
        grid=(n // tn, m // tm),
        in_specs=in_specs,
        out_specs=pl.BlockSpec((tm, tn), lambda j, i: (i, j)),
        out_shape=jax.ShapeDtypeStruct((m, n), out_dtype),
        compiler_params=_cparams("parallel", "parallel"),
        name=name,
    )(*args)


def _gate_up_body(x_ref, wg_ref, wu_ref, o_ref):
    x = x_ref[...]
    g = jnp.dot(x, wg_ref[...], preferred_element_type=F32)
    u = jnp.dot(x, wu_ref[...], preferred_element_type=F32)
    o_ref[...] = (g * jax.nn.sigmoid(g) * u).astype(o_ref.dtype)


def swiglu_gate_up(x, wg, wu, *, tn):
    m, k = x.shape
    n = wg.shape[1]
    tm = TOKEN_TILE
    return pl.pallas_call(
        _gate_up_body,
        grid=(n // tn, m // tm),
        in_specs=[pl.BlockSpec((tm, k), lambda j, i: (i, 0)),
                  pl.BlockSpec((k, tn), lambda j, i: (0, j)),
                  pl.BlockSpec((k, tn), lambda j, i: (0, j))],
        out_specs=pl.BlockSpec((tm, tn), lambda j, i: (i, j)),
        out_shape=jax.ShapeDtypeStruct((m, n), BF16),
        compiler_params=_cparams("parallel", "parallel"),
        name="swiglu_gate_up",
    )(x, wg, wu)


def _moe_gate_up_body(te_ref, x_ref, wg_ref, wu_ref, o_ref):
    del te_ref
    x = x_ref[...]
    g = jnp.dot(x, wg_ref[...], preferred_element_type=F32)
    u = jnp.dot(x, wu_ref[...], preferred_element_type=F32)
    o_ref[...] = (g * jax.nn.sigmoid(g) * u).astype(o_ref.dtype)


def _moe_down_body(te_ref, a_ref, wd_ref, o_ref):
    del te_ref
    o_ref[...] = jnp.dot(a_ref[...], wd_ref[...], preferred_element_type=F32)


def moe_grouped_ffn(xs, tile_expert, wg, wu, wd, *, tf):
    p, d = xs.shape
    f = wg.shape[2]
    tm = TOKEN_TILE
    nt = p // tm
    act = pl.pallas_call(
        _moe_gate_up_body,
        grid_spec=pltpu.PrefetchScalarGridSpec(
            num_scalar_prefetch=1,
            grid=(f // tf, nt),
            in_specs=[pl.BlockSpec((tm, d), lambda j, i, te: (i, 0)),
                      pl.BlockSpec((None, d, tf), lambda j, i, te: (te[i], 0, j)),
                      pl.BlockSpec((None, d, tf), lambda j, i, te: (te[i], 0, j))],
            out_specs=pl.BlockSpec((tm, tf), lambda j, i, te: (i, j)),
        ),
        out_shape=jax.ShapeDtypeStruct((p, f), BF16),
        compiler_params=_cparams("parallel", "arbitrary"),
        name="moe_gate_up",
    )(tile_expert, xs, wg, wu)
    return pl.pallas_call(
        _moe_down_body,
        grid_spec=pltpu.PrefetchScalarGridSpec(
            num_scalar_prefetch=1,
            grid=(nt,),
            in_specs=[pl.BlockSpec((tm, f), lambda i, te: (i, 0)),
                      pl.BlockSpec((None, f, d), lambda i, te: (te[i], 0, 0))],
            out_specs=pl.BlockSpec((tm, d), lambda i, te: (i, 0)),
        ),
        out_shape=jax.ShapeDtypeStruct((p, d), F32),
        compiler_params=_cparams("arbitrary"),
        name="moe_down",
    )(tile_expert, act, wd)


def _rms_norm(x, g):
    xf = x.astype(F32)
    y = xf * lax.rsqrt(jnp.mean(xf * xf, axis=-1, keepdims=True) + NORM_EPS)
    return (y * g.astype(F32)).astype(x.dtype)


def _partial_rope(x, pos):
    half = ROT_DIM // 2
    inv_freq = ROPE_THETA ** (-jnp.arange(half, dtype=F32) / half)
    ang = pos.astype(F32)[:, None] * inv_freq[None, :]
    shape = (1, pos.shape[0]) + (1,) * (x.ndim - 3) + (half,)
    cos = jnp.cos(ang).reshape(shape)
    sin = jnp.sin(ang).reshape(shape)
    xf = x.astype(F32)
    x1, x2 = xf[..., :half], xf[..., half:ROT_DIM]
    out = jnp.concatenate([x1 * cos - x2 * sin, x2 * cos + x1 * sin, xf[..., ROT_DIM:]], axis=-1)
    return out.astype(x.dtype)


def _masked_softmax(s, mask):
    s = jnp.where(mask, s.astype(F32), NEG_INF)
    m = jnp.max(s, axis=-1, keepdims=True)
    e = jnp.where(mask, jnp.exp(s - m), 0.0)
    l = jnp.maximum(jnp.sum(e, axis=-1, keepdims=True), TINY)
    return e / l, (m + jnp.log(l))[..., 0]


def _paged_rows(cache, page_table):
    g = cache[page_table]
    return g.reshape((g.shape[0], g.shape[1] * g.shape[2]) + g.shape[3:])


def _roll_buffer(buf, new):
    n_buf, t = buf.shape[1], new.shape[1]
    if t >= n_buf:
        return new[:, t - n_buf:]
    return jnp.concatenate([buf[:, t:], new], axis=1)


def _gather_rows(buf, new, idx):
    n_buf = buf.shape[1]
    from_buf = buf[:, np.clip(idx, 0, n_buf - 1)]
    from_new = new[:, np.clip(idx - n_buf, 0, new.shape[1] - 1)]
    sel = (idx < n_buf).reshape(idx.shape + (1,) * (buf.ndim - 2))
    return jnp.where(sel, from_buf, from_new)


def _banded_attn(q, k, v, band):
    n, L, g, r, dh = q.shape
    blk = math.gcd(L, Q_BLOCK)
    nb = L // blk
    pad = ((0, 0), (band, 0), (0, 0), (0, 0))
    idx = np.arange(nb)[:, None] * blk + np.arange(blk + band)[None, :]
    kb = jnp.pad(k, pad)[:, idx]
    vb = jnp.pad(v, pad)[:, idx]
    qb = q.reshape(n, nb, blk, g, r, dh)
    s = jnp.einsum('nbqgrd,nbkgd->nbgrqk', qb, kb, preferred_element_type=F32) * SCALE
    qpos = np.arange(nb)[:, None] * blk + np.arange(blk)[None, :]
    kpos = idx - band
    dist = qpos[:, :, None] - kpos[:, None, :]
    mask = (dist >= 0) & (dist <= band) & (kpos[:, None, :] >= 0)
    p, lse = _masked_softmax(s, mask[None, :, None, None])
    o = jnp.einsum('nbgrqk,nbkgd->nbqgrd', p, vb.astype(F32))
    return o.reshape(n, L, g, r, dh).astype(q.dtype), lse.transpose(0, 1, 4, 2, 3).reshape(n, L, g, r)


def _diff_heads(qa, ka, va, pos, g_q, g_k):
    n, t = qa.shape[:2]
    q = _partial_rope(_rms_norm(qa.reshape(n, t, A_HEADS, 2, HEAD_DIM), g_q), pos)
    k = _partial_rope(_rms_norm(ka.reshape(n, t, A_HEADS, 2, HEAD_DIM), g_k), pos)
    return q, k, va.reshape(n, t, A_HEADS, A_VDIM)


def _diff_core(q, k, v, qpos, kpos, lam):
    s = jnp.einsum('nqhmd,nkhmd->nhmqk', q, k, preferred_element_type=F32) * SCALE
    p, _ = _masked_softmax(s, (kpos[None, :] <= qpos[:, None])[None, None, None])
    a = p[:, :, 0] - lam * p[:, :, 1]
    return jnp.einsum('nhqk,nkhe->nqhe', a, v.astype(F32)).astype(v.dtype)


def _diff_attn_prompt(q, k, v, pos, lam):
    n, s = q.shape[:2]
    nb = s // Q_BLOCK
    qb = q.reshape((n, nb, Q_BLOCK) + q.shape[2:]).swapaxes(0, 1)
    ob = lax.map(lambda a: _diff_core(a[0], k, v, a[1], pos, lam), (qb, pos.reshape(nb, Q_BLOCK)))
    return ob.swapaxes(0, 1).reshape(n, s, A_HEADS, A_VDIM)


def _diff_output(o, g_sub, lam_init):
    n, t = o.shape[:2]
    return (_rms_norm(o, g_sub) * (1.0 - lam_init)).reshape(n, t, A_V_W)


def _nsa_heads(qb, kvb, gb, pos, g_q, g_k):
    n, t = qb.shape[:2]
    q = _rms_norm(qb.reshape(n, t, NSA_KV_HEADS, NSA_REP, HEAD_DIM), g_q)
    q_rot = _partial_rope(q, pos)
    kv = kvb.reshape(n, t, 6, NSA_KV_HEADS, HEAD_DIM)
    k_slc = _partial_rope(_rms_norm(kv[:, :, 2], g_k[1]), pos)
    k_win = _partial_rope(_rms_norm(kv[:, :, 4], g_k[2]), pos)
    long_rows = jnp.stack([kv[:, :, 0], kv[:, :, 1], k_slc, kv[:, :, 3]], axis=2)
    win_rows = jnp.stack([k_win, kv[:, :, 5]], axis=2)
    gates = jax.nn.sigmoid(gb.astype(F32)).reshape(n, t, NSA_KV_HEADS, NSA_REP, 3)
    return q, q_rot, long_rows, win_rows, gates


def _nsa_compress(rows, pe, w1, w2):
    n, L, g, dh = rows.shape
    n_cmp = (L - CMP_LEN) // CMP_STRIDE + 1
    idx = np.arange(n_cmp)[:, None] * CMP_STRIDE + np.arange(CMP_LEN)[None, :]
    blocks = rows[:, idx] + pe[None, None, :, None, :]
    flat = blocks.transpose(0, 1, 3, 2, 4).reshape(n, n_cmp, g, CMP_LEN * dh)
    return jax.nn.silu(flat @ w1) @ w2


def _nsa_cmp_attn(q, k_cmp, v_cmp, qpos):
    n_cmp = k_cmp.shape[1]
    end = jnp.asarray(np.arange(n_cmp) * CMP_STRIDE + CMP_LEN - 1)
    s = jnp.einsum('nqgrd,ncgd->nqgrc', q, k_cmp, preferred_element_type=F32) * SCALE
    visible = end[None, :] <= qpos[:, None]
    p, _ = _masked_softmax(s, visible[None, :, None, None, :])
    o = jnp.einsum('nqgrc,ncgd->nqgrd', p, v_cmp.astype(F32)).astype(q.dtype)
    return o, p


def _cmp_to_sel_overlap(n_cmp, n_sel):
    c0 = np.arange(n_cmp)[:, None] * CMP_STRIDE
    s0 = np.arange(n_sel)[None, :] * SEL_BLOCK
    ov = np.minimum(c0 + CMP_LEN, s0 + SEL_BLOCK) - np.maximum(c0, s0)
    return jnp.asarray(np.maximum(ov, 0) / CMP_LEN, dtype=F32)


def _nsa_select(p_cmp, qpos, n_sel):
    imp = jnp.einsum('nqgrc,cj->nqgj', p_cmp, _cmp_to_sel_overlap(p_cmp.shape[-1], n_sel))
    blk = jnp.arange(n_sel)[None, :]
    cur = (qpos // SEL_BLOCK)[:, None]
    valid = blk <= cur
    forced = (blk == 0) | (blk == cur) | (blk == cur - 1)
    score = jnp.where(valid[None, :, None], imp + jnp.where(forced, FORCE_BONUS, 0.0)[None, :, None], NEG_INF)
    _, sel = lax.top_k(score, min(SEL_TOPK, n_sel))
    return sel


def _nsa_sel_attn(q, k_blk, v_blk, sel, qpos):
    n, qc, g, r, dh = q.shape
    kk = sel.shape[-1]
    n_i = jnp.arange(n)[:, None, None, None]
    g_i = jnp.arange(g)[None, None, :, None]
    kg = k_blk[n_i, g_i, sel]
    vg = v_blk[n_i, g_i, sel]
    kpos = sel[..., None] * SEL_BLOCK + jnp.arange(SEL_BLOCK)
    visible = (kpos <= qpos[None, :, None, None, None]).reshape(n, qc, g, 1, kk * SEL_BLOCK)
    s = jnp.einsum('nqgrd,nqgkbd->nqgrkb', q, kg, preferred_element_type=F32)
    p, _ = _masked_softmax(s.reshape(n, qc, g, r, kk * SEL_BLOCK) * SCALE, visible)
    o = jnp.einsum('nqgrx,nqgxd->nqgrd', p, vg.reshape(n, qc, g, kk * SEL_BLOCK, dh).astype(F32))
    return o.astype(q.dtype)


def _nsa_long_branches(q, q_rot, long_all, qpos, g_kc, pe_k, w_k1, w_k2, pe_v, w_v1, w_v2):
    n, L, _, g, dh = long_all.shape
    k_cmp = _rms_norm(_nsa_compress(long_all[:, :, 0], pe_k, w_k1, w_k2), g_kc)
    v_cmp = _nsa_compress(long_all[:, :, 1], pe_v, w_v1, w_v2)
    o_cmp, p_cmp = _nsa_cmp_attn(q, k_cmp, v_cmp, qpos)
    n_sel = -(-L // SEL_BLOCK)
    sel = _nsa_select(p_cmp, qpos, n_sel)

    def to_blocks(x):
        x = jnp.pad(x, ((0, 0), (0, n_sel * SEL_BLOCK - L), (0, 0), (0, 0)))
        return x.reshape(n, n_sel, SEL_BLOCK, g, dh).transpose(0, 3, 1, 2, 4)

    k_blk, v_blk = to_blocks(long_all[:, :, 2]), to_blocks(long_all[:, :, 3])
    nq = q.shape[1]
    qc = math.gcd(nq, SEL_Q_BLOCK)
    nc = nq // qc

    def chunks(x):
        return x.reshape((n, nc, qc) + x.shape[2:]).swapaxes(0, 1)

    o_sel = lax.map(lambda a: _nsa_sel_attn(a[0], k_blk, v_blk, a[1], a[2]),
                    (chunks(q_rot), chunks(sel), qpos.reshape(nc, qc)))
    return o_cmp, o_sel.swapaxes(0, 1).reshape(q.shape)


def _window_attn_sample(q, k_all, v_all, n_buf, window):
    t = q.shape[1]
    dist = (n_buf + np.arange(t))[:, None] - np.arange(n_buf + t)[None, :]
    visible = (dist >= 0) & (dist <= window)
    s = jnp.einsum('ntgrd,nkgd->ntgrk', q, k_all, preferred_element_type=F32) * SCALE
    p, _ = _masked_softmax(s, visible[None, :, None, None, :])
    return jnp.einsum('ntgrk,nkgd->ntgrd', p, v_all.astype(F32)).astype(q.dtype)


def _nsa_merge(gates, o_cmp, o_sel, o_win):
    o = gates[..., 0:1] * o_cmp.astype(F32) + gates[..., 1:2] * o_sel.astype(F32) + gates[..., 2:3] * o_win.astype(F32)
    n, t = o.shape[:2]
    return o.reshape(n, t, NSA_Q_W).astype(o_cmp.dtype)


def _split_in0(proj):
    sizes = [A_QK_W, A_QK_W, A_V_W, NSA_Q_W, NSA_KV_W, NSA_GATE_W]
    return jnp.split(proj, [int(o) for o in np.cumsum(sizes)[:-1]], axis=-1)


def _even_mixer_prompt(proj, pos, mw):
    g_qa, g_ka, lam, lam_init, g_subln, g_qb, g_kb, cmp_w = mw
    n, s = proj.shape[:2]
    qa, ka, va, qb, kvb, gb = _split_in0(proj)
    q, k, v = _diff_heads(qa, ka, va, pos, g_qa, g_ka)
    o_a = _diff_output(_diff_attn_prompt(q, k, v, pos, lam), g_subln, lam_init)
    qn, qr, long_rows, win_rows, gates = _nsa_heads(qb, kvb, gb, pos, g_qb, g_kb)
    o_cmp, o_sel = _nsa_long_branches(qn, qr, long_rows, pos, g_kb[0], *cmp_w)
    o_win, _ = _banded_attn(qr, win_rows[:, :, 0], win_rows[:, :, 1], NSA_WINDOW)
    o_b = _nsa_merge(gates, o_cmp, o_sel, o_win)
    a_rows = jnp.stack([k.reshape(n, s, A_HEADS, A_VDIM), v], axis=2)
    return jnp.concatenate([o_a, o_b], axis=-1), a_rows, long_rows, win_rows[:, s - min(NSA_WINDOW, s):]


def _even_mixer_sample(proj, pos, cache_a_kv, cache_nsa_kv, state_nsa_win, page_table, mw):
    g_qa, g_ka, lam, lam_init, g_subln, g_qb, g_kb, cmp_w = mw
    n, t = proj.shape[:2]
    qa, ka, va, qb, kvb, gb = _split_in0(proj)
    q, k, v = _diff_heads(qa, ka, va, pos, g_qa, g_ka)
    a_rows = jnp.stack([k.reshape(n, t, A_HEADS, A_VDIM), v], axis=2)
    a_all = jnp.concatenate([_paged_rows(cache_a_kv, page_table), a_rows], axis=1)
    L = a_all.shape[1]
    o = _diff_core(q, a_all[:, :, 0].reshape(n, L, A_HEADS, 2, HEAD_DIM), a_all[:, :, 1], pos,
                   jnp.arange(L, dtype=jnp.int32), lam)
    o_a = _diff_output(o, g_subln, lam_init)
    qn, qr, long_rows, win_rows, gates = _nsa_heads(qb, kvb, gb, pos, g_qb, g_kb)
    long_all = jnp.concatenate([_paged_rows(cache_nsa_kv, page_table), long_rows], axis=1)
    o_cmp, o_sel = _nsa_long_branches(qn, qr, long_all, pos, g_kb[0], *cmp_w)
    n_buf = state_nsa_win.shape[1]
    win_all = jnp.concatenate([state_nsa_win, win_rows], axis=1)
    o_win = _window_attn_sample(qr, win_all[:, :, 0], win_all[:, :, 1], n_buf, NSA_WINDOW)
    o_b = _nsa_merge(gates, o_cmp, o_sel, o_win)
    return jnp.concatenate([o_a, o_b], axis=-1), a_rows, long_rows, _roll_buffer(state_nsa_win, win_rows)


def _dilated_heads(proj, pos, g_qc, g_kc):
    n, t = proj.shape[:2]
    proj = proj.reshape(n, t, N_C_GROUPS, 3, C_HEADS, HEAD_DIM)
    return [(_partial_rope(_rms_norm(proj[:, :, gi, 0], g_qc[gi]), pos),
             _partial_rope(_rms_norm(proj[:, :, gi, 1], g_kc[gi]), pos),
             proj[:, :, gi, 2]) for gi in range(N_C_GROUPS)]


def _dilated_attn_prompt(q, k, v, dil, band):
    n, S, h, dh = q.shape
    L = S // dil

    def sub(x):
        return x.reshape(n, L, dil, h, dh).transpose(0, 2, 1, 3, 4).reshape(n * dil, L, h, dh)

    o, lse = _banded_attn(sub(q)[:, :, :, None], sub(k), sub(v), band)
    o = o.reshape(n, dil, L, h, dh).transpose(0, 2, 1, 3, 4).reshape(n, S, h, dh)
    lse = lse.reshape(n, dil, L, h).transpose(0, 2, 1, 3).reshape(n, S, h)
    return o, lse


def _dilated_attn_sample(q, buf, new_rows, dil, window):
    n_buf, t = buf.shape[1], q.shape[1]
    n_keys = window // dil + 1
    idx = n_buf + np.arange(t)[:, None] - dil * np.arange(n_keys)[None, :]
    rows = _gather_rows(buf, new_rows, idx)
    s = jnp.einsum('nthd,ntkhd->nthk', q, rows[:, :, :, 0], preferred_element_type=F32) * SCALE
    p, lse = _masked_softmax(s, (idx >= 0)[None, :, None, :])
    o = jnp.einsum('nthk,ntkhd->nthd', p, rows[:, :, :, 1].astype(F32))
    return o.astype(q.dtype), lse


def _merge_dilations(outs, lses):
    w = jax.nn.softmax(jnp.stack(lses, axis=0), axis=0)
    o = jnp.einsum('gnth,gnthd->nthd', w, jnp.stack(outs, axis=0).astype(F32))
    return o.astype(outs[0].dtype)


def _odd_mixer_prompt(proj, pos, g_qc, g_kc):
    n, s = proj.shape[:2]
    outs, lses, bufs = [], [], []
    for (window, dil), (q, k, v) in zip(C_GROUPS, _dilated_heads(proj, pos, g_qc, g_kc)):
        o, lse = _dilated_attn_prompt(q, k, v, dil, window // dil)
        outs.append(o)
        lses.append(lse)
        bufs.append(jnp.stack([k, v], axis=2)[:, s - min(window, s):])
    return _merge_dilations(outs, lses).reshape(n, s, C_W), bufs


def _odd_mixer_sample(proj, pos, states, g_qc, g_kc):
    n, t = proj.shape[:2]
    outs, lses, bufs = [], [], []
    for (window, dil), (q, k, v), buf in zip(C_GROUPS, _dilated_heads(proj, pos, g_qc, g_kc), states):
        new_rows = jnp.stack([k, v], axis=2)
        o, lse = _dilated_attn_sample(q, buf, new_rows, dil, window)
        outs.append(o)
        lses.append(lse)
        bufs.append(_roll_buffer(buf, new_rows))
    return _merge_dilations(outs, lses).reshape(n, t, C_W), bufs


def _moe(h_bf16, resid, w_router, wg, wu, wd):
    m, d = h_bf16.shape
    tm = TOKEN_TILE
    logits = jnp.dot(h_bf16, w_router.astype(BF16), preferred_element_type=F32)
    top_v, top_i = lax.top_k(logits, TOP_K)
    gate = jax.nn.softmax(top_v, axis=-1)
    flat_e = top_i.reshape(-1)
    order = jnp.argsort(flat_e, stable=True)
    counts = jnp.bincount(flat_e, length=N_EXPERTS)
    padded = ((counts + tm - 1) // tm) * tm
    pstart = jnp.cumsum(padded) - padded
    cstart = jnp.cumsum(counts) - counts
    sorted_e = flat_e[order]
    rank = jnp.arange(m * TOP_K) - cstart[sorted_e]
    dest = pstart[sorted_e] + rank
    p_rows = m * TOP_K + N_EXPERTS * tm
    src_tok = jnp.zeros((p_rows,), jnp.int32).at[dest].set((order // TOP_K).astype(jnp.int32))
    valid = jnp.zeros((p_rows,), jnp.bool_).at[dest].set(True)
    xs = jnp.where(valid[:, None], h_bf16[src_tok], jnp.zeros((), BF16))
    tile_start = jnp.arange(p_rows // tm) * tm
    pend = jnp.cumsum(padded)
    tile_expert = jnp.minimum(jnp.sum(tile_start[:, None] >= pend[None, :], axis=1), N_EXPERTS - 1).astype(jnp.int32)
    ys = moe_grouped_ffn(xs, tile_expert, wg, wu, wd, tf=1792)
    slot_pos = jnp.zeros((m * TOP_K,), jnp.int32).at[order].set(dest.astype(jnp.int32)).reshape(m, TOP_K)
    y = resid
    for kk in range(TOP_K):
        y = y + gate[:, kk:kk + 1] * ys[slot_pos[:, kk]]
    return y


def kernel(x_prompt, x_sample, cache_a_kv, cache_nsa_kv, state_nsa_win, state_c_w128, state_c_w512, state_c_w2048, page_table, norm0_mix, w_in0, g_qa, g_ka, lam_q1, lam_k1, lam_q2, lam_k2, g_subln, g_qb, g_kb, pe_cmp_k, w_cmp_k1, w_cmp_k2, pe_cmp_v, w_cmp_v1, w_cmp_v2, w_out0, norm0_ffn, w_ffn_gate, w_ffn_up, w_ffn_down, norm1_mix, w_in1, g_qc, g_kc, w_out1, norm1_ffn, w_router, w_moe_gate, w_moe_up, w_moe_down):
    nb, seq, d = x_prompt.shape
    db, dt, _ = x_sample.shape
    past = page_table.shape[1] * cache_a_kv.shape[1]
    mp = nb * seq
    ms = db * dt
    pos_p = jnp.arange(seq, dtype=jnp.int32)
    pos_s = past + jnp.arange(dt, dtype=jnp.int32)
    x = jnp.concatenate([x_prompt.reshape(mp, d), x_sample.reshape(ms, d)], axis=0)

    in0_w = w_in0.shape[1]
    in0_pad = -(-in0_w // 128) * 128
    w_in0_b = jnp.pad(w_in0, ((0, 0), (0, in0_pad - in0_w))).astype(BF16)
    proj0 = matmul(rmsnorm_cast(x, norm0_mix), w_in0_b, tn=in0_pad, name="in_proj0")[:, :in0_w]
    lam_init = 0.8 - 0.6 * math.exp(-0.3 * 0)
    f = lambda a: a.astype(F32)
    lam = jnp.exp(jnp.sum(f(lam_q1) * f(lam_k1))) - jnp.exp(jnp.sum(f(lam_q2) * f(lam_k2))) + lam_init
    cmp_w = (pe_cmp_k, w_cmp_k1, w_cmp_k2, pe_cmp_v, w_cmp_v1, w_cmp_v2)
    mw = (g_qa, g_ka, lam, lam_init, g_subln, g_qb, g_kb, cmp_w)
    cat_p, a_kv_p, nsa_kv_p, nsa_win_p = _even_mixer_prompt(proj0[:mp].reshape(nb, seq, in0_w), pos_p, mw)
    cat_s, a_kv_s, nsa_kv_s, nsa_win_s = _even_mixer_sample(
        proj0[mp:].reshape(db, dt, in0_w), pos_s, cache_a_kv, cache_nsa_kv, state_nsa_win, page_table, mw)
    cat = jnp.concatenate([cat_p.reshape(mp, -1), cat_s.reshape(ms, -1)], axis=0).astype(BF16)
    x = matmul(cat, w_out0.astype(BF16), tn=d, res=x, name="out_proj0")
    act = swiglu_gate_up(rmsnorm_cast(x, norm0_ffn), w_ffn_gate.astype(BF16), w_ffn_up.astype(BF16), tn=1408)
    x = matmul(act, w_ffn_down.astype(BF16), tn=d, res=x, name="ffn_down")

    in1_w = w_in1.shape[1]
    proj1 = matmul(rmsnorm_cast(x, norm1_mix), w_in1.astype(BF16), tn=2304, name="in_proj1")
    y_p, c_p = _odd_mixer_prompt(proj1[:mp].reshape(nb, seq, in1_w), pos_p, g_qc, g_kc)
    y_s, c_s = _odd_mixer_sample(proj1[mp:].reshape(db, dt, in1_w), pos_s,
                                 (state_c_w128, state_c_w512, state_c_w2048), g_qc, g_kc)
    mix = jnp.concatenate([y_p.reshape(mp, -1), y_s.reshape(ms, -1)], axis=0).astype(BF16)
    x = matmul(mix, w_out1.astype(BF16), tn=d, res=x, name="out_proj1")
    x = _moe(rmsnorm_cast(x, norm1_ffn), x, w_router, w_moe_gate.astype(BF16), w_moe_up.astype(BF16),
             w_moe_down.astype(BF16))

    hp = x[:mp].reshape(nb, seq, d)
    hs = x[mp:].reshape(db, dt, d)
    return (hp, hs, a_kv_p, a_kv_s, nsa_kv_p, nsa_kv_s, nsa_win_p, nsa_win_s,
            c_p[0], c_s[0], c_p[1], c_s[1], c_p[2], c_s[2])
```

```python
import functools
import math

import jax
import jax.numpy as jnp
import numpy as np
from jax import lax
from jax.experimental import pallas as pl
from jax.experimental.pallas import tpu as pltpu

F32 = jnp.float32
BF16 = jnp.bfloat16

D_MODEL = 1024
HEAD_DIM = 64
ROT_DIM = HEAD_DIM // 4
ROPE_THETA = 500000.0
NORM_EPS = 1e-6
SCALE = HEAD_DIM ** -0.5
Q_BLOCK = 128
NEG_INF = -1e30
TINY = 1e-30
A_HEADS = 4
A_VDIM = 2 * HEAD_DIM
NSA_HEADS = 8
NSA_KV_HEADS = 2
NSA_REP = NSA_HEADS // NSA_KV_HEADS
CMP_LEN = 32
CMP_STRIDE = 16
SEL_BLOCK = 64
SEL_SHIFT = 6
SEL_TOPK = 16
SEL_Q_BLOCK = 64
NSA_WINDOW = 512
FORCE_BONUS = 1e3
C_HEADS = 16
C_GROUPS = ((128, 1), (512, 4), (2048, 16))
N_C_GROUPS = len(C_GROUPS)
A_QK_W = A_HEADS * 2 * HEAD_DIM
A_V_W = A_HEADS * A_VDIM
NSA_Q_W = NSA_HEADS * HEAD_DIM
NSA_KV_W = 6 * NSA_KV_HEADS * HEAD_DIM
NSA_GATE_W = 3 * NSA_HEADS
C_W = C_HEADS * HEAD_DIM
N_EXPERTS = 8
TOP_K = 2

VMEM_LIMIT_BYTES = 56 * 1024 * 1024
TOKEN_TILE = 512


def _cparams(*sem):
    return pltpu.CompilerParams(dimension_semantics=sem, vmem_limit_bytes=VMEM_LIMIT_BYTES)


def _rmsnorm_body(x_ref, g_ref, o_ref):
    x = x_ref[...]
    ms = jnp.mean(x * x, axis=-1, keepdims=True)
    o_ref[...] = (x * lax.rsqrt(ms + NORM_EPS) * g_ref[...]).astype(o_ref.dtype)


def rmsnorm_cast(x, g):
    m, d = x.shape
    tm = TOKEN_TILE
    return pl.pallas_call(
        _rmsnorm_body,
        grid=(m // tm,),
        in_specs=[pl.BlockSpec((tm, d), lambda i: (i, 0)), pl.BlockSpec((1, d), lambda i: (0, 0))],
        out_specs=pl.BlockSpec((tm, d), lambda i: (i, 0)),
        out_shape=jax.ShapeDtypeStruct((m, d), BF16),
        compiler_params=_cparams("parallel"),
        name="rmsnorm",
    )(x, g.reshape(1, d))


def _mm_body(x_ref, w_ref, o_ref):
    o_ref[...] = jnp.dot(x_ref[...], w_ref[...], preferred_element_type=F32).astype(o_ref.dtype)


def _mm_res_body(x_ref, w_ref, r_ref, o_ref):
    acc = jnp.dot(x_ref[...], w_ref[...], preferred_element_type=F32)
    o_ref[...] = (acc + r_ref[...]).astype(o_ref.dtype)


def matmul(x, w, *, tn, res=None, out_dtype=F32, name="matmul"):
    m, k = x.shape
    n = w.shape[1]
    tm = min(TOKEN_TILE, m)
    assert m % tm == 0 and n % tn == 0
    in_specs = [pl.BlockSpec((tm, k), lambda j, i: (i, 0)), pl.BlockSpec((k, tn), lambda j, i: (0, j))]
    args = [x, w]
    body = _mm_body
    if res is not None:
        in_specs.append(pl.BlockSpec((tm, tn), lambda j, i: (i, j)))
        args.append(res)
        body = _mm_res_body
    return pl.pallas_call(
        body,
        grid=(n // tn, m // tm),
        in_specs=in_specs,
        out_specs=pl.BlockSpec((tm, tn), lambda j, i: (i, j)),
        out_shape=jax.ShapeDtypeStruct((m, n), out_dtype),
        compiler_params=_cparams("parallel", "parallel"),
        name=name,
    )(*args)


def _gate_up_body(x_ref, wg_ref, wu_ref, o_ref):
    x = x_ref[...]
    g = jnp.dot(x, wg_ref[...], preferred_element_type=F32)
    u = jnp.dot(x, wu_ref[...], preferred_element_type=F32)
    o_ref[...] = (g * jax.nn.sigmoid(g) * u).astype(o_ref.dtype)


def swiglu_gate_up(x, wg, wu, *, tn):
    m, k = x.shape
    n = wg.shape[1]
    tm = TOKEN_TILE
    return pl.pallas_call(
        _gate_up_body,
        grid=(n // tn, m // tm),
        in_specs=[pl.BlockSpec((tm, k), lambda j, i: (i, 0)),
                  pl.BlockSpec((k, tn), lambda j, i: (0, j)),
                  pl.BlockSpec((k, tn), lambda j, i: (0, j))],
        out_specs=pl.BlockSpec((tm, tn), lambda j, i: (i, j)),
        out_shape=jax.ShapeDtypeStruct((m, n), BF16),
        compiler_params=_cparams("parallel", "parallel"),
        name="swiglu_gate_up",
    )(x, wg, wu)


def _moe_gate_up_body(te_ref, x_ref, wg_ref, wu_ref, o_ref):
    del te_ref
    x = x_ref[...]
    g = jnp.dot(x, wg_ref[...], preferred_element_type=F32)
    u = jnp.dot(x, wu_ref[...], preferred_element_type=F32)
    o_ref[...] = (g * jax.nn.sigmoid(g) * u).astype(o_ref.dtype)


def _moe_down_body(te_ref, a_ref, wd_ref, o_ref):
    del te_ref
    o_ref[...] = jnp.dot(a_ref[...], wd_ref[...], preferred_element_type=F32)


def moe_grouped_ffn(xs, tile_expert, wg, wu, wd, *, tf):
    p, d = xs.shape
    f = wg.shape[2]
    tm = TOKEN_TILE
    nt = p // tm
    act = pl.pallas_call(
        _moe_gate_up_body,
        grid_spec=pltpu.PrefetchScalarGridSpec(
            num_scalar_prefetch=1,
            grid=(f // tf, nt),
            in_specs=[pl.BlockSpec((tm, d), lambda j, i, te: (i, 0)),
                      pl.BlockSpec((None, d, tf), lambda j, i, te: (te[i], 0, j)),
                      pl.BlockSpec((None, d, tf), lambda j, i, te: (te[i], 0, j))],
            out_specs=pl.BlockSpec((tm, tf), lambda j, i, te: (i, j)),
        ),
        out_shape=jax.ShapeDtypeStruct((p, f), BF16),
        compiler_params=_cparams("parallel", "arbitrary"),
        name="moe_gate_up",
    )(tile_expert, xs, wg, wu)
    return pl.pallas_call(
        _moe_down_body,
        grid_spec=pltpu.PrefetchScalarGridSpec(
            num_scalar_prefetch=1,
            grid=(nt,),
            in_specs=[pl.BlockSpec((tm, f), lambda i, te: (i, 0)),
                      pl.BlockSpec((None, f, d), lambda i, te: (te[i], 0, 0))],
            out_specs=pl.BlockSpec((tm, d), lambda i, te: (i, 0)),
        ),
        out_shape=jax.ShapeDtypeStruct((p, d), F32),
        compiler_params=_cparams("arbitrary"),
        name="moe_down",
    )(tile_expert, act, wd)


def _mm2_body(x_ref, w_ref, o32_ref, o16_ref):
    acc = jnp.dot(x_ref[...], w_ref[...], preferred_element_type=F32)
    o32_ref[...] = acc
    o16_ref[...] = acc.astype(BF16)


def matmul_dual(x, w, *, tn, name):
    m, k = x.shape
    n = w.shape[1]
    tm = TOKEN_TILE
    return pl.pallas_call(
        _mm2_body,
        grid=(n // tn, m // tm),
        in_specs=[pl.BlockSpec((tm, k), lambda j, i: (i, 0)), pl.BlockSpec((k, tn), lambda j, i: (0, j))],
        out_specs=[pl.BlockSpec((tm, tn), lambda j, i: (i, j)), pl.BlockSpec((tm, tn), lambda j, i: (i, j))],
        out_shape=[jax.ShapeDtypeStruct((m, n), F32), jax.ShapeDtypeStruct((m, n), BF16)],
        compiler_params=_cparams("parallel", "parallel"),
        name=name,
    )(x, w)


def rope_tables(pos):
    half = ROT_DIM // 2
    inv_freq = ROPE_THETA ** (-jnp.arange(half, dtype=F32) / half)
    ang = pos.astype(F32)[:, None] * inv_freq[None, :]
    cos, sin = jnp.cos(ang), jnp.sin(ang)
    m = pos.shape[0]
    z_half = jnp.zeros((m, half), F32)
    z_rest = jnp.zeros((m, HEAD_DIM - ROT_DIM), F32)
    c = jnp.concatenate([cos, cos, jnp.ones((m, HEAD_DIM - ROT_DIM), F32)], axis=1)
    s1 = jnp.concatenate([z_half, sin, z_rest], axis=1)
    s2 = jnp.concatenate([-sin, z_half, z_rest], axis=1)
    return tuple(jnp.tile(a, (1, 128 // HEAD_DIM)) for a in (c, s1, s2))


def _hnr_body(x_ref, g_ref, c_ref, s1_ref, s2_ref, *o_refs, width, outs):
    tm = x_ref.shape[0]
    lo = lax.broadcasted_iota(jnp.int32, (tm, 128), 1) < HEAD_DIM
    c, s1, s2 = c_ref[...], s1_ref[...], s2_ref[...]
    for j in range(width // 128):
        sl = slice(j * 128, (j + 1) * 128)
        x = x_ref[:, sl]
        x2 = x * x
        s_lo = jnp.sum(jnp.where(lo, x2, 0.0), axis=-1, keepdims=True)
        s_hi = jnp.sum(jnp.where(lo, 0.0, x2), axis=-1, keepdims=True)
        ms = jnp.where(lo, s_lo, s_hi) * (1.0 / HEAD_DIM)
        xn = x * lax.rsqrt(ms + NORM_EPS) * g_ref[:, sl]
        xr = xn * c + pltpu.roll(xn, ROT_DIM // 2, 1) * s1 + pltpu.roll(xn, 128 - ROT_DIM // 2, 1) * s2
        for (rope, _), o_ref in zip(outs, o_refs):
            o_ref[:, sl] = (xr if rope else xn).astype(o_ref.dtype)


def head_norm_rope(x, gains, tables, *, width, col0, outs, name):
    m = x.shape[0]
    ncol = gains.shape[0]
    tm = TOKEN_TILE
    tab_spec = pl.BlockSpec((tm, 128), lambda i, j: (i, 0))
    return pl.pallas_call(
        functools.partial(_hnr_body, width=width, outs=outs),
        grid=(m // tm, ncol),
        in_specs=[pl.BlockSpec((tm, width), lambda i, j: (i, col0 + j)),
                  pl.BlockSpec((None, 1, width), lambda i, j: (j, 0, 0)),
                  tab_spec, tab_spec, tab_spec],
        out_specs=[pl.BlockSpec((tm, width), lambda i, j: (i, j)) for _ in outs],
        out_shape=[jax.ShapeDtypeStruct((m, ncol * width), dt) for _, dt in outs],
        compiler_params=_cparams("parallel", "parallel"),
        name=name,
    )(x, gains.reshape(ncol, 1, width), *tables)


def _head_gain(g, width):
    return jnp.tile(g.astype(F32), width // HEAD_DIM)


def _step_tables(nq, lookback):
    qi, ki, first, last = [], [], [], []
    for q in range(nq):
        ks = list(range(q + 1)) if lookback is None else [k for k in range(q - lookback, q + 1) if k >= 0]
        for n, k in enumerate(ks):
            qi.append(q)
            ki.append(k)
            first.append(int(n == 0))
            last.append(int(n == len(ks) - 1))
    return tuple(jnp.asarray(a, jnp.int32) for a in (qi, ki, first, last))


def _pos_mask(qi, ki, t, band):
    row = lax.broadcasted_iota(jnp.int32, (t, t), 0)
    col = lax.broadcasted_iota(jnp.int32, (t, t), 1)
    d = (qi - ki) * t + row - col
    mask = d >= 0
    if band is not None:
        mask = mask & (d <= band)
    return mask


def _nt_dot(a, b):
    return lax.dot_general(a, b, (((1,), (1,)), ((), ())), preferred_element_type=F32)


def _online_update(sc, mask, v, m_ref, l_ref, acc_ref, idx):
    sc = jnp.where(mask, sc, NEG_INF)
    m_old = m_ref[idx]
    m_new = jnp.maximum(m_old, jnp.max(sc, axis=-1, keepdims=True))
    alpha = jnp.exp(m_old - m_new)
    p = jnp.where(mask, jnp.exp(sc - m_new), 0.0)
    l_ref[idx] = alpha * l_ref[idx] + jnp.sum(p, axis=-1, keepdims=True)
    acc_ref[idx] = alpha * acc_ref[idx] + jnp.dot(p.astype(BF16), v, preferred_element_type=F32)
    m_ref[idx] = m_new


def _init_state(m_ref, l_ref, acc_ref):
    m_ref[...] = jnp.full(m_ref.shape, NEG_INF, F32)
    l_ref[...] = jnp.zeros(l_ref.shape, F32)
    acc_ref[...] = jnp.zeros(acc_ref.shape, F32)


def _split_pair(q_ref, qs_ref, hb, lo):
    q = q_ref[:, hb * 128:(hb + 1) * 128].astype(F32) * SCALE
    qs_ref[2 * hb] = jnp.where(lo, q, 0.0).astype(BF16)
    qs_ref[2 * hb + 1] = jnp.where(lo, 0.0, q).astype(BF16)


def _gqa_query(q_ref, g, r, lo):
    col = g * NSA_REP + r
    blk = q_ref[:, (col // 2) * 128:(col // 2 + 1) * 128].astype(F32) * SCALE
    h = jnp.where(lo if col % 2 == 0 else jnp.logical_not(lo), blk, 0.0)
    d = h + pltpu.roll(h, HEAD_DIM, 1)
    return jnp.where(lo if g == 0 else jnp.logical_not(lo), d, 0.0).astype(BF16)


def _gqa_store(o_ref, outs, g, lo):
    keep = lo if g == 0 else jnp.logical_not(lo)
    dup = []
    for o in outs:
        z = jnp.where(keep, o, 0.0)
        dup.append(z + pltpu.roll(z, HEAD_DIM, 1))
    for pr in range(NSA_REP // 2):
        blk = g * (NSA_REP // 2) + pr
        o_ref[:, blk * 128:(blk + 1) * 128] = jnp.where(lo, dup[2 * pr], dup[2 * pr + 1]).astype(o_ref.dtype)


def _diff_body(qi_ref, ki_ref, fi_ref, la_ref, q_ref, k_ref, v_ref, lam_ref, gs_ref, o_ref,
               qs_ref, m_ref, l_ref, acc_ref, *, t, lam_init):
    s = pl.program_id(1)
    lo = lax.broadcasted_iota(jnp.int32, (t, 128), 1) < HEAD_DIM

    @pl.when(fi_ref[s] == 1)
    def _():
        for h in range(A_HEADS):
            _split_pair(q_ref, qs_ref, h, lo)
        _init_state(m_ref, l_ref, acc_ref)

    mask = _pos_mask(qi_ref[s], ki_ref[s], t, None)
    for h in range(A_HEADS):
        k = k_ref[:, h * 128:(h + 1) * 128]
        v = v_ref[:, h * 128:(h + 1) * 128]
        for var in range(2):
            _online_update(_nt_dot(qs_ref[2 * h + var], k), mask, v, m_ref, l_ref, acc_ref, 2 * h + var)

    @pl.when(la_ref[s] == 1)
    def _():
        lv = lam_ref[...]
        a = jnp.sum(lv[0:1] * lv[1:2], axis=-1, keepdims=True)
        b = jnp.sum(lv[2:3] * lv[3:4], axis=-1, keepdims=True)
        lam = jnp.exp(a) - jnp.exp(b) + lam_init
        for h in range(A_HEADS):
            o1 = acc_ref[2 * h] / jnp.maximum(l_ref[2 * h], TINY)
            o2 = acc_ref[2 * h + 1] / jnp.maximum(l_ref[2 * h + 1], TINY)
            o = o1 - lam * o2
            ms = jnp.mean(o * o, axis=-1, keepdims=True)
            y = o * lax.rsqrt(ms + NORM_EPS) * gs_ref[...] * (1.0 - lam_init)
            o_ref[:, h * 128:(h + 1) * 128] = y.astype(o_ref.dtype)


def diff_attention_prompt(qk16, v16, lam_rows, g_subln, *, nb, seq, lam_init, t=512):
    nq = seq // t
    tabs = _step_tables(nq, None)
    w = A_V_W
    qmap = lambda n, s, qi, ki, fi, la: (n * nq + qi[s], 0)
    kmap = lambda n, s, qi, ki, fi, la: (n * nq + ki[s], 1)
    vmap = lambda n, s, qi, ki, fi, la: (n * nq + ki[s], 2)
    const = lambda n, s, qi, ki, fi, la: (0, 0)
    return pl.pallas_call(
        functools.partial(_diff_body, t=t, lam_init=lam_init),
        grid_spec=pltpu.PrefetchScalarGridSpec(
            num_scalar_prefetch=4,
            grid=(nb, int(tabs[0].shape[0])),
            in_specs=[pl.BlockSpec((t, w), qmap), pl.BlockSpec((t, w), kmap), pl.BlockSpec((t, w), vmap),
                      pl.BlockSpec((8, 128), const), pl.BlockSpec((1, 128), const)],
            out_specs=pl.BlockSpec((t, w), qmap),
            scratch_shapes=[pltpu.VMEM((2 * A_HEADS, t, 128), BF16), pltpu.VMEM((2 * A_HEADS, t, 1), F32),
                            pltpu.VMEM((2 * A_HEADS, t, 1), F32), pltpu.VMEM((2 * A_HEADS, t, 128), F32)],
        ),
        out_shape=jax.ShapeDtypeStruct((nb * seq, w), BF16),
        compiler_params=_cparams("parallel", "arbitrary"),
        name="diff_attn_prompt",
    )(*tabs, qk16, qk16, v16, lam_rows, g_subln.reshape(1, A_VDIM).astype(F32))


def _dil_body(qi_ref, ki_ref, fi_ref, la_ref, q_ref, k_ref, v_ref, o_ref, lse_ref,
              qs_ref, m_ref, l_ref, acc_ref, *, t, band):
    s = pl.program_id(1)
    lo = lax.broadcasted_iota(jnp.int32, (t, 128), 1) < HEAD_DIM
    nhb = C_HEADS // 2

    @pl.when(fi_ref[s] == 1)
    def _():
        for hb in range(nhb):
            _split_pair(q_ref, qs_ref, hb, lo)
        _init_state(m_ref, l_ref, acc_ref)

    mask = _pos_mask(qi_ref[s], ki_ref[s], t, band)
    for hb in range(nhb):
        k = k_ref[:, hb * 128:(hb + 1) * 128]
        v = v_ref[:, hb * 128:(hb + 1) * 128]
        for var in range(2):
            _online_update(_nt_dot(qs_ref[2 * hb + var], k), mask, v, m_ref, l_ref, acc_ref, 2 * hb + var)

    @pl.when(la_ref[s] == 1)
    def _():
        for hb in range(nhb):
            l0 = jnp.maximum(l_ref[2 * hb], TINY)
            l1 = jnp.maximum(l_ref[2 * hb + 1], TINY)
            o = jnp.where(lo, acc_ref[2 * hb] / l0, acc_ref[2 * hb + 1] / l1)
            lse = jnp.where(lo, m_ref[2 * hb] + jnp.log(l0), m_ref[2 * hb + 1] + jnp.log(l1))
            o_ref[:, hb * 128:(hb + 1) * 128] = o.astype(o_ref.dtype)
            lse_ref[:, hb * 128:(hb + 1) * 128] = lse


def dilated_attention_prompt(qk16, v16, *, nb, seq, dil, band, gi, t):
    mp = nb * seq
    sub = seq // dil
    nq = sub // t
    tabs = _step_tables(nq, -(-band // t))
    w = C_W
    qk = qk16.reshape(qk16.shape[0] // dil, dil * 2 * w)
    vv = v16.reshape(v16.shape[0] // dil, dil * v16.shape[1])
    vcols = v16.shape[1] // w
    row = lambda b, x: (b // dil) * nq + x

    def qmap(b, s, qi, ki, fi, la):
        return (row(b, qi[s]), (b % dil) * 2)

    def kmap(b, s, qi, ki, fi, la):
        return (row(b, ki[s]), (b % dil) * 2 + 1)

    def vmap(b, s, qi, ki, fi, la):
        return (row(b, ki[s]), (b % dil) * vcols + gi * 3 + 2)

    def omap(b, s, qi, ki, fi, la):
        return (row(b, qi[s]), b % dil)

    nst = 2 * (C_HEADS // 2)
    o, lse = pl.pallas_call(
        functools.partial(_dil_body, t=t, band=band),
        grid_spec=pltpu.PrefetchScalarGridSpec(
            num_scalar_prefetch=4,
            grid=(nb * dil, int(tabs[0].shape[0])),
            in_specs=[pl.BlockSpec((t, w), qmap), pl.BlockSpec((t, w), kmap), pl.BlockSpec((t, w), vmap)],
            out_specs=[pl.BlockSpec((t, w), omap), pl.BlockSpec((t, w), omap)],
            scratch_shapes=[pltpu.VMEM((nst, t, 128), BF16), pltpu.VMEM((nst, t, 1), F32),
                            pltpu.VMEM((nst, t, 1), F32), pltpu.VMEM((nst, t, 128), F32)],
        ),
        out_shape=[jax.ShapeDtypeStruct((mp // dil, dil * w), BF16), jax.ShapeDtypeStruct((mp // dil, dil * w), F32)],
        compiler_params=_cparams("parallel", "arbitrary"),
        name=f"dilated_attn_prompt_{dil}",
    )(*tabs, qk, qk, vv)
    return o.reshape(mp, w), lse.reshape(mp, w)


def _gqa_body(qi_ref, ki_ref, fi_ref, la_ref, q_ref, k_ref, v_ref, *rest, t, band, use_sel):
    if use_sel:
        sel_ref, o_ref, qs_ref, m_ref, l_ref, acc_ref = rest
    else:
        o_ref, qs_ref, m_ref, l_ref, acc_ref = rest
    s = pl.program_id(1)
    lo = lax.broadcasted_iota(jnp.int32, (t, 128), 1) < HEAD_DIM

    @pl.when(fi_ref[s] == 1)
    def _():
        for g in range(NSA_KV_HEADS):
            for r in range(NSA_REP):
                qs_ref[g * NSA_REP + r] = _gqa_query(q_ref, g, r, lo)
        _init_state(m_ref, l_ref, acc_ref)

    ki = ki_ref[s]
    mask = _pos_mask(qi_ref[s], ki, t, band)
    k = k_ref[...]
    v = v_ref[...]
    if use_sel:
        blk_row = lax.broadcasted_iota(jnp.int32, (128, t), 0)
        blk_col = jnp.right_shift(ki * t + lax.broadcasted_iota(jnp.int32, (128, t), 1), SEL_SHIFT)
        expand = jnp.where(blk_row == blk_col, 1.0, 0.0).astype(BF16)
    for g in range(NSA_KV_HEADS):
        mg = mask
        if use_sel:
            mg = mask & (jnp.dot(sel_ref[g], expand, preferred_element_type=F32) > 0.5)
        for r in range(NSA_REP):
            i = g * NSA_REP + r
            _online_update(_nt_dot(qs_ref[i], k), mg, v, m_ref, l_ref, acc_ref, i)

    @pl.when(la_ref[s] == 1)
    def _():
        for g in range(NSA_KV_HEADS):
            outs = [acc_ref[g * NSA_REP + r] / jnp.maximum(l_ref[g * NSA_REP + r], TINY) for r in range(NSA_REP)]
            _gqa_store(o_ref, outs, g, lo)


def nsa_branch_prompt(q16, k16, kcol, v16, vcol, sel, *, nb, seq, band, t=512, name):
    nq = seq // t
    tabs = _step_tables(nq, None if band is None else -(-band // t))
    qmap = lambda n, s, qi, ki, fi, la: (n * nq + qi[s], 0)
    kmap = lambda n, s, qi, ki, fi, la: (n * nq + ki[s], kcol)
    vmap = lambda n, s, qi, ki, fi, la: (n * nq + ki[s], vcol)
    in_specs = [pl.BlockSpec((t, NSA_Q_W), qmap), pl.BlockSpec((t, 128), kmap), pl.BlockSpec((t, 128), vmap)]
    args = [q16, k16, v16]
    if sel is not None:
        in_specs.append(pl.BlockSpec((NSA_KV_HEADS, t, 128), lambda n, s, qi, ki, fi, la: (0, n * nq + qi[s], 0)))
        args.append(sel)
    nst = NSA_HEADS
    return pl.pallas_call(
        functools.partial(_gqa_body, t=t, band=band, use_sel=sel is not None),
        grid_spec=pltpu.PrefetchScalarGridSpec(
            num_scalar_prefetch=4,
            grid=(nb, int(tabs[0].shape[0])),
            in_specs=in_specs,
            out_specs=pl.BlockSpec((t, NSA_Q_W), qmap),
            scratch_shapes=[pltpu.VMEM((nst, t, 128), BF16), pltpu.VMEM((nst, t, 1), F32),
                            pltpu.VMEM((nst, t, 1), F32), pltpu.VMEM((nst, t, 128), F32)],
        ),
        out_shape=jax.ShapeDtypeStruct((nb * seq, NSA_Q_W), BF16),
        compiler_params=_cparams("parallel", "arbitrary"),
        name=name,
    )(*tabs, *args)


def _cmp_finish_body(ab_ref, pe_ref, w1_ref, w2_ref, g_ref, o_ref, *, hid, norm):
    ab = ab_ref[...]
    pe_term = jnp.dot(pe_ref[...], w1_ref[...], preferred_element_type=F32)[0:1]
    h = ab[:, :hid] + pltpu.roll(ab[:, hid:], ab.shape[0] - 1, 0) + pe_term
    act = (h * jax.nn.sigmoid(h)).astype(BF16)
    y = jnp.dot(act, w2_ref[...], preferred_element_type=F32)
    if norm:
        y = y * lax.rsqrt(jnp.mean(y * y, axis=-1, keepdims=True) + NORM_EPS) * g_ref[...]
    o_ref[...] = y.astype(o_ref.dtype)


def compress_blocks(x_chunks, pe, w1, w2, gain):
    b, nchunk, cw = x_chunks.shape
    hid = w1.shape[1]
    w1_ab = jnp.concatenate([w1[:cw], w1[cw:]], axis=1).astype(BF16)
    ab = matmul(x_chunks.reshape(b * nchunk, cw), w1_ab, tn=2 * hid, name="compress_in")
    pe_rows = jnp.zeros((8, 2 * cw), BF16).at[0].set(pe.reshape(-1).astype(BF16))
    g = jnp.ones((1, HEAD_DIM), F32) if gain is None else gain.reshape(1, HEAD_DIM).astype(F32)
    const = lambda i: (0, 0)
    return pl.pallas_call(
        functools.partial(_cmp_finish_body, hid=hid, norm=gain is not None),
        grid=(b,),
        in_specs=[pl.BlockSpec((nchunk, 2 * hid), lambda i: (i, 0)), pl.BlockSpec((8, 2 * cw), const),
                  pl.BlockSpec((2 * cw, hid), const), pl.BlockSpec((hid, HEAD_DIM), const),
                  pl.BlockSpec((1, HEAD_DIM), const)],
        out_specs=pl.BlockSpec((None, nchunk, HEAD_DIM), lambda i: (i, 0, 0)),
        out_shape=jax.ShapeDtypeStruct((b, nchunk, HEAD_DIM), BF16),
        compiler_params=_cparams("parallel"),
        name="compress_finish",
    )(ab, pe_rows, w1.astype(BF16), w2.astype(BF16), g)


def _overlap_matrix(n_cmp, n_sel):
    c0 = np.arange(128)[:, None] * CMP_STRIDE
    s0 = np.arange(128)[None, :] * SEL_BLOCK
    ov = np.maximum(np.minimum(c0 + CMP_LEN, s0 + SEL_BLOCK) - np.maximum(c0, s0), 0) / CMP_LEN
    ov = ov * (np.arange(128)[:, None] < n_cmp) * (np.arange(128)[None, :] < n_sel)
    return jnp.asarray(ov, BF16)


def _cmp_body(q_ref, kc_ref, vc_ref, ov_ref, o_ref, sel_ref, *, t, pos0, n_cmp, n_sel):
    i = pl.program_id(1)
    lane = lax.broadcasted_iota(jnp.int32, (t, 128), 1)
    qpos = pos0 + i * t + lax.broadcasted_iota(jnp.int32, (t, 128), 0)
    lo = lane < HEAD_DIM
    vis = (lane * CMP_STRIDE + CMP_LEN - 1 <= qpos) & (lane < n_cmp)
    kc, vc, ov = kc_ref[...], vc_ref[...], ov_ref[...]
    cur = jnp.right_shift(qpos, SEL_SHIFT)
    valid = (lane <= cur) & (lane < n_sel)
    forced = (lane == 0) | (lane == cur) | (lane == cur - 1)
    for g in range(NSA_KV_HEADS):
        imp = jnp.zeros((t, 128), F32)
        outs = []
        for r in range(NSA_REP):
            sc = jnp.where(vis, _nt_dot(_gqa_query(q_ref, g, r, lo), kc), NEG_INF)
            m = jnp.max(sc, axis=-1, keepdims=True)
            e = jnp.where(vis, jnp.exp(sc - m), 0.0)
            p = (e / jnp.maximum(jnp.sum(e, axis=-1, keepdims=True), TINY)).astype(BF16)
            outs.append(jnp.dot(p, vc, preferred_element_type=F32))
            imp = imp + jnp.dot(p, ov, preferred_element_type=F32)
        _gqa_store(o_ref, outs, g, lo)
        score = jnp.where(valid, imp + jnp.where(forced, FORCE_BONUS, 0.0), NEG_INF)
        rank = jnp.zeros((t, 128), F32)
        for kk in range(n_sel):
            sk = score[:, kk:kk + 1]
            rank = rank + jnp.where((sk > score) | ((sk == score) & (lane > kk)), 1.0, 0.0)
        sel_ref[g] = jnp.where((rank < SEL_TOPK) & valid, 1.0, 0.0).astype(sel_ref.dtype)


def nsa_compressed_prompt(qn16, k_cmp, v_cmp, *, nb, seq, t=512):
    n_cmp = (seq - CMP_LEN) // CMP_STRIDE + 1
    n_sel = -(-seq // SEL_BLOCK)
    nq = seq // t
    qmap = lambda n, i: (n * nq + i, 0)
    cmap = lambda n, i: (n, 0, 0)
    return pl.pallas_call(
        functools.partial(_cmp_body, t=t, pos0=0, n_cmp=n_cmp, n_sel=n_sel),
        grid=(nb, nq),
        in_specs=[pl.BlockSpec((t, NSA_Q_W), qmap), pl.BlockSpec((None, 128, 128), cmap),
                  pl.BlockSpec((None, 128, 128), cmap), pl.BlockSpec((128, 128), lambda n, i: (0, 0))],
        out_specs=[pl.BlockSpec((t, NSA_Q_W), qmap),
                   pl.BlockSpec((NSA_KV_HEADS, t, 128), lambda n, i: (0, n * nq + i, 0))],
        out_shape=[jax.ShapeDtypeStruct((nb * seq, NSA_Q_W), BF16),
                   jax.ShapeDtypeStruct((NSA_KV_HEADS, nb * seq, 128), BF16)],
        compiler_params=_cparams("parallel", "parallel"),
        name="nsa_cmp_select_prompt",
    )(qn16, k_cmp, v_cmp, _overlap_matrix(n_cmp, n_sel))


def _gate_expand_matrices():
    lane = np.arange(128)[:, None]
    col = np.arange(NSA_Q_W)[None, :]
    return jnp.asarray(np.stack([(lane < NSA_GATE_W) & (lane % 3 == br) & (lane // 3 == col // HEAD_DIM)
                                 for br in range(3)]), BF16)


def _nsa_merge_body(gb_ref, e_ref, oc_ref, os_ref, ow_ref, o_ref):
    gates = jax.nn.sigmoid(gb_ref[...])
    hi = gates.astype(BF16)
    lo = (gates - hi.astype(F32)).astype(BF16)
    acc = jnp.zeros(o_ref.shape, F32)
    for br, b_ref in enumerate((oc_ref, os_ref, ow_ref)):
        w = jnp.dot(hi, e_ref[br], preferred_element_type=F32) + jnp.dot(lo, e_ref[br], preferred_element_type=F32)
        acc = acc + w * b_ref[...].astype(F32)
    o_ref[...] = acc.astype(o_ref.dtype)


def nsa_merge(proj32, gate_col, o_cmp, o_sel, o_win):
    m = o_cmp.shape[0]
    tm = TOKEN_TILE
    spec = pl.BlockSpec((tm, NSA_Q_W), lambda i: (i, 0))
    return pl.pallas_call(
        _nsa_merge_body,
        grid=(m // tm,),
        in_specs=[pl.BlockSpec((tm, 128), lambda i: (i, gate_col)),
                  pl.BlockSpec((3, 128, NSA_Q_W), lambda i: (0, 0, 0)), spec, spec, spec],
        out_specs=spec,
        out_shape=jax.ShapeDtypeStruct((m, NSA_Q_W), BF16),
        compiler_params=_cparams("parallel"),
        name="nsa_merge",
    )(proj32, _gate_expand_matrices(), o_cmp, o_sel, o_win)


def _dil_merge_body(o0, o1, o2, l0, l1, l2, o_ref):
    ls = [l0[...], l1[...], l2[...]]
    m = jnp.maximum(jnp.maximum(ls[0], ls[1]), ls[2])
    es = [jnp.exp(x - m) for x in ls]
    den = es[0] + es[1] + es[2]
    acc = sum((e / den) * o[...].astype(F32) for e, o in zip(es, (o0, o1, o2)))
    o_ref[...] = acc.astype(o_ref.dtype)


def dilation_merge(outs, lses):
    m, w = outs[0].shape
    tm = TOKEN_TILE
    spec = pl.BlockSpec((tm, w), lambda i: (i, 0))
    return pl.pallas_call(
        _dil_merge_body,
        grid=(m // tm,),
        in_specs=[spec] * 6,
        out_specs=spec,
        out_shape=jax.ShapeDtypeStruct((m, w), BF16),
        compiler_params=_cparams("parallel"),
        name="dilation_merge",
    )(*outs, *lses)


def _rms_norm(x, g):
    xf = x.astype(F32)
    y = xf * lax.rsqrt(jnp.mean(xf * xf, axis=-1, keepdims=True) + NORM_EPS)
    return (y * g.astype(F32)).astype(x.dtype)


def _partial_rope(x, pos):
    half = ROT_DIM // 2
    inv_freq = ROPE_THETA ** (-jnp.arange(half, dtype=F32) / half)
    ang = pos.astype(F32)[:, None] * inv_freq[None, :]
    shape = (1, pos.shape[0]) + (1,) * (x.ndim - 3) + (half,)
    cos = jnp.cos(ang).reshape(shape)
    sin = jnp.sin(ang).reshape(shape)
    xf = x.astype(F32)
    x1, x2 = xf[..., :half], xf[..., half:ROT_DIM]
    out = jnp.concatenate([x1 * cos - x2 * sin, x2 * cos + x1 * sin, xf[..., ROT_DIM:]], axis=-1)
    return out.astype(x.dtype)


def _masked_softmax(s, mask):
    s = jnp.where(mask, s.astype(F32), NEG_INF)
    m = jnp.max(s, axis=-1, keepdims=True)
    e = jnp.where(mask, jnp.exp(s - m), 0.0)
    l = jnp.maximum(jnp.sum(e, axis=-1, keepdims=True), TINY)
    return e / l, (m + jnp.log(l))[..., 0]


def _paged_rows(cache, page_table):
    g = cache[page_table]
    return g.reshape((g.shape[0], g.shape[1] * g.shape[2]) + g.shape[3:])


def _roll_buffer(buf, new):
    n_buf, t = buf.shape[1], new.shape[1]
    if t >= n_buf:
        return new[:, t - n_buf:]
    return jnp.concatenate([buf[:, t:], new], axis=1)


def _gather_rows(buf, new, idx):
    n_buf = buf.shape[1]
    from_buf = buf[:, np.clip(idx, 0, n_buf - 1)]
    from_new = new[:, np.clip(idx - n_buf, 0, new.shape[1] - 1)]
    sel = (idx < n_buf).reshape(idx.shape + (1,) * (buf.ndim - 2))
    return jnp.where(sel, from_buf, from_new)


def _banded_attn(q, k, v, band):
    n, L, g, r, dh = q.shape
    blk = math.gcd(L, Q_BLOCK)
    nb = L // blk
    pad = ((0, 0), (band, 0), (0, 0), (0, 0))
    idx = np.arange(nb)[:, None] * blk + np.arange(blk + band)[None, :]
    kb = jnp.pad(k, pad)[:, idx]
    vb = jnp.pad(v, pad)[:, idx]
    qb = q.reshape(n, nb, blk, g, r, dh)
    s = jnp.einsum('nbqgrd,nbkgd->nbgrqk', qb, kb, preferred_element_type=F32) * SCALE
    qpos = np.arange(nb)[:, None] * blk + np.arange(blk)[None, :]
    kpos = idx - band
    dist = qpos[:, :, None] - kpos[:, None, :]
    mask = (dist >= 0) & (dist <= band) & (kpos[:, None, :] >= 0)
    p, lse = _masked_softmax(s, mask[None, :, None, None])
    o = jnp.einsum('nbgrqk,nbkgd->nbqgrd', p, vb.astype(F32))
    return o.reshape(n, L, g, r, dh).astype(q.dtype), lse.transpose(0, 1, 4, 2, 3).reshape(n, L, g, r)


def _diff_heads(qa, ka, va, pos, g_q, g_k):
    n, t = qa.shape[:2]
    q = _partial_rope(_rms_norm(qa.reshape(n, t, A_HEADS, 2, HEAD_DIM), g_q), pos)
    k = _partial_rope(_rms_norm(ka.reshape(n, t, A_HEADS, 2, HEAD_DIM), g_k), pos)
    return q, k, va.reshape(n, t, A_HEADS, A_VDIM)


def _diff_core(q, k, v, qpos, kpos, lam):
    s = jnp.einsum('nqhmd,nkhmd->nhmqk', q, k, preferred_element_type=F32) * SCALE
    p, _ = _masked_softmax(s, (kpos[None, :] <= qpos[:, None])[None, None, None])
    a = p[:, :, 0] - lam * p[:, :, 1]
    return jnp.einsum('nhqk,nkhe->nqhe', a, v.astype(F32)).astype(v.dtype)


def _diff_attn_prompt(q, k, v, pos, lam):
    n, s = q.shape[:2]
    nb = s // Q_BLOCK
    qb = q.reshape((n, nb, Q_BLOCK) + q.shape[2:]).swapaxes(0, 1)
    ob = lax.map(lambda a: _diff_core(a[0], k, v, a[1], pos, lam), (qb, pos.reshape(nb, Q_BLOCK)))
    return ob.swapaxes(0, 1).reshape(n, s, A_HEADS, A_VDIM)


def _diff_output(o, g_sub, lam_init):
    n, t = o.shape[:2]
    return (_rms_norm(o, g_sub) * (1.0 - lam_init)).reshape(n, t, A_V_W)


def _nsa_heads(qb, kvb, gb, pos, g_q, g_k):
    n, t = qb.shape[:2]
    q = _rms_norm(qb.reshape(n, t, NSA_KV_HEADS, NSA_REP, HEAD_DIM), g_q)
    q_rot = _partial_rope(q, pos)
    kv = kvb.reshape(n, t, 6, NSA_KV_HEADS, HEAD_DIM)
    k_slc = _partial_rope(_rms_norm(kv[:, :, 2], g_k[1]), pos)
    k_win = _partial_rope(_rms_norm(kv[:, :, 4], g_k[2]), pos)
    long_rows = jnp.stack([kv[:, :, 0], kv[:, :, 1], k_slc, kv[:, :, 3]], axis=2)
    win_rows = jnp.stack([k_win, kv[:, :, 5]], axis=2)
    gates = jax.nn.sigmoid(gb.astype(F32)).reshape(n, t, NSA_KV_HEADS, NSA_REP, 3)
    return q, q_rot, long_rows, win_rows, gates


def _nsa_compress(rows, pe, w1, w2):
    n, L, g, dh = rows.shape
    n_cmp = (L - CMP_LEN) // CMP_STRIDE + 1
    idx = np.arange(n_cmp)[:, None] * CMP_STRIDE + np.arange(CMP_LEN)[None, :]
    blocks = rows[:, idx] + pe[None, None, :, None, :]
    flat = blocks.transpose(0, 1, 3, 2, 4).reshape(n, n_cmp, g, CMP_LEN * dh)
    return jax.nn.silu(flat @ w1) @ w2


def _nsa_cmp_attn(q, k_cmp, v_cmp, qpos):
    n_cmp = k_cmp.shape[1]
    end = jnp.asarray(np.arange(n_cmp) * CMP_STRIDE + CMP_LEN - 1)
    s = jnp.einsum('nqgrd,ncgd->nqgrc', q, k_cmp, preferred_element_type=F32) * SCALE
    visible = end[None, :] <= qpos[:, None]
    p, _ = _masked_softmax(s, visible[None, :, None, None, :])
    o = jnp.einsum('nqgrc,ncgd->nqgrd', p, v_cmp.astype(F32)).astype(q.dtype)
    return o, p


def _cmp_to_sel_overlap(n_cmp, n_sel):
    c0 = np.arange(n_cmp)[:, None] * CMP_STRIDE
    s0 = np.arange(n_sel)[None, :] * SEL_BLOCK
    ov = np.minimum(c0 + CMP_LEN, s0 + SEL_BLOCK) - np.maximum(c0, s0)
    return jnp.asarray(np.maximum(ov, 0) / CMP_LEN, dtype=F32)


def _nsa_select(p_cmp, qpos, n_sel):
    imp = jnp.einsum('nqgrc,cj->nqgj', p_cmp, _cmp_to_sel_overlap(p_cmp.shape[-1], n_sel))
    blk = jnp.arange(n_sel)[None, :]
    cur = (qpos // SEL_BLOCK)[:, None]
    valid = blk <= cur
    forced = (blk == 0) | (blk == cur) | (blk == cur - 1)
    score = jnp.where(valid[None, :, None], imp + jnp.where(forced, FORCE_BONUS, 0.0)[None, :, None], NEG_INF)
    _, sel = lax.top_k(score, min(SEL_TOPK, n_sel))
    return sel


def _nsa_sel_attn(q, k_blk, v_blk, sel, qpos):
    n, qc, g, r, dh = q.shape
    kk = sel.shape[-1]
    n_i = jnp.arange(n)[:, None, None, None]
    g_i = jnp.arange(g)[None, None, :, None]
    kg = k_blk[n_i, g_i, sel]
    vg = v_blk[n_i, g_i, sel]
    kpos = sel[..., None] * SEL_BLOCK + jnp.arange(SEL_BLOCK)
    visible = (kpos <= qpos[None, :, None, None, None]).reshape(n, qc, g, 1, kk * SEL_BLOCK)
    s = jnp.einsum('nqgrd,nqgkbd->nqgrkb', q, kg, preferred_element_type=F32)
    p, _ = _masked_softmax(s.reshape(n, qc, g, r, kk * SEL_BLOCK) * SCALE, visible)
    o = jnp.einsum('nqgrx,nqgxd->nqgrd', p, vg.reshape(n, qc, g, kk * SEL_BLOCK, dh).astype(F32))
    return o.astype(q.dtype)


def _nsa_long_branches(q, q_rot, long_all, qpos, g_kc, pe_k, w_k1, w_k2, pe_v, w_v1, w_v2):
    n, L, _, g, dh = long_all.shape
    k_cmp = _rms_norm(_nsa_compress(long_all[:, :, 0], pe_k, w_k1, w_k2), g_kc)
    v_cmp = _nsa_compress(long_all[:, :, 1], pe_v, w_v1, w_v2)
    o_cmp, p_cmp = _nsa_cmp_attn(q, k_cmp, v_cmp, qpos)
    n_sel = -(-L // SEL_BLOCK)
    sel = _nsa_select(p_cmp, qpos, n_sel)

    def to_blocks(x):
        x = jnp.pad(x, ((0, 0), (0, n_sel * SEL_BLOCK - L), (0, 0), (0, 0)))
        return x.reshape(n, n_sel, SEL_BLOCK, g, dh).transpose(0, 3, 1, 2, 4)

    k_blk, v_blk = to_blocks(long_all[:, :, 2]), to_blocks(long_all[:, :, 3])
    nq = q.shape[1]
    qc = math.gcd(nq, SEL_Q_BLOCK)
    nc = nq // qc

    def chunks(x):
        return x.reshape((n, nc, qc) + x.shape[2:]).swapaxes(0, 1)

    o_sel = lax.map(lambda a: _nsa_sel_attn(a[0], k_blk, v_blk, a[1], a[2]),
                    (chunks(q_rot), chunks(sel), qpos.reshape(nc, qc)))
    return o_cmp, o_sel.swapaxes(0, 1).reshape(q.shape)


def _window_attn_sample(q, k_all, v_all, n_buf, window):
    t = q.shape[1]
    dist = (n_buf + np.arange(t))[:, None] - np.arange(n_buf + t)[None, :]
    visible = (dist >= 0) & (dist <= window)
    s = jnp.einsum('ntgrd,nkgd->ntgrk', q, k_all, preferred_element_type=F32) * SCALE
    p, _ = _masked_softmax(s, visible[None, :, None, None, :])
    return jnp.einsum('ntgrk,nkgd->ntgrd', p, v_all.astype(F32)).astype(q.dtype)


def _nsa_merge(gates, o_cmp, o_sel, o_win):
    o = gates[..., 0:1] * o_cmp.astype(F32) + gates[..., 1:2] * o_sel.astype(F32) + gates[..., 2:3] * o_win.astype(F32)
    n, t = o.shape[:2]
    return o.reshape(n, t, NSA_Q_W).astype(o_cmp.dtype)


def _split_in0(proj):
    sizes = [A_QK_W, A_QK_W, A_V_W, NSA_Q_W, NSA_KV_W, NSA_GATE_W]
    return jnp.split(proj, [int(o) for o in np.cumsum(sizes)[:-1]], axis=-1)


def _even_mixer_prompt(proj, pos, mw):
    g_qa, g_ka, lam, lam_init, g_subln, g_qb, g_kb, cmp_w = mw
    n, s = proj.shape[:2]
    qa, ka, va, qb, kvb, gb = _split_in0(proj)
    q, k, v = _diff_heads(qa, ka, va, pos, g_qa, g_ka)
    o_a = _diff_output(_diff_attn_prompt(q, k, v, pos, lam), g_subln, lam_init)
    qn, qr, long_rows, win_rows, gates = _nsa_heads(qb, kvb, gb, pos, g_qb, g_kb)
    o_cmp, o_sel = _nsa_long_branches(qn, qr, long_rows, pos, g_kb[0], *cmp_w)
    o_win, _ = _banded_attn(qr, win_rows[:, :, 0], win_rows[:, :, 1], NSA_WINDOW)
    o_b = _nsa_merge(gates, o_cmp, o_sel, o_win)
    a_rows = jnp.stack([k.reshape(n, s, A_HEADS, A_VDIM), v], axis=2)
    return jnp.concatenate([o_a, o_b], axis=-1), a_rows, long_rows, win_rows[:, s - min(NSA_WINDOW, s):]


def _even_mixer_sample(proj, pos, cache_a_kv, cache_nsa_kv, state_nsa_win, page_table, mw):
    g_qa, g_ka, lam, lam_init, g_subln, g_qb, g_kb, cmp_w = mw
    n, t = proj.shape[:2]
    qa, ka, va, qb, kvb, gb = _split_in0(proj)
    q, k, v = _diff_heads(qa, ka, va, pos, g_qa, g_ka)
    a_rows = jnp.stack([k.reshape(n, t, A_HEADS, A_VDIM), v], axis=2)
    a_all = jnp.concatenate([_paged_rows(cache_a_kv, page_table), a_rows], axis=1)
    L = a_all.shape[1]
    o = _diff_core(q, a_all[:, :, 0].reshape(n, L, A_HEADS, 2, HEAD_DIM), a_all[:, :, 1], pos,
                   jnp.arange(L, dtype=jnp.int32), lam)
    o_a = _diff_output(o, g_subln, lam_init)
    qn, qr, long_rows, win_rows, gates = _nsa_heads(qb, kvb, gb, pos, g_qb, g_kb)
    long_all = jnp.concatenate([_paged_rows(cache_nsa_kv, page_table), long_rows], axis=1)
    o_cmp, o_sel = _nsa_long_branches(qn, qr, long_all, pos, g_kb[0], *cmp_w)
    n_buf = state_nsa_win.shape[1]
    win_all = jnp.concatenate([state_nsa_win, win_rows], axis=1)
    o_win = _window_attn_sample(qr, win_all[:, :, 0], win_all[:, :, 1], n_buf, NSA_WINDOW)
    o_b = _nsa_merge(gates, o_cmp, o_sel, o_win)
    return jnp.concatenate([o_a, o_b], axis=-1), a_rows, long_rows, _roll_buffer(state_nsa_win, win_rows)


def _dilated_heads(proj, pos, g_qc, g_kc):
    n, t = proj.shape[:2]
    proj = proj.reshape(n, t, N_C_GROUPS, 3, C_HEADS, HEAD_DIM)
    return [(_partial_rope(_rms_norm(proj[:, :, gi, 0], g_qc[gi]), pos),
             _partial_rope(_rms_norm(proj[:, :, gi, 1], g_kc[gi]), pos),
             proj[:, :, gi, 2]) for gi in range(N_C_GROUPS)]


def _dilated_attn_prompt(q, k, v, dil, band):
    n, S, h, dh = q.shape
    L = S // dil

    def sub(x):
        return x.reshape(n, L, dil, h, dh).transpose(0, 2, 1, 3, 4).reshape(n * dil, L, h, dh)

    o, lse = _banded_attn(sub(q)[:, :, :, None], sub(k), sub(v), band)
    o = o.reshape(n, dil, L, h, dh).transpose(0, 2, 1, 3, 4).reshape(n, S, h, dh)
    lse = lse.reshape(n, dil, L, h).transpose(0, 2, 1, 3).reshape(n, S, h)
    return o, lse


def _dilated_attn_sample(q, buf, new_rows, dil, window):
    n_buf, t = buf.shape[1], q.shape[1]
    n_keys = window // dil + 1
    idx = n_buf + np.arange(t)[:, None] - dil * np.arange(n_keys)[None, :]
    rows = _gather_rows(buf, new_rows, idx)
    s = jnp.einsum('nthd,ntkhd->nthk', q, rows[:, :, :, 0], preferred_element_type=F32) * SCALE
    p, lse = _masked_softmax(s, (idx >= 0)[None, :, None, :])
    o = jnp.einsum('nthk,ntkhd->nthd', p, rows[:, :, :, 1].astype(F32))
    return o.astype(q.dtype), lse


def _merge_dilations(outs, lses):
    w = jax.nn.softmax(jnp.stack(lses, axis=0), axis=0)
    o = jnp.einsum('gnth,gnthd->nthd', w, jnp.stack(outs, axis=0).astype(F32))
    return o.astype(outs[0].dtype)


def _odd_mixer_prompt(proj, pos, g_qc, g_kc):
    n, s = proj.shape[:2]
    outs, lses, bufs = [], [], []
    for (window, dil), (q, k, v) in zip(C_GROUPS, _dilated_heads(proj, pos, g_qc, g_kc)):
        o, lse = _dilated_attn_prompt(q, k, v, dil, window // dil)
        outs.append(o)
        lses.append(lse)
        bufs.append(jnp.stack([k, v], axis=2)[:, s - min(window, s):])
    return _merge_dilations(outs, lses).reshape(n, s, C_W), bufs


def _odd_mixer_sample(proj, pos, states, g_qc, g_kc):
    n, t = proj.shape[:2]
    outs, lses, bufs = [], [], []
    for (window, dil), (q, k, v), buf in zip(C_GROUPS, _dilated_heads(proj, pos, g_qc, g_kc), states):
        new_rows = jnp.stack([k, v], axis=2)
        o, lse = _dilated_attn_sample(q, buf, new_rows, dil, window)
        outs.append(o)
        lses.append(lse)
        bufs.append(_roll_buffer(buf, new_rows))
    return _merge_dilations(outs, lses).reshape(n, t, C_W), bufs


def _moe(h_bf16, resid, w_router, wg, wu, wd):
    m, d = h_bf16.shape
    tm = TOKEN_TILE
    logits = jnp.dot(h_bf16, w_router.astype(BF16), preferred_element_type=F32)
    top_v, top_i = lax.top_k(logits, TOP_K)
    gate = jax.nn.softmax(top_v, axis=-1)
    flat_e = top_i.reshape(-1)
    order = jnp.argsort(flat_e, stable=True)
    counts = jnp.bincount(flat_e, length=N_EXPERTS)
    padded = ((counts + tm - 1) // tm) * tm
    pstart = jnp.cumsum(padded) - padded
    cstart = jnp.cumsum(counts) - counts
    sorted_e = flat_e[order]
    rank = jnp.arange(m * TOP_K) - cstart[sorted_e]
    dest = pstart[sorted_e] + rank
    p_rows = m * TOP_K + N_EXPERTS * tm
    src_tok = jnp.zeros((p_rows,), jnp.int32).at[dest].set((order // TOP_K).astype(jnp.int32))
    valid = jnp.zeros((p_rows,), jnp.bool_).at[dest].set(True)
    xs = jnp.where(valid[:, None], h_bf16[src_tok], jnp.zeros((), BF16))
    tile_start = jnp.arange(p_rows // tm) * tm
    pend = jnp.cumsum(padded)
    tile_expert = jnp.minimum(jnp.sum(tile_start[:, None] >= pend[None, :], axis=1), N_EXPERTS - 1).astype(jnp.int32)
    ys = moe_grouped_ffn(xs, tile_expert, wg, wu, wd, tf=1792)
    slot_pos = jnp.zeros((m * TOP_K,), jnp.int32).at[order].set(dest.astype(jnp.int32)).reshape(m, TOP_K)
    y = resid
    for kk in range(TOP_K):
        y = y + gate[:, kk:kk + 1] * ys[slot_pos[:, kk]]
    return y


COL_QA, COL_KA, COL_VA, COL_QB = 0, A_QK_W, 2 * A_QK_W, 2 * A_QK_W + A_V_W
COL_KVB = COL_QB + NSA_Q_W
COL_GATE = COL_KVB + NSA_KV_W
KVB_PAIR = NSA_KV_HEADS * HEAD_DIM


def even_mixer_prompt(proj, proj16, tables, nb, seq, mw):
    g_qa, g_ka, lam_rows, lam_init, g_subln, g_qb, g_kb, cmp_w = mw
    pe_k, w_k1, w_k2, pe_v, w_v1, w_v2 = cmp_w
    mp = nb * seq
    rope16_32 = ((True, BF16), (True, F32))
    qk_a16, qk_a32 = head_norm_rope(proj, jnp.stack([_head_gain(g_qa, A_QK_W), _head_gain(g_ka, A_QK_W)]), tables,
                                    width=A_QK_W, col0=0, outs=rope16_32, name="hnr_diff_qk")
    qn16, qr16 = head_norm_rope(proj, _head_gain(g_qb, NSA_Q_W)[None], tables, width=NSA_Q_W,
                                col0=COL_QB // NSA_Q_W, outs=((False, BF16), (True, BF16)), name="hnr_nsa_q")
    ks16, ks32 = head_norm_rope(proj, _head_gain(g_kb[1], KVB_PAIR)[None], tables, width=KVB_PAIR,
                                col0=(COL_KVB + 2 * KVB_PAIR) // KVB_PAIR, outs=rope16_32, name="hnr_nsa_kslc")
    kw16, kw32 = head_norm_rope(proj, _head_gain(g_kb[2], KVB_PAIR)[None], tables, width=KVB_PAIR,
                                col0=(COL_KVB + 4 * KVB_PAIR) // KVB_PAIR, outs=rope16_32, name="hnr_nsa_kwin")

    o_a = diff_attention_prompt(qk_a16, proj16, lam_rows, g_subln, nb=nb, seq=seq, lam_init=lam_init)

    nchunk = seq // CMP_STRIDE

    def chunks(col):
        xc = proj16[:mp, col:col + KVB_PAIR].reshape(nb, nchunk, CMP_STRIDE, NSA_KV_HEADS, HEAD_DIM)
        return xc.transpose(0, 3, 1, 2, 4).reshape(nb * NSA_KV_HEADS, nchunk, CMP_STRIDE * HEAD_DIM)

    def pair_lanes(c):
        return c.reshape(nb, NSA_KV_HEADS, nchunk, HEAD_DIM).transpose(0, 2, 1, 3).reshape(nb, nchunk, KVB_PAIR)

    k_cmp = pair_lanes(compress_blocks(chunks(COL_KVB), pe_k, w_k1, w_k2, g_kb[0]))
    v_cmp = pair_lanes(compress_blocks(chunks(COL_KVB + KVB_PAIR), pe_v, w_v1, w_v2, None))
    o_cmp, sel = nsa_compressed_prompt(qn16, k_cmp, v_cmp, nb=nb, seq=seq)
    o_sel = nsa_branch_prompt(qr16, ks16, 0, proj16, (COL_KVB + 3 * KVB_PAIR) // KVB_PAIR, sel,
                              nb=nb, seq=seq, band=None, name="nsa_sel_prompt")
    o_win = nsa_branch_prompt(qr16, kw16, 0, proj16, (COL_KVB + 5 * KVB_PAIR) // KVB_PAIR, None,
                              nb=nb, seq=seq, band=NSA_WINDOW, name="nsa_win_prompt")
    o_b = nsa_merge(proj, COL_GATE // 128, o_cmp, o_sel, o_win)
    cat = jnp.concatenate([o_a, o_b], axis=1)

    a_rows = jnp.concatenate([qk_a32[:mp, A_QK_W:], proj[:mp, COL_VA:COL_VA + A_V_W]], axis=1)
    long_rows = jnp.concatenate([proj[:mp, COL_KVB:COL_KVB + 2 * KVB_PAIR], ks32[:mp],
                                 proj[:mp, COL_KVB + 3 * KVB_PAIR:COL_KVB + 4 * KVB_PAIR]], axis=1)
    win_rows = jnp.concatenate([kw32[:mp], proj[:mp, COL_KVB + 5 * KVB_PAIR:COL_KVB + 6 * KVB_PAIR]], axis=1)
    keep = min(NSA_WINDOW, seq)
    return (cat, a_rows.reshape(nb, seq, 2, A_HEADS, A_VDIM), long_rows.reshape(nb, seq, 4, NSA_KV_HEADS, HEAD_DIM),
            win_rows.reshape(nb, seq, 2, NSA_KV_HEADS, HEAD_DIM)[:, seq - keep:])


def odd_mixer_prompt(proj, proj16, tables, nb, seq, g_qc, g_kc):
    mp = nb * seq
    outs, lses, bufs = [], [], []
    for gi, (window, dil) in enumerate(C_GROUPS):
        gains = jnp.stack([_head_gain(g_qc[gi], C_W), _head_gain(g_kc[gi], C_W)])
        qk16, qk32 = head_norm_rope(proj, gains, tables, width=C_W, col0=3 * gi,
                                    outs=((True, BF16), (True, F32)), name=f"hnr_dil_{dil}")
        sub = seq // dil
        o, lse = dilated_attention_prompt(qk16, proj16, nb=nb, seq=seq, dil=dil, band=window // dil, gi=gi,
                                          t=min(sub, 256 if dil == 1 else 128))
        outs.append(o)
        lses.append(lse)
        kv = jnp.concatenate([qk32[:mp, C_W:], proj[:mp, (3 * gi + 2) * C_W:(3 * gi + 3) * C_W]], axis=1)
        bufs.append(kv.reshape(nb, seq, 2, C_HEADS, HEAD_DIM)[:, seq - min(window, seq):])
    return dilation_merge(outs, lses), bufs


def kernel(x_prompt, x_sample, cache_a_kv, cache_nsa_kv, state_nsa_win, state_c_w128, state_c_w512, state_c_w2048, page_table, norm0_mix, w_in0, g_qa, g_ka, lam_q1, lam_k1, lam_q2, lam_k2, g_subln, g_qb, g_kb, pe_cmp_k, w_cmp_k1, w_cmp_k2, pe_cmp_v, w_cmp_v1, w_cmp_v2, w_out0, norm0_ffn, w_ffn_gate, w_ffn_up, w_ffn_down, norm1_mix, w_in1, g_qc, g_kc, w_out1, norm1_ffn, w_router, w_moe_gate, w_moe_up, w_moe_down):
    nb, seq, d = x_prompt.shape
    db, dt, _ = x_sample.shape
    past = page_table.shape[1] * cache_a_kv.shape[1]
    mp = nb * seq
    ms = db * dt
    pos_p = jnp.arange(seq, dtype=jnp.int32)
    pos_s = past + jnp.arange(dt, dtype=jnp.int32)
    x = jnp.concatenate([x_prompt.reshape(mp, d), x_sample.reshape(ms, d)], axis=0)

    in0_w = w_in0.shape[1]
    in0_pad = -(-in0_w // 128) * 128
    w_in0_b = jnp.pad(w_in0, ((0, 0), (0, in0_pad - in0_w))).astype(BF16)
    tables = rope_tables(jnp.concatenate([jnp.tile(pos_p, nb), jnp.tile(pos_s, db)]))
    proj0, proj0_16 = matmul_dual(rmsnorm_cast(x, norm0_mix), w_in0_b, tn=in0_pad, name="in_proj0")
    lam_init = 0.8 - 0.6 * math.exp(-0.3 * 0)
    f = lambda a: a.astype(F32)
    lam = jnp.exp(jnp.sum(f(lam_q1) * f(lam_k1))) - jnp.exp(jnp.sum(f(lam_q2) * f(lam_k2))) + lam_init
    cmp_w = (pe_cmp_k, w_cmp_k1, w_cmp_k2, pe_cmp_v, w_cmp_v1, w_cmp_v2)
    mw = (g_qa, g_ka, lam, lam_init, g_subln, g_qb, g_kb, cmp_w)
    lam_rows = jnp.zeros((8, 128), F32).at[:4, :HEAD_DIM].set(jnp.stack([f(lam_q1), f(lam_k1), f(lam_q2), f(lam_k2)]))
    cat_p, a_kv_p, nsa_kv_p, nsa_win_p = even_mixer_prompt(
        proj0, proj0_16, tables, nb, seq, (g_qa, g_ka, lam_rows, lam_init, g_subln, g_qb, g_kb, cmp_w))
    cat_s, a_kv_s, nsa_kv_s, nsa_win_s = _even_mixer_sample(
        proj0[mp:, :in0_w].reshape(db, dt, in0_w), pos_s, cache_a_kv, cache_nsa_kv, state_nsa_win, page_table, mw)
    cat = jnp.concatenate([cat_p, cat_s.reshape(ms, -1).astype(BF16)], axis=0)
    x = matmul(cat, w_out0.astype(BF16), tn=d, res=x, name="out_proj0")
    act = swiglu_gate_up(rmsnorm_cast(x, norm0_ffn), w_ffn_gate.astype(BF16), w_ffn_up.astype(BF16), tn=1408)
    x = matmul(act, w_ffn_down.astype(BF16), tn=d, res=x, name="ffn_down")

    in1_w = w_in1.shape[1]
    proj1, proj1_16 = matmul_dual(rmsnorm_cast(x, norm1_mix), w_in1.astype(BF16), tn=2304, name="in_proj1")
    y_p, c_p = odd_mixer_prompt(proj1, proj1_16, tables, nb, seq, g_qc, g_kc)
    y_s, c_s = _odd_mixer_sample(proj1[mp:].reshape(db, dt, in1_w), pos_s,
                                 (state_c_w128, state_c_w512, state_c_w2048), g_qc, g_kc)
    mix = jnp.concatenate([y_p, y_s.reshape(ms, -1).astype(BF16)], axis=0)
    x = matmul(mix, w_out1.astype(BF16), tn=d, res=x, name="out_proj1")
    x = _moe(rmsnorm_cast(x, norm1_ffn), x, w_router, w_moe_gate.astype(BF16), w_moe_up.astype(BF16),
             w_moe_down.astype(BF16))

    hp = x[:mp].reshape(nb, seq, d)
    hs = x[mp:].reshape(db, dt, d)
    return (hp, hs, a_kv_p, a_kv_s, nsa_kv_p, nsa_kv_s, nsa_win_p, nsa_win_s,
            c_p[0], c_s[0], c_p[1], c_s[1], c_p[2], c_s[2])
```

```python
import functools
import math

import jax
import jax.numpy as jnp
import numpy as np
from jax import lax
from jax.experimental import pallas as pl
from jax.experimental.pallas import tpu as pltpu

F32 = jnp.float32
BF16 = jnp.bfloat16

D_MODEL = 1024
HEAD_DIM = 64
ROT_DIM = HEAD_DIM // 4
ROPE_THETA = 500000.0
NORM_EPS = 1e-6
SCALE = HEAD_DIM ** -0.5
Q_BLOCK = 128
NEG_INF = -1e30
TINY = 1e-30
A_HEADS = 4
A_VDIM = 2 * HEAD_DIM
NSA_HEADS = 8
NSA_KV_HEADS = 2
NSA_REP = NSA_HEADS // NSA_KV_HEADS
CMP_LEN = 32
CMP_STRIDE = 16
SEL_BLOCK = 64
SEL_SHIFT = 6
SEL_TOPK = 16
SEL_Q_BLOCK = 64
NSA_WINDOW = 512
FORCE_BONUS = 1e3
C_HEADS = 16
C_GROUPS = ((128, 1), (512, 4), (2048, 16))
N_C_GROUPS = len(C_GROUPS)
A_QK_W = A_HEADS * 2 * HEAD_DIM
A_V_W = A_HEADS * A_VDIM
NSA_Q_W = NSA_HEADS * HEAD_DIM
NSA_KV_W = 6 * NSA_KV_HEADS * HEAD_DIM
NSA_GATE_W = 3 * NSA_HEADS
C_W = C_HEADS * HEAD_DIM
N_EXPERTS = 8
TOP_K = 2

VMEM_LIMIT_BYTES = 56 * 1024 * 1024
TOKEN_TILE = 512


def _cparams(*sem):
    return pltpu.CompilerParams(dimension_semantics=sem, vmem_limit_bytes=VMEM_LIMIT_BYTES)


def _rmsnorm_body(x_ref, g_ref, o_ref):
    x = x_ref[...]
    ms = jnp.mean(x * x, axis=-1, keepdims=True)
    o_ref[...] = (x * lax.rsqrt(ms + NORM_EPS) * g_ref[...]).astype(o_ref.dtype)


def rmsnorm_cast(x, g):
    m, d = x.shape
    tm = TOKEN_TILE
    return pl.pallas_call(
        _rmsnorm_body,
        grid=(m // tm,),
        in_specs=[pl.BlockSpec((tm, d), lambda i: (i, 0)), pl.BlockSpec((1, d), lambda i: (0, 0))],
        out_specs=pl.BlockSpec((tm, d), lambda i: (i, 0)),
        out_shape=jax.ShapeDtypeStruct((m, d), BF16),
        compiler_params=_cparams("parallel"),
        name="rmsnorm",
    )(x, g.reshape(1, d))


def _mm_body(x_ref, w_ref, o_ref):
    o_ref[...] = jnp.dot(x_ref[...], w_ref[...], preferred_element_type=F32).astype(o_ref.dtype)


def _mm_res_body(x_ref, w_ref, r_ref, o_ref):
    acc = jnp.dot(x_ref[...], w_ref[...], preferred_element_type=F32)
    o_ref[...] = (acc + r_ref[...]).astype(o_ref.dtype)


def matmul(x, w, *, tn, res=None, out_dtype=F32, name="matmul"):
    m, k = x.shape
    n = w.shape[1]
    tm = min(TOKEN_TILE, m)
    assert m % tm == 0 and n % tn == 0
    in_specs = [pl.BlockSpec((tm, k), lambda j, i: (i, 0)), pl.BlockSpec((k, tn), lambda j, i: (0, j))]
    args = [x, w]
    body = _mm_body
    if res is not None:
        in_specs.append(pl.BlockSpec((tm, tn), lambda j, i: (i, j)))
        args.append(res)
        body = _mm_res_body
    return pl.pallas_call(
        body,
        grid=(n // tn, m // tm),
        in_specs=in_specs,
        out_specs=pl.BlockSpec((tm, tn), lambda j, i: (i, j)),
        out_shape=jax.ShapeDtypeStruct((m, n), out_dtype),
        compiler_params=_cparams("parallel", "parallel"),
        name=name,
    )(*args)


def _gate_up_body(x_ref, wg_ref, wu_ref, o_ref):
    x = x_ref[...]
    g = jnp.dot(x, wg_ref[...], preferred_element_type=F32)
    u = jnp.dot(x, wu_ref[...], preferred_element_type=F32)
    o_ref[...] = (g * jax.nn.sigmoid(g) * u).astype(o_ref.dtype)


def swiglu_gate_up(x, wg, wu, *, tn):
    m, k = x.shape
    n = wg.shape[1]
    tm = TOKEN_TILE
    return pl.pallas_call(
        _gate_up_body,
        grid=(n // tn, m // tm),
        in_specs=[pl.BlockSpec((tm, k), lambda j, i: (i, 0)),
                  pl.BlockSpec((k, tn), lambda j, i: (0, j)),
                  pl.BlockSpec((k, tn), lambda j, i: (0, j))],
        out_specs=pl.BlockSpec((tm, tn), lambda j, i: (i, j)),
        out_shape=jax.ShapeDtypeStruct((m, n), BF16),
        compiler_params=_cparams("parallel", "parallel"),
        name="swiglu_gate_up",
    )(x, wg, wu)


def _moe_gate_up_body(te_ref, x_ref, wg_ref, wu_ref, o_ref):
    del te_ref
    x = x_ref[...]
    g = jnp.dot(x, wg_ref[...], preferred_element_type=F32)
    u = jnp.dot(x, wu_ref[...], preferred_element_type=F32)
    o_ref[...] = (g * jax.nn.sigmoid(g) * u).astype(o_ref.dtype)


def _moe_down_body(te_ref, a_ref, wd_ref, o_ref):
    del te_ref
    o_ref[...] = jnp.dot(a_ref[...], wd_ref[...], preferred_element_type=F32)


def moe_grouped_ffn(xs, tile_expert, wg, wu, wd, *, tf):
    p, d = xs.shape
    f = wg.shape[2]
    tm = TOKEN_TILE
    nt = p // tm
    act = pl.pallas_call(
        _moe_gate_up_body,
        grid_spec=pltpu.PrefetchScalarGridSpec(
            num_scalar_prefetch=1,
            grid=(f // tf, nt),
            in_specs=[pl.BlockSpec((tm, d), lambda j, i, te: (i, 0)),
                      pl.BlockSpec((None, d, tf), lambda j, i, te: (te[i], 0, j)),
                      pl.BlockSpec((None, d, tf), lambda j, i, te: (te[i], 0, j))],
            out_specs=pl.BlockSpec((tm, tf), lambda j, i, te: (i, j)),
        ),
        out_shape=jax.ShapeDtypeStruct((p, f), BF16),
        compiler_params=_cparams("parallel", "arbitrary"),
        name="moe_gate_up",
    )(tile_expert, xs, wg, wu)
    return pl.pallas_call(
        _moe_down_body,
        grid_spec=pltpu.PrefetchScalarGridSpec(
            num_scalar_prefetch=1,
            grid=(nt,),
            in_specs=[pl.BlockSpec((tm, f), lambda i, te: (i, 0)),
                      pl.BlockSpec((None, f, d), lambda i, te: (te[i], 0, 0))],
            out_specs=pl.BlockSpec((tm, d), lambda i, te: (i, 0)),
        ),
        out_shape=jax.ShapeDtypeStruct((p, d), F32),
        compiler_params=_cparams("arbitrary"),
        name="moe_down",
    )(tile_expert, act, wd)


def _mm2_body(x_ref, w_ref, o32_ref, o16_ref):
    acc = jnp.dot(x_ref[...], w_ref[...], preferred_element_type=F32)
    o32_ref[...] = acc
    o16_ref[...] = acc.astype(BF16)


def matmul_dual(x, w, *, tn, name):
    m, k = x.shape
    n = w.shape[1]
    tm = TOKEN_TILE
    return pl.pallas_call(
        _mm2_body,
        grid=(n // tn, m // tm),
        in_specs=[pl.BlockSpec((tm, k), lambda j, i: (i, 0)), pl.BlockSpec((k, tn), lambda j, i: (0, j))],
        out_specs=[pl.BlockSpec((tm, tn), lambda j, i: (i, j)), pl.BlockSpec((tm, tn), lambda j, i: (i, j))],
        out_shape=[jax.ShapeDtypeStruct((m, n), F32), jax.ShapeDtypeStruct((m, n), BF16)],
        compiler_params=_cparams("parallel", "parallel"),
        name=name,
    )(x, w)


def rope_tables(pos):
    half = ROT_DIM // 2
    inv_freq = ROPE_THETA ** (-jnp.arange(half, dtype=F32) / half)
    ang = pos.astype(F32)[:, None] * inv_freq[None, :]
    cos, sin = jnp.cos(ang), jnp.sin(ang)
    m = pos.shape[0]
    z_half = jnp.zeros((m, half), F32)
    z_rest = jnp.zeros((m, HEAD_DIM - ROT_DIM), F32)
    c = jnp.concatenate([cos, cos, jnp.ones((m, HEAD_DIM - ROT_DIM), F32)], axis=1)
    s1 = jnp.concatenate([z_half, sin, z_rest], axis=1)
    s2 = jnp.concatenate([-sin, z_half, z_rest], axis=1)
    return tuple(jnp.tile(a, (1, 128 // HEAD_DIM)) for a in (c, s1, s2))


def _hnr_body(x_ref, g_ref, c_ref, s1_ref, s2_ref, *o_refs, width, outs):
    tm = x_ref.shape[0]
    lo = lax.broadcasted_iota(jnp.int32, (tm, 128), 1) < HEAD_DIM
    c, s1, s2 = c_ref[...], s1_ref[...], s2_ref[...]
    for j in range(width // 128):
        sl = slice(j * 128, (j + 1) * 128)
        x = x_ref[:, sl]
        x2 = x * x
        s_lo = jnp.sum(jnp.where(lo, x2, 0.0), axis=-1, keepdims=True)
        s_hi = jnp.sum(jnp.where(lo, 0.0, x2), axis=-1, keepdims=True)
        ms = jnp.where(lo, s_lo, s_hi) * (1.0 / HEAD_DIM)
        xn = x * lax.rsqrt(ms + NORM_EPS) * g_ref[:, sl]
        xr = xn * c + pltpu.roll(xn, ROT_DIM // 2, 1) * s1 + pltpu.roll(xn, 128 - ROT_DIM // 2, 1) * s2
        for (rope, _), o_ref in zip(outs, o_refs):
            o_ref[:, sl] = (xr if rope else xn).astype(o_ref.dtype)


def head_norm_rope(x, gains, tables, *, width, col0, outs, name):
    m = x.shape[0]
    ncol = gains.shape[0]
    tm = TOKEN_TILE
    tab_spec = pl.BlockSpec((tm, 128), lambda i, j: (i, 0))
    return pl.pallas_call(
        functools.partial(_hnr_body, width=width, outs=outs),
        grid=(m // tm, ncol),
        in_specs=[pl.BlockSpec((tm, width), lambda i, j: (i, col0 + j)),
                  pl.BlockSpec((None, 1, width), lambda i, j: (j, 0, 0)),
                  tab_spec, tab_spec, tab_spec],
        out_specs=[pl.BlockSpec((tm, width), lambda i, j: (i, j)) for _ in outs],
        out_shape=[jax.ShapeDtypeStruct((m, ncol * width), dt) for _, dt in outs],
        compiler_params=_cparams("parallel", "parallel"),
        name=name,
    )(x, gains.reshape(ncol, 1, width), *tables)


def _head_gain(g, width):
    return jnp.tile(g.astype(F32), width // HEAD_DIM)


def _step_tables(nq, lookback):
    qi, ki, first, last = [], [], [], []
    for q in range(nq):
        ks = list(range(q + 1)) if lookback is None else [k for k in range(q - lookback, q + 1) if k >= 0]
        for n, k in enumerate(ks):
            qi.append(q)
            ki.append(k)
            first.append(int(n == 0))
            last.append(int(n == len(ks) - 1))
    return tuple(jnp.asarray(a, jnp.int32) for a in (qi, ki, first, last))


def _pos_mask(qi, ki, t, band):
    row = lax.broadcasted_iota(jnp.int32, (t, t), 0)
    col = lax.broadcasted_iota(jnp.int32, (t, t), 1)
    d = (qi - ki) * t + row - col
    mask = d >= 0
    if band is not None:
        mask = mask & (d <= band)
    return mask


def _nt_dot(a, b):
    return lax.dot_general(a, b, (((1,), (1,)), ((), ())), preferred_element_type=F32)


def _online_update(sc, mask, v, m_ref, l_ref, acc_ref, idx):
    sc = jnp.where(mask, sc, NEG_INF)
    m_old = m_ref[idx]
    m_new = jnp.maximum(m_old, jnp.max(sc, axis=-1, keepdims=True))
    alpha = jnp.exp(m_old - m_new)
    p = jnp.where(mask, jnp.exp(sc - m_new), 0.0)
    l_ref[idx] = alpha * l_ref[idx] + jnp.sum(p, axis=-1, keepdims=True)
    acc_ref[idx] = alpha * acc_ref[idx] + jnp.dot(p.astype(BF16), v, preferred_element_type=F32)
    m_ref[idx] = m_new


def _init_state(m_ref, l_ref, acc_ref):
    m_ref[...] = jnp.full(m_ref.shape, NEG_INF, F32)
    l_ref[...] = jnp.zeros(l_ref.shape, F32)
    acc_ref[...] = jnp.zeros(acc_ref.shape, F32)


def _split_pair(q_ref, qs_ref, hb, lo):
    q = q_ref[:, hb * 128:(hb + 1) * 128].astype(F32) * SCALE
    qs_ref[2 * hb] = jnp.where(lo, q, 0.0).astype(BF16)
    qs_ref[2 * hb + 1] = jnp.where(lo, 0.0, q).astype(BF16)


def _gqa_query(q_ref, g, r, lo):
    col = g * NSA_REP + r
    blk = q_ref[:, (col // 2) * 128:(col // 2 + 1) * 128].astype(F32) * SCALE
    h = jnp.where(lo if col % 2 == 0 else jnp.logical_not(lo), blk, 0.0)
    d = h + pltpu.roll(h, HEAD_DIM, 1)
    return jnp.where(lo if g == 0 else jnp.logical_not(lo), d, 0.0).astype(BF16)


def _gqa_store(o_ref, outs, g, lo):
    keep = lo if g == 0 else jnp.logical_not(lo)
    dup = []
    for o in outs:
        z = jnp.where(keep, o, 0.0)
        dup.append(z + pltpu.roll(z, HEAD_DIM, 1))
    for pr in range(NSA_REP // 2):
        blk = g * (NSA_REP // 2) + pr
        o_ref[:, blk * 128:(blk + 1) * 128] = jnp.where(lo, dup[2 * pr], dup[2 * pr + 1]).astype(o_ref.dtype)


def _diff_body(qi_ref, ki_ref, fi_ref, la_ref, q_ref, k_ref, v_ref, lam_ref, gs_ref, o_ref,
               qs_ref, m_ref, l_ref, acc_ref, *, t, lam_init):
    s = pl.program_id(1)
    lo = lax.broadcasted_iota(jnp.int32, (t, 128), 1) < HEAD_DIM

    @pl.when(fi_ref[s] == 1)
    def _():
        for h in range(A_HEADS):
            _split_pair(q_ref, qs_ref, h, lo)
        _init_state(m_ref, l_ref, acc_ref)

    mask = _pos_mask(qi_ref[s], ki_ref[s], t, None)
    for h in range(A_HEADS):
        k = k_ref[:, h * 128:(h + 1) * 128]
        v = v_ref[:, h * 128:(h + 1) * 128]
        for var in range(2):
            _online_update(_nt_dot(qs_ref[2 * h + var], k), mask, v, m_ref, l_ref, acc_ref, 2 * h + var)

    @pl.when(la_ref[s] == 1)
    def _():
        lv = lam_ref[...]
        a = jnp.sum(lv[0:1] * lv[1:2], axis=-1, keepdims=True)
        b = jnp.sum(lv[2:3] * lv[3:4], axis=-1, keepdims=True)
        lam = jnp.exp(a) - jnp.exp(b) + lam_init
        for h in range(A_HEADS):
            o1 = acc_ref[2 * h] / jnp.maximum(l_ref[2 * h], TINY)
            o2 = acc_ref[2 * h + 1] / jnp.maximum(l_ref[2 * h + 1], TINY)
            o = o1 - lam * o2
            ms = jnp.mean(o * o, axis=-1, keepdims=True)
            y = o * lax.rsqrt(ms + NORM_EPS) * gs_ref[...] * (1.0 - lam_init)
            o_ref[:, h * 128:(h + 1) * 128] = y.astype(o_ref.dtype)


def diff_attention_prompt(qk16, v16, lam_rows, g_subln, *, nb, seq, lam_init, t=512):
    nq = seq // t
    tabs = _step_tables(nq, None)
    w = A_V_W
    qmap = lambda n, s, qi, ki, fi, la: (n * nq + qi[s], 0)
    kmap = lambda n, s, qi, ki, fi, la: (n * nq + ki[s], 1)
    vmap = lambda n, s, qi, ki, fi, la: (n * nq + ki[s], 2)
    const = lambda n, s, qi, ki, fi, la: (0, 0)
    return pl.pallas_call(
        functools.partial(_diff_body, t=t, lam_init=lam_init),
        grid_spec=pltpu.PrefetchScalarGridSpec(
            num_scalar_prefetch=4,
            grid=(nb, int(tabs[0].shape[0])),
            in_specs=[pl.BlockSpec((t, w), qmap), pl.BlockSpec((t, w), kmap), pl.BlockSpec((t, w), vmap),
                      pl.BlockSpec((8, 128), const), pl.BlockSpec((1, 128), const)],
            out_specs=pl.BlockSpec((t, w), qmap),
            scratch_shapes=[pltpu.VMEM((2 * A_HEADS, t, 128), BF16), pltpu.VMEM((2 * A_HEADS, t, 1), F32),
                            pltpu.VMEM((2 * A_HEADS, t, 1), F32), pltpu.VMEM((2 * A_HEADS, t, 128), F32)],
        ),
        out_shape=jax.ShapeDtypeStruct((nb * seq, w), BF16),
        compiler_params=_cparams("parallel", "arbitrary"),
        name="diff_attn_prompt",
    )(*tabs, qk16, qk16, v16, lam_rows, g_subln.reshape(1, A_VDIM).astype(F32))


def _dil_body(qi_ref, ki_ref, fi_ref, la_ref, q_ref, k_ref, v_ref, o_ref, lse_ref,
              qs_ref, m_ref, l_ref, acc_ref, *, t, band):
    s = pl.program_id(1)
    lo = lax.broadcasted_iota(jnp.int32, (t, 128), 1) < HEAD_DIM
    nhb = C_HEADS // 2

    @pl.when(fi_ref[s] == 1)
    def _():
        for hb in range(nhb):
            _split_pair(q_ref, qs_ref, hb, lo)
        _init_state(m_ref, l_ref, acc_ref)

    mask = _pos_mask(qi_ref[s], ki_ref[s], t, band)
    for hb in range(nhb):
        k = k_ref[:, hb * 128:(hb + 1) * 128]
        v = v_ref[:, hb * 128:(hb + 1) * 128]
        for var in range(2):
            _online_update(_nt_dot(qs_ref[2 * hb + var], k), mask, v, m_ref, l_ref, acc_ref, 2 * hb + var)

    @pl.when(la_ref[s] == 1)
    def _():
        for hb in range(nhb):
            l0 = jnp.maximum(l_ref[2 * hb], TINY)
            l1 = jnp.maximum(l_ref[2 * hb + 1], TINY)
            o = jnp.where(lo, acc_ref[2 * hb] / l0, acc_ref[2 * hb + 1] / l1)
            lse = jnp.where(lo, m_ref[2 * hb] + jnp.log(l0), m_ref[2 * hb + 1] + jnp.log(l1))
            o_ref[:, hb * 128:(hb + 1) * 128] = o.astype(o_ref.dtype)
            lse_ref[:, hb * 128:(hb + 1) * 128] = lse


def dilated_attention_prompt(qk16, v16, *, nb, seq, dil, band, gi, t):
    mp = nb * seq
    sub = seq // dil
    nq = sub // t
    tabs = _step_tables(nq, -(-band // t))
    w = C_W
    qk = qk16.reshape(qk16.shape[0] // dil, dil * 2 * w)
    vv = v16.reshape(v16.shape[0] // dil, dil * v16.shape[1])
    vcols = v16.shape[1] // w
    row = lambda b, x: (b // dil) * nq + x

    def qmap(b, s, qi, ki, fi, la):
        return (row(b, qi[s]), (b % dil) * 2)

    def kmap(b, s, qi, ki, fi, la):
        return (row(b, ki[s]), (b % dil) * 2 + 1)

    def vmap(b, s, qi, ki, fi, la):
        return (row(b, ki[s]), (b % dil) * vcols + gi * 3 + 2)

    def omap(b, s, qi, ki, fi, la):
        return (row(b, qi[s]), b % dil)

    nst = 2 * (C_HEADS // 2)
    o, lse = pl.pallas_call(
        functools.partial(_dil_body, t=t, band=band),
        grid_spec=pltpu.PrefetchScalarGridSpec(
            num_scalar_prefetch=4,
            grid=(nb * dil, int(tabs[0].shape[0])),
            in_specs=[pl.BlockSpec((t, w), qmap), pl.BlockSpec((t, w), kmap), pl.BlockSpec((t, w), vmap)],
            out_specs=[pl.BlockSpec((t, w), omap), pl.BlockSpec((t, w), omap)],
            scratch_shapes=[pltpu.VMEM((nst, t, 128), BF16), pltpu.VMEM((nst, t, 1), F32),
                            pltpu.VMEM((nst, t, 1), F32), pltpu.VMEM((nst, t, 128), F32)],
        ),
        out_shape=[jax.ShapeDtypeStruct((mp // dil, dil * w), BF16), jax.ShapeDtypeStruct((mp // dil, dil * w), F32)],
        compiler_params=_cparams("parallel", "arbitrary"),
        name=f"dilated_attn_prompt_{dil}",
    )(*tabs, qk, qk, vv)
    return o.reshape(mp, w), lse.reshape(mp, w)


def _gqa_body(qi_ref, ki_ref, fi_ref, la_ref, q_ref, k_ref, v_ref, *rest, t, band, use_sel):
    if use_sel:
        sel_ref, o_ref, qs_ref, m_ref, l_ref, acc_ref = rest
    else:
        o_ref, qs_ref, m_ref, l_ref, acc_ref = rest
    s = pl.program_id(1)
    lo = lax.broadcasted_iota(jnp.int32, (t, 128), 1) < HEAD_DIM

    @pl.when(fi_ref[s] == 1)
    def _():
        for g in range(NSA_KV_HEADS):
            for r in range(NSA_REP):
                qs_ref[g * NSA_REP + r] = _gqa_query(q_ref, g, r, lo)
        _init_state(m_ref, l_ref, acc_ref)

    ki = ki_ref[s]
    mask = _pos_mask(qi_ref[s], ki, t, band)
    k = k_ref[...]
    v = v_ref[...]
    if use_sel:
        blk_row = lax.broadcasted_iota(jnp.int32, (128, t), 0)
        blk_col = jnp.right_shift(ki * t + lax.broadcasted_iota(jnp.int32, (128, t), 1), SEL_SHIFT)
        expand = jnp.where(blk_row == blk_col, 1.0, 0.0).astype(BF16)
    for g in range(NSA_KV_HEADS):
        mg = mask
        if use_sel:
            mg = mask & (jnp.dot(sel_ref[g], expand, preferred_element_type=F32) > 0.5)
        for r in range(NSA_REP):
            i = g * NSA_REP + r
            _online_update(_nt_dot(qs_ref[i], k), mg, v, m_ref, l_ref, acc_ref, i)

    @pl.when(la_ref[s] == 1)
    def _():
        for g in range(NSA_KV_HEADS):
            outs = [acc_ref[g * NSA_REP + r] / jnp.maximum(l_ref[g * NSA_REP + r], TINY) for r in range(NSA_REP)]
            _gqa_store(o_ref, outs, g, lo)


def nsa_branch_prompt(q16, k16, kcol, v16, vcol, sel, *, nb, seq, band, t=512, name):
    nq = seq // t
    tabs = _step_tables(nq, None if band is None else -(-band // t))
    qmap = lambda n, s, qi, ki, fi, la: (n * nq + qi[s], 0)
    kmap = lambda n, s, qi, ki, fi, la: (n * nq + ki[s], kcol)
    vmap = lambda n, s, qi, ki, fi, la: (n * nq + ki[s], vcol)
    in_specs = [pl.BlockSpec((t, NSA_Q_W), qmap), pl.BlockSpec((t, 128), kmap), pl.BlockSpec((t, 128), vmap)]
    args = [q16, k16, v16]
    if sel is not None:
        in_specs.append(pl.BlockSpec((NSA_KV_HEADS, t, 128), lambda n, s, qi, ki, fi, la: (0, n * nq + qi[s], 0)))
        args.append(sel)
    nst = NSA_HEADS
    return pl.pallas_call(
        functools.partial(_gqa_body, t=t, band=band, use_sel=sel is not None),
        grid_spec=pltpu.PrefetchScalarGridSpec(
            num_scalar_prefetch=4,
            grid=(nb, int(tabs[0].shape[0])),
            in_specs=in_specs,
            out_specs=pl.BlockSpec((t, NSA_Q_W), qmap),
            scratch_shapes=[pltpu.VMEM((nst, t, 128), BF16), pltpu.VMEM((nst, t, 1), F32),
                            pltpu.VMEM((nst, t, 1), F32), pltpu.VMEM((nst, t, 128), F32)],
        ),
        out_shape=jax.ShapeDtypeStruct((nb * seq, NSA_Q_W), BF16),
        compiler_params=_cparams("parallel", "arbitrary"),
        name=name,
    )(*tabs, *args)


def _cmp_finish_body(ab_ref, pe_ref, w1_ref, w2_ref, g_ref, o_ref, *, hid, norm):
    ab = ab_ref[...]
    pe_term = jnp.dot(pe_ref[...], w1_ref[...], preferred_element_type=F32)[0:1]
    h = ab[:, :hid] + pltpu.roll(ab[:, hid:], ab.shape[0] - 1, 0) + pe_term
    act = (h * jax.nn.sigmoid(h)).astype(BF16)
    y = jnp.dot(act, w2_ref[...], preferred_element_type=F32)
    if norm:
        y = y * lax.rsqrt(jnp.mean(y * y, axis=-1, keepdims=True) + NORM_EPS) * g_ref[...]
    o_ref[...] = y.astype(o_ref.dtype)


def _w1_ab(w1):
    half = w1.shape[0] // 2
    return jnp.concatenate([w1[:half], w1[half:]], axis=1).astype(BF16)


def compress_blocks(x_chunks, pe, w1, w2, gain):
    b, nchunk, cw = x_chunks.shape
    hid = w1.shape[1]
    ab = matmul(x_chunks.reshape(b * nchunk, cw), _w1_ab(w1), tn=2 * hid, name="compress_in")
    pe_rows = jnp.zeros((8, 2 * cw), BF16).at[0].set(pe.reshape(-1).astype(BF16))
    g = jnp.ones((1, HEAD_DIM), F32) if gain is None else gain.reshape(1, HEAD_DIM).astype(F32)
    const = lambda i: (0, 0)
    return pl.pallas_call(
        functools.partial(_cmp_finish_body, hid=hid, norm=gain is not None),
        grid=(b,),
        in_specs=[pl.BlockSpec((nchunk, 2 * hid), lambda i: (i, 0)), pl.BlockSpec((8, 2 * cw), const),
                  pl.BlockSpec((2 * cw, hid), const), pl.BlockSpec((hid, HEAD_DIM), const),
                  pl.BlockSpec((1, HEAD_DIM), const)],
        out_specs=pl.BlockSpec((None, nchunk, HEAD_DIM), lambda i: (i, 0, 0)),
        out_shape=jax.ShapeDtypeStruct((b, nchunk, HEAD_DIM), BF16),
        compiler_params=_cparams("parallel"),
        name="compress_finish",
    )(ab, pe_rows, w1.astype(BF16), w2.astype(BF16), g)


def _overlap_matrix(n_cmp, n_sel):
    c0 = np.arange(128)[:, None] * CMP_STRIDE
    s0 = np.arange(128)[None, :] * SEL_BLOCK
    ov = np.maximum(np.minimum(c0 + CMP_LEN, s0 + SEL_BLOCK) - np.maximum(c0, s0), 0) / CMP_LEN
    ov = ov * (np.arange(128)[:, None] < n_cmp) * (np.arange(128)[None, :] < n_sel)
    return jnp.asarray(ov, BF16)


def _cmp_body(q_ref, kc_ref, vc_ref, ov_ref, o_ref, sel_ref, *, t, pos0, n_cmp, n_sel):
    i = pl.program_id(1)
    lane = lax.broadcasted_iota(jnp.int32, (t, 128), 1)
    qpos = pos0 + i * t + lax.broadcasted_iota(jnp.int32, (t, 128), 0)
    lo = lane < HEAD_DIM
    vis = (lane * CMP_STRIDE + CMP_LEN - 1 <= qpos) & (lane < n_cmp)
    kc, vc, ov = kc_ref[...], vc_ref[...], ov_ref[...]
    cur = jnp.right_shift(qpos, SEL_SHIFT)
    valid = (lane <= cur) & (lane < n_sel)
    forced = (lane == 0) | (lane == cur) | (lane == cur - 1)
    for g in range(NSA_KV_HEADS):
        imp = jnp.zeros((t, 128), F32)
        outs = []
        for r in range(NSA_REP):
            sc = jnp.where(vis, _nt_dot(_gqa_query(q_ref, g, r, lo), kc), NEG_INF)
            m = jnp.max(sc, axis=-1, keepdims=True)
            e = jnp.where(vis, jnp.exp(sc - m), 0.0)
            p = (e / jnp.maximum(jnp.sum(e, axis=-1, keepdims=True), TINY)).astype(BF16)
            outs.append(jnp.dot(p, vc, preferred_element_type=F32))
            imp = imp + jnp.dot(p, ov, preferred_element_type=F32)
        _gqa_store(o_ref, outs, g, lo)
        score = jnp.where(valid, imp + jnp.where(forced, FORCE_BONUS, 0.0), NEG_INF)
        rank = jnp.zeros((t, 128), F32)
        for kk in range(n_sel):
            sk = score[:, kk:kk + 1]
            rank = rank + jnp.where((sk > score) | ((sk == score) & (lane > kk)), 1.0, 0.0)
        sel_ref[g] = jnp.where((rank < SEL_TOPK) & valid, 1.0, 0.0).astype(sel_ref.dtype)


def nsa_compressed_prompt(qn16, k_cmp, v_cmp, *, nb, seq, t=512):
    n_cmp = (seq - CMP_LEN) // CMP_STRIDE + 1
    n_sel = -(-seq // SEL_BLOCK)
    nq = seq // t
    qmap = lambda n, i: (n * nq + i, 0)
    cmap = lambda n, i: (n, 0, 0)
    return pl.pallas_call(
        functools.partial(_cmp_body, t=t, pos0=0, n_cmp=n_cmp, n_sel=n_sel),
        grid=(nb, nq),
        in_specs=[pl.BlockSpec((t, NSA_Q_W), qmap), pl.BlockSpec((None, 128, 128), cmap),
                  pl.BlockSpec((None, 128, 128), cmap), pl.BlockSpec((128, 128), lambda n, i: (0, 0))],
        out_specs=[pl.BlockSpec((t, NSA_Q_W), qmap),
                   pl.BlockSpec((NSA_KV_HEADS, t, 128), lambda n, i: (0, n * nq + i, 0))],
        out_shape=[jax.ShapeDtypeStruct((nb * seq, NSA_Q_W), BF16),
                   jax.ShapeDtypeStruct((NSA_KV_HEADS, nb * seq, 128), BF16)],
        compiler_params=_cparams("parallel", "parallel"),
        name="nsa_cmp_select_prompt",
    )(qn16, k_cmp, v_cmp, _overlap_matrix(n_cmp, n_sel))


def _gate_expand_matrices():
    lane = np.arange(128)[:, None]
    col = np.arange(NSA_Q_W)[None, :]
    return jnp.asarray(np.stack([(lane < NSA_GATE_W) & (lane % 3 == br) & (lane // 3 == col // HEAD_DIM)
                                 for br in range(3)]), BF16)


def _nsa_merge_body(gb_ref, e_ref, oc_ref, os_ref, ow_ref, o_ref):
    gates = jax.nn.sigmoid(gb_ref[...])
    hi = gates.astype(BF16)
    lo = (gates - hi.astype(F32)).astype(BF16)
    acc = jnp.zeros(o_ref.shape, F32)
    for br, b_ref in enumerate((oc_ref, os_ref, ow_ref)):
        w = jnp.dot(hi, e_ref[br], preferred_element_type=F32) + jnp.dot(lo, e_ref[br], preferred_element_type=F32)
        acc = acc + w * b_ref[...].astype(F32)
    o_ref[...] = acc.astype(o_ref.dtype)


def nsa_merge(proj32, gate_col, o_cmp, o_sel, o_win):
    m = o_cmp.shape[0]
    tm = TOKEN_TILE
    spec = pl.BlockSpec((tm, NSA_Q_W), lambda i: (i, 0))
    return pl.pallas_call(
        _nsa_merge_body,
        grid=(m // tm,),
        in_specs=[pl.BlockSpec((tm, 128), lambda i: (i, gate_col)),
                  pl.BlockSpec((3, 128, NSA_Q_W), lambda i: (0, 0, 0)), spec, spec, spec],
        out_specs=spec,
        out_shape=jax.ShapeDtypeStruct((m, NSA_Q_W), BF16),
        compiler_params=_cparams("parallel"),
        name="nsa_merge",
    )(proj32, _gate_expand_matrices(), o_cmp, o_sel, o_win)


def _dil_merge_body(o0, o1, o2, l0, l1, l2, o_ref):
    ls = [l0[...], l1[...], l2[...]]
    m = jnp.maximum(jnp.maximum(ls[0], ls[1]), ls[2])
    es = [jnp.exp(x - m) for x in ls]
    den = es[0] + es[1] + es[2]
    acc = sum((e / den) * o[...].astype(F32) for e, o in zip(es, (o0, o1, o2)))
    o_ref[...] = acc.astype(o_ref.dtype)


def dilation_merge(outs, lses):
    m, w = outs[0].shape
    tm = TOKEN_TILE
    spec = pl.BlockSpec((tm, w), lambda i: (i, 0))
    return pl.pallas_call(
        _dil_merge_body,
        grid=(m // tm,),
        in_specs=[spec] * 6,
        out_specs=spec,
        out_shape=jax.ShapeDtypeStruct((m, w), BF16),
        compiler_params=_cparams("parallel"),
        name="dilation_merge",
    )(*outs, *lses)


def _softmax_step(sc, mask, m, l):
    sc = jnp.where(mask, sc, NEG_INF)
    m_new = jnp.maximum(m, jnp.max(sc, axis=-1, keepdims=True))
    alpha = jnp.exp(m - m_new)
    p = jnp.where(mask, jnp.exp(sc - m_new), 0.0)
    return p, alpha, m_new, alpha * l + jnp.sum(p, axis=-1, keepdims=True)


def _diff_sample_body(pt_ref, *refs, npages, lam_init):
    del pt_ref
    page_refs = refs[:npages]
    q_ref, new_ref, lam_ref, gs_ref, o_ref = refs[npages:]
    rows = 2 * A_HEADS * 4
    ri = lax.broadcasted_iota(jnp.int32, (rows, 128), 0)
    lane = lax.broadcasted_iota(jnp.int32, (rows, 128), 1)
    first_variant = ri < rows // 2
    qs = jnp.where(first_variant == (lane < HEAD_DIM), q_ref[...] * SCALE, 0.0).astype(BF16)
    m = jnp.full((rows, 1), NEG_INF, F32)
    l = jnp.zeros((rows, 1), F32)
    acc = jnp.zeros((rows, 128), F32)

    def attend(page, mask, m, l, acc):
        pg = page.astype(BF16)
        p, alpha, m, l = _softmax_step(_nt_dot(qs, pg), mask, m, l)
        pv = jnp.dot(pltpu.roll(p, A_HEADS, 1).astype(BF16), pg, preferred_element_type=F32)
        return m, l, alpha * acc + pv

    ncols = page_refs[0].shape[0]
    col = lax.broadcasted_iota(jnp.int32, (rows, ncols), 1)
    head = jnp.bitwise_and(jnp.right_shift(lax.broadcasted_iota(jnp.int32, (rows, ncols), 0), 2), A_HEADS - 1)
    page_mask = jnp.bitwise_and(col, 2 * A_HEADS - 1) == head
    for pr in page_refs:
        m, l, acc = attend(pr[...], page_mask, m, l, acc)
    tok = jnp.bitwise_and(ri, 3)
    new_mask = ((jnp.bitwise_and(lane, 2 * A_HEADS - 1) == jnp.bitwise_and(jnp.right_shift(ri, 2), A_HEADS - 1))
                & (jnp.right_shift(lane, 3) <= tok) & (lane < 4 * 2 * A_HEADS))
    m, l, acc = attend(new_ref[...], new_mask, m, l, acc)

    lv = lam_ref[...]
    a = jnp.sum(lv[0:1] * lv[1:2], axis=-1, keepdims=True)
    b = jnp.sum(lv[2:3] * lv[3:4], axis=-1, keepdims=True)
    lam = jnp.exp(a) - jnp.exp(b) + lam_init
    o = acc / jnp.maximum(l, TINY)
    o = o[:rows // 2] - lam * o[rows // 2:]
    y = o * lax.rsqrt(jnp.mean(o * o, axis=-1, keepdims=True) + NORM_EPS) * gs_ref[...] * (1.0 - lam_init)
    o_ref[...] = y.astype(o_ref.dtype)


def diff_attention_sample(q_rows, new_page, cache_rows, page_table, lam_rows, g_subln, *, lam_init):
    db = q_rows.shape[0]
    npages = page_table.shape[1]
    prow = cache_rows.shape[1]
    page_specs = [pl.BlockSpec((None, prow, 128), functools.partial(lambda b, pt, p: (pt[b, p], 0, 0), p=p))
                  for p in range(npages)]
    per_b = lambda b, pt: (b, 0, 0)
    const = lambda b, pt: (0, 0)
    return pl.pallas_call(
        functools.partial(_diff_sample_body, npages=npages, lam_init=lam_init),
        grid_spec=pltpu.PrefetchScalarGridSpec(
            num_scalar_prefetch=1,
            grid=(db,),
            in_specs=page_specs + [pl.BlockSpec((None, 32, 128), per_b), pl.BlockSpec((None, 128, 128), per_b),
                                   pl.BlockSpec((8, 128), const), pl.BlockSpec((1, 128), const)],
            out_specs=pl.BlockSpec((None, 16, 128), per_b),
        ),
        out_shape=jax.ShapeDtypeStruct((db, 16, 128), F32),
        compiler_params=_cparams("parallel"),
        name="diff_attn_sample",
    )(page_table, *([cache_rows] * npages), q_rows, new_page, lam_rows, g_subln.reshape(1, A_VDIM).astype(F32))


def _place_new_columns(rolled, new_rows, t):
    sq = jnp.concatenate([new_rows, jnp.zeros_like(new_rows)], axis=1)
    new_t = pltpu.roll(jnp.transpose(sq)[:HEAD_DIM], 124, 1)
    if t > 128:
        new_t = jnp.concatenate([jnp.zeros((HEAD_DIM, t - 128), F32), new_t], axis=1)
    lane = lax.broadcasted_iota(jnp.int32, (HEAD_DIM, t), 1)
    return jnp.where(lane >= t - 4, new_t, rolled)


def _nsa_sample_body(pt_ref, *refs, npages, past, n_cmp, n_sel, hid):
    del pt_ref
    cn_refs = refs[:npages]
    ab_refs = refs[npages:2 * npages]
    (qn_ref, qr_ref, new_ref, sw_ref, pe_ref, w1k_ref, w1v_ref, w2k_ref, w2v_ref, gk_ref, ov_ref,
     oc_ref, os_ref, ow_ref, wout_ref) = refs[2 * npages:]
    rows = NSA_REP * 8
    nwin = sw_ref.shape[-1]
    lane = lax.broadcasted_iota(jnp.int32, (rows, 128), 1)
    tok = jnp.bitwise_and(lax.broadcasted_iota(jnp.int32, (rows, 128), 0), 7)
    lane8 = lane[:8]
    tok8 = tok[:8]
    vis = (lane * CMP_STRIDE + CMP_LEN - 1 <= past + tok) & (lane < n_cmp)
    cur = jnp.right_shift(past + tok8, SEL_SHIFT)
    valid = (lane8 <= cur) & (lane8 < n_sel)
    forced = (lane8 == 0) | (lane8 == cur) | (lane8 == cur - 1)
    new_mask = (lane <= tok) & (lane < 4)
    ov = ov_ref[...]
    pe_terms = [jnp.dot(pe_ref[c], w_ref[...], preferred_element_type=F32)[0:1]
                for c, w_ref in enumerate((w1k_ref, w1v_ref))]

    for g in range(NSA_KV_HEADS):
        cmp = []
        for c, w2_ref in enumerate((w2k_ref, w2v_ref)):
            a = jnp.concatenate([r[c, g, :, :hid] for r in ab_refs], axis=0)
            bb = jnp.concatenate([r[c, g, :, hid:] for r in ab_refs], axis=0)
            h = a + pltpu.roll(bb, a.shape[0] - 1, 0) + pe_terms[c]
            y = jnp.dot((h * jax.nn.sigmoid(h)).astype(BF16), w2_ref[...], preferred_element_type=F32)
            if c == 0:
                y = y * lax.rsqrt(jnp.mean(y * y, axis=-1, keepdims=True) + NORM_EPS) * gk_ref[...]
            cmp.append(y.astype(BF16))
        qn = (qn_ref[g].astype(F32) * SCALE).astype(BF16)
        qr = (qr_ref[g].astype(F32) * SCALE).astype(BF16)

        sc = jnp.where(vis, _nt_dot(qn, cmp[0]), NEG_INF)
        mx = jnp.max(sc, axis=-1, keepdims=True)
        e = jnp.where(vis, jnp.exp(sc - mx), 0.0)
        p = (e / jnp.maximum(jnp.sum(e, axis=-1, keepdims=True), TINY)).astype(BF16)
        oc_ref[g] = jnp.dot(p, cmp[1], preferred_element_type=F32)
        imp_r = jnp.dot(p, ov, preferred_element_type=F32)
        imp = imp_r[0:8] + imp_r[8:16] + imp_r[16:24] + imp_r[24:32]
        score = jnp.where(valid, imp + jnp.where(forced, FORCE_BONUS, 0.0), NEG_INF)
        rank = jnp.zeros((8, 128), F32)
        for kk in range(n_sel):
            sk = score[:, kk:kk + 1]
            rank = rank + jnp.where((sk > score) | ((sk == score) & (lane8 > kk)), 1.0, 0.0)
        sel = (rank < SEL_TOPK) & valid

        m = jnp.full((rows, 1), NEG_INF, F32)
        l = jnp.zeros((rows, 1), F32)
        acc = jnp.zeros((rows, HEAD_DIM), F32)
        lo8 = lane8 < SEL_BLOCK
        for pi, r in enumerate(cn_refs):
            blocks_per_page = 128 // SEL_BLOCK
            flag = jnp.where(lo8, jnp.where(sel[:, blocks_per_page * pi:blocks_per_page * pi + 1], 1.0, 0.0),
                             jnp.where(sel[:, blocks_per_page * pi + 1:blocks_per_page * pi + 2], 1.0, 0.0))
            mask = jnp.concatenate([flag] * NSA_REP, axis=0) > 0.5
            pr, alpha, m, l = _softmax_step(jnp.dot(qr, r[0, g].astype(BF16), preferred_element_type=F32), mask, m, l)
            acc = alpha * acc + _nt_dot(pr.astype(BF16), r[1, g].astype(BF16))
        last_blk = past // SEL_BLOCK
        flag = jnp.where(sel[:, last_blk:last_blk + 1], 1.0, 0.0) + jnp.zeros((8, 128), F32)
        mask = new_mask & (jnp.concatenate([flag] * NSA_REP, axis=0) > 0.5)
        pr, alpha, m, l = _softmax_step(_nt_dot(qr, new_ref[0, g].astype(BF16)), mask, m, l)
        acc = alpha * acc + jnp.dot(pr.astype(BF16), new_ref[1, g].astype(BF16), preferred_element_type=F32)
        os_ref[g] = acc / jnp.maximum(l, TINY)

        wl = lax.broadcasted_iota(jnp.int32, (rows, nwin), 1)
        wt = jnp.bitwise_and(lax.broadcasted_iota(jnp.int32, (rows, nwin), 0), 7)
        wmask = wl >= nwin + wt - NSA_WINDOW
        kt = sw_ref[0, g]
        vt = sw_ref[1, g]
        m = jnp.full((rows, 1), NEG_INF, F32)
        l = jnp.zeros((rows, 1), F32)
        pr, alpha, m, l = _softmax_step(jnp.dot(qr, kt.astype(BF16), preferred_element_type=F32), wmask, m, l)
        acc = _nt_dot(pr.astype(BF16), vt.astype(BF16))
        pr, alpha, m, l = _softmax_step(_nt_dot(qr, new_ref[2, g].astype(BF16)), new_mask, m, l)
        acc = alpha * acc + jnp.dot(pr.astype(BF16), new_ref[3, g].astype(BF16), preferred_element_type=F32)
        ow_ref[g] = acc / jnp.maximum(l, TINY)
        wout_ref[0, g] = _place_new_columns(pltpu.roll(kt, nwin - 4, 1), new_ref[2, g], nwin)
        wout_ref[1, g] = _place_new_columns(pltpu.roll(vt, nwin - 4, 1), new_ref[3, g], nwin)


def nsa_sample(qn, qr, new_rows, cache_t, ab, win_state, page_table, cmp_w, g_kc, *, past):
    pe_k, w_k1, w_k2, pe_v, w_v1, w_v2 = cmp_w
    db = qn.shape[0]
    npages = page_table.shape[1]
    hid = w_k1.shape[1]
    nwin = win_state.shape[-1]
    n_cmp = (past + 4 - CMP_LEN) // CMP_STRIDE + 1
    n_sel = -(-(past + 4) // SEL_BLOCK)
    assert n_cmp <= npages * 8 - 1 and past % SEL_BLOCK == 0
    pe_rows = jnp.zeros((2, 8, pe_k.size), BF16).at[:, 0].set(
        jnp.stack([pe_k.reshape(-1), pe_v.reshape(-1)]).astype(BF16))
    cn_specs = [pl.BlockSpec((None, 2, NSA_KV_HEADS, HEAD_DIM, 128),
                             functools.partial(lambda b, pt, p: (pt[b, p], 1, 0, 0, 0), p=p)) for p in range(npages)]
    ab_specs = [pl.BlockSpec((2, None, NSA_KV_HEADS, 8, 2 * hid),
                             functools.partial(lambda b, pt, p: (0, pt[b, p], 0, 0, 0), p=p)) for p in range(npages)]
    b4 = lambda b, pt: (b, 0, 0, 0)
    b5 = lambda b, pt: (b, 0, 0, 0, 0)
    c2 = lambda b, pt: (0, 0)
    c3 = lambda b, pt: (0, 0, 0)
    o_spec = pl.BlockSpec((None, NSA_KV_HEADS, 32, HEAD_DIM), b4)
    o_shape = jax.ShapeDtypeStruct((db, NSA_KV_HEADS, 32, HEAD_DIM), F32)
    return pl.pallas_call(
        functools.partial(_nsa_sample_body, npages=npages, past=past, n_cmp=n_cmp, n_sel=n_sel, hid=hid),
        grid_spec=pltpu.PrefetchScalarGridSpec(
            num_scalar_prefetch=1,
            grid=(db,),
            in_specs=cn_specs + ab_specs + [
                pl.BlockSpec((None, NSA_KV_HEADS, 32, HEAD_DIM), b4), pl.BlockSpec((None, NSA_KV_HEADS, 32, HEAD_DIM), b4),
                pl.BlockSpec((None, 4, NSA_KV_HEADS, 128, HEAD_DIM), b5),
                pl.BlockSpec((None, 2, NSA_KV_HEADS, HEAD_DIM, nwin), b5),
                pl.BlockSpec((2, 8, pe_k.size), c3),
                pl.BlockSpec(w_k1.shape, c2), pl.BlockSpec(w_v1.shape, c2),
                pl.BlockSpec(w_k2.shape, c2), pl.BlockSpec(w_v2.shape, c2),
                pl.BlockSpec((1, HEAD_DIM), c2), pl.BlockSpec((128, 128), c2)],
            out_specs=[o_spec, o_spec, o_spec, pl.BlockSpec((None, 2, NSA_KV_HEADS, HEAD_DIM, nwin), b5)],
        ),
        out_shape=[o_shape, o_shape, o_shape, jax.ShapeDtypeStruct(win_state.shape, F32)],
        compiler_params=_cparams("parallel"),
        name="nsa_sample",
    )(page_table, *([cache_t] * npages), *([ab] * npages), qn, qr, new_rows, win_state, pe_rows,
      w_k1.astype(BF16), w_v1.astype(BF16), w_k2.astype(BF16), w_v2.astype(BF16),
      g_kc.reshape(1, HEAD_DIM).astype(F32), _overlap_matrix(n_cmp, n_sel))


def _dil_sample_body(q_ref, kn_ref, vn_ref, st_ref, oext_ref, roll_ref, *, window, dil, hg):
    nbuf = st_ref.shape[-1]
    col = lax.broadcasted_iota(jnp.int32, (8, nbuf), 1)
    dist = nbuf + lax.broadcasted_iota(jnp.int32, (8, nbuf), 0) - col
    mask = (dist <= window) & (jnp.bitwise_and(dist, dil - 1) == 0)
    ncol = lax.broadcasted_iota(jnp.int32, (8, 128), 1)
    nd = lax.broadcasted_iota(jnp.int32, (8, 128), 0) - ncol
    nmask = (nd >= 0) & (jnp.bitwise_and(nd, dil - 1) == 0) & (ncol < 4)
    for h in range(hg):
        sl = slice(h * HEAD_DIM, (h + 1) * HEAD_DIM)
        q = (q_ref[:, sl] * SCALE).astype(BF16)
        kt, vt = st_ref[0, h], st_ref[1, h]
        kn, vn = kn_ref[:, sl], vn_ref[:, sl]
        s1 = jnp.where(mask, jnp.dot(q, kt.astype(BF16), preferred_element_type=F32), NEG_INF)
        s2 = jnp.where(nmask, _nt_dot(q, kn.astype(BF16)), NEG_INF)
        m = jnp.maximum(jnp.max(s1, axis=-1, keepdims=True), jnp.max(s2, axis=-1, keepdims=True))
        e1 = jnp.where(mask, jnp.exp(s1 - m), 0.0)
        e2 = jnp.where(nmask, jnp.exp(s2 - m), 0.0)
        l = jnp.maximum(jnp.sum(e1, axis=-1, keepdims=True) + jnp.sum(e2, axis=-1, keepdims=True), TINY)
        o = (_nt_dot(e1.astype(BF16), vt.astype(BF16))
             + jnp.dot(e2.astype(BF16), vn.astype(BF16), preferred_element_type=F32)) / l
        lse = m + jnp.log(l)
        oext_ref[h] = jnp.concatenate([o, lse + jnp.zeros((8, HEAD_DIM), F32)], axis=1)
        roll_ref[0, h] = _place_new_columns(pltpu.roll(kt, nbuf - 4, 1), kn, nbuf)
        roll_ref[1, h] = _place_new_columns(pltpu.roll(vt, nbuf - 4, 1), vn, nbuf)


def dilated_attention_sample(q8, k_new, v_new, state_t, *, window, dil, hg=4):
    db = q8.shape[0]
    nbuf = state_t.shape[-1]
    assert dil & (dil - 1) == 0 and nbuf >= window
    w = hg * HEAD_DIM
    return pl.pallas_call(
        functools.partial(_dil_sample_body, window=window, dil=dil, hg=hg),
        grid=(db, C_HEADS // hg),
        in_specs=[pl.BlockSpec((None, 8, w), lambda b, j: (b, 0, j)),
                  pl.BlockSpec((None, 128, w), lambda b, j: (b, 0, j)),
                  pl.BlockSpec((None, 128, w), lambda b, j: (b, 0, j)),
                  pl.BlockSpec((None, 2, hg, HEAD_DIM, nbuf), lambda b, j: (b, 0, j, 0, 0))],
        out_specs=[pl.BlockSpec((None, hg, 8, 128), lambda b, j: (b, j, 0, 0)),
                   pl.BlockSpec((None, 2, hg, HEAD_DIM, nbuf), lambda b, j: (b, 0, j, 0, 0))],
        out_shape=[jax.ShapeDtypeStruct((db, C_HEADS, 8, 128), F32), jax.ShapeDtypeStruct(state_t.shape, F32)],
        compiler_params=_cparams("parallel", "parallel"),
        name=f"dilated_attn_sample_{dil}",
    )(q8, k_new, v_new, state_t)


def _rms_norm(x, g):
    xf = x.astype(F32)
    y = xf * lax.rsqrt(jnp.mean(xf * xf, axis=-1, keepdims=True) + NORM_EPS)
    return (y * g.astype(F32)).astype(x.dtype)


def _partial_rope(x, pos):
    half = ROT_DIM // 2
    inv_freq = ROPE_THETA ** (-jnp.arange(half, dtype=F32) / half)
    ang = pos.astype(F32)[:, None] * inv_freq[None, :]
    shape = (1, pos.shape[0]) + (1,) * (x.ndim - 3) + (half,)
    cos = jnp.cos(ang).reshape(shape)
    sin = jnp.sin(ang).reshape(shape)
    xf = x.astype(F32)
    x1, x2 = xf[..., :half], xf[..., half:ROT_DIM]
    out = jnp.concatenate([x1 * cos - x2 * sin, x2 * cos + x1 * sin, xf[..., ROT_DIM:]], axis=-1)
    return out.astype(x.dtype)


def _masked_softmax(s, mask):
    s = jnp.where(mask, s.astype(F32), NEG_INF)
    m = jnp.max(s, axis=-1, keepdims=True)
    e = jnp.where(mask, jnp.exp(s - m), 0.0)
    l = jnp.maximum(jnp.sum(e, axis=-1, keepdims=True), TINY)
    return e / l, (m + jnp.log(l))[..., 0]


def _paged_rows(cache, page_table):
    g = cache[page_table]
    return g.reshape((g.shape[0], g.shape[1] * g.shape[2]) + g.shape[3:])


def _roll_buffer(buf, new):
    n_buf, t = buf.shape[1], new.shape[1]
    if t >= n_buf:
        return new[:, t - n_buf:]
    return jnp.concatenate([buf[:, t:], new], axis=1)


def _gather_rows(buf, new, idx):
    n_buf = buf.shape[1]
    from_buf = buf[:, np.clip(idx, 0, n_buf - 1)]
    from_new = new[:, np.clip(idx - n_buf, 0, new.shape[1] - 1)]
    sel = (idx < n_buf).reshape(idx.shape + (1,) * (buf.ndim - 2))
    return jnp.where(sel, from_buf, from_new)


def _banded_attn(q, k, v, band):
    n, L, g, r, dh = q.shape
    blk = math.gcd(L, Q_BLOCK)
    nb = L // blk
    pad = ((0, 0), (band, 0), (0, 0), (0, 0))
    idx = np.arange(nb)[:, None] * blk + np.arange(blk + band)[None, :]
    kb = jnp.pad(k, pad)[:, idx]
    vb = jnp.pad(v, pad)[:, idx]
    qb = q.reshape(n, nb, blk, g, r, dh)
    s = jnp.einsum('nbqgrd,nbkgd->nbgrqk', qb, kb, preferred_element_type=F32) * SCALE
    qpos = np.arange(nb)[:, None] * blk + np.arange(blk)[None, :]
    kpos = idx - band
    dist = qpos[:, :, None] - kpos[:, None, :]
    mask = (dist >= 0) & (dist <= band) & (kpos[:, None, :] >= 0)
    p, lse = _masked_softmax(s, mask[None, :, None, None])
    o = jnp.einsum('nbgrqk,nbkgd->nbqgrd', p, vb.astype(F32))
    return o.reshape(n, L, g, r, dh).astype(q.dtype), lse.transpose(0, 1, 4, 2, 3).reshape(n, L, g, r)


def _diff_heads(qa, ka, va, pos, g_q, g_k):
    n, t = qa.shape[:2]
    q = _partial_rope(_rms_norm(qa.reshape(n, t, A_HEADS, 2, HEAD_DIM), g_q), pos)
    k = _partial_rope(_rms_norm(ka.reshape(n, t, A_HEADS, 2, HEAD_DIM), g_k), pos)
    return q, k, va.reshape(n, t, A_HEADS, A_VDIM)


def _diff_core(q, k, v, qpos, kpos, lam):
    s = jnp.einsum('nqhmd,nkhmd->nhmqk', q, k, preferred_element_type=F32) * SCALE
    p, _ = _masked_softmax(s, (kpos[None, :] <= qpos[:, None])[None, None, None])
    a = p[:, :, 0] - lam * p[:, :, 1]
    return jnp.einsum('nhqk,nkhe->nqhe', a, v.astype(F32)).astype(v.dtype)


def _diff_attn_prompt(q, k, v, pos, lam):
    n, s = q.shape[:2]
    nb = s // Q_BLOCK
    qb = q.reshape((n, nb, Q_BLOCK) + q.shape[2:]).swapaxes(0, 1)
    ob = lax.map(lambda a: _diff_core(a[0], k, v, a[1], pos, lam), (qb, pos.reshape(nb, Q_BLOCK)))
    return ob.swapaxes(0, 1).reshape(n, s, A_HEADS, A_VDIM)


def _diff_output(o, g_sub, lam_init):
    n, t = o.shape[:2]
    return (_rms_norm(o, g_sub) * (1.0 - lam_init)).reshape(n, t, A_V_W)


def _nsa_heads(qb, kvb, gb, pos, g_q, g_k):
    n, t = qb.shape[:2]
    q = _rms_norm(qb.reshape(n, t, NSA_KV_HEADS, NSA_REP, HEAD_DIM), g_q)
    q_rot = _partial_rope(q, pos)
    kv = kvb.reshape(n, t, 6, NSA_KV_HEADS, HEAD_DIM)
    k_slc = _partial_rope(_rms_norm(kv[:, :, 2], g_k[1]), pos)
    k_win = _partial_rope(_rms_norm(kv[:, :, 4], g_k[2]), pos)
    long_rows = jnp.stack([kv[:, :, 0], kv[:, :, 1], k_slc, kv[:, :, 3]], axis=2)
    win_rows = jnp.stack([k_win, kv[:, :, 5]], axis=2)
    gates = jax.nn.sigmoid(gb.astype(F32)).reshape(n, t, NSA_KV_HEADS, NSA_REP, 3)
    return q, q_rot, long_rows, win_rows, gates


def _nsa_compress(rows, pe, w1, w2):
    n, L, g, dh = rows.shape
    n_cmp = (L - CMP_LEN) // CMP_STRIDE + 1
    idx = np.arange(n_cmp)[:, None] * CMP_STRIDE + np.arange(CMP_LEN)[None, :]
    blocks = rows[:, idx] + pe[None, None, :, None, :]
    flat = blocks.transpose(0, 1, 3, 2, 4).reshape(n, n_cmp, g, CMP_LEN * dh)
    return jax.nn.silu(flat @ w1) @ w2


def _nsa_cmp_attn(q, k_cmp, v_cmp, qpos):
    n_cmp = k_cmp.shape[1]
    end = jnp.asarray(np.arange(n_cmp) * CMP_STRIDE + CMP_LEN - 1)
    s = jnp.einsum('nqgrd,ncgd->nqgrc', q, k_cmp, preferred_element_type=F32) * SCALE
    visible = end[None, :] <= qpos[:, None]
    p, _ = _masked_softmax(s, visible[None, :, None, None, :])
    o = jnp.einsum('nqgrc,ncgd->nqgrd', p, v_cmp.astype(F32)).astype(q.dtype)
    return o, p


def _cmp_to_sel_overlap(n_cmp, n_sel):
    c0 = np.arange(n_cmp)[:, None] * CMP_STRIDE
    s0 = np.arange(n_sel)[None, :] * SEL_BLOCK
    ov = np.minimum(c0 + CMP_LEN, s0 + SEL_BLOCK) - np.maximum(c0, s0)
    return jnp.asarray(np.maximum(ov, 0) / CMP_LEN, dtype=F32)


def _nsa_select(p_cmp, qpos, n_sel):
    imp = jnp.einsum('nqgrc,cj->nqgj', p_cmp, _cmp_to_sel_overlap(p_cmp.shape[-1], n_sel))
    blk = jnp.arange(n_sel)[None, :]
    cur = (qpos // SEL_BLOCK)[:, None]
    valid = blk <= cur
    forced = (blk == 0) | (blk == cur) | (blk == cur - 1)
    score = jnp.where(valid[None, :, None], imp + jnp.where(forced, FORCE_BONUS, 0.0)[None, :, None], NEG_INF)
    _, sel = lax.top_k(score, min(SEL_TOPK, n_sel))
    return sel


def _nsa_sel_attn(q, k_blk, v_blk, sel, qpos):
    n, qc, g, r, dh = q.shape
    kk = sel.shape[-1]
    n_i = jnp.arange(n)[:, None, None, None]
    g_i = jnp.arange(g)[None, None, :, None]
    kg = k_blk[n_i, g_i, sel]
    vg = v_blk[n_i, g_i, sel]
    kpos = sel[..., None] * SEL_BLOCK + jnp.arange(SEL_BLOCK)
    visible = (kpos <= qpos[None, :, None, None, None]).reshape(n, qc, g, 1, kk * SEL_BLOCK)
    s = jnp.einsum('nqgrd,nqgkbd->nqgrkb', q, kg, preferred_element_type=F32)
    p, _ = _masked_softmax(s.reshape(n, qc, g, r, kk * SEL_BLOCK) * SCALE, visible)
    o = jnp.einsum('nqgrx,nqgxd->nqgrd', p, vg.reshape(n, qc, g, kk * SEL_BLOCK, dh).astype(F32))
    return o.astype(q.dtype)


def _nsa_long_branches(q, q_rot, long_all, qpos, g_kc, pe_k, w_k1, w_k2, pe_v, w_v1, w_v2):
    n, L, _, g, dh = long_all.shape
    k_cmp = _rms_norm(_nsa_compress(long_all[:, :, 0], pe_k, w_k1, w_k2), g_kc)
    v_cmp = _nsa_compress(long_all[:, :, 1], pe_v, w_v1, w_v2)
    o_cmp, p_cmp = _nsa_cmp_attn(q, k_cmp, v_cmp, qpos)
    n_sel = -(-L // SEL_BLOCK)
    sel = _nsa_select(p_cmp, qpos, n_sel)

    def to_blocks(x):
        x = jnp.pad(x, ((0, 0), (0, n_sel * SEL_BLOCK - L), (0, 0), (0, 0)))
        return x.reshape(n, n_sel, SEL_BLOCK, g, dh).transpose(0, 3, 1, 2, 4)

    k_blk, v_blk = to_blocks(long_all[:, :, 2]), to_blocks(long_all[:, :, 3])
    nq = q.shape[1]
    qc = math.gcd(nq, SEL_Q_BLOCK)
    nc = nq // qc

    def chunks(x):
        return x.reshape((n, nc, qc) + x.shape[2:]).swapaxes(0, 1)

    o_sel = lax.map(lambda a: _nsa_sel_attn(a[0], k_blk, v_blk, a[1], a[2]),
                    (chunks(q_rot), chunks(sel), qpos.reshape(nc, qc)))
    return o_cmp, o_sel.swapaxes(0, 1).reshape(q.shape)


def _window_attn_sample(q, k_all, v_all, n_buf, window):
    t = q.shape[1]
    dist = (n_buf + np.arange(t))[:, None] - np.arange(n_buf + t)[None, :]
    visible = (dist >= 0) & (dist <= window)
    s = jnp.einsum('ntgrd,nkgd->ntgrk', q, k_all, preferred_element_type=F32) * SCALE
    p, _ = _masked_softmax(s, visible[None, :, None, None, :])
    return jnp.einsum('ntgrk,nkgd->ntgrd', p, v_all.astype(F32)).astype(q.dtype)


def _nsa_merge(gates, o_cmp, o_sel, o_win):
    o = gates[..., 0:1] * o_cmp.astype(F32) + gates[..., 1:2] * o_sel.astype(F32) + gates[..., 2:3] * o_win.astype(F32)
    n, t = o.shape[:2]
    return o.reshape(n, t, NSA_Q_W).astype(o_cmp.dtype)


def _split_in0(proj):
    sizes = [A_QK_W, A_QK_W, A_V_W, NSA_Q_W, NSA_KV_W, NSA_GATE_W]
    return jnp.split(proj, [int(o) for o in np.cumsum(sizes)[:-1]], axis=-1)


def _even_mixer_prompt(proj, pos, mw):
    g_qa, g_ka, lam, lam_init, g_subln, g_qb, g_kb, cmp_w = mw
    n, s = proj.shape[:2]
    qa, ka, va, qb, kvb, gb = _split_in0(proj)
    q, k, v = _diff_heads(qa, ka, va, pos, g_qa, g_ka)
    o_a = _diff_output(_diff_attn_prompt(q, k, v, pos, lam), g_subln, lam_init)
    qn, qr, long_rows, win_rows, gates = _nsa_heads(qb, kvb, gb, pos, g_qb, g_kb)
    o_cmp, o_sel = _nsa_long_branches(qn, qr, long_rows, pos, g_kb[0], *cmp_w)
    o_win, _ = _banded_attn(qr, win_rows[:, :, 0], win_rows[:, :, 1], NSA_WINDOW)
    o_b = _nsa_merge(gates, o_cmp, o_sel, o_win)
    a_rows = jnp.stack([k.reshape(n, s, A_HEADS, A_VDIM), v], axis=2)
    return jnp.concatenate([o_a, o_b], axis=-1), a_rows, long_rows, win_rows[:, s - min(NSA_WINDOW, s):]


def _even_mixer_sample(proj, pos, cache_a_kv, cache_nsa_kv, state_nsa_win, page_table, mw):
    g_qa, g_ka, lam, lam_init, g_subln, g_qb, g_kb, cmp_w = mw
    n, t = proj.shape[:2]
    qa, ka, va, qb, kvb, gb = _split_in0(proj)
    q, k, v = _diff_heads(qa, ka, va, pos, g_qa, g_ka)
    a_rows = jnp.stack([k.reshape(n, t, A_HEADS, A_VDIM), v], axis=2)
    a_all = jnp.concatenate([_paged_rows(cache_a_kv, page_table), a_rows], axis=1)
    L = a_all.shape[1]
    o = _diff_core(q, a_all[:, :, 0].reshape(n, L, A_HEADS, 2, HEAD_DIM), a_all[:, :, 1], pos,
                   jnp.arange(L, dtype=jnp.int32), lam)
    o_a = _diff_output(o, g_subln, lam_init)
    qn, qr, long_rows, win_rows, gates = _nsa_heads(qb, kvb, gb, pos, g_qb, g_kb)
    long_all = jnp.concatenate([_paged_rows(cache_nsa_kv, page_table), long_rows], axis=1)
    o_cmp, o_sel = _nsa_long_branches(qn, qr, long_all, pos, g_kb[0], *cmp_w)
    n_buf = state_nsa_win.shape[1]
    win_all = jnp.concatenate([state_nsa_win, win_rows], axis=1)
    o_win = _window_attn_sample(qr, win_all[:, :, 0], win_all[:, :, 1], n_buf, NSA_WINDOW)
    o_b = _nsa_merge(gates, o_cmp, o_sel, o_win)
    return jnp.concatenate([o_a, o_b], axis=-1), a_rows, long_rows, _roll_buffer(state_nsa_win, win_rows)


def _dilated_heads(proj, pos, g_qc, g_kc):
    n, t = proj.shape[:2]
    proj = proj.reshape(n, t, N_C_GROUPS, 3, C_HEADS, HEAD_DIM)
    return [(_partial_rope(_rms_norm(proj[:, :, gi, 0], g_qc[gi]), pos),
             _partial_rope(_rms_norm(proj[:, :, gi, 1], g_kc[gi]), pos),
             proj[:, :, gi, 2]) for gi in range(N_C_GROUPS)]


def _dilated_attn_prompt(q, k, v, dil, band):
    n, S, h, dh = q.shape
    L = S // dil

    def sub(x):
        return x.reshape(n, L, dil, h, dh).transpose(0, 2, 1, 3, 4).reshape(n * dil, L, h, dh)

    o, lse = _banded_attn(sub(q)[:, :, :, None], sub(k), sub(v), band)
    o = o.reshape(n, dil, L, h, dh).transpose(0, 2, 1, 3, 4).reshape(n, S, h, dh)
    lse = lse.reshape(n, dil, L, h).transpose(0, 2, 1, 3).reshape(n, S, h)
    return o, lse


def _dilated_attn_sample(q, buf, new_rows, dil, window):
    n_buf, t = buf.shape[1], q.shape[1]
    n_keys = window // dil + 1
    idx = n_buf + np.arange(t)[:, None] - dil * np.arange(n_keys)[None, :]
    rows = _gather_rows(buf, new_rows, idx)
    s = jnp.einsum('nthd,ntkhd->nthk', q, rows[:, :, :, 0], preferred_element_type=F32) * SCALE
    p, lse = _masked_softmax(s, (idx >= 0)[None, :, None, :])
    o = jnp.einsum('nthk,ntkhd->nthd', p, rows[:, :, :, 1].astype(F32))
    return o.astype(q.dtype), lse


def _merge_dilations(outs, lses):
    w = jax.nn.softmax(jnp.stack(lses, axis=0), axis=0)
    o = jnp.einsum('gnth,gnthd->nthd', w, jnp.stack(outs, axis=0).astype(F32))
    return o.astype(outs[0].dtype)


def _odd_mixer_prompt(proj, pos, g_qc, g_kc):
    n, s = proj.shape[:2]
    outs, lses, bufs = [], [], []
    for (window, dil), (q, k, v) in zip(C_GROUPS, _dilated_heads(proj, pos, g_qc, g_kc)):
        o, lse = _dilated_attn_prompt(q, k, v, dil, window // dil)
        outs.append(o)
        lses.append(lse)
        bufs.append(jnp.stack([k, v], axis=2)[:, s - min(window, s):])
    return _merge_dilations(outs, lses).reshape(n, s, C_W), bufs


def _odd_mixer_sample(proj, pos, states, g_qc, g_kc):
    n, t = proj.shape[:2]
    outs, lses, bufs = [], [], []
    for (window, dil), (q, k, v), buf in zip(C_GROUPS, _dilated_heads(proj, pos, g_qc, g_kc), states):
        new_rows = jnp.stack([k, v], axis=2)
        o, lse = _dilated_attn_sample(q, buf, new_rows, dil, window)
        outs.append(o)
        lses.append(lse)
        bufs.append(_roll_buffer(buf, new_rows))
    return _merge_dilations(outs, lses).reshape(n, t, C_W), bufs


def _router_body(x_ref, w_ref, idx_ref, gate_ref):
    logits = jnp.dot(x_ref[...], w_ref[...], preferred_element_type=F32)
    lane = lax.broadcasted_iota(jnp.int32, logits.shape, 1).astype(F32)
    lg = jnp.where(lane < N_EXPERTS, logits, NEG_INF)
    v1 = jnp.max(lg, axis=-1, keepdims=True)
    i1 = jnp.min(jnp.where(lg == v1, lane, 128.0), axis=-1, keepdims=True)
    lg2 = jnp.where(lane == i1, NEG_INF, lg)
    v2 = jnp.max(lg2, axis=-1, keepdims=True)
    i2 = jnp.min(jnp.where(lg2 == v2, lane, 128.0), axis=-1, keepdims=True)
    e = jnp.exp(v2 - v1)
    idx_ref[...] = jnp.where(lane == 0, i1, jnp.where(lane == 1, i2, 0.0)).astype(jnp.int32)
    gate_ref[...] = jnp.where(lane == 0, 1.0 / (1.0 + e), jnp.where(lane == 1, e / (1.0 + e), 0.0))


def moe_router(h_bf16, w_router):
    m, d = h_bf16.shape
    tm = TOKEN_TILE
    w = jnp.pad(w_router, ((0, 0), (0, 128 - w_router.shape[1]))).astype(BF16)
    spec = pl.BlockSpec((tm, 128), lambda i: (i, 0))
    return pl.pallas_call(
        _router_body,
        grid=(m // tm,),
        in_specs=[pl.BlockSpec((tm, d), lambda i: (i, 0)), pl.BlockSpec((d, 128), lambda i: (0, 0))],
        out_specs=[spec, spec],
        out_shape=[jax.ShapeDtypeStruct((m, 128), jnp.int32), jax.ShapeDtypeStruct((m, 128), F32)],
        compiler_params=_cparams("parallel"),
        name="moe_router",
    )(h_bf16, w)


def _moe_combine_body(r_ref, g_ref, y0_ref, y1_ref, o_ref):
    g = g_ref[...]
    o_ref[...] = r_ref[...] + g[:, 0:1] * y0_ref[...] + g[:, 1:2] * y1_ref[...]


def moe_combine(resid, gates, y0, y1):
    m, d = resid.shape
    tm = TOKEN_TILE
    spec = pl.BlockSpec((tm, d), lambda i: (i, 0))
    return pl.pallas_call(
        _moe_combine_body,
        grid=(m // tm,),
        in_specs=[spec, pl.BlockSpec((tm, 128), lambda i: (i, 0)), spec, spec],
        out_specs=spec,
        out_shape=jax.ShapeDtypeStruct((m, d), F32),
        compiler_params=_cparams("parallel"),
        name="moe_combine",
    )(resid, gates, y0, y1)


def _moe(h_bf16, resid, w_router, wg, wu, wd):
    m, d = h_bf16.shape
    tm = TOKEN_TILE
    idx, gates = moe_router(h_bf16, w_router)
    top_i = idx[:, :TOP_K]
    flat_e = top_i.reshape(-1)
    order = jnp.argsort(flat_e, stable=True)
    counts = jnp.bincount(flat_e, length=N_EXPERTS)
    padded = ((counts + tm - 1) // tm) * tm
    pstart = jnp.cumsum(padded) - padded
    cstart = jnp.cumsum(counts) - counts
    sorted_e = flat_e[order]
    rank = jnp.arange(m * TOP_K) - cstart[sorted_e]
    dest = pstart[sorted_e] + rank
    p_rows = m * TOP_K + N_EXPERTS * tm
    src_tok = jnp.zeros((p_rows,), jnp.int32).at[dest].set((order // TOP_K).astype(jnp.int32))
    valid = jnp.zeros((p_rows,), jnp.bool_).at[dest].set(True)
    xs = jnp.where(valid[:, None], h_bf16[src_tok], jnp.zeros((), BF16))
    tile_start = jnp.arange(p_rows // tm) * tm
    pend = jnp.cumsum(padded)
    tile_expert = jnp.minimum(jnp.sum(tile_start[:, None] >= pend[None, :], axis=1), N_EXPERTS - 1).astype(jnp.int32)
    ys = moe_grouped_ffn(xs, tile_expert, wg, wu, wd, tf=1792)
    slot_pos = jnp.zeros((m * TOP_K,), jnp.int32).at[order].set(dest.astype(jnp.int32)).reshape(m, TOP_K)
    return moe_combine(resid, gates, ys[slot_pos[:, 0]], ys[slot_pos[:, 1]])


COL_QA, COL_KA, COL_VA, COL_QB = 0, A_QK_W, 2 * A_QK_W, 2 * A_QK_W + A_V_W
COL_KVB = COL_QB + NSA_Q_W
COL_GATE = COL_KVB + NSA_KV_W
KVB_PAIR = NSA_KV_HEADS * HEAD_DIM


def _pad_axis(x, axis, size):
    pad = [(0, 0)] * x.ndim
    pad[axis] = (0, size - x.shape[axis])
    return jnp.pad(x, pad)


def even_mixer(proj, proj16, tables, nb, seq, db, dt, caches, mw):
    g_qa, g_ka, lam_rows, lam_init, g_subln, g_qb, g_kb, cmp_w = mw
    cache_a_kv, cache_nsa_kv, state_nsa_win, page_table = caches
    pe_k, w_k1, w_k2, pe_v, w_v1, w_v2 = cmp_w
    mp = nb * seq
    ms = db * dt
    assert dt == 4
    rope16_32 = ((True, BF16), (True, F32))
    qk_a16, qk_a32 = head_norm_rope(proj, jnp.stack([_head_gain(g_qa, A_QK_W), _head_gain(g_ka, A_QK_W)]), tables,
                                    width=A_QK_W, col0=0, outs=rope16_32, name="hnr_diff_qk")
    qn16, qr16 = head_norm_rope(proj, _head_gain(g_qb, NSA_Q_W)[None], tables, width=NSA_Q_W,
                                col0=COL_QB // NSA_Q_W, outs=((False, BF16), (True, BF16)), name="hnr_nsa_q")
    ks16, ks32 = head_norm_rope(proj, _head_gain(g_kb[1], KVB_PAIR)[None], tables, width=KVB_PAIR,
                                col0=(COL_KVB + 2 * KVB_PAIR) // KVB_PAIR, outs=rope16_32, name="hnr_nsa_kslc")
    kw16, kw32 = head_norm_rope(proj, _head_gain(g_kb[2], KVB_PAIR)[None], tables, width=KVB_PAIR,
                                col0=(COL_KVB + 4 * KVB_PAIR) // KVB_PAIR, outs=rope16_32, name="hnr_nsa_kwin")

    o_a = diff_attention_prompt(qk_a16, proj16, lam_rows, g_subln, nb=nb, seq=seq, lam_init=lam_init)

    nchunk = seq // CMP_STRIDE

    def chunks(col):
        xc = proj16[:mp, col:col + KVB_PAIR].reshape(nb, nchunk, CMP_STRIDE, NSA_KV_HEADS, HEAD_DIM)
        return xc.transpose(0, 3, 1, 2, 4).reshape(nb * NSA_KV_HEADS, nchunk, CMP_STRIDE * HEAD_DIM)

    def pair_lanes(c):
        return c.reshape(nb, NSA_KV_HEADS, nchunk, HEAD_DIM).transpose(0, 2, 1, 3).reshape(nb, nchunk, KVB_PAIR)

    k_cmp = pair_lanes(compress_blocks(chunks(COL_KVB), pe_k, w_k1, w_k2, g_kb[0]))
    v_cmp = pair_lanes(compress_blocks(chunks(COL_KVB + KVB_PAIR), pe_v, w_v1, w_v2, None))
    o_cmp, sel = nsa_compressed_prompt(qn16, k_cmp, v_cmp, nb=nb, seq=seq)
    o_sel = nsa_branch_prompt(qr16, ks16, 0, proj16, (COL_KVB + 3 * KVB_PAIR) // KVB_PAIR, sel,
                              nb=nb, seq=seq, band=None, name="nsa_sel_prompt")
    o_win = nsa_branch_prompt(qr16, kw16, 0, proj16, (COL_KVB + 5 * KVB_PAIR) // KVB_PAIR, None,
                              nb=nb, seq=seq, band=NSA_WINDOW, name="nsa_win_prompt")
    a_rows = jnp.concatenate([qk_a32[:, A_QK_W:], proj[:, COL_VA:COL_VA + A_V_W]], axis=1)
    long_rows = jnp.concatenate([proj[:, COL_KVB:COL_KVB + 2 * KVB_PAIR], ks32,
                                 proj[:, COL_KVB + 3 * KVB_PAIR:COL_KVB + 4 * KVB_PAIR]], axis=1)
    win_rows = jnp.concatenate([kw32, proj[:, COL_KVB + 5 * KVB_PAIR:COL_KVB + 6 * KVB_PAIR]], axis=1)

    past = page_table.shape[1] * cache_a_kv.shape[1]
    q_s = qk_a32[mp:, :A_QK_W].reshape(db, dt, A_HEADS, 128).transpose(0, 2, 1, 3).reshape(db, A_HEADS * dt, 128)
    new_page = _pad_axis(a_rows[mp:].reshape(db, dt * 2 * A_HEADS, 128), 1, 128)
    cache_rows = cache_a_kv.reshape(cache_a_kv.shape[0], -1, A_VDIM)
    y_s = diff_attention_sample(jnp.concatenate([q_s, q_s], axis=1), new_page, cache_rows, page_table, lam_rows,
                                g_subln, lam_init=lam_init)
    o_a_s = y_s.reshape(db, A_HEADS, dt, A_VDIM).transpose(0, 2, 1, 3).reshape(ms, A_V_W).astype(BF16)

    pool = cache_nsa_kv.shape[0]
    xc = jnp.transpose(cache_nsa_kv[:, :, :2], (2, 0, 3, 1, 4)).astype(BF16)
    xc = xc.reshape(2, pool * NSA_KV_HEADS * (cache_nsa_kv.shape[1] // CMP_STRIDE), CMP_STRIDE * HEAD_DIM)
    ab = jnp.stack([matmul(xc[c], _w1_ab(w1), tn=2 * w1.shape[1], name="compress_cache")
                    for c, w1 in enumerate((w_k1, w_v1))])
    ab = ab.reshape(2, pool, NSA_KV_HEADS, cache_nsa_kv.shape[1] // CMP_STRIDE, ab.shape[-1])

    def sample_q(q16):
        qq = q16[mp:].reshape(db, dt, NSA_KV_HEADS, NSA_REP, HEAD_DIM).transpose(0, 2, 3, 1, 4)
        return _pad_axis(qq, 3, 8).reshape(db, NSA_KV_HEADS, NSA_REP * 8, HEAD_DIM)

    def sample_kv(x):
        return x.reshape(db, dt, NSA_KV_HEADS, HEAD_DIM).transpose(0, 2, 1, 3)

    new_rows = jnp.stack([sample_kv(ks32[mp:]), sample_kv(proj[mp:, COL_KVB + 3 * KVB_PAIR:COL_KVB + 4 * KVB_PAIR]),
                          sample_kv(kw32[mp:]), sample_kv(proj[mp:, COL_KVB + 5 * KVB_PAIR:COL_KVB + 6 * KVB_PAIR])],
                         axis=1)
    o_cmp_s, o_sel_s, o_win_s, win_state = nsa_sample(
        sample_q(qn16), sample_q(qr16), _pad_axis(new_rows, 3, 128), jnp.transpose(cache_nsa_kv, (0, 2, 3, 4, 1)),
        ab, jnp.transpose(state_nsa_win, (0, 2, 3, 4, 1)), page_table, cmp_w, g_kb[0], past=past)

    def sample_o(o):
        oo = o.reshape(db, NSA_KV_HEADS, NSA_REP, 8, HEAD_DIM)[:, :, :, :dt]
        return oo.transpose(0, 3, 1, 2, 4).reshape(ms, NSA_Q_W).astype(BF16)

    o_b = nsa_merge(proj, COL_GATE // 128, jnp.concatenate([o_cmp, sample_o(o_cmp_s)]),
                    jnp.concatenate([o_sel, sample_o(o_sel_s)]), jnp.concatenate([o_win, sample_o(o_win_s)]))
    cat = jnp.concatenate([jnp.concatenate([o_a, o_a_s]), o_b], axis=1)

    keep = min(NSA_WINDOW, seq)
    return (cat,
            a_rows[:mp].reshape(nb, seq, 2, A_HEADS, A_VDIM), a_rows[mp:].reshape(db, dt, 2, A_HEADS, A_VDIM),
            long_rows[:mp].reshape(nb, seq, 4, NSA_KV_HEADS, HEAD_DIM),
            long_rows[mp:].reshape(db, dt, 4, NSA_KV_HEADS, HEAD_DIM),
            win_rows[:mp].reshape(nb, seq, 2, NSA_KV_HEADS, HEAD_DIM)[:, seq - keep:],
            jnp.transpose(win_state, (0, 4, 1, 2, 3)))


def odd_mixer(proj, proj16, tables, nb, seq, db, dt, states, g_qc, g_kc):
    mp = nb * seq
    ms = db * dt
    outs, lses, bufs_p, bufs_s = [], [], [], []
    for gi, ((window, dil), state) in enumerate(zip(C_GROUPS, states)):
        gains = jnp.stack([_head_gain(g_qc[gi], C_W), _head_gain(g_kc[gi], C_W)])
        qk16, qk32 = head_norm_rope(proj, gains, tables, width=C_W, col0=3 * gi,
                                    outs=((True, BF16), (True, F32)), name=f"hnr_dil_{dil}")
        v32 = proj[:, (3 * gi + 2) * C_W:(3 * gi + 3) * C_W]
        sub = seq // dil
        o_p, lse_p = dilated_attention_prompt(qk16, proj16, nb=nb, seq=seq, dil=dil, band=window // dil, gi=gi,
                                              t=min(sub, 256 if dil == 1 else 128))
        kv = jnp.concatenate([qk32[:mp, C_W:], v32[:mp]], axis=1)
        bufs_p.append(kv.reshape(nb, seq, 2, C_HEADS, HEAD_DIM)[:, seq - min(window, seq):])

        tok3 = lambda x: x.reshape(db, dt, C_W)
        oext, rolled = dilated_attention_sample(
            _pad_axis(tok3(qk32[mp:, :C_W]), 1, 8), _pad_axis(tok3(qk32[mp:, C_W:]), 1, 128),
            _pad_axis(tok3(v32[mp:]), 1, 128), jnp.transpose(state, (0, 2, 3, 4, 1)), window=window, dil=dil)
        o_s = oext[:, :, :dt, :HEAD_DIM].transpose(0, 2, 1, 3).reshape(ms, C_W).astype(BF16)
        lse_s = jnp.repeat(oext[:, :, :dt, HEAD_DIM].transpose(0, 2, 1).reshape(ms, C_HEADS), HEAD_DIM, axis=1)
        outs.append(jnp.concatenate([o_p, o_s]))
        lses.append(jnp.concatenate([lse_p, lse_s]))
        bufs_s.append(jnp.transpose(rolled, (0, 4, 1, 2, 3)))
    return dilation_merge(outs, lses), bufs_p, bufs_s


def odd_mixer_prompt(proj, proj16, tables, nb, seq, g_qc, g_kc):
    mp = nb * seq
    outs, lses, bufs = [], [], []
    for gi, (window, dil) in enumerate(C_GROUPS):
        gains = jnp.stack([_head_gain(g_qc[gi], C_W), _head_gain(g_kc[gi], C_W)])
        qk16, qk32 = head_norm_rope(proj, gains, tables, width=C_W, col0=3 * gi,
                                    outs=((True, BF16), (True, F32)), name=f"hnr_dil_{dil}")
        sub = seq // dil
        o, lse = dilated_attention_prompt(qk16, proj16, nb=nb, seq=seq, dil=dil, band=window // dil, gi=gi,
                                          t=min(sub, 256 if dil == 1 else 128))
        outs.append(o)
        lses.append(lse)
        kv = jnp.concatenate([qk32[:mp, C_W:], proj[:mp, (3 * gi + 2) * C_W:(3 * gi + 3) * C_W]], axis=1)
        bufs.append(kv.reshape(nb, seq, 2, C_HEADS, HEAD_DIM)[:, seq - min(window, seq):])
    return dilation_merge(outs, lses), bufs


def kernel(x_prompt, x_sample, cache_a_kv, cache_nsa_kv, state_nsa_win, state_c_w128, state_c_w512, state_c_w2048, page_table, norm0_mix, w_in0, g_qa, g_ka, lam_q1, lam_k1, lam_q2, lam_k2, g_subln, g_qb, g_kb, pe_cmp_k, w_cmp_k1, w_cmp_k2, pe_cmp_v, w_cmp_v1, w_cmp_v2, w_out0, norm0_ffn, w_ffn_gate, w_ffn_up, w_ffn_down, norm1_mix, w_in1, g_qc, g_kc, w_out1, norm1_ffn, w_router, w_moe_gate, w_moe_up, w_moe_down):
    nb, seq, d = x_prompt.shape
    db, dt, _ = x_sample.shape
    past = page_table.shape[1] * cache_a_kv.shape[1]
    mp = nb * seq
    ms = db * dt
    pos_p = jnp.arange(seq, dtype=jnp.int32)
    pos_s = past + jnp.arange(dt, dtype=jnp.int32)
    x = jnp.concatenate([x_prompt.reshape(mp, d), x_sample.reshape(ms, d)], axis=0)

    in0_w = w_in0.shape[1]
    in0_pad = -(-in0_w // 128) * 128
    w_in0_b = jnp.pad(w_in0, ((0, 0), (0, in0_pad - in0_w))).astype(BF16)
    tables = rope_tables(jnp.concatenate([jnp.tile(pos_p, nb), jnp.tile(pos_s, db)]))
    proj0, proj0_16 = matmul_dual(rmsnorm_cast(x, norm0_mix), w_in0_b, tn=in0_pad, name="in_proj0")
    lam_init = 0.8 - 0.6 * math.exp(-0.3 * 0)
    f = lambda a: a.astype(F32)
    cmp_w = (pe_cmp_k, w_cmp_k1, w_cmp_k2, pe_cmp_v, w_cmp_v1, w_cmp_v2)
    lam_rows = jnp.zeros((8, 128), F32).at[:4, :HEAD_DIM].set(jnp.stack([f(lam_q1), f(lam_k1), f(lam_q2), f(lam_k2)]))
    cat, a_kv_p, a_kv_s, nsa_kv_p, nsa_kv_s, nsa_win_p, nsa_win_s = even_mixer(
        proj0, proj0_16, tables, nb, seq, db, dt, (cache_a_kv, cache_nsa_kv, state_nsa_win, page_table),
        (g_qa, g_ka, lam_rows, lam_init, g_subln, g_qb, g_kb, cmp_w))
    x = matmul(cat, w_out0.astype(BF16), tn=d, res=x, name="out_proj0")
    act = swiglu_gate_up(rmsnorm_cast(x, norm0_ffn), w_ffn_gate.astype(BF16), w_ffn_up.astype(BF16), tn=1408)
    x = matmul(act, w_ffn_down.astype(BF16), tn=d, res=x, name="ffn_down")

    in1_w = w_in1.shape[1]
    proj1, proj1_16 = matmul_dual(rmsnorm_cast(x, norm1_mix), w_in1.astype(BF16), tn=2304, name="in_proj1")
    mix, c_p, c_s = odd_mixer(proj1, proj1_16, tables, nb, seq, db, dt,
                              (state_c_w128, state_c_w512, state_c_w2048), g_qc, g_kc)
    x = matmul(mix, w_out1.astype(BF16), tn=d, res=x, name="out_proj1")
    x = _moe(rmsnorm_cast(x, norm1_ffn), x, w_router, w_moe_gate.astype(BF16), w_moe_up.astype(BF16),
             w_moe_down.astype(BF16))

    hp = x[:mp].reshape(nb, seq, d)
    hs = x[mp:].reshape(db, dt, d)
    return (hp, hs, a_kv_p, a_kv_s, nsa_kv_p, nsa_kv_s, nsa_win_p, nsa_win_s,
            c_p[0], c_s[0], c_p[1], c_s[1], c_p[2], c_s[2])
```

```python
import functools
import math

import jax
import jax.numpy as jnp
import numpy as np
from jax import lax
from jax.experimental import pallas as pl
from jax.experimental.pallas import tpu as pltpu

F32 = jnp.float32
BF16 = jnp.bfloat16

D_MODEL = 1024
HEAD_DIM = 64
ROT_DIM = HEAD_DIM // 4
ROPE_THETA = 500000.0
NORM_EPS = 1e-6
SCALE = HEAD_DIM ** -0.5
Q_BLOCK = 128
NEG_INF = -1e30
TINY = 1e-30
A_HEADS = 4
A_VDIM = 2 * HEAD_DIM
NSA_HEADS = 8
NSA_KV_HEADS = 2
NSA_REP = NSA_HEADS // NSA_KV_HEADS
CMP_LEN = 32
CMP_STRIDE = 16
SEL_BLOCK = 64
SEL_SHIFT = 6
SEL_TOPK = 16
SEL_Q_BLOCK = 64
NSA_WINDOW = 512
FORCE_BONUS = 1e3
C_HEADS = 16
C_GROUPS = ((128, 1), (512, 4), (2048, 16))
N_C_GROUPS = len(C_GROUPS)
A_QK_W = A_HEADS * 2 * HEAD_DIM
A_V_W = A_HEADS * A_VDIM
NSA_Q_W = NSA_HEADS * HEAD_DIM
NSA_KV_W = 6 * NSA_KV_HEADS * HEAD_DIM
NSA_GATE_W = 3 * NSA_HEADS
C_W = C_HEADS * HEAD_DIM
N_EXPERTS = 8
TOP_K = 2

VMEM_LIMIT_BYTES = 56 * 1024 * 1024
TOKEN_TILE = 512


def _cparams(*sem):
    return pltpu.CompilerParams(dimension_semantics=sem, vmem_limit_bytes=VMEM_LIMIT_BYTES)


def _rmsnorm_body(x_ref, g_ref, o_ref):
    x = x_ref[...]
    ms = jnp.mean(x * x, axis=-1, keepdims=True)
    o_ref[...] = (x * lax.rsqrt(ms + NORM_EPS) * g_ref[...]).astype(o_ref.dtype)


def rmsnorm_cast(x, g):
    m, d = x.shape
    tm = TOKEN_TILE
    return pl.pallas_call(
        _rmsnorm_body,
        grid=(m // tm,),
        in_specs=[pl.BlockSpec((tm, d), lambda i: (i, 0)), pl.BlockSpec((1, d), lambda i: (0, 0))],
        out_specs=pl.BlockSpec((tm, d), lambda i: (i, 0)),
        out_shape=jax.ShapeDtypeStruct((m, d), BF16),
        compiler_params=_cparams("parallel"),
        name="rmsnorm",
    )(x, g.reshape(1, d))


def _mm_body(x_ref, w_ref, o_ref):
    o_ref[...] = jnp.dot(x_ref[...], w_ref[...], preferred_element_type=F32).astype(o_ref.dtype)


def _mm_res_body(x_ref, w_ref, r_ref, o_ref):
    acc = jnp.dot(x_ref[...], w_ref[...], preferred_element_type=F32)
    o_ref[...] = (acc + r_ref[...]).astype(o_ref.dtype)


def matmul(x, w, *, tn, res=None, out_dtype=F32, name="matmul"):
    m, k = x.shape
    n = w.shape[1]
    tm = min(TOKEN_TILE, m)
    assert m % tm == 0 and n % tn == 0
    in_specs = [pl.BlockSpec((tm, k), lambda j, i: (i, 0)), pl.BlockSpec((k, tn), lambda j, i: (0, j))]
    args = [x, w]
    body = _mm_body
    if res is not None:
        in_specs.append(pl.BlockSpec((tm, tn), lambda j, i: (i, j)))
        args.append(res)
        body = _mm_res_body
    return pl.pallas_call(
        body,
        grid=(n // tn, m // tm),
        in_specs=in_specs,
        out_specs=pl.BlockSpec((tm, tn), lambda j, i: (i, j)),
        out_shape=jax.ShapeDtypeStruct((m, n), out_dtype),
        compiler_params=_cparams("parallel", "parallel"),
        name=name,
    )(*args)


def _gate_up_body(x_ref, wg_ref, wu_ref, o_ref):
    x = x_ref[...]
    g = jnp.dot(x, wg_ref[...], preferred_element_type=F32)
    u = jnp.dot(x, wu_ref[...], preferred_element_type=F32)
    o_ref[...] = (g * jax.nn.sigmoid(g) * u).astype(o_ref.dtype)


def swiglu_gate_up(x, wg, wu, *, tn):
    m, k = x.shape
    n = wg.shape[1]
    tm = TOKEN_TILE
    return pl.pallas_call(
        _gate_up_body,
        grid=(n // tn, m // tm),
        in_specs=[pl.BlockSpec((tm, k), lambda j, i: (i, 0)),
                  pl.BlockSpec((k, tn), lambda j, i: (0, j)),
                  pl.BlockSpec((k, tn), lambda j, i: (0, j))],
        out_specs=pl.BlockSpec((tm, tn), lambda j, i: (i, j)),
        out_shape=jax.ShapeDtypeStruct((m, n), BF16),
        compiler_params=_cparams("parallel", "parallel"),
        name="swiglu_gate_up",
    )(x, wg, wu)


def _moe_gate_up_body(te_ref, x_ref, wg_ref, wu_ref, o_ref):
    del te_ref
    x = x_ref[...]
    g = jnp.dot(x, wg_ref[...], preferred_element_type=F32)
    u = jnp.dot(x, wu_ref[...], preferred_element_type=F32)
    o_ref[...] = (g * jax.nn.sigmoid(g) * u).astype(o_ref.dtype)


def _moe_down_body(te_ref, a_ref, wd_ref, o_ref):
    del te_ref
    o_ref[...] = jnp.dot(a_ref[...], wd_ref[...], preferred_element_type=F32)


def moe_grouped_ffn(xs, tile_expert, wg, wu, wd, *, tf):
    p, d = xs.shape
    f = wg.shape[2]
    tm = TOKEN_TILE
    nt = p // tm
    act = pl.pallas_call(
        _moe_gate_up_body,
        grid_spec=pltpu.PrefetchScalarGridSpec(
            num_scalar_prefetch=1,
            grid=(f // tf, nt),
            in_specs=[pl.BlockSpec((tm, d), lambda j, i, te: (i, 0)),
                      pl.BlockSpec((None, d, tf), lambda j, i, te: (te[i], 0, j)),
                      pl.BlockSpec((None, d, tf), lambda j, i, te: (te[i], 0, j))],
            out_specs=pl.BlockSpec((tm, tf), lambda j, i, te: (i, j)),
        ),
        out_shape=jax.ShapeDtypeStruct((p, f), BF16),
        compiler_params=_cparams("parallel", "arbitrary"),
        name="moe_gate_up",
    )(tile_expert, xs, wg, wu)
    return pl.pallas_call(
        _moe_down_body,
        grid_spec=pltpu.PrefetchScalarGridSpec(
            num_scalar_prefetch=1,
            grid=(nt,),
            in_specs=[pl.BlockSpec((tm, f), lambda i, te: (i, 0)),
                      pl.BlockSpec((None, f, d), lambda i, te: (te[i], 0, 0))],
            out_specs=pl.BlockSpec((tm, d), lambda i, te: (i, 0)),
        ),
        out_shape=jax.ShapeDtypeStruct((p, d), F32),
        compiler_params=_cparams("arbitrary"),
        name="moe_down",
    )(tile_expert, act, wd)


def _mm2_body(x_ref, w_ref, o32_ref, o16_ref):
    acc = jnp.dot(x_ref[...], w_ref[...], preferred_element_type=F32)
    o32_ref[...] = acc
    o16_ref[...] = acc.astype(BF16)


def matmul_dual(x, w, *, tn, name):
    m, k = x.shape
    n = w.shape[1]
    tm = TOKEN_TILE
    return pl.pallas_call(
        _mm2_body,
        grid=(n // tn, m // tm),
        in_specs=[pl.BlockSpec((tm, k), lambda j, i: (i, 0)), pl.BlockSpec((k, tn), lambda j, i: (0, j))],
        out_specs=[pl.BlockSpec((tm, tn), lambda j, i: (i, j)), pl.BlockSpec((tm, tn), lambda j, i: (i, j))],
        out_shape=[jax.ShapeDtypeStruct((m, n), F32), jax.ShapeDtypeStruct((m, n), BF16)],
        compiler_params=_cparams("parallel", "parallel"),
        name=name,
    )(x, w)


def rope_tables(pos):
    half = ROT_DIM // 2
    inv_freq = ROPE_THETA ** (-jnp.arange(half, dtype=F32) / half)
    ang = pos.astype(F32)[:, None] * inv_freq[None, :]
    cos, sin = jnp.cos(ang), jnp.sin(ang)
    m = pos.shape[0]
    z_half = jnp.zeros((m, half), F32)
    z_rest = jnp.zeros((m, HEAD_DIM - ROT_DIM), F32)
    c = jnp.concatenate([cos, cos, jnp.ones((m, HEAD_DIM - ROT_DIM), F32)], axis=1)
    s1 = jnp.concatenate([z_half, sin, z_rest], axis=1)
    s2 = jnp.concatenate([-sin, z_half, z_rest], axis=1)
    return tuple(jnp.tile(a, (1, 128 // HEAD_DIM)) for a in (c, s1, s2))


def _hnr_body(x_ref, g_ref, c_ref, s1_ref, s2_ref, *o_refs, width, outs):
    tm = x_ref.shape[0]
    lo = lax.broadcasted_iota(jnp.int32, (tm, 128), 1) < HEAD_DIM
    c, s1, s2 = c_ref[...], s1_ref[...], s2_ref[...]
    for j in range(width // 128):
        sl = slice(j * 128, (j + 1) * 128)
        x = x_ref[:, sl]
        x2 = x * x
        s_lo = jnp.sum(jnp.where(lo, x2, 0.0), axis=-1, keepdims=True)
        s_hi = jnp.sum(jnp.where(lo, 0.0, x2), axis=-1, keepdims=True)
        ms = jnp.where(lo, s_lo, s_hi) * (1.0 / HEAD_DIM)
        xn = x * lax.rsqrt(ms + NORM_EPS) * g_ref[:, sl]
        xr = xn * c + pltpu.roll(xn, ROT_DIM // 2, 1) * s1 + pltpu.roll(xn, 128 - ROT_DIM // 2, 1) * s2
        for (rope, _), o_ref in zip(outs, o_refs):
            o_ref[:, sl] = (xr if rope else xn).astype(o_ref.dtype)


def head_norm_rope(x, gains, tables, *, width, col0, outs, name):
    m = x.shape[0]
    ncol = gains.shape[0]
    tm = TOKEN_TILE
    tab_spec = pl.BlockSpec((tm, 128), lambda i, j: (i, 0))
    return pl.pallas_call(
        functools.partial(_hnr_body, width=width, outs=outs),
        grid=(m // tm, ncol),
        in_specs=[pl.BlockSpec((tm, width), lambda i, j: (i, col0 + j)),
                  pl.BlockSpec((None, 1, width), lambda i, j: (j, 0, 0)),
                  tab_spec, tab_spec, tab_spec],
        out_specs=[pl.BlockSpec((tm, width), lambda i, j: (i, j)) for _ in outs],
        out_shape=[jax.ShapeDtypeStruct((m, ncol * width), dt) for _, dt in outs],
        compiler_params=_cparams("parallel", "parallel"),
        name=name,
    )(x, gains.reshape(ncol, 1, width), *tables)


def _head_gain(g, width):
    return jnp.tile(g.astype(F32), width // HEAD_DIM)


def _step_tables(nq, lookback):
    qi, ki, first, last = [], [], [], []
    for q in range(nq):
        ks = list(range(q + 1)) if lookback is None else [k for k in range(q - lookback, q + 1) if k >= 0]
        for n, k in enumerate(ks):
            qi.append(q)
            ki.append(k)
            first.append(int(n == 0))
            last.append(int(n == len(ks) - 1))
    return tuple(jnp.asarray(a, jnp.int32) for a in (qi, ki, first, last))


def _pos_mask(qi, ki, t, band):
    row = lax.broadcasted_iota(jnp.int32, (t, t), 0)
    col = lax.broadcasted_iota(jnp.int32, (t, t), 1)
    d = (qi - ki) * t + row - col
    mask = d >= 0
    if band is not None:
        mask = mask & (d <= band)
    return mask


def _nt_dot(a, b):
    return lax.dot_general(a, b, (((1,), (1,)), ((), ())), preferred_element_type=F32)


def _online_update(sc, mask, v, m_ref, l_ref, acc_ref, idx):
    sc = jnp.where(mask, sc, NEG_INF)
    m_old = m_ref[idx]
    m_new = jnp.maximum(m_old, jnp.max(sc, axis=-1, keepdims=True))
    alpha = jnp.exp(m_old - m_new)
    p = jnp.where(mask, jnp.exp(sc - m_new), 0.0)
    l_ref[idx] = alpha * l_ref[idx] + jnp.sum(p, axis=-1, keepdims=True)
    acc_ref[idx] = alpha * acc_ref[idx] + jnp.dot(p.astype(BF16), v, preferred_element_type=F32)
    m_ref[idx] = m_new


def _init_state(m_ref, l_ref, acc_ref):
    m_ref[...] = jnp.full(m_ref.shape, NEG_INF, F32)
    l_ref[...] = jnp.zeros(l_ref.shape, F32)
    acc_ref[...] = jnp.zeros(acc_ref.shape, F32)


def _split_pair(q_ref, qs_ref, hb, lo):
    q = q_ref[:, hb * 128:(hb + 1) * 128].astype(F32) * SCALE
    qs_ref[2 * hb] = jnp.where(lo, q, 0.0).astype(BF16)
    qs_ref[2 * hb + 1] = jnp.where(lo, 0.0, q).astype(BF16)


def _gqa_query(q_ref, g, r, lo):
    col = g * NSA_REP + r
    blk = q_ref[:, (col // 2) * 128:(col // 2 + 1) * 128].astype(F32) * SCALE
    h = jnp.where(lo if col % 2 == 0 else jnp.logical_not(lo), blk, 0.0)
    d = h + pltpu.roll(h, HEAD_DIM, 1)
    return jnp.where(lo if g == 0 else jnp.logical_not(lo), d, 0.0).astype(BF16)


def _gqa_store(o_ref, outs, g, lo):
    keep = lo if g == 0 else jnp.logical_not(lo)
    dup = []
    for o in outs:
        z = jnp.where(keep, o, 0.0)
        dup.append(z + pltpu.roll(z, HEAD_DIM, 1))
    for pr in range(NSA_REP // 2):
        blk = g * (NSA_REP // 2) + pr
        o_ref[:, blk * 128:(blk + 1) * 128] = jnp.where(lo, dup[2 * pr], dup[2 * pr + 1]).astype(o_ref.dtype)


def _diff_body(qi_ref, ki_ref, fi_ref, la_ref, q_ref, k_ref, v_ref, lam_ref, gs_ref, o_ref,
               qs_ref, m_ref, l_ref, acc_ref, *, t, lam_init):
    s = pl.program_id(1)
    lo = lax.broadcasted_iota(jnp.int32, (t, 128), 1) < HEAD_DIM

    @pl.when(fi_ref[s] == 1)
    def _():
        for h in range(A_HEADS):
            _split_pair(q_ref, qs_ref, h, lo)
        _init_state(m_ref, l_ref, acc_ref)

    mask = _pos_mask(qi_ref[s], ki_ref[s], t, None)
    for h in range(A_HEADS):
        k = k_ref[:, h * 128:(h + 1) * 128]
        v = v_ref[:, h * 128:(h + 1) * 128]
        for var in range(2):
            _online_update(_nt_dot(qs_ref[2 * h + var], k), mask, v, m_ref, l_ref, acc_ref, 2 * h + var)

    @pl.when(la_ref[s] == 1)
    def _():
        lv = lam_ref[...]
        a = jnp.sum(lv[0:1] * lv[1:2], axis=-1, keepdims=True)
        b = jnp.sum(lv[2:3] * lv[3:4], axis=-1, keepdims=True)
        lam = jnp.exp(a) - jnp.exp(b) + lam_init
        for h in range(A_HEADS):
            o1 = acc_ref[2 * h] / jnp.maximum(l_ref[2 * h], TINY)
            o2 = acc_ref[2 * h + 1] / jnp.maximum(l_ref[2 * h + 1], TINY)
            o = o1 - lam * o2
            ms = jnp.mean(o * o, axis=-1, keepdims=True)
            y = o * lax.rsqrt(ms + NORM_EPS) * gs_ref[...] * (1.0 - lam_init)
            o_ref[:, h * 128:(h + 1) * 128] = y.astype(o_ref.dtype)


def diff_attention_prompt(qk16, v16, lam_rows, g_subln, *, nb, seq, lam_init, t=512):
    nq = seq // t
    tabs = _step_tables(nq, None)
    w = A_V_W
    qmap = lambda n, s, qi, ki, fi, la: (n * nq + qi[s], 0)
    kmap = lambda n, s, qi, ki, fi, la: (n * nq + ki[s], 1)
    vmap = lambda n, s, qi, ki, fi, la: (n * nq + ki[s], 2)
    const = lambda n, s, qi, ki, fi, la: (0, 0)
    return pl.pallas_call(
        functools.partial(_diff_body, t=t, lam_init=lam_init),
        grid_spec=pltpu.PrefetchScalarGridSpec(
            num_scalar_prefetch=4,
            grid=(nb, int(tabs[0].shape[0])),
            in_specs=[pl.BlockSpec((t, w), qmap), pl.BlockSpec((t, w), kmap), pl.BlockSpec((t, w), vmap),
                      pl.BlockSpec((8, 128), const), pl.BlockSpec((1, 128), const)],
            out_specs=pl.BlockSpec((t, w), qmap),
            scratch_shapes=[pltpu.VMEM((2 * A_HEADS, t, 128), BF16), pltpu.VMEM((2 * A_HEADS, t, 1), F32),
                            pltpu.VMEM((2 * A_HEADS, t, 1), F32), pltpu.VMEM((2 * A_HEADS, t, 128), F32)],
        ),
        out_shape=jax.ShapeDtypeStruct((nb * seq, w), BF16),
        compiler_params=_cparams("parallel", "arbitrary"),
        name="diff_attn_prompt",
    )(*tabs, qk16, qk16, v16, lam_rows, g_subln.reshape(1, A_VDIM).astype(F32))


def _dil_body(qi_ref, ki_ref, fi_ref, la_ref, q_ref, k_ref, v_ref, o_ref, lse_ref,
              qs_ref, m_ref, l_ref, acc_ref, *, t, band):
    s = pl.program_id(1)
    lo = lax.broadcasted_iota(jnp.int32, (t, 128), 1) < HEAD_DIM
    nhb = C_HEADS // 2

    @pl.when(fi_ref[s] == 1)
    def _():
        for hb in range(nhb):
            _split_pair(q_ref, qs_ref, hb, lo)
        _init_state(m_ref, l_ref, acc_ref)

    mask = _pos_mask(qi_ref[s], ki_ref[s], t, band)
    for hb in range(nhb):
        k = k_ref[:, hb * 128:(hb + 1) * 128]
        v = v_ref[:, hb * 128:(hb + 1) * 128]
        for var in range(2):
            _online_update(_nt_dot(qs_ref[2 * hb + var], k), mask, v, m_ref, l_ref, acc_ref, 2 * hb + var)

    @pl.when(la_ref[s] == 1)
    def _():
        for hb in range(nhb):
            l0 = jnp.maximum(l_ref[2 * hb], TINY)
            l1 = jnp.maximum(l_ref[2 * hb + 1], TINY)
            o = jnp.where(lo, acc_ref[2 * hb] / l0, acc_ref[2 * hb + 1] / l1)
            lse = jnp.where(lo, m_ref[2 * hb] + jnp.log(l0), m_ref[2 * hb + 1] + jnp.log(l1))
            o_ref[:, hb * 128:(hb + 1) * 128] = o.astype(o_ref.dtype)
            lse_ref[:, hb * 128:(hb + 1) * 128] = lse


def dilated_attention_prompt(qk16, v16, *, nb, seq, dil, band, gi, t):
    mp = nb * seq
    sub = seq // dil
    nq = sub // t
    tabs = _step_tables(nq, -(-band // t))
    w = C_W
    qk = qk16.reshape(qk16.shape[0] // dil, dil * 2 * w)
    vv = v16.reshape(v16.shape[0] // dil, dil * v16.shape[1])
    vcols = v16.shape[1] // w
    row = lambda b, x: (b // dil) * nq + x

    def qmap(b, s, qi, ki, fi, la):
        return (row(b, qi[s]), (b % dil) * 2)

    def kmap(b, s, qi, ki, fi, la):
        return (row(b, ki[s]), (b % dil) * 2 + 1)

    def vmap(b, s, qi, ki, fi, la):
        return (row(b, ki[s]), (b % dil) * vcols + gi * 3 + 2)

    def omap(b, s, qi, ki, fi, la):
        return (row(b, qi[s]), b % dil)

    nst = 2 * (C_HEADS // 2)
    o, lse = pl.pallas_call(
        functools.partial(_dil_body, t=t, band=band),
        grid_spec=pltpu.PrefetchScalarGridSpec(
            num_scalar_prefetch=4,
            grid=(nb * dil, int(tabs[0].shape[0])),
            in_specs=[pl.BlockSpec((t, w), qmap), pl.BlockSpec((t, w), kmap), pl.BlockSpec((t, w), vmap)],
            out_specs=[pl.BlockSpec((t, w), omap), pl.BlockSpec((t, w), omap)],
            scratch_shapes=[pltpu.VMEM((nst, t, 128), BF16), pltpu.VMEM((nst, t, 1), F32),
                            pltpu.VMEM((nst, t, 1), F32), pltpu.VMEM((nst, t, 128), F32)],
        ),
        out_shape=[jax.ShapeDtypeStruct((mp // dil, dil * w), BF16), jax.ShapeDtypeStruct((mp // dil, dil * w), F32)],
        compiler_params=_cparams("parallel", "arbitrary"),
        name=f"dilated_attn_prompt_{dil}",
    )(*tabs, qk, qk, vv)
    return o.reshape(mp, w), lse.reshape(mp, w)


def _dil_fused_body(*refs, seq):
    n_g = len(C_GROUPS)
    in_refs = refs[:3 * n_g]
    o_ref = refs[3 * n_g]
    og_refs = refs[3 * n_g + 1:3 * n_g + 1 + n_g]
    lg_refs = refs[3 * n_g + 1 + n_g:]
    for gi, (window, dil) in enumerate(C_GROUPS):
        q_ref, k_ref, v_ref = in_refs[3 * gi:3 * gi + 3]
        og_ref, lg_ref = og_refs[gi], lg_refs[gi]
        sub = seq // dil
        band = window // dil
        t = min(sub, 256 if dil == 1 else 128)
        nq = sub // t
        look = -(-band // t)
        lo = lax.broadcasted_iota(jnp.int32, (t, 128), 1) < HEAD_DIM

        def rows(rho, tile, t=t, dil=dil):
            return pl.ds(rho + dil * tile * t, t, stride=dil) if dil > 1 else pl.ds(tile * t, t)

        def residue(rho, carry, q_ref=q_ref, k_ref=k_ref, v_ref=v_ref, og_ref=og_ref, lg_ref=lg_ref,
                    t=t, nq=nq, look=look, band=band, lo=lo, rows=rows):
            for qi in range(nq):
                q = q_ref[rows(rho, qi), :] * SCALE
                qs = (jnp.where(lo, q, 0.0).astype(BF16), jnp.where(lo, 0.0, q).astype(BF16))
                m = [jnp.full((t, 1), NEG_INF, F32)] * 2
                l = [jnp.zeros((t, 1), F32)] * 2
                acc = [jnp.zeros((t, 128), F32)] * 2
                for ki in range(max(0, qi - look), qi + 1):
                    k = k_ref[rows(rho, ki), :].astype(BF16)
                    v = v_ref[rows(rho, ki), :].astype(BF16)
                    mask = _pos_mask(qi, ki, t, band)
                    for var in range(2):
                        p, alpha, m[var], l[var] = _softmax_step(_nt_dot(qs[var], k), mask, m[var], l[var])
                        acc[var] = alpha * acc[var] + jnp.dot(p.astype(BF16), v, preferred_element_type=F32)
                l = [jnp.maximum(x, TINY) for x in l]
                og_ref[rows(rho, qi), :] = jnp.where(lo, acc[0] / l[0], acc[1] / l[1])
                lg_ref[rows(rho, qi), :] = jnp.where(lo, m[0] + jnp.log(l[0]), m[1] + jnp.log(l[1]))
            return carry

        if dil == 1:
            residue(0, 0)
        else:
            lax.fori_loop(0, dil, residue, 0)

    chunk = 256
    for c in range(seq // chunk):
        sl = pl.ds(c * chunk, chunk)
        ls = [r[sl, :] for r in lg_refs]
        mx = functools.reduce(jnp.maximum, ls)
        es = [jnp.exp(x - mx) for x in ls]
        den = functools.reduce(lambda a, b: a + b, es)
        acc = functools.reduce(lambda a, b: a + b, [(e / den) * r[sl, :] for e, r in zip(es, og_refs)])
        o_ref[sl, :] = acc.astype(o_ref.dtype)


def dilated_attention_prompt_fused(qk32s, proj32, *, nb, seq):
    n_g = len(C_GROUPS)
    nhb = C_W // 128
    in_specs, args = [], []
    for gi in range(n_g):
        in_specs += [pl.BlockSpec((seq, 128), lambda n, hb: (n, hb)),
                     pl.BlockSpec((seq, 128), lambda n, hb: (n, nhb + hb)),
                     pl.BlockSpec((seq, 128), functools.partial(lambda n, hb, gi: (n, (3 * gi + 2) * nhb + hb), gi=gi))]
        args += [qk32s[gi], qk32s[gi], proj32]
    return pl.pallas_call(
        functools.partial(_dil_fused_body, seq=seq),
        grid=(nb, nhb),
        in_specs=in_specs,
        out_specs=pl.BlockSpec((seq, 128), lambda n, hb: (n, hb)),
        out_shape=jax.ShapeDtypeStruct((nb * seq, C_W), BF16),
        scratch_shapes=[pltpu.VMEM((seq, 128), F32) for _ in range(2 * n_g)],
        compiler_params=_cparams("parallel", "parallel"),
        name="dilated_attn_prompt",
    )(*args)


def _gqa_body(qi_ref, ki_ref, fi_ref, la_ref, q_ref, k_ref, v_ref, *rest, t, band, use_sel):
    if use_sel:
        sel_ref, o_ref, qs_ref, m_ref, l_ref, acc_ref = rest
    else:
        o_ref, qs_ref, m_ref, l_ref, acc_ref = rest
    s = pl.program_id(1)
    lo = lax.broadcasted_iota(jnp.int32, (t, 128), 1) < HEAD_DIM

    @pl.when(fi_ref[s] == 1)
    def _():
        for g in range(NSA_KV_HEADS):
            for r in range(NSA_REP):
                qs_ref[g * NSA_REP + r] = _gqa_query(q_ref, g, r, lo)
        _init_state(m_ref, l_ref, acc_ref)

    ki = ki_ref[s]
    mask = _pos_mask(qi_ref[s], ki, t, band)
    k = k_ref[...]
    v = v_ref[...]
    if use_sel:
        blk_row = lax.broadcasted_iota(jnp.int32, (128, t), 0)
        blk_col = jnp.right_shift(ki * t + lax.broadcasted_iota(jnp.int32, (128, t), 1), SEL_SHIFT)
        expand = jnp.where(blk_row == blk_col, 1.0, 0.0).astype(BF16)
    for g in range(NSA_KV_HEADS):
        mg = mask
        if use_sel:
            mg = mask & (jnp.dot(sel_ref[g], expand, preferred_element_type=F32) > 0.5)
        for r in range(NSA_REP):
            i = g * NSA_REP + r
            _online_update(_nt_dot(qs_ref[i], k), mg, v, m_ref, l_ref, acc_ref, i)

    @pl.when(la_ref[s] == 1)
    def _():
        for g in range(NSA_KV_HEADS):
            outs = [acc_ref[g * NSA_REP + r] / jnp.maximum(l_ref[g * NSA_REP + r], TINY) for r in range(NSA_REP)]
            _gqa_store(o_ref, outs, g, lo)


def nsa_branch_prompt(q16, k16, kcol, v16, vcol, sel, *, nb, seq, band, t=512, name):
    nq = seq // t
    tabs = _step_tables(nq, None if band is None else -(-band // t))
    qmap = lambda n, s, qi, ki, fi, la: (n * nq + qi[s], 0)
    kmap = lambda n, s, qi, ki, fi, la: (n * nq + ki[s], kcol)
    vmap = lambda n, s, qi, ki, fi, la: (n * nq + ki[s], vcol)
    in_specs = [pl.BlockSpec((t, NSA_Q_W), qmap), pl.BlockSpec((t, 128), kmap), pl.BlockSpec((t, 128), vmap)]
    args = [q16, k16, v16]
    if sel is not None:
        in_specs.append(pl.BlockSpec((NSA_KV_HEADS, t, 128), lambda n, s, qi, ki, fi, la: (0, n * nq + qi[s], 0)))
        args.append(sel)
    nst = NSA_HEADS
    return pl.pallas_call(
        functools.partial(_gqa_body, t=t, band=band, use_sel=sel is not None),
        grid_spec=pltpu.PrefetchScalarGridSpec(
            num_scalar_prefetch=4,
            grid=(nb, int(tabs[0].shape[0])),
            in_specs=in_specs,
            out_specs=pl.BlockSpec((t, NSA_Q_W), qmap),
            scratch_shapes=[pltpu.VMEM((nst, t, 128), BF16), pltpu.VMEM((nst, t, 1), F32),
                            pltpu.VMEM((nst, t, 1), F32), pltpu.VMEM((nst, t, 128), F32)],
        ),
        out_shape=jax.ShapeDtypeStruct((nb * seq, NSA_Q_W), BF16),
        compiler_params=_cparams("parallel", "arbitrary"),
        name=name,
    )(*tabs, *args)


def _cmp_finish_body(ab_ref, pe_ref, w1_ref, w2_ref, g_ref, o_ref, *, hid, norm):
    ab = ab_ref[...]
    pe_term = jnp.dot(pe_ref[...], w1_ref[...], preferred_element_type=F32)[0:1]
    h = ab[:, :hid] + pltpu.roll(ab[:, hid:], ab.shape[0] - 1, 0) + pe_term
    act = (h * jax.nn.sigmoid(h)).astype(BF16)
    y = jnp.dot(act, w2_ref[...], preferred_element_type=F32)
    if norm:
        y = y * lax.rsqrt(jnp.mean(y * y, axis=-1, keepdims=True) + NORM_EPS) * g_ref[...]
    o_ref[...] = y.astype(o_ref.dtype)


def _w1_ab(w1):
    half = w1.shape[0] // 2
    return jnp.concatenate([w1[:half], w1[half:]], axis=1).astype(BF16)


def compress_blocks(x_chunks, pe, w1, w2, gain):
    b, nchunk, cw = x_chunks.shape
    hid = w1.shape[1]
    ab = matmul(x_chunks.reshape(b * nchunk, cw), _w1_ab(w1), tn=2 * hid, name="compress_in")
    pe_rows = jnp.zeros((8, 2 * cw), BF16).at[0].set(pe.reshape(-1).astype(BF16))
    g = jnp.ones((1, HEAD_DIM), F32) if gain is None else gain.reshape(1, HEAD_DIM).astype(F32)
    const = lambda i: (0, 0)
    return pl.pallas_call(
        functools.partial(_cmp_finish_body, hid=hid, norm=gain is not None),
        grid=(b,),
        in_specs=[pl.BlockSpec((nchunk, 2 * hid), lambda i: (i, 0)), pl.BlockSpec((8, 2 * cw), const),
                  pl.BlockSpec((2 * cw, hid), const), pl.BlockSpec((hid, HEAD_DIM), const),
                  pl.BlockSpec((1, HEAD_DIM), const)],
        out_specs=pl.BlockSpec((None, nchunk, HEAD_DIM), lambda i: (i, 0, 0)),
        out_shape=jax.ShapeDtypeStruct((b, nchunk, HEAD_DIM), BF16),
        compiler_params=_cparams("parallel"),
        name="compress_finish",
    )(ab, pe_rows, w1.astype(BF16), w2.astype(BF16), g)


def _overlap_matrix(n_cmp, n_sel):
    c0 = np.arange(128)[:, None] * CMP_STRIDE
    s0 = np.arange(128)[None, :] * SEL_BLOCK
    ov = np.maximum(np.minimum(c0 + CMP_LEN, s0 + SEL_BLOCK) - np.maximum(c0, s0), 0) / CMP_LEN
    ov = ov * (np.arange(128)[:, None] < n_cmp) * (np.arange(128)[None, :] < n_sel)
    return jnp.asarray(ov, BF16)


def _cmp_body(q_ref, kc_ref, vc_ref, ov_ref, o_ref, sel_ref, *, t, pos0, n_cmp, n_sel):
    i = pl.program_id(1)
    lane = lax.broadcasted_iota(jnp.int32, (t, 128), 1)
    qpos = pos0 + i * t + lax.broadcasted_iota(jnp.int32, (t, 128), 0)
    lo = lane < HEAD_DIM
    vis = (lane * CMP_STRIDE + CMP_LEN - 1 <= qpos) & (lane < n_cmp)
    kc, vc, ov = kc_ref[...], vc_ref[...], ov_ref[...]
    cur = jnp.right_shift(qpos, SEL_SHIFT)
    valid = (lane <= cur) & (lane < n_sel)
    forced = (lane == 0) | (lane == cur) | (lane == cur - 1)
    for g in range(NSA_KV_HEADS):
        imp = jnp.zeros((t, 128), F32)
        outs = []
        for r in range(NSA_REP):
            sc = jnp.where(vis, _nt_dot(_gqa_query(q_ref, g, r, lo), kc), NEG_INF)
            m = jnp.max(sc, axis=-1, keepdims=True)
            e = jnp.where(vis, jnp.exp(sc - m), 0.0)
            p = (e / jnp.maximum(jnp.sum(e, axis=-1, keepdims=True), TINY)).astype(BF16)
            outs.append(jnp.dot(p, vc, preferred_element_type=F32))
            imp = imp + jnp.dot(p, ov, preferred_element_type=F32)
        _gqa_store(o_ref, outs, g, lo)
        score = jnp.where(valid, imp + jnp.where(forced, FORCE_BONUS, 0.0), NEG_INF)
        rank = jnp.zeros((t, 128), F32)
        for kk in range(n_sel):
            sk = score[:, kk:kk + 1]
            rank = rank + jnp.where((sk > score) | ((sk == score) & (lane > kk)), 1.0, 0.0)
        sel_ref[g] = jnp.where((rank < SEL_TOPK) & valid, 1.0, 0.0).astype(sel_ref.dtype)


def nsa_compressed_prompt(qn16, k_cmp, v_cmp, *, nb, seq, t=512):
    n_cmp = (seq - CMP_LEN) // CMP_STRIDE + 1
    n_sel = -(-seq // SEL_BLOCK)
    nq = seq // t
    qmap = lambda n, i: (n * nq + i, 0)
    cmap = lambda n, i: (n, 0, 0)
    return pl.pallas_call(
        functools.partial(_cmp_body, t=t, pos0=0, n_cmp=n_cmp, n_sel=n_sel),
        grid=(nb, nq),
        in_specs=[pl.BlockSpec((t, NSA_Q_W), qmap), pl.BlockSpec((None, 128, 128), cmap),
                  pl.BlockSpec((None, 128, 128), cmap), pl.BlockSpec((128, 128), lambda n, i: (0, 0))],
        out_specs=[pl.BlockSpec((t, NSA_Q_W), qmap),
                   pl.BlockSpec((NSA_KV_HEADS, t, 128), lambda n, i: (0, n * nq + i, 0))],
        out_shape=[jax.ShapeDtypeStruct((nb * seq, NSA_Q_W), BF16),
                   jax.ShapeDtypeStruct((NSA_KV_HEADS, nb * seq, 128), BF16)],
        compiler_params=_cparams("parallel", "parallel"),
        name="nsa_cmp_select_prompt",
    )(qn16, k_cmp, v_cmp, _overlap_matrix(n_cmp, n_sel))


def _gate_expand_matrices():
    lane = np.arange(128)[:, None]
    col = np.arange(NSA_Q_W)[None, :]
    return jnp.asarray(np.stack([(lane < NSA_GATE_W) & (lane % 3 == br) & (lane // 3 == col // HEAD_DIM)
                                 for br in range(3)]), BF16)


def _nsa_merge_body(gb_ref, e_ref, oc_ref, os_ref, ow_ref, o_ref):
    gates = jax.nn.sigmoid(gb_ref[...])
    hi = gates.astype(BF16)
    lo = (gates - hi.astype(F32)).astype(BF16)
    acc = jnp.zeros(o_ref.shape, F32)
    for br, b_ref in enumerate((oc_ref, os_ref, ow_ref)):
        w = jnp.dot(hi, e_ref[br], preferred_element_type=F32) + jnp.dot(lo, e_ref[br], preferred_element_type=F32)
        acc = acc + w * b_ref[...].astype(F32)
    o_ref[...] = acc.astype(o_ref.dtype)


def nsa_merge(proj32, gate_col, o_cmp, o_sel, o_win):
    m = o_cmp.shape[0]
    tm = TOKEN_TILE
    spec = pl.BlockSpec((tm, NSA_Q_W), lambda i: (i, 0))
    return pl.pallas_call(
        _nsa_merge_body,
        grid=(m // tm,),
        in_specs=[pl.BlockSpec((tm, 128), lambda i: (i, gate_col)),
                  pl.BlockSpec((3, 128, NSA_Q_W), lambda i: (0, 0, 0)), spec, spec, spec],
        out_specs=spec,
        out_shape=jax.ShapeDtypeStruct((m, NSA_Q_W), BF16),
        compiler_params=_cparams("parallel"),
        name="nsa_merge",
    )(proj32, _gate_expand_matrices(), o_cmp, o_sel, o_win)


def _dil_merge_body(o0, o1, o2, l0, l1, l2, o_ref):
    ls = [l0[...], l1[...], l2[...]]
    m = jnp.maximum(jnp.maximum(ls[0], ls[1]), ls[2])
    es = [jnp.exp(x - m) for x in ls]
    den = es[0] + es[1] + es[2]
    acc = sum((e / den) * o[...].astype(F32) for e, o in zip(es, (o0, o1, o2)))
    o_ref[...] = acc.astype(o_ref.dtype)


def dilation_merge(outs, lses):
    m, w = outs[0].shape
    tm = TOKEN_TILE
    spec = pl.BlockSpec((tm, w), lambda i: (i, 0))
    return pl.pallas_call(
        _dil_merge_body,
        grid=(m // tm,),
        in_specs=[spec] * 6,
        out_specs=spec,
        out_shape=jax.ShapeDtypeStruct((m, w), BF16),
        compiler_params=_cparams("parallel"),
        name="dilation_merge",
    )(*outs, *lses)


def _softmax_step(sc, mask, m, l):
    sc = jnp.where(mask, sc, NEG_INF)
    m_new = jnp.maximum(m, jnp.max(sc, axis=-1, keepdims=True))
    alpha = jnp.exp(m - m_new)
    p = jnp.where(mask, jnp.exp(sc - m_new), 0.0)
    return p, alpha, m_new, alpha * l + jnp.sum(p, axis=-1, keepdims=True)


def _diff_sample_body(pt_ref, *refs, npages, lam_init):
    del pt_ref
    page_refs = refs[:npages]
    q_ref, new_ref, lam_ref, gs_ref, o_ref = refs[npages:]
    rows = 2 * A_HEADS * 4
    ri = lax.broadcasted_iota(jnp.int32, (rows, 128), 0)
    lane = lax.broadcasted_iota(jnp.int32, (rows, 128), 1)
    first_variant = ri < rows // 2
    qs = jnp.where(first_variant == (lane < HEAD_DIM), q_ref[...] * SCALE, 0.0).astype(BF16)
    m = jnp.full((rows, 1), NEG_INF, F32)
    l = jnp.zeros((rows, 1), F32)
    acc = jnp.zeros((rows, 128), F32)

    def attend(page, mask, m, l, acc):
        pg = page.astype(BF16)
        p, alpha, m, l = _softmax_step(_nt_dot(qs, pg), mask, m, l)
        pv = jnp.dot(pltpu.roll(p, A_HEADS, 1).astype(BF16), pg, preferred_element_type=F32)
        return m, l, alpha * acc + pv

    ncols = page_refs[0].shape[0]
    col = lax.broadcasted_iota(jnp.int32, (rows, ncols), 1)
    head = jnp.bitwise_and(jnp.right_shift(lax.broadcasted_iota(jnp.int32, (rows, ncols), 0), 2), A_HEADS - 1)
    page_mask = jnp.bitwise_and(col, 2 * A_HEADS - 1) == head
    for pr in page_refs:
        m, l, acc = attend(pr[...], page_mask, m, l, acc)
    tok = jnp.bitwise_and(ri, 3)
    new_mask = ((jnp.bitwise_and(lane, 2 * A_HEADS - 1) == jnp.bitwise_and(jnp.right_shift(ri, 2), A_HEADS - 1))
                & (jnp.right_shift(lane, 3) <= tok) & (lane < 4 * 2 * A_HEADS))
    m, l, acc = attend(new_ref[...], new_mask, m, l, acc)

    lv = lam_ref[...]
    a = jnp.sum(lv[0:1] * lv[1:2], axis=-1, keepdims=True)
    b = jnp.sum(lv[2:3] * lv[3:4], axis=-1, keepdims=True)
    lam = jnp.exp(a) - jnp.exp(b) + lam_init
    o = acc / jnp.maximum(l, TINY)
    o = o[:rows // 2] - lam * o[rows // 2:]
    y = o * lax.rsqrt(jnp.mean(o * o, axis=-1, keepdims=True) + NORM_EPS) * gs_ref[...] * (1.0 - lam_init)
    o_ref[...] = y.astype(o_ref.dtype)


def diff_attention_sample(q_rows, new_page, cache_rows, page_table, lam_rows, g_subln, *, lam_init):
    db = q_rows.shape[0]
    npages = page_table.shape[1]
    prow = cache_rows.shape[1]
    page_specs = [pl.BlockSpec((None, prow, 128), functools.partial(lambda b, pt, p: (pt[b, p], 0, 0), p=p))
                  for p in range(npages)]
    per_b = lambda b, pt: (b, 0, 0)
    const = lambda b, pt: (0, 0)
    return pl.pallas_call(
        functools.partial(_diff_sample_body, npages=npages, lam_init=lam_init),
        grid_spec=pltpu.PrefetchScalarGridSpec(
            num_scalar_prefetch=1,
            grid=(db,),
            in_specs=page_specs + [pl.BlockSpec((None, 32, 128), per_b), pl.BlockSpec((None, 128, 128), per_b),
                                   pl.BlockSpec((8, 128), const), pl.BlockSpec((1, 128), const)],
            out_specs=pl.BlockSpec((None, 16, 128), per_b),
        ),
        out_shape=jax.ShapeDtypeStruct((db, 16, 128), F32),
        compiler_params=_cparams("parallel"),
        name="diff_attn_sample",
    )(page_table, *([cache_rows] * npages), q_rows, new_page, lam_rows, g_subln.reshape(1, A_VDIM).astype(F32))


def _place_new_columns(rolled, new_rows, t):
    sq = jnp.concatenate([new_rows, jnp.zeros_like(new_rows)], axis=1)
    new_t = pltpu.roll(jnp.transpose(sq)[:HEAD_DIM], 124, 1)
    if t > 128:
        new_t = jnp.concatenate([jnp.zeros((HEAD_DIM, t - 128), F32), new_t], axis=1)
    lane = lax.broadcasted_iota(jnp.int32, (HEAD_DIM, t), 1)
    return jnp.where(lane >= t - 4, new_t, rolled)


def _nsa_sample_body(pt_ref, *refs, npages, past, n_cmp, n_sel, hid):
    del pt_ref
    cn_refs = refs[:npages]
    ab_refs = refs[npages:2 * npages]
    (qn_ref, qr_ref, new_ref, sw_ref, pe_ref, w1k_ref, w1v_ref, w2k_ref, w2v_ref, gk_ref, ov_ref,
     oc_ref, os_ref, ow_ref, wout_ref) = refs[2 * npages:]
    rows = NSA_REP * 8
    nwin = sw_ref.shape[-1]
    lane = lax.broadcasted_iota(jnp.int32, (rows, 128), 1)
    tok = jnp.bitwise_and(lax.broadcasted_iota(jnp.int32, (rows, 128), 0), 7)
    lane8 = lane[:8]
    tok8 = tok[:8]
    vis = (lane * CMP_STRIDE + CMP_LEN - 1 <= past + tok) & (lane < n_cmp)
    cur = jnp.right_shift(past + tok8, SEL_SHIFT)
    valid = (lane8 <= cur) & (lane8 < n_sel)
    forced = (lane8 == 0) | (lane8 == cur) | (lane8 == cur - 1)
    new_mask = (lane <= tok) & (lane < 4)
    ov = ov_ref[...]
    pe_terms = [jnp.dot(pe_ref[c], w_ref[...], preferred_element_type=F32)[0:1]
                for c, w_ref in enumerate((w1k_ref, w1v_ref))]

    for g in range(NSA_KV_HEADS):
        cmp = []
        for c, w2_ref in enumerate((w2k_ref, w2v_ref)):
            a = jnp.concatenate([r[c, g, :, :hid] for r in ab_refs], axis=0)
            bb = jnp.concatenate([r[c, g, :, hid:] for r in ab_refs], axis=0)
            h = a + pltpu.roll(bb, a.shape[0] - 1, 0) + pe_terms[c]
            y = jnp.dot((h * jax.nn.sigmoid(h)).astype(BF16), w2_ref[...], preferred_element_type=F32)
            if c == 0:
                y = y * lax.rsqrt(jnp.mean(y * y, axis=-1, keepdims=True) + NORM_EPS) * gk_ref[...]
            cmp.append(y.astype(BF16))
        qn = (qn_ref[g].astype(F32) * SCALE).astype(BF16)
        qr = (qr_ref[g].astype(F32) * SCALE).astype(BF16)

        sc = jnp.where(vis, _nt_dot(qn, cmp[0]), NEG_INF)
        mx = jnp.max(sc, axis=-1, keepdims=True)
        e = jnp.where(vis, jnp.exp(sc - mx), 0.0)
        p = (e / jnp.maximum(jnp.sum(e, axis=-1, keepdims=True), TINY)).astype(BF16)
        oc_ref[g] = jnp.dot(p, cmp[1], preferred_element_type=F32)
        imp_r = jnp.dot(p, ov, preferred_element_type=F32)
        imp = imp_r[0:8] + imp_r[8:16] + imp_r[16:24] + imp_r[24:32]
        score = jnp.where(valid, imp + jnp.where(forced, FORCE_BONUS, 0.0), NEG_INF)
        rank = jnp.zeros((8, 128), F32)
        for kk in range(n_sel):
            sk = score[:, kk:kk + 1]
            rank = rank + jnp.where((sk > score) | ((sk == score) & (lane8 > kk)), 1.0, 0.0)
        sel = (rank < SEL_TOPK) & valid

        m = jnp.full((rows, 1), NEG_INF, F32)
        l = jnp.zeros((rows, 1), F32)
        acc = jnp.zeros((rows, HEAD_DIM), F32)
        lo8 = lane8 < SEL_BLOCK
        for pi, r in enumerate(cn_refs):
            blocks_per_page = 128 // SEL_BLOCK
            flag = jnp.where(lo8, jnp.where(sel[:, blocks_per_page * pi:blocks_per_page * pi + 1], 1.0, 0.0),
                             jnp.where(sel[:, blocks_per_page * pi + 1:blocks_per_page * pi + 2], 1.0, 0.0))
            mask = jnp.concatenate([flag] * NSA_REP, axis=0) > 0.5
            pr, alpha, m, l = _softmax_step(jnp.dot(qr, r[0, g].astype(BF16), preferred_element_type=F32), mask, m, l)
            acc = alpha * acc + _nt_dot(pr.astype(BF16), r[1, g].astype(BF16))
        last_blk = past // SEL_BLOCK
        flag = jnp.where(sel[:, last_blk:last_blk + 1], 1.0, 0.0) + jnp.zeros((8, 128), F32)
        mask = new_mask & (jnp.concatenate([flag] * NSA_REP, axis=0) > 0.5)
        pr, alpha, m, l = _softmax_step(_nt_dot(qr, new_ref[0, g].astype(BF16)), mask, m, l)
        acc = alpha * acc + jnp.dot(pr.astype(BF16), new_ref[1, g].astype(BF16), preferred_element_type=F32)
        os_ref[g] = acc / jnp.maximum(l, TINY)

        wl = lax.broadcasted_iota(jnp.int32, (rows, nwin), 1)
        wt = jnp.bitwise_and(lax.broadcasted_iota(jnp.int32, (rows, nwin), 0), 7)
        wmask = wl >= nwin + wt - NSA_WINDOW
        kt = sw_ref[0, g]
        vt = sw_ref[1, g]
        m = jnp.full((rows, 1), NEG_INF, F32)
        l = jnp.zeros((rows, 1), F32)
        pr, alpha, m, l = _softmax_step(jnp.dot(qr, kt.astype(BF16), preferred_element_type=F32), wmask, m, l)
        acc = _nt_dot(pr.astype(BF16), vt.astype(BF16))
        pr, alpha, m, l = _softmax_step(_nt_dot(qr, new_ref[2, g].astype(BF16)), new_mask, m, l)
        acc = alpha * acc + jnp.dot(pr.astype(BF16), new_ref[3, g].astype(BF16), preferred_element_type=F32)
        ow_ref[g] = acc / jnp.maximum(l, TINY)
        wout_ref[0, g] = _place_new_columns(pltpu.roll(kt, nwin - 4, 1), new_ref[2, g], nwin)
        wout_ref[1, g] = _place_new_columns(pltpu.roll(vt, nwin - 4, 1), new_ref[3, g], nwin)


def nsa_sample(qn, qr, new_rows, cache_t, ab, win_state, page_table, cmp_w, g_kc, *, past):
    pe_k, w_k1, w_k2, pe_v, w_v1, w_v2 = cmp_w
    db = qn.shape[0]
    npages = page_table.shape[1]
    hid = w_k1.shape[1]
    nwin = win_state.shape[-1]
    n_cmp = (past + 4 - CMP_LEN) // CMP_STRIDE + 1
    n_sel = -(-(past + 4) // SEL_BLOCK)
    assert n_cmp <= npages * 8 - 1 and past % SEL_BLOCK == 0
    pe_rows = jnp.zeros((2, 8, pe_k.size), BF16).at[:, 0].set(
        jnp.stack([pe_k.reshape(-1), pe_v.reshape(-1)]).astype(BF16))
    cn_specs = [pl.BlockSpec((None, 2, NSA_KV_HEADS, HEAD_DIM, 128),
                             functools.partial(lambda b, pt, p: (pt[b, p], 1, 0, 0, 0), p=p)) for p in range(npages)]
    ab_specs = [pl.BlockSpec((2, None, NSA_KV_HEADS, 8, 2 * hid),
                             functools.partial(lambda b, pt, p: (0, pt[b, p], 0, 0, 0), p=p)) for p in range(npages)]
    b4 = lambda b, pt: (b, 0, 0, 0)
    b5 = lambda b, pt: (b, 0, 0, 0, 0)
    c2 = lambda b, pt: (0, 0)
    c3 = lambda b, pt: (0, 0, 0)
    o_spec = pl.BlockSpec((None, NSA_KV_HEADS, 32, HEAD_DIM), b4)
    o_shape = jax.ShapeDtypeStruct((db, NSA_KV_HEADS, 32, HEAD_DIM), F32)
    return pl.pallas_call(
        functools.partial(_nsa_sample_body, npages=npages, past=past, n_cmp=n_cmp, n_sel=n_sel, hid=hid),
        grid_spec=pltpu.PrefetchScalarGridSpec(
            num_scalar_prefetch=1,
            grid=(db,),
            in_specs=cn_specs + ab_specs + [
                pl.BlockSpec((None, NSA_KV_HEADS, 32, HEAD_DIM), b4), pl.BlockSpec((None, NSA_KV_HEADS, 32, HEAD_DIM), b4),
                pl.BlockSpec((None, 4, NSA_KV_HEADS, 128, HEAD_DIM), b5),
                pl.BlockSpec((None, 2, NSA_KV_HEADS, HEAD_DIM, nwin), b5),
                pl.BlockSpec((2, 8, pe_k.size), c3),
                pl.BlockSpec(w_k1.shape, c2), pl.BlockSpec(w_v1.shape, c2),
                pl.BlockSpec(w_k2.shape, c2), pl.BlockSpec(w_v2.shape, c2),
                pl.BlockSpec((1, HEAD_DIM), c2), pl.BlockSpec((128, 128), c2)],
            out_specs=[o_spec, o_spec, o_spec, pl.BlockSpec((None, 2, NSA_KV_HEADS, HEAD_DIM, nwin), b5)],
        ),
        out_shape=[o_shape, o_shape, o_shape, jax.ShapeDtypeStruct(win_state.shape, F32)],
        compiler_params=_cparams("parallel"),
        name="nsa_sample",
    )(page_table, *([cache_t] * npages), *([ab] * npages), qn, qr, new_rows, win_state, pe_rows,
      w_k1.astype(BF16), w_v1.astype(BF16), w_k2.astype(BF16), w_v2.astype(BF16),
      g_kc.reshape(1, HEAD_DIM).astype(F32), _overlap_matrix(n_cmp, n_sel))


def _dil_sample_body(q_ref, kn_ref, vn_ref, st_ref, oext_ref, roll_ref, *, window, dil, hg):
    nbuf = st_ref.shape[-1]
    col = lax.broadcasted_iota(jnp.int32, (8, nbuf), 1)
    dist = nbuf + lax.broadcasted_iota(jnp.int32, (8, nbuf), 0) - col
    mask = (dist <= window) & (jnp.bitwise_and(dist, dil - 1) == 0)
    ncol = lax.broadcasted_iota(jnp.int32, (8, 128), 1)
    nd = lax.broadcasted_iota(jnp.int32, (8, 128), 0) - ncol
    nmask = (nd >= 0) & (jnp.bitwise_and(nd, dil - 1) == 0) & (ncol < 4)
    for h in range(hg):
        sl = slice(h * HEAD_DIM, (h + 1) * HEAD_DIM)
        q = (q_ref[:, sl] * SCALE).astype(BF16)
        kt, vt = st_ref[0, h], st_ref[1, h]
        kn, vn = kn_ref[:, sl], vn_ref[:, sl]
        s1 = jnp.where(mask, jnp.dot(q, kt.astype(BF16), preferred_element_type=F32), NEG_INF)
        s2 = jnp.where(nmask, _nt_dot(q, kn.astype(BF16)), NEG_INF)
        m = jnp.maximum(jnp.max(s1, axis=-1, keepdims=True), jnp.max(s2, axis=-1, keepdims=True))
        e1 = jnp.where(mask, jnp.exp(s1 - m), 0.0)
        e2 = jnp.where(nmask, jnp.exp(s2 - m), 0.0)
        l = jnp.maximum(jnp.sum(e1, axis=-1, keepdims=True) + jnp.sum(e2, axis=-1, keepdims=True), TINY)
        o = (_nt_dot(e1.astype(BF16), vt.astype(BF16))
             + jnp.dot(e2.astype(BF16), vn.astype(BF16), preferred_element_type=F32)) / l
        lse = m + jnp.log(l)
        oext_ref[h] = jnp.concatenate([o, lse + jnp.zeros((8, HEAD_DIM), F32)], axis=1)
        roll_ref[0, h] = _place_new_columns(pltpu.roll(kt, nbuf - 4, 1), kn, nbuf)
        roll_ref[1, h] = _place_new_columns(pltpu.roll(vt, nbuf - 4, 1), vn, nbuf)


def dilated_attention_sample(q8, k_new, v_new, state_t, *, window, dil, hg=4):
    db = q8.shape[0]
    nbuf = state_t.shape[-1]
    assert dil & (dil - 1) == 0 and nbuf >= window
    w = hg * HEAD_DIM
    return pl.pallas_call(
        functools.partial(_dil_sample_body, window=window, dil=dil, hg=hg),
        grid=(db, C_HEADS // hg),
        in_specs=[pl.BlockSpec((None, 8, w), lambda b, j: (b, 0, j)),
                  pl.BlockSpec((None, 128, w), lambda b, j: (b, 0, j)),
                  pl.BlockSpec((None, 128, w), lambda b, j: (b, 0, j)),
                  pl.BlockSpec((None, 2, hg, HEAD_DIM, nbuf), lambda b, j: (b, 0, j, 0, 0))],
        out_specs=[pl.BlockSpec((None, hg, 8, 128), lambda b, j: (b, j, 0, 0)),
                   pl.BlockSpec((None, 2, hg, HEAD_DIM, nbuf), lambda b, j: (b, 0, j, 0, 0))],
        out_shape=[jax.ShapeDtypeStruct((db, C_HEADS, 8, 128), F32), jax.ShapeDtypeStruct(state_t.shape, F32)],
        compiler_params=_cparams("parallel", "parallel"),
        name=f"dilated_attn_sample_{dil}",
    )(q8, k_new, v_new, state_t)


def _rms_norm(x, g):
    xf = x.astype(F32)
    y = xf * lax.rsqrt(jnp.mean(xf * xf, axis=-1, keepdims=True) + NORM_EPS)
    return (y * g.astype(F32)).astype(x.dtype)


def _partial_rope(x, pos):
    half = ROT_DIM // 2
    inv_freq = ROPE_THETA ** (-jnp.arange(half, dtype=F32) / half)
    ang = pos.astype(F32)[:, None] * inv_freq[None, :]
    shape = (1, pos.shape[0]) + (1,) * (x.ndim - 3) + (half,)
    cos = jnp.cos(ang).reshape(shape)
    sin = jnp.sin(ang).reshape(shape)
    xf = x.astype(F32)
    x1, x2 = xf[..., :half], xf[..., half:ROT_DIM]
    out = jnp.concatenate([x1 * cos - x2 * sin, x2 * cos + x1 * sin, xf[..., ROT_DIM:]], axis=-1)
    return out.astype(x.dtype)


def _masked_softmax(s, mask):
    s = jnp.where(mask, s.astype(F32), NEG_INF)
    m = jnp.max(s, axis=-1, keepdims=True)
    e = jnp.where(mask, jnp.exp(s - m), 0.0)
    l = jnp.maximum(jnp.sum(e, axis=-1, keepdims=True), TINY)
    return e / l, (m + jnp.log(l))[..., 0]


def _paged_rows(cache, page_table):
    g = cache[page_table]
    return g.reshape((g.shape[0], g.shape[1] * g.shape[2]) + g.shape[3:])


def _roll_buffer(buf, new):
    n_buf, t = buf.shape[1], new.shape[1]
    if t >= n_buf:
        return new[:, t - n_buf:]
    return jnp.concatenate([buf[:, t:], new], axis=1)


def _gather_rows(buf, new, idx):
    n_buf = buf.shape[1]
    from_buf = buf[:, np.clip(idx, 0, n_buf - 1)]
    from_new = new[:, np.clip(idx - n_buf, 0, new.shape[1] - 1)]
    sel = (idx < n_buf).reshape(idx.shape + (1,) * (buf.ndim - 2))
    return jnp.where(sel, from_buf, from_new)


def _banded_attn(q, k, v, band):
    n, L, g, r, dh = q.shape
    blk = math.gcd(L, Q_BLOCK)
    nb = L // blk
    pad = ((0, 0), (band, 0), (0, 0), (0, 0))
    idx = np.arange(nb)[:, None] * blk + np.arange(blk + band)[None, :]
    kb = jnp.pad(k, pad)[:, idx]
    vb = jnp.pad(v, pad)[:, idx]
    qb = q.reshape(n, nb, blk, g, r, dh)
    s = jnp.einsum('nbqgrd,nbkgd->nbgrqk', qb, kb, preferred_element_type=F32) * SCALE
    qpos = np.arange(nb)[:, None] * blk + np.arange(blk)[None, :]
    kpos = idx - band
    dist = qpos[:, :, None] - kpos[:, None, :]
    mask = (dist >= 0) & (dist <= band) & (kpos[:, None, :] >= 0)
    p, lse = _masked_softmax(s, mask[None, :, None, None])
    o = jnp.einsum('nbgrqk,nbkgd->nbqgrd', p, vb.astype(F32))
    return o.reshape(n, L, g, r, dh).astype(q.dtype), lse.transpose(0, 1, 4, 2, 3).reshape(n, L, g, r)


def _diff_heads(qa, ka, va, pos, g_q, g_k):
    n, t = qa.shape[:2]
    q = _partial_rope(_rms_norm(qa.reshape(n, t, A_HEADS, 2, HEAD_DIM), g_q), pos)
    k = _partial_rope(_rms_norm(ka.reshape(n, t, A_HEADS, 2, HEAD_DIM), g_k), pos)
    return q, k, va.reshape(n, t, A_HEADS, A_VDIM)


def _diff_core(q, k, v, qpos, kpos, lam):
    s = jnp.einsum('nqhmd,nkhmd->nhmqk', q, k, preferred_element_type=F32) * SCALE
    p, _ = _masked_softmax(s, (kpos[None, :] <= qpos[:, None])[None, None, None])
    a = p[:, :, 0] - lam * p[:, :, 1]
    return jnp.einsum('nhqk,nkhe->nqhe', a, v.astype(F32)).astype(v.dtype)


def _diff_attn_prompt(q, k, v, pos, lam):
    n, s = q.shape[:2]
    nb = s // Q_BLOCK
    qb = q.reshape((n, nb, Q_BLOCK) + q.shape[2:]).swapaxes(0, 1)
    ob = lax.map(lambda a: _diff_core(a[0], k, v, a[1], pos, lam), (qb, pos.reshape(nb, Q_BLOCK)))
    return ob.swapaxes(0, 1).reshape(n, s, A_HEADS, A_VDIM)


def _diff_output(o, g_sub, lam_init):
    n, t = o.shape[:2]
    return (_rms_norm(o, g_sub) * (1.0 - lam_init)).reshape(n, t, A_V_W)


def _nsa_heads(qb, kvb, gb, pos, g_q, g_k):
    n, t = qb.shape[:2]
    q = _rms_norm(qb.reshape(n, t, NSA_KV_HEADS, NSA_REP, HEAD_DIM), g_q)
    q_rot = _partial_rope(q, pos)
    kv = kvb.reshape(n, t, 6, NSA_KV_HEADS, HEAD_DIM)
    k_slc = _partial_rope(_rms_norm(kv[:, :, 2], g_k[1]), pos)
    k_win = _partial_rope(_rms_norm(kv[:, :, 4], g_k[2]), pos)
    long_rows = jnp.stack([kv[:, :, 0], kv[:, :, 1], k_slc, kv[:, :, 3]], axis=2)
    win_rows = jnp.stack([k_win, kv[:, :, 5]], axis=2)
    gates = jax.nn.sigmoid(gb.astype(F32)).reshape(n, t, NSA_KV_HEADS, NSA_REP, 3)
    return q, q_rot, long_rows, win_rows, gates


def _nsa_compress(rows, pe, w1, w2):
    n, L, g, dh = rows.shape
    n_cmp = (L - CMP_LEN) // CMP_STRIDE + 1
    idx = np.arange(n_cmp)[:, None] * CMP_STRIDE + np.arange(CMP_LEN)[None, :]
    blocks = rows[:, idx] + pe[None, None, :, None, :]
    flat = blocks.transpose(0, 1, 3, 2, 4).reshape(n, n_cmp, g, CMP_LEN * dh)
    return jax.nn.silu(flat @ w1) @ w2


def _nsa_cmp_attn(q, k_cmp, v_cmp, qpos):
    n_cmp = k_cmp.shape[1]
    end = jnp.asarray(np.arange(n_cmp) * CMP_STRIDE + CMP_LEN - 1)
    s = jnp.einsum('nqgrd,ncgd->nqgrc', q, k_cmp, preferred_element_type=F32) * SCALE
    visible = end[None, :] <= qpos[:, None]
    p, _ = _masked_softmax(s, visible[None, :, None, None, :])
    o = jnp.einsum('nqgrc,ncgd->nqgrd', p, v_cmp.astype(F32)).astype(q.dtype)
    return o, p


def _cmp_to_sel_overlap(n_cmp, n_sel):
    c0 = np.arange(n_cmp)[:, None] * CMP_STRIDE
    s0 = np.arange(n_sel)[None, :] * SEL_BLOCK
    ov = np.minimum(c0 + CMP_LEN, s0 + SEL_BLOCK) - np.maximum(c0, s0)
    return jnp.asarray(np.maximum(ov, 0) / CMP_LEN, dtype=F32)


def _nsa_select(p_cmp, qpos, n_sel):
    imp = jnp.einsum('nqgrc,cj->nqgj', p_cmp, _cmp_to_sel_overlap(p_cmp.shape[-1], n_sel))
    blk = jnp.arange(n_sel)[None, :]
    cur = (qpos // SEL_BLOCK)[:, None]
    valid = blk <= cur
    forced = (blk == 0) | (blk == cur) | (blk == cur - 1)
    score = jnp.where(valid[None, :, None], imp + jnp.where(forced, FORCE_BONUS, 0.0)[None, :, None], NEG_INF)
    _, sel = lax.top_k(score, min(SEL_TOPK, n_sel))
    return sel


def _nsa_sel_attn(q, k_blk, v_blk, sel, qpos):
    n, qc, g, r, dh = q.shape
    kk = sel.shape[-1]
    n_i = jnp.arange(n)[:, None, None, None]
    g_i = jnp.arange(g)[None, None, :, None]
    kg = k_blk[n_i, g_i, sel]
    vg = v_blk[n_i, g_i, sel]
    kpos = sel[..., None] * SEL_BLOCK + jnp.arange(SEL_BLOCK)
    visible = (kpos <= qpos[None, :, None, None, None]).reshape(n, qc, g, 1, kk * SEL_BLOCK)
    s = jnp.einsum('nqgrd,nqgkbd->nqgrkb', q, kg, preferred_element_type=F32)
    p, _ = _masked_softmax(s.reshape(n, qc, g, r, kk * SEL_BLOCK) * SCALE, visible)
    o = jnp.einsum('nqgrx,nqgxd->nqgrd', p, vg.reshape(n, qc, g, kk * SEL_BLOCK, dh).astype(F32))
    return o.astype(q.dtype)


def _nsa_long_branches(q, q_rot, long_all, qpos, g_kc, pe_k, w_k1, w_k2, pe_v, w_v1, w_v2):
    n, L, _, g, dh = long_all.shape
    k_cmp = _rms_norm(_nsa_compress(long_all[:, :, 0], pe_k, w_k1, w_k2), g_kc)
    v_cmp = _nsa_compress(long_all[:, :, 1], pe_v, w_v1, w_v2)
    o_cmp, p_cmp = _nsa_cmp_attn(q, k_cmp, v_cmp, qpos)
    n_sel = -(-L // SEL_BLOCK)
    sel = _nsa_select(p_cmp, qpos, n_sel)

    def to_blocks(x):
        x = jnp.pad(x, ((0, 0), (0, n_sel * SEL_BLOCK - L), (0, 0), (0, 0)))
        return x.reshape(n, n_sel, SEL_BLOCK, g, dh).transpose(0, 3, 1, 2, 4)

    k_blk, v_blk = to_blocks(long_all[:, :, 2]), to_blocks(long_all[:, :, 3])
    nq = q.shape[1]
    qc = math.gcd(nq, SEL_Q_BLOCK)
    nc = nq // qc

    def chunks(x):
        return x.reshape((n, nc, qc) + x.shape[2:]).swapaxes(0, 1)

    o_sel = lax.map(lambda a: _nsa_sel_attn(a[0], k_blk, v_blk, a[1], a[2]),
                    (chunks(q_rot), chunks(sel), qpos.reshape(nc, qc)))
    return o_cmp, o_sel.swapaxes(0, 1).reshape(q.shape)


def _window_attn_sample(q, k_all, v_all, n_buf, window):
    t = q.shape[1]
    dist = (n_buf + np.arange(t))[:, None] - np.arange(n_buf + t)[None, :]
    visible = (dist >= 0) & (dist <= window)
    s = jnp.einsum('ntgrd,nkgd->ntgrk', q, k_all, preferred_element_type=F32) * SCALE
    p, _ = _masked_softmax(s, visible[None, :, None, None, :])
    return jnp.einsum('ntgrk,nkgd->ntgrd', p, v_all.astype(F32)).astype(q.dtype)


def _nsa_merge(gates, o_cmp, o_sel, o_win):
    o = gates[..., 0:1] * o_cmp.astype(F32) + gates[..., 1:2] * o_sel.astype(F32) + gates[..., 2:3] * o_win.astype(F32)
    n, t = o.shape[:2]
    return o.reshape(n, t, NSA_Q_W).astype(o_cmp.dtype)


def _split_in0(proj):
    sizes = [A_QK_W, A_QK_W, A_V_W, NSA_Q_W, NSA_KV_W, NSA_GATE_W]
    return jnp.split(proj, [int(o) for o in np.cumsum(sizes)[:-1]], axis=-1)


def _even_mixer_prompt(proj, pos, mw):
    g_qa, g_ka, lam, lam_init, g_subln, g_qb, g_kb, cmp_w = mw
    n, s = proj.shape[:2]
    qa, ka, va, qb, kvb, gb = _split_in0(proj)
    q, k, v = _diff_heads(qa, ka, va, pos, g_qa, g_ka)
    o_a = _diff_output(_diff_attn_prompt(q, k, v, pos, lam), g_subln, lam_init)
    qn, qr, long_rows, win_rows, gates = _nsa_heads(qb, kvb, gb, pos, g_qb, g_kb)
    o_cmp, o_sel = _nsa_long_branches(qn, qr, long_rows, pos, g_kb[0], *cmp_w)
    o_win, _ = _banded_attn(qr, win_rows[:, :, 0], win_rows[:, :, 1], NSA_WINDOW)
    o_b = _nsa_merge(gates, o_cmp, o_sel, o_win)
    a_rows = jnp.stack([k.reshape(n, s, A_HEADS, A_VDIM), v], axis=2)
    return jnp.concatenate([o_a, o_b], axis=-1), a_rows, long_rows, win_rows[:, s - min(NSA_WINDOW, s):]


def _even_mixer_sample(proj, pos, cache_a_kv, cache_nsa_kv, state_nsa_win, page_table, mw):
    g_qa, g_ka, lam, lam_init, g_subln, g_qb, g_kb, cmp_w = mw
    n, t = proj.shape[:2]
    qa, ka, va, qb, kvb, gb = _split_in0(proj)
    q, k, v = _diff_heads(qa, ka, va, pos, g_qa, g_ka)
    a_rows = jnp.stack([k.reshape(n, t, A_HEADS, A_VDIM), v], axis=2)
    a_all = jnp.concatenate([_paged_rows(cache_a_kv, page_table), a_rows], axis=1)
    L = a_all.shape[1]
    o = _diff_core(q, a_all[:, :, 0].reshape(n, L, A_HEADS, 2, HEAD_DIM), a_all[:, :, 1], pos,
                   jnp.arange(L, dtype=jnp.int32), lam)
    o_a = _diff_output(o, g_subln, lam_init)
    qn, qr, long_rows, win_rows, gates = _nsa_heads(qb, kvb, gb, pos, g_qb, g_kb)
    long_all = jnp.concatenate([_paged_rows(cache_nsa_kv, page_table), long_rows], axis=1)
    o_cmp, o_sel = _nsa_long_branches(qn, qr, long_all, pos, g_kb[0], *cmp_w)
    n_buf = state_nsa_win.shape[1]
    win_all = jnp.concatenate([state_nsa_win, win_rows], axis=1)
    o_win = _window_attn_sample(qr, win_all[:, :, 0], win_all[:, :, 1], n_buf, NSA_WINDOW)
    o_b = _nsa_merge(gates, o_cmp, o_sel, o_win)
    return jnp.concatenate([o_a, o_b], axis=-1), a_rows, long_rows, _roll_buffer(state_nsa_win, win_rows)


def _dilated_heads(proj, pos, g_qc, g_kc):
    n, t = proj.shape[:2]
    proj = proj.reshape(n, t, N_C_GROUPS, 3, C_HEADS, HEAD_DIM)
    return [(_partial_rope(_rms_norm(proj[:, :, gi, 0], g_qc[gi]), pos),
             _partial_rope(_rms_norm(proj[:, :, gi, 1], g_kc[gi]), pos),
             proj[:, :, gi, 2]) for gi in range(N_C_GROUPS)]


def _dilated_attn_prompt(q, k, v, dil, band):
    n, S, h, dh = q.shape
    L = S // dil

    def sub(x):
        return x.reshape(n, L, dil, h, dh).transpose(0, 2, 1, 3, 4).reshape(n * dil, L, h, dh)

    o, lse = _banded_attn(sub(q)[:, :, :, None], sub(k), sub(v), band)
    o = o.reshape(n, dil, L, h, dh).transpose(0, 2, 1, 3, 4).reshape(n, S, h, dh)
    lse = lse.reshape(n, dil, L, h).transpose(0, 2, 1, 3).reshape(n, S, h)
    return o, lse


def _dilated_attn_sample(q, buf, new_rows, dil, window):
    n_buf, t = buf.shape[1], q.shape[1]
    n_keys = window // dil + 1
    idx = n_buf + np.arange(t)[:, None] - dil * np.arange(n_keys)[None, :]
    rows = _gather_rows(buf, new_rows, idx)
    s = jnp.einsum('nthd,ntkhd->nthk', q, rows[:, :, :, 0], preferred_element_type=F32) * SCALE
    p, lse = _masked_softmax(s, (idx >= 0)[None, :, None, :])
    o = jnp.einsum('nthk,ntkhd->nthd', p, rows[:, :, :, 1].astype(F32))
    return o.astype(q.dtype), lse


def _merge_dilations(outs, lses):
    w = jax.nn.softmax(jnp.stack(lses, axis=0), axis=0)
    o = jnp.einsum('gnth,gnthd->nthd', w, jnp.stack(outs, axis=0).astype(F32))
    return o.astype(outs[0].dtype)


def _odd_mixer_prompt(proj, pos, g_qc, g_kc):
    n, s = proj.shape[:2]
    outs, lses, bufs = [], [], []
    for (window, dil), (q, k, v) in zip(C_GROUPS, _dilated_heads(proj, pos, g_qc, g_kc)):
        o, lse = _dilated_attn_prompt(q, k, v, dil, window // dil)
        outs.append(o)
        lses.append(lse)
        bufs.append(jnp.stack([k, v], axis=2)[:, s - min(window, s):])
    return _merge_dilations(outs, lses).reshape(n, s, C_W), bufs


def _odd_mixer_sample(proj, pos, states, g_qc, g_kc):
    n, t = proj.shape[:2]
    outs, lses, bufs = [], [], []
    for (window, dil), (q, k, v), buf in zip(C_GROUPS, _dilated_heads(proj, pos, g_qc, g_kc), states):
        new_rows = jnp.stack([k, v], axis=2)
        o, lse = _dilated_attn_sample(q, buf, new_rows, dil, window)
        outs.append(o)
        lses.append(lse)
        bufs.append(_roll_buffer(buf, new_rows))
    return _merge_dilations(outs, lses).reshape(n, t, C_W), bufs


def _router_body(x_ref, w_ref, idx_ref, gate_ref):
    logits = jnp.dot(x_ref[...], w_ref[...], preferred_element_type=F32)
    lane = lax.broadcasted_iota(jnp.int32, logits.shape, 1).astype(F32)
    lg = jnp.where(lane < N_EXPERTS, logits, NEG_INF)
    v1 = jnp.max(lg, axis=-1, keepdims=True)
    i1 = jnp.min(jnp.where(lg == v1, lane, 128.0), axis=-1, keepdims=True)
    lg2 = jnp.where(lane == i1, NEG_INF, lg)
    v2 = jnp.max(lg2, axis=-1, keepdims=True)
    i2 = jnp.min(jnp.where(lg2 == v2, lane, 128.0), axis=-1, keepdims=True)
    e = jnp.exp(v2 - v1)
    idx_ref[...] = jnp.where(lane == 0, i1, jnp.where(lane == 1, i2, 0.0)).astype(jnp.int32)
    gate_ref[...] = jnp.where(lane == 0, 1.0 / (1.0 + e), jnp.where(lane == 1, e / (1.0 + e), 0.0))


def moe_router(h_bf16, w_router):
    m, d = h_bf16.shape
    tm = TOKEN_TILE
    w = jnp.pad(w_router, ((0, 0), (0, 128 - w_router.shape[1]))).astype(BF16)
    spec = pl.BlockSpec((tm, 128), lambda i: (i, 0))
    return pl.pallas_call(
        _router_body,
        grid=(m // tm,),
        in_specs=[pl.BlockSpec((tm, d), lambda i: (i, 0)), pl.BlockSpec((d, 128), lambda i: (0, 0))],
        out_specs=[spec, spec],
        out_shape=[jax.ShapeDtypeStruct((m, 128), jnp.int32), jax.ShapeDtypeStruct((m, 128), F32)],
        compiler_params=_cparams("parallel"),
        name="moe_router",
    )(h_bf16, w)


def _moe_combine_body(r_ref, g_ref, y0_ref, y1_ref, o_ref):
    g = g_ref[...]
    o_ref[...] = r_ref[...] + g[:, 0:1] * y0_ref[...] + g[:, 1:2] * y1_ref[...]


def moe_combine(resid, gates, y0, y1):
    m, d = resid.shape
    tm = TOKEN_TILE
    spec = pl.BlockSpec((tm, d), lambda i: (i, 0))
    return pl.pallas_call(
        _moe_combine_body,
        grid=(m // tm,),
        in_specs=[spec, pl.BlockSpec((tm, 128), lambda i: (i, 0)), spec, spec],
        out_specs=spec,
        out_shape=jax.ShapeDtypeStruct((m, d), F32),
        compiler_params=_cparams("parallel"),
        name="moe_combine",
    )(resid, gates, y0, y1)


ROW_COPY_TILE = 512


def _row_copy_desc(src_hbm, out_hbm, sem, s, d):
    return pltpu.make_async_copy(src_hbm.at[pl.ds(s, 1)], out_hbm.at[pl.ds(d, 1)], sem)


def _row_copy_body(src_idx_ref, dst_idx_ref, src_hbm, init_hbm, out_hbm, sem):
    del init_hbm
    n = src_idx_ref.shape[-1]

    def start(r, c):
        _row_copy_desc(src_hbm, out_hbm, sem, src_idx_ref[0, r], dst_idx_ref[0, r]).start()
        return c

    def wait(r, c):
        _row_copy_desc(src_hbm, out_hbm, sem, src_idx_ref[0, r], dst_idx_ref[0, r]).wait()
        return c

    lax.fori_loop(0, n, start, 0, unroll=8)
    lax.fori_loop(0, n, wait, 0, unroll=8)


def row_copy(src, src_idx, dst_idx, init, *, name):
    n = src_idx.shape[0]
    t = ROW_COPY_TILE
    assert n % t == 0 and src.dtype.itemsize == 4 and src.shape[1:] == init.shape[1:]
    idx_spec = pl.BlockSpec((None, 1, t), lambda i: (i, 0, 0), memory_space=pltpu.SMEM)
    any_spec = pl.BlockSpec(memory_space=pl.ANY)
    return pl.pallas_call(
        _row_copy_body,
        grid=(n // t,),
        in_specs=[idx_spec, idx_spec, any_spec, any_spec],
        out_specs=any_spec,
        out_shape=jax.ShapeDtypeStruct(init.shape, init.dtype),
        scratch_shapes=[pltpu.SemaphoreType.DMA(())],
        input_output_aliases={3: 0},
        compiler_params=_cparams("arbitrary"),
        name=name,
    )(src_idx.reshape(n // t, 1, t), dst_idx.reshape(n // t, 1, t), src, init)


def _moe(h_bf16, resid, w_router, wg, wu, wd):
    m, d = h_bf16.shape
    tm = TOKEN_TILE
    idx, gates = moe_router(h_bf16, w_router)
    top_i = idx[:, :TOP_K]
    flat_e = top_i.reshape(-1)
    onehot = (flat_e[:, None] == jnp.arange(N_EXPERTS, dtype=jnp.int32)[None, :]).astype(jnp.int32)
    running = jnp.cumsum(onehot, axis=0)
    counts = running[-1]
    rank = jnp.sum(running * onehot, axis=1) - 1
    padded = ((counts + tm - 1) // tm) * tm
    pend = jnp.cumsum(padded)
    dest = (jnp.sum((pend - padded)[None, :] * onehot, axis=1) + rank).astype(jnp.int32)
    p_rows = m * TOP_K + N_EXPERTS * tm
    tile_start = jnp.arange(p_rows // tm) * tm
    tile_expert = jnp.minimum(jnp.sum(tile_start[:, None] >= pend[None, :], axis=1), N_EXPERTS - 1).astype(jnp.int32)

    words = lax.bitcast_convert_type(h_bf16.reshape(m, d // 2, 2), jnp.uint32)
    slot = jnp.arange(m * TOP_K, dtype=jnp.int32)
    xs_words = row_copy(words, slot // TOP_K, dest, jnp.zeros((p_rows, d // 2), jnp.uint32), name="moe_dispatch")
    xs = lax.bitcast_convert_type(xs_words, BF16).reshape(p_rows, d)
    ys = moe_grouped_ffn(xs, tile_expert, wg, wu, wd, tf=1792)
    back = dest.reshape(m, TOP_K).T.reshape(-1)
    yk = row_copy(ys, back, slot, jnp.zeros((m * TOP_K, d), F32), name="moe_collect")
    return moe_combine(resid, gates, yk[:m], yk[m:])


COL_QA, COL_KA, COL_VA, COL_QB = 0, A_QK_W, 2 * A_QK_W, 2 * A_QK_W + A_V_W
COL_KVB = COL_QB + NSA_Q_W
COL_GATE = COL_KVB + NSA_KV_W
KVB_PAIR = NSA_KV_HEADS * HEAD_DIM


def _pad_axis(x, axis, size):
    pad = [(0, 0)] * x.ndim
    pad[axis] = (0, size - x.shape[axis])
    return jnp.pad(x, pad)


def even_mixer(proj, proj16, tables, nb, seq, db, dt, caches, mw):
    g_qa, g_ka, lam_rows, lam_init, g_subln, g_qb, g_kb, cmp_w = mw
    cache_a_kv, cache_nsa_kv, state_nsa_win, page_table = caches
    pe_k, w_k1, w_k2, pe_v, w_v1, w_v2 = cmp_w
    mp = nb * seq
    ms = db * dt
    assert dt == 4
    rope16_32 = ((True, BF16), (True, F32))
    qk_a16, qk_a32 = head_norm_rope(proj, jnp.stack([_head_gain(g_qa, A_QK_W), _head_gain(g_ka, A_QK_W)]), tables,
                                    width=A_QK_W, col0=0, outs=rope16_32, name="hnr_diff_qk")
    qn16, qr16 = head_norm_rope(proj, _head_gain(g_qb, NSA_Q_W)[None], tables, width=NSA_Q_W,
                                col0=COL_QB // NSA_Q_W, outs=((False, BF16), (True, BF16)), name="hnr_nsa_q")
    ks16, ks32 = head_norm_rope(proj, _head_gain(g_kb[1], KVB_PAIR)[None], tables, width=KVB_PAIR,
                                col0=(COL_KVB + 2 * KVB_PAIR) // KVB_PAIR, outs=rope16_32, name="hnr_nsa_kslc")
    kw16, kw32 = head_norm_rope(proj, _head_gain(g_kb[2], KVB_PAIR)[None], tables, width=KVB_PAIR,
                                col0=(COL_KVB + 4 * KVB_PAIR) // KVB_PAIR, outs=rope16_32, name="hnr_nsa_kwin")

    o_a = diff_attention_prompt(qk_a16, proj16, lam_rows, g_subln, nb=nb, seq=seq, lam_init=lam_init)

    nchunk = seq // CMP_STRIDE

    def chunks(col):
        xc = proj16[:mp, col:col + KVB_PAIR].reshape(nb, nchunk, CMP_STRIDE, NSA_KV_HEADS, HEAD_DIM)
        return xc.transpose(0, 3, 1, 2, 4).reshape(nb * NSA_KV_HEADS, nchunk, CMP_STRIDE * HEAD_DIM)

    def pair_lanes(c):
        return c.reshape(nb, NSA_KV_HEADS, nchunk, HEAD_DIM).transpose(0, 2, 1, 3).reshape(nb, nchunk, KVB_PAIR)

    k_cmp = pair_lanes(compress_blocks(chunks(COL_KVB), pe_k, w_k1, w_k2, g_kb[0]))
    v_cmp = pair_lanes(compress_blocks(chunks(COL_KVB + KVB_PAIR), pe_v, w_v1, w_v2, None))
    o_cmp, sel = nsa_compressed_prompt(qn16, k_cmp, v_cmp, nb=nb, seq=seq)
    o_sel = nsa_branch_prompt(qr16, ks16, 0, proj16, (COL_KVB + 3 * KVB_PAIR) // KVB_PAIR, sel,
                              nb=nb, seq=seq, band=None, name="nsa_sel_prompt")
    o_win = nsa_branch_prompt(qr16, kw16, 0, proj16, (COL_KVB + 5 * KVB_PAIR) // KVB_PAIR, None,
                              nb=nb, seq=seq, band=NSA_WINDOW, name="nsa_win_prompt")
    a_rows = jnp.concatenate([qk_a32[:, A_QK_W:], proj[:, COL_VA:COL_VA + A_V_W]], axis=1)
    long_rows = jnp.concatenate([proj[:, COL_KVB:COL_KVB + 2 * KVB_PAIR], ks32,
                                 proj[:, COL_KVB + 3 * KVB_PAIR:COL_KVB + 4 * KVB_PAIR]], axis=1)
    win_rows = jnp.concatenate([kw32, proj[:, COL_KVB + 5 * KVB_PAIR:COL_KVB + 6 * KVB_PAIR]], axis=1)

    past = page_table.shape[1] * cache_a_kv.shape[1]
    q_s = qk_a32[mp:, :A_QK_W].reshape(db, dt, A_HEADS, 128).transpose(0, 2, 1, 3).reshape(db, A_HEADS * dt, 128)
    new_page = _pad_axis(a_rows[mp:].reshape(db, dt * 2 * A_HEADS, 128), 1, 128)
    cache_rows = cache_a_kv.reshape(cache_a_kv.shape[0], -1, A_VDIM)
    y_s = diff_attention_sample(jnp.concatenate([q_s, q_s], axis=1), new_page, cache_rows, page_table, lam_rows,
                                g_subln, lam_init=lam_init)
    o_a_s = y_s.reshape(db, A_HEADS, dt, A_VDIM).transpose(0, 2, 1, 3).reshape(ms, A_V_W).astype(BF16)

    pool = cache_nsa_kv.shape[0]
    xc = jnp.transpose(cache_nsa_kv[:, :, :2], (2, 0, 3, 1, 4)).astype(BF16)
    xc = xc.reshape(2, pool * NSA_KV_HEADS * (cache_nsa_kv.shape[1] // CMP_STRIDE), CMP_STRIDE * HEAD_DIM)
    ab = jnp.stack([matmul(xc[c], _w1_ab(w1), tn=2 * w1.shape[1], name="compress_cache")
                    for c, w1 in enumerate((w_k1, w_v1))])
    ab = ab.reshape(2, pool, NSA_KV_HEADS, cache_nsa_kv.shape[1] // CMP_STRIDE, ab.shape[-1])

    def sample_q(q16):
        qq = q16[mp:].reshape(db, dt, NSA_KV_HEADS, NSA_REP, HEAD_DIM).transpose(0, 2, 3, 1, 4)
        return _pad_axis(qq, 3, 8).reshape(db, NSA_KV_HEADS, NSA_REP * 8, HEAD_DIM)

    def sample_kv(x):
        return x.reshape(db, dt, NSA_KV_HEADS, HEAD_DIM).transpose(0, 2, 1, 3)

    new_rows = jnp.stack([sample_kv(ks32[mp:]), sample_kv(proj[mp:, COL_KVB + 3 * KVB_PAIR:COL_KVB + 4 * KVB_PAIR]),
                          sample_kv(kw32[mp:]), sample_kv(proj[mp:, COL_KVB + 5 * KVB_PAIR:COL_KVB + 6 * KVB_PAIR])],
                         axis=1)
    o_cmp_s, o_sel_s, o_win_s, win_state = nsa_sample(
        sample_q(qn16), sample_q(qr16), _pad_axis(new_rows, 3, 128), jnp.transpose(cache_nsa_kv, (0, 2, 3, 4, 1)),
        ab, jnp.transpose(state_nsa_win, (0, 2, 3, 4, 1)), page_table, cmp_w, g_kb[0], past=past)

    def sample_o(o):
        oo = o.reshape(db, NSA_KV_HEADS, NSA_REP, 8, HEAD_DIM)[:, :, :, :dt]
        return oo.transpose(0, 3, 1, 2, 4).reshape(ms, NSA_Q_W).astype(BF16)

    o_b = nsa_merge(proj, COL_GATE // 128, jnp.concatenate([o_cmp, sample_o(o_cmp_s)]),
                    jnp.concatenate([o_sel, sample_o(o_sel_s)]), jnp.concatenate([o_win, sample_o(o_win_s)]))
    cat = jnp.concatenate([jnp.concatenate([o_a, o_a_s]), o_b], axis=1)

    keep = min(NSA_WINDOW, seq)
    return (cat,
            a_rows[:mp].reshape(nb, seq, 2, A_HEADS, A_VDIM), a_rows[mp:].reshape(db, dt, 2, A_HEADS, A_VDIM),
            long_rows[:mp].reshape(nb, seq, 4, NSA_KV_HEADS, HEAD_DIM),
            long_rows[mp:].reshape(db, dt, 4, NSA_KV_HEADS, HEAD_DIM),
            win_rows[:mp].reshape(nb, seq, 2, NSA_KV_HEADS, HEAD_DIM)[:, seq - keep:],
            jnp.transpose(win_state, (0, 4, 1, 2, 3)))


def odd_mixer(proj, proj16, tables, nb, seq, db, dt, states, g_qc, g_kc):
    mp = nb * seq
    ms = db * dt
    outs, lses, bufs_p, bufs_s, qk32s = [], [], [], [], []
    for gi, ((window, dil), state) in enumerate(zip(C_GROUPS, states)):
        gains = jnp.stack([_head_gain(g_qc[gi], C_W), _head_gain(g_kc[gi], C_W)])
        (qk32,) = head_norm_rope(proj, gains, tables, width=C_W, col0=3 * gi, outs=((True, F32),),
                                 name=f"hnr_dil_{dil}")
        qk32s.append(qk32)
        v32 = proj[:, (3 * gi + 2) * C_W:(3 * gi + 3) * C_W]
        kv = jnp.concatenate([qk32[:mp, C_W:], v32[:mp]], axis=1)
        bufs_p.append(kv.reshape(nb, seq, 2, C_HEADS, HEAD_DIM)[:, seq - min(window, seq):])

        tok3 = lambda x: x.reshape(db, dt, C_W)
        oext, rolled = dilated_attention_sample(
            _pad_axis(tok3(qk32[mp:, :C_W]), 1, 8), _pad_axis(tok3(qk32[mp:, C_W:]), 1, 128),
            _pad_axis(tok3(v32[mp:]), 1, 128), jnp.transpose(state, (0, 2, 3, 4, 1)), window=window, dil=dil,
            hg=C_HEADS if state.shape[1] <= 512 else 4)
        outs.append(oext[:, :, :dt, :HEAD_DIM].transpose(0, 2, 1, 3).reshape(ms, C_W).astype(BF16))
        lses.append(jnp.repeat(oext[:, :, :dt, HEAD_DIM].transpose(0, 2, 1).reshape(ms, C_HEADS), HEAD_DIM, axis=1))
        bufs_s.append(jnp.transpose(rolled, (0, 4, 1, 2, 3)))
    y_p = dilated_attention_prompt_fused(qk32s, proj, nb=nb, seq=seq)
    return jnp.concatenate([y_p, dilation_merge(outs, lses)]), bufs_p, bufs_s


def odd_mixer_prompt(proj, proj16, tables, nb, seq, g_qc, g_kc):
    mp = nb * seq
    outs, lses, bufs = [], [], []
    for gi, (window, dil) in enumerate(C_GROUPS):
        gains = jnp.stack([_head_gain(g_qc[gi], C_W), _head_gain(g_kc[gi], C_W)])
        qk16, qk32 = head_norm_rope(proj, gains, tables, width=C_W, col0=3 * gi,
                                    outs=((True, BF16), (True, F32)), name=f"hnr_dil_{dil}")
        sub = seq // dil
        o, lse = dilated_attention_prompt(qk16, proj16, nb=nb, seq=seq, dil=dil, band=window // dil, gi=gi,
                                          t=min(sub, 256 if dil == 1 else 128))
        outs.append(o)
        lses.append(lse)
        kv = jnp.concatenate([qk32[:mp, C_W:], proj[:mp, (3 * gi + 2) * C_W:(3 * gi + 3) * C_W]], axis=1)
        bufs.append(kv.reshape(nb, seq, 2, C_HEADS, HEAD_DIM)[:, seq - min(window, seq):])
    return dilation_merge(outs, lses), bufs


def kernel(x_prompt, x_sample, cache_a_kv, cache_nsa_kv, state_nsa_win, state_c_w128, state_c_w512, state_c_w2048, page_table, norm0_mix, w_in0, g_qa, g_ka, lam_q1, lam_k1, lam_q2, lam_k2, g_subln, g_qb, g_kb, pe_cmp_k, w_cmp_k1, w_cmp_k2, pe_cmp_v, w_cmp_v1, w_cmp_v2, w_out0, norm0_ffn, w_ffn_gate, w_ffn_up, w_ffn_down, norm1_mix, w_in1, g_qc, g_kc, w_out1, norm1_ffn, w_router, w_moe_gate, w_moe_up, w_moe_down):
    nb, seq, d = x_prompt.shape
    db, dt, _ = x_sample.shape
    past = page_table.shape[1] * cache_a_kv.shape[1]
    mp = nb * seq
    ms = db * dt
    pos_p = jnp.arange(seq, dtype=jnp.int32)
    pos_s = past + jnp.arange(dt, dtype=jnp.int32)
    x = jnp.concatenate([x_prompt.reshape(mp, d), x_sample.reshape(ms, d)], axis=0)

    in0_w = w_in0.shape[1]
    in0_pad = -(-in0_w // 128) * 128
    w_in0_b = jnp.pad(w_in0, ((0, 0), (0, in0_pad - in0_w))).astype(BF16)
    tables = rope_tables(jnp.concatenate([jnp.tile(pos_p, nb), jnp.tile(pos_s, db)]))
    proj0, proj0_16 = matmul_dual(rmsnorm_cast(x, norm0_mix), w_in0_b, tn=in0_pad, name="in_proj0")
    lam_init = 0.8 - 0.6 * math.exp(-0.3 * 0)
    f = lambda a: a.astype(F32)
    cmp_w = (pe_cmp_k, w_cmp_k1, w_cmp_k2, pe_cmp_v, w_cmp_v1, w_cmp_v2)
    lam_rows = jnp.zeros((8, 128), F32).at[:4, :HEAD_DIM].set(jnp.stack([f(lam_q1), f(lam_k1), f(lam_q2), f(lam_k2)]))
    cat, a_kv_p, a_kv_s, nsa_kv_p, nsa_kv_s, nsa_win_p, nsa_win_s = even_mixer(
        proj0, proj0_16, tables, nb, seq, db, dt, (cache_a_kv, cache_nsa_kv, state_nsa_win, page_table),
        (g_qa, g_ka, lam_rows, lam_init, g_subln, g_qb, g_kb, cmp_w))
    x = matmul(cat, w_out0.astype(BF16), tn=d, res=x, name="out_proj0")
    act = swiglu_gate_up(rmsnorm_cast(x, norm0_ffn), w_ffn_gate.astype(BF16), w_ffn_up.astype(BF16), tn=1408)
    x = matmul(act, w_ffn_down.astype(BF16), tn=d, res=x, name="ffn_down")

    in1_w = w_in1.shape[1]
    proj1 = matmul(rmsnorm_cast(x, norm1_mix), w_in1.astype(BF16), tn=2304, name="in_proj1")
    mix, c_p, c_s = odd_mixer(proj1, None, tables, nb, seq, db, dt,
                              (state_c_w128, state_c_w512, state_c_w2048), g_qc, g_kc)
    x = matmul(mix, w_out1.astype(BF16), tn=d, res=x, name="out_proj1")
    x = _moe(rmsnorm_cast(x, norm1_ffn), x, w_router, w_moe_gate.astype(BF16), w_moe_up.astype(BF16),
             w_moe_down.astype(BF16))

    hp = x[:mp].reshape(nb, seq, d)
    hs = x[mp:].reshape(db, dt, d)
    return (hp, hs, a_kv_p, a_kv_s, nsa_kv_p, nsa_kv_s, nsa_win_p, nsa_win_s,
            c_p[0], c_s[0], c_p[1], c_s[1], c_p[2], c_s[2])
```

```python
import functools
import math

import jax
import jax.numpy as jnp
import numpy as np
from jax import lax
from jax.experimental import pallas as pl
from jax.experimental.pallas import tpu as pltpu

F32 = jnp.float32
BF16 = jnp.bfloat16

D_MODEL = 1024
HEAD_DIM = 64
ROT_DIM = HEAD_DIM // 4
ROPE_THETA = 500000.0
NORM_EPS = 1e-6
SCALE = HEAD_DIM ** -0.5
Q_BLOCK = 128
NEG_INF = -1e30
TINY = 1e-30
A_HEADS = 4
A_VDIM = 2 * HEAD_DIM
NSA_HEADS = 8
NSA_KV_HEADS = 2
NSA_REP = NSA_HEADS // NSA_KV_HEADS
CMP_LEN = 32
CMP_STRIDE = 16
SEL_BLOCK = 64
SEL_SHIFT = 6
SEL_TOPK = 16
SEL_Q_BLOCK = 64
NSA_WINDOW = 512
FORCE_BONUS = 1e3
C_HEADS = 16
C_GROUPS = ((128, 1), (512, 4), (2048, 16))
N_C_GROUPS = len(C_GROUPS)
A_QK_W = A_HEADS * 2 * HEAD_DIM
A_V_W = A_HEADS * A_VDIM
NSA_Q_W = NSA_HEADS * HEAD_DIM
NSA_KV_W = 6 * NSA_KV_HEADS * HEAD_DIM
NSA_GATE_W = 3 * NSA_HEADS
C_W = C_HEADS * HEAD_DIM
N_EXPERTS = 8
TOP_K = 2

VMEM_LIMIT_BYTES = 56 * 1024 * 1024
TOKEN_TILE = 512


def _cparams(*sem):
    return pltpu.CompilerParams(dimension_semantics=sem, vmem_limit_bytes=VMEM_LIMIT_BYTES)


def _rmsnorm_body(x_ref, g_ref, *o_refs):
    x = x_ref[...]
    ms = jnp.mean(x * x, axis=-1, keepdims=True)
    y = x * lax.rsqrt(ms + NORM_EPS) * g_ref[...]
    for o_ref in o_refs:
        o_ref[...] = y.astype(o_ref.dtype)


def rmsnorm_cast(x, g, *, with_f32=False):
    m, d = x.shape
    tm = TOKEN_TILE
    spec = pl.BlockSpec((tm, d), lambda i: (i, 0))
    outs = pl.pallas_call(
        _rmsnorm_body,
        grid=(m // tm,),
        in_specs=[spec, pl.BlockSpec((1, d), lambda i: (0, 0))],
        out_specs=[spec, spec] if with_f32 else [spec],
        out_shape=[jax.ShapeDtypeStruct((m, d), dt) for dt in ((BF16, F32) if with_f32 else (BF16,))],
        compiler_params=_cparams("parallel"),
        name="rmsnorm",
    )(x, g.reshape(1, d))
    return tuple(outs) if with_f32 else outs[0]


def _mm_body(x_ref, w_ref, o_ref):
    o_ref[...] = jnp.dot(x_ref[...], w_ref[...], preferred_element_type=F32).astype(o_ref.dtype)


def _mm_res_body(x_ref, w_ref, r_ref, o_ref):
    acc = jnp.dot(x_ref[...], w_ref[...], preferred_element_type=F32)
    o_ref[...] = (acc + r_ref[...]).astype(o_ref.dtype)


def matmul(x, w, *, tn, res=None, out_dtype=F32, name="matmul"):
    m, k = x.shape
    n = w.shape[1]
    tm = min(TOKEN_TILE, m)
    assert m % tm == 0 and n % tn == 0
    in_specs = [pl.BlockSpec((tm, k), lambda j, i: (i, 0)), pl.BlockSpec((k, tn), lambda j, i: (0, j))]
    args = [x, w]
    body = _mm_body
    if res is not None:
        in_specs.append(pl.BlockSpec((tm, tn), lambda j, i: (i, j)))
        args.append(res)
        body = _mm_res_body
    return pl.pallas_call(
        body,
        grid=(n // tn, m // tm),
        in_specs=in_specs,
        out_specs=pl.BlockSpec((tm, tn), lambda j, i: (i, j)),
        out_shape=jax.ShapeDtypeStruct((m, n), out_dtype),
        compiler_params=_cparams("parallel", "parallel"),
        name=name,
    )(*args)


def _gate_up_body(x_ref, wg_ref, wu_ref, o_ref):
    x = x_ref[...]
    g = jnp.dot(x, wg_ref[...], preferred_element_type=F32)
    u = jnp.dot(x, wu_ref[...], preferred_element_type=F32)
    o_ref[...] = (g * jax.nn.sigmoid(g) * u).astype(o_ref.dtype)


def swiglu_gate_up(x, wg, wu, *, tn):
    m, k = x.shape
    n = wg.shape[1]
    tm = TOKEN_TILE
    return pl.pallas_call(
        _gate_up_body,
        grid=(n // tn, m // tm),
        in_specs=[pl.BlockSpec((tm, k), lambda j, i: (i, 0)),
                  pl.BlockSpec((k, tn), lambda j, i: (0, j)),
                  pl.BlockSpec((k, tn), lambda j, i: (0, j))],
        out_specs=pl.BlockSpec((tm, tn), lambda j, i: (i, j)),
        out_shape=jax.ShapeDtypeStruct((m, n), BF16),
        compiler_params=_cparams("parallel", "parallel"),
        name="swiglu_gate_up",
    )(x, wg, wu)


def _moe_gate_up_body(te_ref, x_ref, wg_ref, wu_ref, o_ref):
    del te_ref
    x = x_ref[...].astype(BF16)
    g = jnp.dot(x, wg_ref[...], preferred_element_type=F32)
    u = jnp.dot(x, wu_ref[...], preferred_element_type=F32)
    o_ref[...] = (g * jax.nn.sigmoid(g) * u).astype(o_ref.dtype)


def _moe_down_body(te_ref, a_ref, wd_ref, o_ref):
    del te_ref
    o_ref[...] = jnp.dot(a_ref[...], wd_ref[...], preferred_element_type=F32)


def moe_grouped_ffn(xs, tile_expert, wg, wu, wd, *, tf):
    p, d = xs.shape
    f = wg.shape[2]
    tm = TOKEN_TILE
    nt = p // tm
    act = pl.pallas_call(
        _moe_gate_up_body,
        grid_spec=pltpu.PrefetchScalarGridSpec(
            num_scalar_prefetch=1,
            grid=(f // tf, nt),
            in_specs=[pl.BlockSpec((tm, d), lambda j, i, te: (i, 0)),
                      pl.BlockSpec((None, d, tf), lambda j, i, te: (te[i], 0, j)),
                      pl.BlockSpec((None, d, tf), lambda j, i, te: (te[i], 0, j))],
            out_specs=pl.BlockSpec((tm, tf), lambda j, i, te: (i, j)),
        ),
        out_shape=jax.ShapeDtypeStruct((p, f), BF16),
        compiler_params=_cparams("parallel", "arbitrary"),
        name="moe_gate_up",
    )(tile_expert, xs, wg, wu)
    return pl.pallas_call(
        _moe_down_body,
        grid_spec=pltpu.PrefetchScalarGridSpec(
            num_scalar_prefetch=1,
            grid=(nt,),
            in_specs=[pl.BlockSpec((tm, f), lambda i, te: (i, 0)),
                      pl.BlockSpec((None, f, d), lambda i, te: (te[i], 0, 0))],
            out_specs=pl.BlockSpec((tm, d), lambda i, te: (i, 0)),
        ),
        out_shape=jax.ShapeDtypeStruct((p, d), F32),
        compiler_params=_cparams("arbitrary"),
        name="moe_down",
    )(tile_expert, act, wd)


def _mm2_body(x_ref, w_ref, o32_ref, o16_ref):
    acc = jnp.dot(x_ref[...], w_ref[...], preferred_element_type=F32)
    o32_ref[...] = acc
    o16_ref[...] = acc.astype(BF16)


def matmul_dual(x, w, *, tn, name):
    m, k = x.shape
    n = w.shape[1]
    tm = TOKEN_TILE
    return pl.pallas_call(
        _mm2_body,
        grid=(n // tn, m // tm),
        in_specs=[pl.BlockSpec((tm, k), lambda j, i: (i, 0)), pl.BlockSpec((k, tn), lambda j, i: (0, j))],
        out_specs=[pl.BlockSpec((tm, tn), lambda j, i: (i, j)), pl.BlockSpec((tm, tn), lambda j, i: (i, j))],
        out_shape=[jax.ShapeDtypeStruct((m, n), F32), jax.ShapeDtypeStruct((m, n), BF16)],
        compiler_params=_cparams("parallel", "parallel"),
        name=name,
    )(x, w)


def rope_tables(pos):
    half = ROT_DIM // 2
    inv_freq = ROPE_THETA ** (-jnp.arange(half, dtype=F32) / half)
    ang = pos.astype(F32)[:, None] * inv_freq[None, :]
    cos, sin = jnp.cos(ang), jnp.sin(ang)
    m = pos.shape[0]
    z_half = jnp.zeros((m, half), F32)
    z_rest = jnp.zeros((m, HEAD_DIM - ROT_DIM), F32)
    c = jnp.concatenate([cos, cos, jnp.ones((m, HEAD_DIM - ROT_DIM), F32)], axis=1)
    s1 = jnp.concatenate([z_half, sin, z_rest], axis=1)
    s2 = jnp.concatenate([-sin, z_half, z_rest], axis=1)
    return tuple(jnp.tile(a, (1, 128 // HEAD_DIM)) for a in (c, s1, s2))


def _hnr_body(x_ref, g_ref, c_ref, s1_ref, s2_ref, *o_refs, width, outs):
    tm = x_ref.shape[0]
    lo = lax.broadcasted_iota(jnp.int32, (tm, 128), 1) < HEAD_DIM
    c, s1, s2 = c_ref[...], s1_ref[...], s2_ref[...]
    for j in range(width // 128):
        sl = slice(j * 128, (j + 1) * 128)
        x = x_ref[:, sl]
        x2 = x * x
        s_lo = jnp.sum(jnp.where(lo, x2, 0.0), axis=-1, keepdims=True)
        s_hi = jnp.sum(jnp.where(lo, 0.0, x2), axis=-1, keepdims=True)
        ms = jnp.where(lo, s_lo, s_hi) * (1.0 / HEAD_DIM)
        xn = x * lax.rsqrt(ms + NORM_EPS) * g_ref[:, sl]
        xr = xn * c + pltpu.roll(xn, ROT_DIM // 2, 1) * s1 + pltpu.roll(xn, 128 - ROT_DIM // 2, 1) * s2
        for (rope, _), o_ref in zip(outs, o_refs):
            o_ref[:, sl] = (xr if rope else xn).astype(o_ref.dtype)


def head_norm_rope(x, gains, tables, *, width, col0, outs, name):
    m = x.shape[0]
    ncol = gains.shape[0]
    tm = TOKEN_TILE
    tab_spec = pl.BlockSpec((tm, 128), lambda i, j: (i, 0))
    return pl.pallas_call(
        functools.partial(_hnr_body, width=width, outs=outs),
        grid=(m // tm, ncol),
        in_specs=[pl.BlockSpec((tm, width), lambda i, j: (i, col0 + j)),
                  pl.BlockSpec((None, 1, width), lambda i, j: (j, 0, 0)),
                  tab_spec, tab_spec, tab_spec],
        out_specs=[pl.BlockSpec((tm, width), lambda i, j: (i, j)) for _ in outs],
        out_shape=[jax.ShapeDtypeStruct((m, ncol * width), dt) for _, dt in outs],
        compiler_params=_cparams("parallel", "parallel"),
        name=name,
    )(x, gains.reshape(ncol, 1, width), *tables)


def _head_gain(g, width):
    return jnp.tile(g.astype(F32), width // HEAD_DIM)


def _step_tables(nq, lookback):
    qi, ki, first, last = [], [], [], []
    for q in range(nq):
        ks = list(range(q + 1)) if lookback is None else [k for k in range(q - lookback, q + 1) if k >= 0]
        for n, k in enumerate(ks):
            qi.append(q)
            ki.append(k)
            first.append(int(n == 0))
            last.append(int(n == len(ks) - 1))
    return tuple(jnp.asarray(a, jnp.int32) for a in (qi, ki, first, last))


def _pos_mask(qi, ki, t, band):
    row = lax.broadcasted_iota(jnp.int32, (t, t), 0)
    col = lax.broadcasted_iota(jnp.int32, (t, t), 1)
    d = (qi - ki) * t + row - col
    mask = d >= 0
    if band is not None:
        mask = mask & (d <= band)
    return mask


def _nt_dot(a, b):
    return lax.dot_general(a, b, (((1,), (1,)), ((), ())), preferred_element_type=F32)


def _online_update(sc, mask, v, m_ref, l_ref, acc_ref, idx):
    sc = jnp.where(mask, sc, NEG_INF)
    m_old = m_ref[idx]
    m_new = jnp.maximum(m_old, jnp.max(sc, axis=-1, keepdims=True))
    alpha = jnp.exp(m_old - m_new)
    p = jnp.where(mask, jnp.exp(sc - m_new), 0.0)
    l_ref[idx] = alpha * l_ref[idx] + jnp.sum(p, axis=-1, keepdims=True)
    acc_ref[idx] = alpha * acc_ref[idx] + jnp.dot(p.astype(BF16), v, preferred_element_type=F32)
    m_ref[idx] = m_new


def _init_state(m_ref, l_ref, acc_ref):
    m_ref[...] = jnp.full(m_ref.shape, NEG_INF, F32)
    l_ref[...] = jnp.zeros(l_ref.shape, F32)
    acc_ref[...] = jnp.zeros(acc_ref.shape, F32)


def _split_pair(q_ref, qs_ref, hb, lo):
    q = q_ref[:, hb * 128:(hb + 1) * 128].astype(F32) * SCALE
    qs_ref[2 * hb] = jnp.where(lo, q, 0.0).astype(BF16)
    qs_ref[2 * hb + 1] = jnp.where(lo, 0.0, q).astype(BF16)


def _gqa_query(q_ref, g, r, lo):
    col = g * NSA_REP + r
    blk = q_ref[:, (col // 2) * 128:(col // 2 + 1) * 128].astype(F32) * SCALE
    h = jnp.where(lo if col % 2 == 0 else jnp.logical_not(lo), blk, 0.0)
    d = h + pltpu.roll(h, HEAD_DIM, 1)
    return jnp.where(lo if g == 0 else jnp.logical_not(lo), d, 0.0).astype(BF16)


def _gqa_store(o_ref, outs, g, lo):
    keep = lo if g == 0 else jnp.logical_not(lo)
    dup = []
    for o in outs:
        z = jnp.where(keep, o, 0.0)
        dup.append(z + pltpu.roll(z, HEAD_DIM, 1))
    for pr in range(NSA_REP // 2):
        blk = g * (NSA_REP // 2) + pr
        o_ref[:, blk * 128:(blk + 1) * 128] = jnp.where(lo, dup[2 * pr], dup[2 * pr + 1]).astype(o_ref.dtype)


def _diff_body(qi_ref, ki_ref, fi_ref, la_ref, q_ref, k_ref, v_ref, lam_ref, gs_ref, o_ref,
               qs_ref, m_ref, l_ref, acc_ref, *, t, lam_init):
    s = pl.program_id(1)
    lo = lax.broadcasted_iota(jnp.int32, (t, 128), 1) < HEAD_DIM

    @pl.when(fi_ref[s] == 1)
    def _():
        for h in range(A_HEADS):
            _split_pair(q_ref, qs_ref, h, lo)
        _init_state(m_ref, l_ref, acc_ref)

    mask = _pos_mask(qi_ref[s], ki_ref[s], t, None)
    for h in range(A_HEADS):
        k = k_ref[:, h * 128:(h + 1) * 128]
        v = v_ref[:, h * 128:(h + 1) * 128]
        for var in range(2):
            _online_update(_nt_dot(qs_ref[2 * h + var], k), mask, v, m_ref, l_ref, acc_ref, 2 * h + var)

    @pl.when(la_ref[s] == 1)
    def _():
        lv = lam_ref[...]
        a = jnp.sum(lv[0:1] * lv[1:2], axis=-1, keepdims=True)
        b = jnp.sum(lv[2:3] * lv[3:4], axis=-1, keepdims=True)
        lam = jnp.exp(a) - jnp.exp(b) + lam_init
        for h in range(A_HEADS):
            o1 = acc_ref[2 * h] / jnp.maximum(l_ref[2 * h], TINY)
            o2 = acc_ref[2 * h + 1] / jnp.maximum(l_ref[2 * h + 1], TINY)
            o = o1 - lam * o2
            ms = jnp.mean(o * o, axis=-1, keepdims=True)
            y = o * lax.rsqrt(ms + NORM_EPS) * gs_ref[...] * (1.0 - lam_init)
            o_ref[:, h * 128:(h + 1) * 128] = y.astype(o_ref.dtype)


def diff_attention_prompt(qk16, v16, lam_rows, g_subln, *, nb, seq, lam_init, t=512):
    nq = seq // t
    tabs = _step_tables(nq, None)
    w = A_V_W
    qmap = lambda n, s, qi, ki, fi, la: (n * nq + qi[s], 0)
    kmap = lambda n, s, qi, ki, fi, la: (n * nq + ki[s], 1)
    vmap = lambda n, s, qi, ki, fi, la: (n * nq + ki[s], 2)
    const = lambda n, s, qi, ki, fi, la: (0, 0)
    return pl.pallas_call(
        functools.partial(_diff_body, t=t, lam_init=lam_init),
        grid_spec=pltpu.PrefetchScalarGridSpec(
            num_scalar_prefetch=4,
            grid=(nb, int(tabs[0].shape[0])),
            in_specs=[pl.BlockSpec((t, w), qmap), pl.BlockSpec((t, w), kmap), pl.BlockSpec((t, w), vmap),
                      pl.BlockSpec((8, 128), const), pl.BlockSpec((1, 128), const)],
            out_specs=pl.BlockSpec((t, w), qmap),
            scratch_shapes=[pltpu.VMEM((2 * A_HEADS, t, 128), BF16), pltpu.VMEM((2 * A_HEADS, t, 1), F32),
                            pltpu.VMEM((2 * A_HEADS, t, 1), F32), pltpu.VMEM((2 * A_HEADS, t, 128), F32)],
        ),
        out_shape=jax.ShapeDtypeStruct((nb * seq, w), BF16),
        compiler_params=_cparams("parallel", "arbitrary"),
        name="diff_attn_prompt",
    )(*tabs, qk16, qk16, v16, lam_rows, g_subln.reshape(1, A_VDIM).astype(F32))


def _dil_body(qi_ref, ki_ref, fi_ref, la_ref, q_ref, k_ref, v_ref, o_ref, lse_ref,
              qs_ref, m_ref, l_ref, acc_ref, *, t, band):
    s = pl.program_id(1)
    lo = lax.broadcasted_iota(jnp.int32, (t, 128), 1) < HEAD_DIM
    nhb = C_HEADS // 2

    @pl.when(fi_ref[s] == 1)
    def _():
        for hb in range(nhb):
            _split_pair(q_ref, qs_ref, hb, lo)
        _init_state(m_ref, l_ref, acc_ref)

    mask = _pos_mask(qi_ref[s], ki_ref[s], t, band)
    for hb in range(nhb):
        k = k_ref[:, hb * 128:(hb + 1) * 128]
        v = v_ref[:, hb * 128:(hb + 1) * 128]
        for var in range(2):
            _online_update(_nt_dot(qs_ref[2 * hb + var], k), mask, v, m_ref, l_ref, acc_ref, 2 * hb + var)

    @pl.when(la_ref[s] == 1)
    def _():
        for hb in range(nhb):
            l0 = jnp.maximum(l_ref[2 * hb], TINY)
            l1 = jnp.maximum(l_ref[2 * hb + 1], TINY)
            o = jnp.where(lo, acc_ref[2 * hb] / l0, acc_ref[2 * hb + 1] / l1)
            lse = jnp.where(lo, m_ref[2 * hb] + jnp.log(l0), m_ref[2 * hb + 1] + jnp.log(l1))
            o_ref[:, hb * 128:(hb + 1) * 128] = o.astype(o_ref.dtype)
            lse_ref[:, hb * 128:(hb + 1) * 128] = lse


def dilated_attention_prompt(qk16, v16, *, nb, seq, dil, band, gi, t):
    mp = nb * seq
    sub = seq // dil
    nq = sub // t
    tabs = _step_tables(nq, -(-band // t))
    w = C_W
    qk = qk16.reshape(qk16.shape[0] // dil, dil * 2 * w)
    vv = v16.reshape(v16.shape[0] // dil, dil * v16.shape[1])
    vcols = v16.shape[1] // w
    row = lambda b, x: (b // dil) * nq + x

    def qmap(b, s, qi, ki, fi, la):
        return (row(b, qi[s]), (b % dil) * 2)

    def kmap(b, s, qi, ki, fi, la):
        return (row(b, ki[s]), (b % dil) * 2 + 1)

    def vmap(b, s, qi, ki, fi, la):
        return (row(b, ki[s]), (b % dil) * vcols + gi * 3 + 2)

    def omap(b, s, qi, ki, fi, la):
        return (row(b, qi[s]), b % dil)

    nst = 2 * (C_HEADS // 2)
    o, lse = pl.pallas_call(
        functools.partial(_dil_body, t=t, band=band),
        grid_spec=pltpu.PrefetchScalarGridSpec(
            num_scalar_prefetch=4,
            grid=(nb * dil, int(tabs[0].shape[0])),
            in_specs=[pl.BlockSpec((t, w), qmap), pl.BlockSpec((t, w), kmap), pl.BlockSpec((t, w), vmap)],
            out_specs=[pl.BlockSpec((t, w), omap), pl.BlockSpec((t, w), omap)],
            scratch_shapes=[pltpu.VMEM((nst, t, 128), BF16), pltpu.VMEM((nst, t, 1), F32),
                            pltpu.VMEM((nst, t, 1), F32), pltpu.VMEM((nst, t, 128), F32)],
        ),
        out_shape=[jax.ShapeDtypeStruct((mp // dil, dil * w), BF16), jax.ShapeDtypeStruct((mp // dil, dil * w), F32)],
        compiler_params=_cparams("parallel", "arbitrary"),
        name=f"dilated_attn_prompt_{dil}",
    )(*tabs, qk, qk, vv)
    return o.reshape(mp, w), lse.reshape(mp, w)


def _dil_fused_body(*refs, seq):
    n_g = len(C_GROUPS)
    in_refs = refs[:3 * n_g]
    o_ref = refs[3 * n_g]
    og_refs = refs[3 * n_g + 1:3 * n_g + 1 + n_g]
    lg_refs = refs[3 * n_g + 1 + n_g:]
    for gi, (window, dil) in enumerate(C_GROUPS):
        q_ref, k_ref, v_ref = in_refs[3 * gi:3 * gi + 3]
        og_ref, lg_ref = og_refs[gi], lg_refs[gi]
        sub = seq // dil
        band = window // dil
        t = min(sub, 256 if dil == 1 else 128)
        nq = sub // t
        look = -(-band // t)
        lo = lax.broadcasted_iota(jnp.int32, (t, 128), 1) < HEAD_DIM

        def rows(rho, tile, t=t, dil=dil):
            return pl.ds(rho + dil * tile * t, t, stride=dil) if dil > 1 else pl.ds(tile * t, t)

        def residue(rho, carry, q_ref=q_ref, k_ref=k_ref, v_ref=v_ref, og_ref=og_ref, lg_ref=lg_ref,
                    t=t, nq=nq, look=look, band=band, lo=lo, rows=rows):
            for qi in range(nq):
                q = q_ref[rows(rho, qi), :] * SCALE
                qs = (jnp.where(lo, q, 0.0).astype(BF16), jnp.where(lo, 0.0, q).astype(BF16))
                m = [jnp.full((t, 1), NEG_INF, F32)] * 2
                l = [jnp.zeros((t, 1), F32)] * 2
                acc = [jnp.zeros((t, 128), F32)] * 2
                for ki in range(max(0, qi - look), qi + 1):
                    k = k_ref[rows(rho, ki), :].astype(BF16)
                    v = v_ref[rows(rho, ki), :].astype(BF16)
                    mask = _pos_mask(qi, ki, t, band)
                    for var in range(2):
                        p, alpha, m[var], l[var] = _softmax_step(_nt_dot(qs[var], k), mask, m[var], l[var])
                        acc[var] = alpha * acc[var] + jnp.dot(p.astype(BF16), v, preferred_element_type=F32)
                l = [jnp.maximum(x, TINY) for x in l]
                og_ref[rows(rho, qi), :] = jnp.where(lo, acc[0] / l[0], acc[1] / l[1])
                lg_ref[rows(rho, qi), :] = jnp.where(lo, m[0] + jnp.log(l[0]), m[1] + jnp.log(l[1]))
            return carry

        if dil == 1:
            residue(0, 0)
        else:
            lax.fori_loop(0, dil, residue, 0)

    chunk = 256
    for c in range(seq // chunk):
        sl = pl.ds(c * chunk, chunk)
        ls = [r[sl, :] for r in lg_refs]
        mx = functools.reduce(jnp.maximum, ls)
        es = [jnp.exp(x - mx) for x in ls]
        den = functools.reduce(lambda a, b: a + b, es)
        acc = functools.reduce(lambda a, b: a + b, [(e / den) * r[sl, :] for e, r in zip(es, og_refs)])
        o_ref[sl, :] = acc.astype(o_ref.dtype)


def dilated_attention_prompt_fused(qk32s, proj32, *, nb, seq):
    n_g = len(C_GROUPS)
    nhb = C_W // 128
    in_specs, args = [], []
    for gi in range(n_g):
        in_specs += [pl.BlockSpec((seq, 128), lambda n, hb: (n, hb)),
                     pl.BlockSpec((seq, 128), lambda n, hb: (n, nhb + hb)),
                     pl.BlockSpec((seq, 128), functools.partial(lambda n, hb, gi: (n, (3 * gi + 2) * nhb + hb), gi=gi))]
        args += [qk32s[gi], qk32s[gi], proj32]
    return pl.pallas_call(
        functools.partial(_dil_fused_body, seq=seq),
        grid=(nb, nhb),
        in_specs=in_specs,
        out_specs=pl.BlockSpec((seq, 128), lambda n, hb: (n, hb)),
        out_shape=jax.ShapeDtypeStruct((nb * seq, C_W), BF16),
        scratch_shapes=[pltpu.VMEM((seq, 128), F32) for _ in range(2 * n_g)],
        compiler_params=_cparams("parallel", "parallel"),
        name="dilated_attn_prompt",
    )(*args)


def _gqa_body(qi_ref, ki_ref, fi_ref, la_ref, q_ref, k_ref, v_ref, *rest, t, band, use_sel):
    if use_sel:
        sel_ref, o_ref, qs_ref, m_ref, l_ref, acc_ref = rest
    else:
        o_ref, qs_ref, m_ref, l_ref, acc_ref = rest
    s = pl.program_id(1)
    lo = lax.broadcasted_iota(jnp.int32, (t, 128), 1) < HEAD_DIM

    @pl.when(fi_ref[s] == 1)
    def _():
        for g in range(NSA_KV_HEADS):
            for r in range(NSA_REP):
                qs_ref[g * NSA_REP + r] = _gqa_query(q_ref, g, r, lo)
        _init_state(m_ref, l_ref, acc_ref)

    ki = ki_ref[s]
    mask = _pos_mask(qi_ref[s], ki, t, band)
    k = k_ref[...]
    v = v_ref[...]
    if use_sel:
        blk_row = lax.broadcasted_iota(jnp.int32, (128, t), 0)
        blk_col = jnp.right_shift(ki * t + lax.broadcasted_iota(jnp.int32, (128, t), 1), SEL_SHIFT)
        expand = jnp.where(blk_row == blk_col, 1.0, 0.0).astype(BF16)
    for g in range(NSA_KV_HEADS):
        mg = mask
        if use_sel:
            mg = mask & (jnp.dot(sel_ref[g], expand, preferred_element_type=F32) > 0.5)
        for r in range(NSA_REP):
            i = g * NSA_REP + r
            _online_update(_nt_dot(qs_ref[i], k), mg, v, m_ref, l_ref, acc_ref, i)

    @pl.when(la_ref[s] == 1)
    def _():
        for g in range(NSA_KV_HEADS):
            outs = [acc_ref[g * NSA_REP + r] / jnp.maximum(l_ref[g * NSA_REP + r], TINY) for r in range(NSA_REP)]
            _gqa_store(o_ref, outs, g, lo)


def nsa_branch_prompt(q16, k16, kcol, v16, vcol, sel, *, nb, seq, band, t=512, name):
    nq = seq // t
    tabs = _step_tables(nq, None if band is None else -(-band // t))
    qmap = lambda n, s, qi, ki, fi, la: (n * nq + qi[s], 0)
    kmap = lambda n, s, qi, ki, fi, la: (n * nq + ki[s], kcol)
    vmap = lambda n, s, qi, ki, fi, la: (n * nq + ki[s], vcol)
    in_specs = [pl.BlockSpec((t, NSA_Q_W), qmap), pl.BlockSpec((t, 128), kmap), pl.BlockSpec((t, 128), vmap)]
    args = [q16, k16, v16]
    if sel is not None:
        in_specs.append(pl.BlockSpec((NSA_KV_HEADS, t, 128), lambda n, s, qi, ki, fi, la: (0, n * nq + qi[s], 0)))
        args.append(sel)
    nst = NSA_HEADS
    return pl.pallas_call(
        functools.partial(_gqa_body, t=t, band=band, use_sel=sel is not None),
        grid_spec=pltpu.PrefetchScalarGridSpec(
            num_scalar_prefetch=4,
            grid=(nb, int(tabs[0].shape[0])),
            in_specs=in_specs,
            out_specs=pl.BlockSpec((t, NSA_Q_W), qmap),
            scratch_shapes=[pltpu.VMEM((nst, t, 128), BF16), pltpu.VMEM((nst, t, 1), F32),
                            pltpu.VMEM((nst, t, 1), F32), pltpu.VMEM((nst, t, 128), F32)],
        ),
        out_shape=jax.ShapeDtypeStruct((nb * seq, NSA_Q_W), BF16),
        compiler_params=_cparams("parallel", "arbitrary"),
        name=name,
    )(*tabs, *args)


def _cmp_finish_body(ab_ref, pe_ref, w1_ref, w2_ref, g_ref, o_ref, *, hid, norm):
    ab = ab_ref[...]
    pe_term = jnp.dot(pe_ref[...], w1_ref[...], preferred_element_type=F32)[0:1]
    h = ab[:, :hid] + pltpu.roll(ab[:, hid:], ab.shape[0] - 1, 0) + pe_term
    act = (h * jax.nn.sigmoid(h)).astype(BF16)
    y = jnp.dot(act, w2_ref[...], preferred_element_type=F32)
    if norm:
        y = y * lax.rsqrt(jnp.mean(y * y, axis=-1, keepdims=True) + NORM_EPS) * g_ref[...]
    o_ref[...] = y.astype(o_ref.dtype)


def _w1_ab(w1):
    half = w1.shape[0] // 2
    return jnp.concatenate([w1[:half], w1[half:]], axis=1).astype(BF16)


def compress_blocks(x_chunks, pe, w1, w2, gain):
    b, nchunk, cw = x_chunks.shape
    hid = w1.shape[1]
    ab = matmul(x_chunks.reshape(b * nchunk, cw), _w1_ab(w1), tn=2 * hid, name="compress_in")
    pe_rows = jnp.zeros((8, 2 * cw), BF16).at[0].set(pe.reshape(-1).astype(BF16))
    g = jnp.ones((1, HEAD_DIM), F32) if gain is None else gain.reshape(1, HEAD_DIM).astype(F32)
    const = lambda i: (0, 0)
    return pl.pallas_call(
        functools.partial(_cmp_finish_body, hid=hid, norm=gain is not None),
        grid=(b,),
        in_specs=[pl.BlockSpec((nchunk, 2 * hid), lambda i: (i, 0)), pl.BlockSpec((8, 2 * cw), const),
                  pl.BlockSpec((2 * cw, hid), const), pl.BlockSpec((hid, HEAD_DIM), const),
                  pl.BlockSpec((1, HEAD_DIM), const)],
        out_specs=pl.BlockSpec((None, nchunk, HEAD_DIM), lambda i: (i, 0, 0)),
        out_shape=jax.ShapeDtypeStruct((b, nchunk, HEAD_DIM), BF16),
        compiler_params=_cparams("parallel"),
        name="compress_finish",
    )(ab, pe_rows, w1.astype(BF16), w2.astype(BF16), g)


def _overlap_matrix(n_cmp, n_sel):
    c0 = np.arange(128)[:, None] * CMP_STRIDE
    s0 = np.arange(128)[None, :] * SEL_BLOCK
    ov = np.maximum(np.minimum(c0 + CMP_LEN, s0 + SEL_BLOCK) - np.maximum(c0, s0), 0) / CMP_LEN
    ov = ov * (np.arange(128)[:, None] < n_cmp) * (np.arange(128)[None, :] < n_sel)
    return jnp.asarray(ov, BF16)


def _cmp_body(q_ref, kc_ref, vc_ref, ov_ref, o_ref, sel_ref, *, t, pos0, n_cmp, n_sel):
    i = pl.program_id(1)
    lane = lax.broadcasted_iota(jnp.int32, (t, 128), 1)
    qpos = pos0 + i * t + lax.broadcasted_iota(jnp.int32, (t, 128), 0)
    lo = lane < HEAD_DIM
    vis = (lane * CMP_STRIDE + CMP_LEN - 1 <= qpos) & (lane < n_cmp)
    kc, vc, ov = kc_ref[...], vc_ref[...], ov_ref[...]
    cur = jnp.right_shift(qpos, SEL_SHIFT)
    valid = (lane <= cur) & (lane < n_sel)
    forced = (lane == 0) | (lane == cur) | (lane == cur - 1)
    for g in range(NSA_KV_HEADS):
        imp = jnp.zeros((t, 128), F32)
        outs = []
        for r in range(NSA_REP):
            sc = jnp.where(vis, _nt_dot(_gqa_query(q_ref, g, r, lo), kc), NEG_INF)
            m = jnp.max(sc, axis=-1, keepdims=True)
            e = jnp.where(vis, jnp.exp(sc - m), 0.0)
            p = (e / jnp.maximum(jnp.sum(e, axis=-1, keepdims=True), TINY)).astype(BF16)
            outs.append(jnp.dot(p, vc, preferred_element_type=F32))
            imp = imp + jnp.dot(p, ov, preferred_element_type=F32)
        _gqa_store(o_ref, outs, g, lo)
        score = jnp.where(valid, imp + jnp.where(forced, FORCE_BONUS, 0.0), NEG_INF)
        rank = jnp.zeros((t, 128), F32)
        for kk in range(n_sel):
            sk = score[:, kk:kk + 1]
            rank = rank + jnp.where((sk > score) | ((sk == score) & (lane > kk)), 1.0, 0.0)
        sel_ref[g] = jnp.where((rank < SEL_TOPK) & valid, 1.0, 0.0).astype(sel_ref.dtype)


def nsa_compressed_prompt(qn16, k_cmp, v_cmp, *, nb, seq, t=512):
    n_cmp = (seq - CMP_LEN) // CMP_STRIDE + 1
    n_sel = -(-seq // SEL_BLOCK)
    nq = seq // t
    qmap = lambda n, i: (n * nq + i, 0)
    cmap = lambda n, i: (n, 0, 0)
    return pl.pallas_call(
        functools.partial(_cmp_body, t=t, pos0=0, n_cmp=n_cmp, n_sel=n_sel),
        grid=(nb, nq),
        in_specs=[pl.BlockSpec((t, NSA_Q_W), qmap), pl.BlockSpec((None, 128, 128), cmap),
                  pl.BlockSpec((None, 128, 128), cmap), pl.BlockSpec((128, 128), lambda n, i: (0, 0))],
        out_specs=[pl.BlockSpec((t, NSA_Q_W), qmap),
                   pl.BlockSpec((NSA_KV_HEADS, t, 128), lambda n, i: (0, n * nq + i, 0))],
        out_shape=[jax.ShapeDtypeStruct((nb * seq, NSA_Q_W), BF16),
                   jax.ShapeDtypeStruct((NSA_KV_HEADS, nb * seq, 128), BF16)],
        compiler_params=_cparams("parallel", "parallel"),
        name="nsa_cmp_select_prompt",
    )(qn16, k_cmp, v_cmp, _overlap_matrix(n_cmp, n_sel))


def _gate_expand_matrices():
    lane = np.arange(128)[:, None]
    col = np.arange(NSA_Q_W)[None, :]
    return jnp.asarray(np.stack([(lane < NSA_GATE_W) & (lane % 3 == br) & (lane // 3 == col // HEAD_DIM)
                                 for br in range(3)]), BF16)


def _nsa_merge_body(gb_ref, e_ref, oc_ref, os_ref, ow_ref, o_ref):
    gates = jax.nn.sigmoid(gb_ref[...])
    hi = gates.astype(BF16)
    lo = (gates - hi.astype(F32)).astype(BF16)
    acc = jnp.zeros(o_ref.shape, F32)
    for br, b_ref in enumerate((oc_ref, os_ref, ow_ref)):
        w = jnp.dot(hi, e_ref[br], preferred_element_type=F32) + jnp.dot(lo, e_ref[br], preferred_element_type=F32)
        acc = acc + w * b_ref[...].astype(F32)
    o_ref[...] = acc.astype(o_ref.dtype)


def nsa_merge(proj32, gate_col, o_cmp, o_sel, o_win):
    m = o_cmp.shape[0]
    tm = TOKEN_TILE
    spec = pl.BlockSpec((tm, NSA_Q_W), lambda i: (i, 0))
    return pl.pallas_call(
        _nsa_merge_body,
        grid=(m // tm,),
        in_specs=[pl.BlockSpec((tm, 128), lambda i: (i, gate_col)),
                  pl.BlockSpec((3, 128, NSA_Q_W), lambda i: (0, 0, 0)), spec, spec, spec],
        out_specs=spec,
        out_shape=jax.ShapeDtypeStruct((m, NSA_Q_W), BF16),
        compiler_params=_cparams("parallel"),
        name="nsa_merge",
    )(proj32, _gate_expand_matrices(), o_cmp, o_sel, o_win)


def _dil_merge_body(o0, o1, o2, l0, l1, l2, o_ref):
    ls = [l0[...], l1[...], l2[...]]
    m = jnp.maximum(jnp.maximum(ls[0], ls[1]), ls[2])
    es = [jnp.exp(x - m) for x in ls]
    den = es[0] + es[1] + es[2]
    acc = sum((e / den) * o[...].astype(F32) for e, o in zip(es, (o0, o1, o2)))
    o_ref[...] = acc.astype(o_ref.dtype)


def dilation_merge(outs, lses):
    m, w = outs[0].shape
    tm = TOKEN_TILE
    spec = pl.BlockSpec((tm, w), lambda i: (i, 0))
    return pl.pallas_call(
        _dil_merge_body,
        grid=(m // tm,),
        in_specs=[spec] * 6,
        out_specs=spec,
        out_shape=jax.ShapeDtypeStruct((m, w), BF16),
        compiler_params=_cparams("parallel"),
        name="dilation_merge",
    )(*outs, *lses)


def _softmax_step(sc, mask, m, l):
    sc = jnp.where(mask, sc, NEG_INF)
    m_new = jnp.maximum(m, jnp.max(sc, axis=-1, keepdims=True))
    alpha = jnp.exp(m - m_new)
    p = jnp.where(mask, jnp.exp(sc - m_new), 0.0)
    return p, alpha, m_new, alpha * l + jnp.sum(p, axis=-1, keepdims=True)


def _diff_sample_body(pt_ref, *refs, npages, lam_init):
    del pt_ref
    page_refs = refs[:npages]
    q_ref, new_ref, lam_ref, gs_ref, o_ref = refs[npages:]
    rows = 2 * A_HEADS * 4
    ri = lax.broadcasted_iota(jnp.int32, (rows, 128), 0)
    lane = lax.broadcasted_iota(jnp.int32, (rows, 128), 1)
    first_variant = ri < rows // 2
    qs = jnp.where(first_variant == (lane < HEAD_DIM), q_ref[...] * SCALE, 0.0).astype(BF16)
    ncols = page_refs[0].shape[0]
    col = lax.broadcasted_iota(jnp.int32, (rows, ncols), 1)
    head = jnp.bitwise_and(jnp.right_shift(lax.broadcasted_iota(jnp.int32, (rows, ncols), 0), 2), A_HEADS - 1)
    page_mask = jnp.bitwise_and(col, 2 * A_HEADS - 1) == head
    tok = jnp.bitwise_and(ri, 3)
    new_mask = ((jnp.bitwise_and(lane, 2 * A_HEADS - 1) == jnp.bitwise_and(jnp.right_shift(ri, 2), A_HEADS - 1))
                & (jnp.right_shift(lane, 3) <= tok) & (lane < 4 * 2 * A_HEADS))
    pages = [pr[...].astype(BF16) for pr in page_refs] + [new_ref[...].astype(BF16)]
    masks = [page_mask] * npages + [new_mask]
    scs = [jnp.where(mk, _nt_dot(qs, pg), NEG_INF) for pg, mk in zip(pages, masks)]
    m = functools.reduce(jnp.maximum, [jnp.max(s, axis=-1, keepdims=True) for s in scs])
    ps = [jnp.where(mk, jnp.exp(s - m), 0.0) for s, mk in zip(scs, masks)]
    l = functools.reduce(lambda a, b: a + b, [jnp.sum(p, axis=-1, keepdims=True) for p in ps])
    acc = functools.reduce(lambda a, b: a + b, [
        jnp.dot(pltpu.roll(p, A_HEADS, 1).astype(BF16), pg, preferred_element_type=F32) for p, pg in zip(ps, pages)])

    lv = lam_ref[...]
    a = jnp.sum(lv[0:1] * lv[1:2], axis=-1, keepdims=True)
    b = jnp.sum(lv[2:3] * lv[3:4], axis=-1, keepdims=True)
    lam = jnp.exp(a) - jnp.exp(b) + lam_init
    o = acc / jnp.maximum(l, TINY)
    o = o[:rows // 2] - lam * o[rows // 2:]
    y = o * lax.rsqrt(jnp.mean(o * o, axis=-1, keepdims=True) + NORM_EPS) * gs_ref[...] * (1.0 - lam_init)
    o_ref[...] = y.astype(o_ref.dtype)


def diff_attention_sample(q_rows, new_page, cache_rows, page_table, lam_rows, g_subln, *, lam_init):
    db = q_rows.shape[0]
    npages = page_table.shape[1]
    prow = cache_rows.shape[1]
    page_specs = [pl.BlockSpec((None, prow, 128), functools.partial(lambda b, pt, p: (pt[b, p], 0, 0), p=p))
                  for p in range(npages)]
    per_b = lambda b, pt: (b, 0, 0)
    const = lambda b, pt: (0, 0)
    return pl.pallas_call(
        functools.partial(_diff_sample_body, npages=npages, lam_init=lam_init),
        grid_spec=pltpu.PrefetchScalarGridSpec(
            num_scalar_prefetch=1,
            grid=(db,),
            in_specs=page_specs + [pl.BlockSpec((None, 32, 128), per_b), pl.BlockSpec((None, 128, 128), per_b),
                                   pl.BlockSpec((8, 128), const), pl.BlockSpec((1, 128), const)],
            out_specs=pl.BlockSpec((None, 16, 128), per_b),
        ),
        out_shape=jax.ShapeDtypeStruct((db, 16, 128), F32),
        compiler_params=_cparams("parallel"),
        name="diff_attn_sample",
    )(page_table, *([cache_rows] * npages), q_rows, new_page, lam_rows, g_subln.reshape(1, A_VDIM).astype(F32))


def _place_new_columns(rolled, new_rows, t):
    sq = jnp.concatenate([new_rows, jnp.zeros_like(new_rows)], axis=1)
    new_t = pltpu.roll(jnp.transpose(sq)[:HEAD_DIM], 124, 1)
    if t > 128:
        new_t = jnp.concatenate([jnp.zeros((HEAD_DIM, t - 128), F32), new_t], axis=1)
    lane = lax.broadcasted_iota(jnp.int32, (HEAD_DIM, t), 1)
    return jnp.where(lane >= t - 4, new_t, rolled)


def _nsa_sample_body(pt_ref, *refs, npages, past, n_cmp, n_sel, hid):
    del pt_ref
    cn_refs = refs[:npages]
    ab_refs = refs[npages:2 * npages]
    (qn_ref, qr_ref, new_ref, sw_ref, pe_ref, w1k_ref, w1v_ref, w2k_ref, w2v_ref, gk_ref, ov_ref,
     oc_ref, os_ref, ow_ref, wout_ref) = refs[2 * npages:]
    rows = NSA_REP * 8
    nwin = sw_ref.shape[-1]
    lane = lax.broadcasted_iota(jnp.int32, (rows, 128), 1)
    tok = jnp.bitwise_and(lax.broadcasted_iota(jnp.int32, (rows, 128), 0), 7)
    lane8 = lane[:8]
    tok8 = tok[:8]
    vis = (lane * CMP_STRIDE + CMP_LEN - 1 <= past + tok) & (lane < n_cmp)
    cur = jnp.right_shift(past + tok8, SEL_SHIFT)
    valid = (lane8 <= cur) & (lane8 < n_sel)
    forced = (lane8 == 0) | (lane8 == cur) | (lane8 == cur - 1)
    new_mask = (lane <= tok) & (lane < 4)
    ov = ov_ref[...]
    pe_terms = [jnp.dot(pe_ref[c], w_ref[...], preferred_element_type=F32)[0:1]
                for c, w_ref in enumerate((w1k_ref, w1v_ref))]

    for g in range(NSA_KV_HEADS):
        cmp = []
        for c, w2_ref in enumerate((w2k_ref, w2v_ref)):
            a = jnp.concatenate([r[c, g, :, :hid] for r in ab_refs], axis=0)
            bb = jnp.concatenate([r[c, g, :, hid:] for r in ab_refs], axis=0)
            h = a + pltpu.roll(bb, a.shape[0] - 1, 0) + pe_terms[c]
            y = jnp.dot((h * jax.nn.sigmoid(h)).astype(BF16), w2_ref[...], preferred_element_type=F32)
            if c == 0:
                y = y * lax.rsqrt(jnp.mean(y * y, axis=-1, keepdims=True) + NORM_EPS) * gk_ref[...]
            cmp.append(y.astype(BF16))
        qn = (qn_ref[g].astype(F32) * SCALE).astype(BF16)
        qr = (qr_ref[g].astype(F32) * SCALE).astype(BF16)

        sc = jnp.where(vis, _nt_dot(qn, cmp[0]), NEG_INF)
        mx = jnp.max(sc, axis=-1, keepdims=True)
        e = jnp.where(vis, jnp.exp(sc - mx), 0.0)
        p = (e / jnp.maximum(jnp.sum(e, axis=-1, keepdims=True), TINY)).astype(BF16)
        oc_ref[g] = jnp.dot(p, cmp[1], preferred_element_type=F32)
        imp_r = jnp.dot(p, ov, preferred_element_type=F32)
        imp = imp_r[0:8] + imp_r[8:16] + imp_r[16:24] + imp_r[24:32]
        score = jnp.where(valid, imp + jnp.where(forced, FORCE_BONUS, 0.0), NEG_INF)
        rank = jnp.zeros((8, 128), F32)
        for kk in range(n_sel):
            sk = score[:, kk:kk + 1]
            rank = rank + jnp.where((sk > score) | ((sk == score) & (lane8 > kk)), 1.0, 0.0)
        sel = (rank < SEL_TOPK) & valid

        kt_all = jnp.concatenate([r[0, g] for r in cn_refs], axis=1).astype(BF16)
        vt_all = jnp.concatenate([r[1, g] for r in cn_refs], axis=1).astype(BF16)
        sel_f = jnp.where(sel, 1.0, 0.0)
        ncached = kt_all.shape[1]
        blk_of_col = jnp.right_shift(lax.broadcasted_iota(jnp.int32, (128, ncached), 1), SEL_SHIFT)
        expand = jnp.where(lax.broadcasted_iota(jnp.int32, (128, ncached), 0) == blk_of_col, 1.0, 0.0).astype(BF16)
        flags = jnp.dot(sel_f.astype(BF16), expand, preferred_element_type=F32)
        mask = jnp.concatenate([flags] * NSA_REP, axis=0) > 0.5
        last_blk = past // SEL_BLOCK
        flag_new = sel_f[:, last_blk:last_blk + 1] + jnp.zeros((8, 128), F32)
        mask_new = new_mask & (jnp.concatenate([flag_new] * NSA_REP, axis=0) > 0.5)
        s1 = jnp.where(mask, jnp.dot(qr, kt_all, preferred_element_type=F32), NEG_INF)
        s2 = jnp.where(mask_new, _nt_dot(qr, new_ref[0, g].astype(BF16)), NEG_INF)
        m = jnp.maximum(jnp.max(s1, axis=-1, keepdims=True), jnp.max(s2, axis=-1, keepdims=True))
        e1 = jnp.where(mask, jnp.exp(s1 - m), 0.0)
        e2 = jnp.where(mask_new, jnp.exp(s2 - m), 0.0)
        l = jnp.maximum(jnp.sum(e1, axis=-1, keepdims=True) + jnp.sum(e2, axis=-1, keepdims=True), TINY)
        os_ref[g] = (_nt_dot(e1.astype(BF16), vt_all)
                     + jnp.dot(e2.astype(BF16), new_ref[1, g].astype(BF16), preferred_element_type=F32)) / l

        wl = lax.broadcasted_iota(jnp.int32, (rows, nwin), 1)
        wt = jnp.bitwise_and(lax.broadcasted_iota(jnp.int32, (rows, nwin), 0), 7)
        wmask = wl >= nwin + wt - NSA_WINDOW
        kt = sw_ref[0, g]
        vt = sw_ref[1, g]
        m = jnp.full((rows, 1), NEG_INF, F32)
        l = jnp.zeros((rows, 1), F32)
        pr, alpha, m, l = _softmax_step(jnp.dot(qr, kt.astype(BF16), preferred_element_type=F32), wmask, m, l)
        acc = _nt_dot(pr.astype(BF16), vt.astype(BF16))
        pr, alpha, m, l = _softmax_step(_nt_dot(qr, new_ref[2, g].astype(BF16)), new_mask, m, l)
        acc = alpha * acc + jnp.dot(pr.astype(BF16), new_ref[3, g].astype(BF16), preferred_element_type=F32)
        ow_ref[g] = acc / jnp.maximum(l, TINY)
        wout_ref[0, g] = _place_new_columns(pltpu.roll(kt, nwin - 4, 1), new_ref[2, g], nwin)
        wout_ref[1, g] = _place_new_columns(pltpu.roll(vt, nwin - 4, 1), new_ref[3, g], nwin)


def nsa_sample(qn, qr, new_rows, cache_t, ab, win_state, page_table, cmp_w, g_kc, *, past):
    pe_k, w_k1, w_k2, pe_v, w_v1, w_v2 = cmp_w
    db = qn.shape[0]
    npages = page_table.shape[1]
    hid = w_k1.shape[1]
    nwin = win_state.shape[-1]
    n_cmp = (past + 4 - CMP_LEN) // CMP_STRIDE + 1
    n_sel = -(-(past + 4) // SEL_BLOCK)
    assert n_cmp <= npages * 8 - 1 and past % SEL_BLOCK == 0
    pe_rows = jnp.zeros((2, 8, pe_k.size), BF16).at[:, 0].set(
        jnp.stack([pe_k.reshape(-1), pe_v.reshape(-1)]).astype(BF16))
    cn_specs = [pl.BlockSpec((None, 2, NSA_KV_HEADS, HEAD_DIM, 128),
                             functools.partial(lambda b, pt, p: (pt[b, p], 1, 0, 0, 0), p=p)) for p in range(npages)]
    ab_specs = [pl.BlockSpec((2, None, NSA_KV_HEADS, 8, 2 * hid),
                             functools.partial(lambda b, pt, p: (0, pt[b, p], 0, 0, 0), p=p)) for p in range(npages)]
    b4 = lambda b, pt: (b, 0, 0, 0)
    b5 = lambda b, pt: (b, 0, 0, 0, 0)
    c2 = lambda b, pt: (0, 0)
    c3 = lambda b, pt: (0, 0, 0)
    o_spec = pl.BlockSpec((None, NSA_KV_HEADS, 32, HEAD_DIM), b4)
    o_shape = jax.ShapeDtypeStruct((db, NSA_KV_HEADS, 32, HEAD_DIM), F32)
    return pl.pallas_call(
        functools.partial(_nsa_sample_body, npages=npages, past=past, n_cmp=n_cmp, n_sel=n_sel, hid=hid),
        grid_spec=pltpu.PrefetchScalarGridSpec(
            num_scalar_prefetch=1,
            grid=(db,),
            in_specs=cn_specs + ab_specs + [
                pl.BlockSpec((None, NSA_KV_HEADS, 32, HEAD_DIM), b4), pl.BlockSpec((None, NSA_KV_HEADS, 32, HEAD_DIM), b4),
                pl.BlockSpec((None, 4, NSA_KV_HEADS, 128, HEAD_DIM), b5),
                pl.BlockSpec((None, 2, NSA_KV_HEADS, HEAD_DIM, nwin), b5),
                pl.BlockSpec((2, 8, pe_k.size), c3),
                pl.BlockSpec(w_k1.shape, c2), pl.BlockSpec(w_v1.shape, c2),
                pl.BlockSpec(w_k2.shape, c2), pl.BlockSpec(w_v2.shape, c2),
                pl.BlockSpec((1, HEAD_DIM), c2), pl.BlockSpec((128, 128), c2)],
            out_specs=[o_spec, o_spec, o_spec, pl.BlockSpec((None, 2, NSA_KV_HEADS, HEAD_DIM, nwin), b5)],
        ),
        out_shape=[o_shape, o_shape, o_shape, jax.ShapeDtypeStruct(win_state.shape, F32)],
        compiler_params=_cparams("parallel"),
        name="nsa_sample",
    )(page_table, *([cache_t] * npages), *([ab] * npages), qn, qr, new_rows, win_state, pe_rows,
      w_k1.astype(BF16), w_v1.astype(BF16), w_k2.astype(BF16), w_v2.astype(BF16),
      g_kc.reshape(1, HEAD_DIM).astype(F32), _overlap_matrix(n_cmp, n_sel))


def _dil_sample_body(q_ref, kn_ref, vn_ref, st_ref, o_ref, lse_ref, roll_ref, *, window, dil, hg):
    nbuf = st_ref.shape[-1]
    w = hg * HEAD_DIM
    r = hg * 8
    own = (jnp.right_shift(lax.broadcasted_iota(jnp.int32, (r, w), 0), 3)
           == jnp.right_shift(lax.broadcasted_iota(jnp.int32, (r, w), 1), 6))
    qbd = jnp.where(own, jnp.concatenate([q_ref[...] * SCALE] * hg, axis=0), 0.0).astype(BF16)
    kt, vt = st_ref[0], st_ref[1]
    kn, vn = kn_ref[...], vn_ref[...]
    tok = jnp.bitwise_and(lax.broadcasted_iota(jnp.int32, (r, nbuf), 0), 7)
    dist = nbuf + tok - lax.broadcasted_iota(jnp.int32, (r, nbuf), 1)
    mask = (dist <= window) & (jnp.bitwise_and(dist, dil - 1) == 0)
    ncol = lax.broadcasted_iota(jnp.int32, (r, 128), 1)
    nd = jnp.bitwise_and(lax.broadcasted_iota(jnp.int32, (r, 128), 0), 7) - ncol
    nmask = (nd >= 0) & (jnp.bitwise_and(nd, dil - 1) == 0) & (ncol < 4)
    s1 = jnp.where(mask, jnp.dot(qbd, kt.astype(BF16), preferred_element_type=F32), NEG_INF)
    s2 = jnp.where(nmask, _nt_dot(qbd, kn.astype(BF16)), NEG_INF)
    m = jnp.maximum(jnp.max(s1, axis=-1, keepdims=True), jnp.max(s2, axis=-1, keepdims=True))
    e1 = jnp.where(mask, jnp.exp(s1 - m), 0.0)
    e2 = jnp.where(nmask, jnp.exp(s2 - m), 0.0)
    l = jnp.maximum(jnp.sum(e1, axis=-1, keepdims=True) + jnp.sum(e2, axis=-1, keepdims=True), TINY)
    o_all = jnp.where(own, (_nt_dot(e1.astype(BF16), vt.astype(BF16))
                            + jnp.dot(e2.astype(BF16), vn.astype(BF16), preferred_element_type=F32)) / l, 0.0)
    o = o_all[0:8]
    for h in range(1, hg):
        o = o + o_all[h * 8:(h + 1) * 8]
    o_ref[...] = o
    lse_ref[...] = m + jnp.log(l) + jnp.zeros((r, 128), F32)

    lane = lax.broadcasted_iota(jnp.int32, (w, nbuf), 1)
    for kv, (old, new) in enumerate(((kt, kn), (vt, vn))):
        new_t = jnp.concatenate([jnp.transpose(new[:, b * 128:(b + 1) * 128]) for b in range(w // 128)], axis=0)
        new_t = pltpu.roll(new_t, 124, 1)
        if nbuf > 128:
            new_t = jnp.concatenate([jnp.zeros((w, nbuf - 128), F32), new_t], axis=1)
        roll_ref[kv] = jnp.where(lane >= nbuf - 4, new_t, pltpu.roll(old, nbuf - 4, 1))


def dilated_attention_sample(q8, k_new, v_new, state_t, *, window, dil, hg=4):
    db = q8.shape[0]
    nbuf = state_t.shape[-1]
    assert dil & (dil - 1) == 0 and nbuf >= window and hg % 2 == 0
    w = hg * HEAD_DIM
    return pl.pallas_call(
        functools.partial(_dil_sample_body, window=window, dil=dil, hg=hg),
        grid=(db, C_HEADS // hg),
        in_specs=[pl.BlockSpec((None, 8, w), lambda b, j: (b, 0, j)),
                  pl.BlockSpec((None, 128, w), lambda b, j: (b, 0, j)),
                  pl.BlockSpec((None, 128, w), lambda b, j: (b, 0, j)),
                  pl.BlockSpec((None, 2, w, nbuf), lambda b, j: (b, 0, j, 0))],
        out_specs=[pl.BlockSpec((None, 8, w), lambda b, j: (b, 0, j)),
                   pl.BlockSpec((None, hg * 8, 128), lambda b, j: (b, j, 0)),
                   pl.BlockSpec((None, 2, w, nbuf), lambda b, j: (b, 0, j, 0))],
        out_shape=[jax.ShapeDtypeStruct((db, 8, C_W), F32), jax.ShapeDtypeStruct((db, C_HEADS * 8, 128), F32),
                   jax.ShapeDtypeStruct(state_t.shape, F32)],
        compiler_params=_cparams("parallel", "parallel"),
        name=f"dilated_attn_sample_{dil}",
    )(q8, k_new, v_new, state_t)


def _rms_norm(x, g):
    xf = x.astype(F32)
    y = xf * lax.rsqrt(jnp.mean(xf * xf, axis=-1, keepdims=True) + NORM_EPS)
    return (y * g.astype(F32)).astype(x.dtype)


def _partial_rope(x, pos):
    half = ROT_DIM // 2
    inv_freq = ROPE_THETA ** (-jnp.arange(half, dtype=F32) / half)
    ang = pos.astype(F32)[:, None] * inv_freq[None, :]
    shape = (1, pos.shape[0]) + (1,) * (x.ndim - 3) + (half,)
    cos = jnp.cos(ang).reshape(shape)
    sin = jnp.sin(ang).reshape(shape)
    xf = x.astype(F32)
    x1, x2 = xf[..., :half], xf[..., half:ROT_DIM]
    out = jnp.concatenate([x1 * cos - x2 * sin, x2 * cos + x1 * sin, xf[..., ROT_DIM:]], axis=-1)
    return out.astype(x.dtype)


def _masked_softmax(s, mask):
    s = jnp.where(mask, s.astype(F32), NEG_INF)
    m = jnp.max(s, axis=-1, keepdims=True)
    e = jnp.where(mask, jnp.exp(s - m), 0.0)
    l = jnp.maximum(jnp.sum(e, axis=-1, keepdims=True), TINY)
    return e / l, (m + jnp.log(l))[..., 0]


def _paged_rows(cache, page_table):
    g = cache[page_table]
    return g.reshape((g.shape[0], g.shape[1] * g.shape[2]) + g.shape[3:])


def _roll_buffer(buf, new):
    n_buf, t = buf.shape[1], new.shape[1]
    if t >= n_buf:
        return new[:, t - n_buf:]
    return jnp.concatenate([buf[:, t:], new], axis=1)


def _gather_rows(buf, new, idx):
    n_buf = buf.shape[1]
    from_buf = buf[:, np.clip(idx, 0, n_buf - 1)]
    from_new = new[:, np.clip(idx - n_buf, 0, new.shape[1] - 1)]
    sel = (idx < n_buf).reshape(idx.shape + (1,) * (buf.ndim - 2))
    return jnp.where(sel, from_buf, from_new)


def _banded_attn(q, k, v, band):
    n, L, g, r, dh = q.shape
    blk = math.gcd(L, Q_BLOCK)
    nb = L // blk
    pad = ((0, 0), (band, 0), (0, 0), (0, 0))
    idx = np.arange(nb)[:, None] * blk + np.arange(blk + band)[None, :]
    kb = jnp.pad(k, pad)[:, idx]
    vb = jnp.pad(v, pad)[:, idx]
    qb = q.reshape(n, nb, blk, g, r, dh)
    s = jnp.einsum('nbqgrd,nbkgd->nbgrqk', qb, kb, preferred_element_type=F32) * SCALE
    qpos = np.arange(nb)[:, None] * blk + np.arange(blk)[None, :]
    kpos = idx - band
    dist = qpos[:, :, None] - kpos[:, None, :]
    mask = (dist >= 0) & (dist <= band) & (kpos[:, None, :] >= 0)
    p, lse = _masked_softmax(s, mask[None, :, None, None])
    o = jnp.einsum('nbgrqk,nbkgd->nbqgrd', p, vb.astype(F32))
    return o.reshape(n, L, g, r, dh).astype(q.dtype), lse.transpose(0, 1, 4, 2, 3).reshape(n, L, g, r)


def _diff_heads(qa, ka, va, pos, g_q, g_k):
    n, t = qa.shape[:2]
    q = _partial_rope(_rms_norm(qa.reshape(n, t, A_HEADS, 2, HEAD_DIM), g_q), pos)
    k = _partial_rope(_rms_norm(ka.reshape(n, t, A_HEADS, 2, HEAD_DIM), g_k), pos)
    return q, k, va.reshape(n, t, A_HEADS, A_VDIM)


def _diff_core(q, k, v, qpos, kpos, lam):
    s = jnp.einsum('nqhmd,nkhmd->nhmqk', q, k, preferred_element_type=F32) * SCALE
    p, _ = _masked_softmax(s, (kpos[None, :] <= qpos[:, None])[None, None, None])
    a = p[:, :, 0] - lam * p[:, :, 1]
    return jnp.einsum('nhqk,nkhe->nqhe', a, v.astype(F32)).astype(v.dtype)


def _diff_attn_prompt(q, k, v, pos, lam):
    n, s = q.shape[:2]
    nb = s // Q_BLOCK
    qb = q.reshape((n, nb, Q_BLOCK) + q.shape[2:]).swapaxes(0, 1)
    ob = lax.map(lambda a: _diff_core(a[0], k, v, a[1], pos, lam), (qb, pos.reshape(nb, Q_BLOCK)))
    return ob.swapaxes(0, 1).reshape(n, s, A_HEADS, A_VDIM)


def _diff_output(o, g_sub, lam_init):
    n, t = o.shape[:2]
    return (_rms_norm(o, g_sub) * (1.0 - lam_init)).reshape(n, t, A_V_W)


def _nsa_heads(qb, kvb, gb, pos, g_q, g_k):
    n, t = qb.shape[:2]
    q = _rms_norm(qb.reshape(n, t, NSA_KV_HEADS, NSA_REP, HEAD_DIM), g_q)
    q_rot = _partial_rope(q, pos)
    kv = kvb.reshape(n, t, 6, NSA_KV_HEADS, HEAD_DIM)
    k_slc = _partial_rope(_rms_norm(kv[:, :, 2], g_k[1]), pos)
    k_win = _partial_rope(_rms_norm(kv[:, :, 4], g_k[2]), pos)
    long_rows = jnp.stack([kv[:, :, 0], kv[:, :, 1], k_slc, kv[:, :, 3]], axis=2)
    win_rows = jnp.stack([k_win, kv[:, :, 5]], axis=2)
    gates = jax.nn.sigmoid(gb.astype(F32)).reshape(n, t, NSA_KV_HEADS, NSA_REP, 3)
    return q, q_rot, long_rows, win_rows, gates


def _nsa_compress(rows, pe, w1, w2):
    n, L, g, dh = rows.shape
    n_cmp = (L - CMP_LEN) // CMP_STRIDE + 1
    idx = np.arange(n_cmp)[:, None] * CMP_STRIDE + np.arange(CMP_LEN)[None, :]
    blocks = rows[:, idx] + pe[None, None, :, None, :]
    flat = blocks.transpose(0, 1, 3, 2, 4).reshape(n, n_cmp, g, CMP_LEN * dh)
    return jax.nn.silu(flat @ w1) @ w2


def _nsa_cmp_attn(q, k_cmp, v_cmp, qpos):
    n_cmp = k_cmp.shape[1]
    end = jnp.asarray(np.arange(n_cmp) * CMP_STRIDE + CMP_LEN - 1)
    s = jnp.einsum('nqgrd,ncgd->nqgrc', q, k_cmp, preferred_element_type=F32) * SCALE
    visible = end[None, :] <= qpos[:, None]
    p, _ = _masked_softmax(s, visible[None, :, None, None, :])
    o = jnp.einsum('nqgrc,ncgd->nqgrd', p, v_cmp.astype(F32)).astype(q.dtype)
    return o, p


def _cmp_to_sel_overlap(n_cmp, n_sel):
    c0 = np.arange(n_cmp)[:, None] * CMP_STRIDE
    s0 = np.arange(n_sel)[None, :] * SEL_BLOCK
    ov = np.minimum(c0 + CMP_LEN, s0 + SEL_BLOCK) - np.maximum(c0, s0)
    return jnp.asarray(np.maximum(ov, 0) / CMP_LEN, dtype=F32)


def _nsa_select(p_cmp, qpos, n_sel):
    imp = jnp.einsum('nqgrc,cj->nqgj', p_cmp, _cmp_to_sel_overlap(p_cmp.shape[-1], n_sel))
    blk = jnp.arange(n_sel)[None, :]
    cur = (qpos // SEL_BLOCK)[:, None]
    valid = blk <= cur
    forced = (blk == 0) | (blk == cur) | (blk == cur - 1)
    score = jnp.where(valid[None, :, None], imp + jnp.where(forced, FORCE_BONUS, 0.0)[None, :, None], NEG_INF)
    _, sel = lax.top_k(score, min(SEL_TOPK, n_sel))
    return sel


def _nsa_sel_attn(q, k_blk, v_blk, sel, qpos):
    n, qc, g, r, dh = q.shape
    kk = sel.shape[-1]
    n_i = jnp.arange(n)[:, None, None, None]
    g_i = jnp.arange(g)[None, None, :, None]
    kg = k_blk[n_i, g_i, sel]
    vg = v_blk[n_i, g_i, sel]
    kpos = sel[..., None] * SEL_BLOCK + jnp.arange(SEL_BLOCK)
    visible = (kpos <= qpos[None, :, None, None, None]).reshape(n, qc, g, 1, kk * SEL_BLOCK)
    s = jnp.einsum('nqgrd,nqgkbd->nqgrkb', q, kg, preferred_element_type=F32)
    p, _ = _masked_softmax(s.reshape(n, qc, g, r, kk * SEL_BLOCK) * SCALE, visible)
    o = jnp.einsum('nqgrx,nqgxd->nqgrd', p, vg.reshape(n, qc, g, kk * SEL_BLOCK, dh).astype(F32))
    return o.astype(q.dtype)


def _nsa_long_branches(q, q_rot, long_all, qpos, g_kc, pe_k, w_k1, w_k2, pe_v, w_v1, w_v2):
    n, L, _, g, dh = long_all.shape
    k_cmp = _rms_norm(_nsa_compress(long_all[:, :, 0], pe_k, w_k1, w_k2), g_kc)
    v_cmp = _nsa_compress(long_all[:, :, 1], pe_v, w_v1, w_v2)
    o_cmp, p_cmp = _nsa_cmp_attn(q, k_cmp, v_cmp, qpos)
    n_sel = -(-L // SEL_BLOCK)
    sel = _nsa_select(p_cmp, qpos, n_sel)

    def to_blocks(x):
        x = jnp.pad(x, ((0, 0), (0, n_sel * SEL_BLOCK - L), (0, 0), (0, 0)))
        return x.reshape(n, n_sel, SEL_BLOCK, g, dh).transpose(0, 3, 1, 2, 4)

    k_blk, v_blk = to_blocks(long_all[:, :, 2]), to_blocks(long_all[:, :, 3])
    nq = q.shape[1]
    qc = math.gcd(nq, SEL_Q_BLOCK)
    nc = nq // qc

    def chunks(x):
        return x.reshape((n, nc, qc) + x.shape[2:]).swapaxes(0, 1)

    o_sel = lax.map(lambda a: _nsa_sel_attn(a[0], k_blk, v_blk, a[1], a[2]),
                    (chunks(q_rot), chunks(sel), qpos.reshape(nc, qc)))
    return o_cmp, o_sel.swapaxes(0, 1).reshape(q.shape)


def _window_attn_sample(q, k_all, v_all, n_buf, window):
    t = q.shape[1]
    dist = (n_buf + np.arange(t))[:, None] - np.arange(n_buf + t)[None, :]
    visible = (dist >= 0) & (dist <= window)
    s = jnp.einsum('ntgrd,nkgd->ntgrk', q, k_all, preferred_element_type=F32) * SCALE
    p, _ = _masked_softmax(s, visible[None, :, None, None, :])
    return jnp.einsum('ntgrk,nkgd->ntgrd', p, v_all.astype(F32)).astype(q.dtype)


def _nsa_merge(gates, o_cmp, o_sel, o_win):
    o = gates[..., 0:1] * o_cmp.astype(F32) + gates[..., 1:2] * o_sel.astype(F32) + gates[..., 2:3] * o_win.astype(F32)
    n, t = o.shape[:2]
    return o.reshape(n, t, NSA_Q_W).astype(o_cmp.dtype)


def _split_in0(proj):
    sizes = [A_QK_W, A_QK_W, A_V_W, NSA_Q_W, NSA_KV_W, NSA_GATE_W]
    return jnp.split(proj, [int(o) for o in np.cumsum(sizes)[:-1]], axis=-1)


def _even_mixer_prompt(proj, pos, mw):
    g_qa, g_ka, lam, lam_init, g_subln, g_qb, g_kb, cmp_w = mw
    n, s = proj.shape[:2]
    qa, ka, va, qb, kvb, gb = _split_in0(proj)
    q, k, v = _diff_heads(qa, ka, va, pos, g_qa, g_ka)
    o_a = _diff_output(_diff_attn_prompt(q, k, v, pos, lam), g_subln, lam_init)
    qn, qr, long_rows, win_rows, gates = _nsa_heads(qb, kvb, gb, pos, g_qb, g_kb)
    o_cmp, o_sel = _nsa_long_branches(qn, qr, long_rows, pos, g_kb[0], *cmp_w)
    o_win, _ = _banded_attn(qr, win_rows[:, :, 0], win_rows[:, :, 1], NSA_WINDOW)
    o_b = _nsa_merge(gates, o_cmp, o_sel, o_win)
    a_rows = jnp.stack([k.reshape(n, s, A_HEADS, A_VDIM), v], axis=2)
    return jnp.concatenate([o_a, o_b], axis=-1), a_rows, long_rows, win_rows[:, s - min(NSA_WINDOW, s):]


def _even_mixer_sample(proj, pos, cache_a_kv, cache_nsa_kv, state_nsa_win, page_table, mw):
    g_qa, g_ka, lam, lam_init, g_subln, g_qb, g_kb, cmp_w = mw
    n, t = proj.shape[:2]
    qa, ka, va, qb, kvb, gb = _split_in0(proj)
    q, k, v = _diff_heads(qa, ka, va, pos, g_qa, g_ka)
    a_rows = jnp.stack([k.reshape(n, t, A_HEADS, A_VDIM), v], axis=2)
    a_all = jnp.concatenate([_paged_rows(cache_a_kv, page_table), a_rows], axis=1)
    L = a_all.shape[1]
    o = _diff_core(q, a_all[:, :, 0].reshape(n, L, A_HEADS, 2, HEAD_DIM), a_all[:, :, 1], pos,
                   jnp.arange(L, dtype=jnp.int32), lam)
    o_a = _diff_output(o, g_subln, lam_init)
    qn, qr, long_rows, win_rows, gates = _nsa_heads(qb, kvb, gb, pos, g_qb, g_kb)
    long_all = jnp.concatenate([_paged_rows(cache_nsa_kv, page_table), long_rows], axis=1)
    o_cmp, o_sel = _nsa_long_branches(qn, qr, long_all, pos, g_kb[0], *cmp_w)
    n_buf = state_nsa_win.shape[1]
    win_all = jnp.concatenate([state_nsa_win, win_rows], axis=1)
    o_win = _window_attn_sample(qr, win_all[:, :, 0], win_all[:, :, 1], n_buf, NSA_WINDOW)
    o_b = _nsa_merge(gates, o_cmp, o_sel, o_win)
    return jnp.concatenate([o_a, o_b], axis=-1), a_rows, long_rows, _roll_buffer(state_nsa_win, win_rows)


def _dilated_heads(proj, pos, g_qc, g_kc):
    n, t = proj.shape[:2]
    proj = proj.reshape(n, t, N_C_GROUPS, 3, C_HEADS, HEAD_DIM)
    return [(_partial_rope(_rms_norm(proj[:, :, gi, 0], g_qc[gi]), pos),
             _partial_rope(_rms_norm(proj[:, :, gi, 1], g_kc[gi]), pos),
             proj[:, :, gi, 2]) for gi in range(N_C_GROUPS)]


def _dilated_attn_prompt(q, k, v, dil, band):
    n, S, h, dh = q.shape
    L = S // dil

    def sub(x):
        return x.reshape(n, L, dil, h, dh).transpose(0, 2, 1, 3, 4).reshape(n * dil, L, h, dh)

    o, lse = _banded_attn(sub(q)[:, :, :, None], sub(k), sub(v), band)
    o = o.reshape(n, dil, L, h, dh).transpose(0, 2, 1, 3, 4).reshape(n, S, h, dh)
    lse = lse.reshape(n, dil, L, h).transpose(0, 2, 1, 3).reshape(n, S, h)
    return o, lse


def _dilated_attn_sample(q, buf, new_rows, dil, window):
    n_buf, t = buf.shape[1], q.shape[1]
    n_keys = window // dil + 1
    idx = n_buf + np.arange(t)[:, None] - dil * np.arange(n_keys)[None, :]
    rows = _gather_rows(buf, new_rows, idx)
    s = jnp.einsum('nthd,ntkhd->nthk', q, rows[:, :, :, 0], preferred_element_type=F32) * SCALE
    p, lse = _masked_softmax(s, (idx >= 0)[None, :, None, :])
    o = jnp.einsum('nthk,ntkhd->nthd', p, rows[:, :, :, 1].astype(F32))
    return o.astype(q.dtype), lse


def _merge_dilations(outs, lses):
    w = jax.nn.softmax(jnp.stack(lses, axis=0), axis=0)
    o = jnp.einsum('gnth,gnthd->nthd', w, jnp.stack(outs, axis=0).astype(F32))
    return o.astype(outs[0].dtype)


def _odd_mixer_prompt(proj, pos, g_qc, g_kc):
    n, s = proj.shape[:2]
    outs, lses, bufs = [], [], []
    for (window, dil), (q, k, v) in zip(C_GROUPS, _dilated_heads(proj, pos, g_qc, g_kc)):
        o, lse = _dilated_attn_prompt(q, k, v, dil, window // dil)
        outs.append(o)
        lses.append(lse)
        bufs.append(jnp.stack([k, v], axis=2)[:, s - min(window, s):])
    return _merge_dilations(outs, lses).reshape(n, s, C_W), bufs


def _odd_mixer_sample(proj, pos, states, g_qc, g_kc):
    n, t = proj.shape[:2]
    outs, lses, bufs = [], [], []
    for (window, dil), (q, k, v), buf in zip(C_GROUPS, _dilated_heads(proj, pos, g_qc, g_kc), states):
        new_rows = jnp.stack([k, v], axis=2)
        o, lse = _dilated_attn_sample(q, buf, new_rows, dil, window)
        outs.append(o)
        lses.append(lse)
        bufs.append(_roll_buffer(buf, new_rows))
    return _merge_dilations(outs, lses).reshape(n, t, C_W), bufs


def _router_body(x_ref, w_ref, idx_ref, gate_ref):
    logits = jnp.dot(x_ref[...], w_ref[...], preferred_element_type=F32)
    lane = lax.broadcasted_iota(jnp.int32, logits.shape, 1).astype(F32)
    lg = jnp.where(lane < N_EXPERTS, logits, NEG_INF)
    v1 = jnp.max(lg, axis=-1, keepdims=True)
    i1 = jnp.min(jnp.where(lg == v1, lane, 128.0), axis=-1, keepdims=True)
    lg2 = jnp.where(lane == i1, NEG_INF, lg)
    v2 = jnp.max(lg2, axis=-1, keepdims=True)
    i2 = jnp.min(jnp.where(lg2 == v2, lane, 128.0), axis=-1, keepdims=True)
    e = jnp.exp(v2 - v1)
    idx_ref[...] = jnp.where(lane == 0, i1, jnp.where(lane == 1, i2, 0.0)).astype(jnp.int32)
    gate_ref[...] = jnp.where(lane == 0, 1.0 / (1.0 + e), jnp.where(lane == 1, e / (1.0 + e), 0.0))


def moe_router(h_bf16, w_router):
    m, d = h_bf16.shape
    tm = TOKEN_TILE
    w = jnp.pad(w_router, ((0, 0), (0, 128 - w_router.shape[1]))).astype(BF16)
    spec = pl.BlockSpec((tm, 128), lambda i: (i, 0))
    return pl.pallas_call(
        _router_body,
        grid=(m // tm,),
        in_specs=[pl.BlockSpec((tm, d), lambda i: (i, 0)), pl.BlockSpec((d, 128), lambda i: (0, 0))],
        out_specs=[spec, spec],
        out_shape=[jax.ShapeDtypeStruct((m, 128), jnp.int32), jax.ShapeDtypeStruct((m, 128), F32)],
        compiler_params=_cparams("parallel"),
        name="moe_router",
    )(h_bf16, w)


def _moe_combine_body(r_ref, g_ref, y0_ref, y1_ref, o_ref):
    g = g_ref[...]
    o_ref[...] = r_ref[...] + g[:, 0:1] * y0_ref[...] + g[:, 1:2] * y1_ref[...]


def moe_combine(resid, gates, y0, y1):
    m, d = resid.shape
    tm = TOKEN_TILE
    spec = pl.BlockSpec((tm, d), lambda i: (i, 0))
    return pl.pallas_call(
        _moe_combine_body,
        grid=(m // tm,),
        in_specs=[spec, pl.BlockSpec((tm, 128), lambda i: (i, 0)), spec, spec],
        out_specs=spec,
        out_shape=jax.ShapeDtypeStruct((m, d), F32),
        compiler_params=_cparams("parallel"),
        name="moe_combine",
    )(resid, gates, y0, y1)


def _moe(h_bf16, h_f32, resid, w_router, wg, wu, wd):
    m, d = h_bf16.shape
    tm = TOKEN_TILE
    idx, gates = moe_router(h_bf16, w_router)
    top_i = idx[:, :TOP_K]
    flat_e = top_i.reshape(-1)
    onehot = (flat_e[:, None] == jnp.arange(N_EXPERTS, dtype=jnp.int32)[None, :]).astype(jnp.int32)
    running = jnp.cumsum(onehot, axis=0)
    counts = running[-1]
    rank = jnp.sum(running * onehot, axis=1) - 1
    padded = ((counts + tm - 1) // tm) * tm
    pend = jnp.cumsum(padded)
    dest = (jnp.sum((pend - padded)[None, :] * onehot, axis=1) + rank).astype(jnp.int32)
    p_rows = m * TOP_K + N_EXPERTS * tm
    tile_start = jnp.arange(p_rows // tm) * tm
    tile_expert = jnp.minimum(jnp.sum(tile_start[:, None] >= pend[None, :], axis=1), N_EXPERTS - 1).astype(jnp.int32)

    slot = jnp.arange(m * TOP_K, dtype=jnp.int32)
    src_tok = jnp.zeros((p_rows,), jnp.int32).at[dest].set(slot // TOP_K)
    ys = moe_grouped_ffn(h_f32[src_tok], tile_expert, wg, wu, wd, tf=1792)
    back = dest.reshape(m, TOP_K)
    return moe_combine(resid, gates, ys[back[:, 0]], ys[back[:, 1]])


COL_QA, COL_KA, COL_VA, COL_QB = 0, A_QK_W, 2 * A_QK_W, 2 * A_QK_W + A_V_W
COL_KVB = COL_QB + NSA_Q_W
COL_GATE = COL_KVB + NSA_KV_W
KVB_PAIR = NSA_KV_HEADS * HEAD_DIM


def _pad_axis(x, axis, size):
    pad = [(0, 0)] * x.ndim
    pad[axis] = (0, size - x.shape[axis])
    return jnp.pad(x, pad)


def even_mixer(proj, proj16, tables, nb, seq, db, dt, caches, mw):
    g_qa, g_ka, lam_rows, lam_init, g_subln, g_qb, g_kb, cmp_w = mw
    cache_a_kv, cache_nsa_kv, state_nsa_win, page_table = caches
    pe_k, w_k1, w_k2, pe_v, w_v1, w_v2 = cmp_w
    mp = nb * seq
    ms = db * dt
    assert dt == 4
    rope16_32 = ((True, BF16), (True, F32))
    qk_a16, qk_a32 = head_norm_rope(proj, jnp.stack([_head_gain(g_qa, A_QK_W), _head_gain(g_ka, A_QK_W)]), tables,
                                    width=A_QK_W, col0=0, outs=rope16_32, name="hnr_diff_qk")
    qn16, qr16 = head_norm_rope(proj, _head_gain(g_qb, NSA_Q_W)[None], tables, width=NSA_Q_W,
                                col0=COL_QB // NSA_Q_W, outs=((False, BF16), (True, BF16)), name="hnr_nsa_q")
    ks16, ks32 = head_norm_rope(proj, _head_gain(g_kb[1], KVB_PAIR)[None], tables, width=KVB_PAIR,
                                col0=(COL_KVB + 2 * KVB_PAIR) // KVB_PAIR, outs=rope16_32, name="hnr_nsa_kslc")
    kw16, kw32 = head_norm_rope(proj, _head_gain(g_kb[2], KVB_PAIR)[None], tables, width=KVB_PAIR,
                                col0=(COL_KVB + 4 * KVB_PAIR) // KVB_PAIR, outs=rope16_32, name="hnr_nsa_kwin")

    o_a = diff_attention_prompt(qk_a16, proj16, lam_rows, g_subln, nb=nb, seq=seq, lam_init=lam_init)

    nchunk = seq // CMP_STRIDE

    def chunks(col):
        xc = proj16[:mp, col:col + KVB_PAIR].reshape(nb, nchunk, CMP_STRIDE, NSA_KV_HEADS, HEAD_DIM)
        return xc.transpose(0, 3, 1, 2, 4).reshape(nb * NSA_KV_HEADS, nchunk, CMP_STRIDE * HEAD_DIM)

    def pair_lanes(c):
        return c.reshape(nb, NSA_KV_HEADS, nchunk, HEAD_DIM).transpose(0, 2, 1, 3).reshape(nb, nchunk, KVB_PAIR)

    k_cmp = pair_lanes(compress_blocks(chunks(COL_KVB), pe_k, w_k1, w_k2, g_kb[0]))
    v_cmp = pair_lanes(compress_blocks(chunks(COL_KVB + KVB_PAIR), pe_v, w_v1, w_v2, None))
    o_cmp, sel = nsa_compressed_prompt(qn16, k_cmp, v_cmp, nb=nb, seq=seq)
    o_sel = nsa_branch_prompt(qr16, ks16, 0, proj16, (COL_KVB + 3 * KVB_PAIR) // KVB_PAIR, sel,
                              nb=nb, seq=seq, band=None, name="nsa_sel_prompt")
    o_win = nsa_branch_prompt(qr16, kw16, 0, proj16, (COL_KVB + 5 * KVB_PAIR) // KVB_PAIR, None,
                              nb=nb, seq=seq, band=NSA_WINDOW, name="nsa_win_prompt")
    a_rows = jnp.concatenate([qk_a32[:, A_QK_W:], proj[:, COL_VA:COL_VA + A_V_W]], axis=1)
    long_rows = jnp.concatenate([proj[:, COL_KVB:COL_KVB + 2 * KVB_PAIR], ks32,
                                 proj[:, COL_KVB + 3 * KVB_PAIR:COL_KVB + 4 * KVB_PAIR]], axis=1)
    win_rows = jnp.concatenate([kw32, proj[:, COL_KVB + 5 * KVB_PAIR:COL_KVB + 6 * KVB_PAIR]], axis=1)

    past = page_table.shape[1] * cache_a_kv.shape[1]
    q_s = qk_a32[mp:, :A_QK_W].reshape(db, dt, A_HEADS, 128).transpose(0, 2, 1, 3).reshape(db, A_HEADS * dt, 128)
    new_page = _pad_axis(a_rows[mp:].reshape(db, dt * 2 * A_HEADS, 128), 1, 128)
    cache_rows = cache_a_kv.reshape(cache_a_kv.shape[0], -1, A_VDIM)
    y_s = diff_attention_sample(jnp.concatenate([q_s, q_s], axis=1), new_page, cache_rows, page_table, lam_rows,
                                g_subln, lam_init=lam_init)
    o_a_s = y_s.reshape(db, A_HEADS, dt, A_VDIM).transpose(0, 2, 1, 3).reshape(ms, A_V_W).astype(BF16)

    pool = cache_nsa_kv.shape[0]
    xc = jnp.transpose(cache_nsa_kv[:, :, :2], (2, 0, 3, 1, 4)).astype(BF16)
    xc = xc.reshape(2, pool * NSA_KV_HEADS * (cache_nsa_kv.shape[1] // CMP_STRIDE), CMP_STRIDE * HEAD_DIM)
    ab = jnp.stack([matmul(xc[c], _w1_ab(w1), tn=2 * w1.shape[1], name="compress_cache")
                    for c, w1 in enumerate((w_k1, w_v1))])
    ab = ab.reshape(2, pool, NSA_KV_HEADS, cache_nsa_kv.shape[1] // CMP_STRIDE, ab.shape[-1])

    def sample_q(q16):
        qq = q16[mp:].reshape(db, dt, NSA_KV_HEADS, NSA_REP, HEAD_DIM).transpose(0, 2, 3, 1, 4)
        return _pad_axis(qq, 3, 8).reshape(db, NSA_KV_HEADS, NSA_REP * 8, HEAD_DIM)

    def sample_kv(x):
        return x.reshape(db, dt, NSA_KV_HEADS, HEAD_DIM).transpose(0, 2, 1, 3)

    new_rows = jnp.stack([sample_kv(ks32[mp:]), sample_kv(proj[mp:, COL_KVB + 3 * KVB_PAIR:COL_KVB + 4 * KVB_PAIR]),
                          sample_kv(kw32[mp:]), sample_kv(proj[mp:, COL_KVB + 5 * KVB_PAIR:COL_KVB + 6 * KVB_PAIR])],
                         axis=1)
    o_cmp_s, o_sel_s, o_win_s, win_state = nsa_sample(
        sample_q(qn16), sample_q(qr16), _pad_axis(new_rows, 3, 128), jnp.transpose(cache_nsa_kv, (0, 2, 3, 4, 1)),
        ab, jnp.transpose(state_nsa_win, (0, 2, 3, 4, 1)), page_table, cmp_w, g_kb[0], past=past)

    def sample_o(o):
        oo = o.reshape(db, NSA_KV_HEADS, NSA_REP, 8, HEAD_DIM)[:, :, :, :dt]
        return oo.transpose(0, 3, 1, 2, 4).reshape(ms, NSA_Q_W).astype(BF16)

    o_b = nsa_merge(proj, COL_GATE // 128, jnp.concatenate([o_cmp, sample_o(o_cmp_s)]),
                    jnp.concatenate([o_sel, sample_o(o_sel_s)]), jnp.concatenate([o_win, sample_o(o_win_s)]))
    cat = jnp.concatenate([jnp.concatenate([o_a, o_a_s]), o_b], axis=1)

    keep = min(NSA_WINDOW, seq)
    return (cat,
            a_rows[:mp].reshape(nb, seq, 2, A_HEADS, A_VDIM), a_rows[mp:].reshape(db, dt, 2, A_HEADS, A_VDIM),
            long_rows[:mp].reshape(nb, seq, 4, NSA_KV_HEADS, HEAD_DIM),
            long_rows[mp:].reshape(db, dt, 4, NSA_KV_HEADS, HEAD_DIM),
            win_rows[:mp].reshape(nb, seq, 2, NSA_KV_HEADS, HEAD_DIM)[:, seq - keep:],
            jnp.transpose(win_state, (0, 4, 1, 2, 3)))


def odd_mixer(proj, proj16, tables, nb, seq, db, dt, states, g_qc, g_kc):
    mp = nb * seq
    ms = db * dt
    outs, lses, bufs_p, bufs_s, qk32s = [], [], [], [], []
    for gi, ((window, dil), state) in enumerate(zip(C_GROUPS, states)):
        gains = jnp.stack([_head_gain(g_qc[gi], C_W), _head_gain(g_kc[gi], C_W)])
        (qk32,) = head_norm_rope(proj, gains, tables, width=C_W, col0=3 * gi, outs=((True, F32),),
                                 name=f"hnr_dil_{dil}")
        qk32s.append(qk32)
        v32 = proj[:, (3 * gi + 2) * C_W:(3 * gi + 3) * C_W]
        kv = jnp.concatenate([qk32[:mp, C_W:], v32[:mp]], axis=1)
        bufs_p.append(kv.reshape(nb, seq, 2, C_HEADS, HEAD_DIM)[:, seq - min(window, seq):])

        tok3 = lambda x: x.reshape(db, dt, C_W)
        nbuf = state.shape[1]
        o_s, lse_s, rolled = dilated_attention_sample(
            _pad_axis(tok3(qk32[mp:, :C_W]), 1, 8), _pad_axis(tok3(qk32[mp:, C_W:]), 1, 128),
            _pad_axis(tok3(v32[mp:]), 1, 128), jnp.transpose(state, (0, 2, 3, 4, 1)).reshape(db, 2, C_W, nbuf),
            window=window, dil=dil, hg=C_HEADS if nbuf <= 512 else 4)
        outs.append(o_s[:, :dt].reshape(ms, C_W).astype(BF16))
        lse_s = lse_s[:, :, 0].reshape(db, C_HEADS, 8)[:, :, :dt].transpose(0, 2, 1).reshape(ms, C_HEADS)
        lses.append(jnp.repeat(lse_s, HEAD_DIM, axis=1))
        bufs_s.append(jnp.transpose(rolled.reshape(db, 2, C_HEADS, HEAD_DIM, nbuf), (0, 4, 1, 2, 3)))
    y_p = dilated_attention_prompt_fused(qk32s, proj, nb=nb, seq=seq)
    return jnp.concatenate([y_p, dilation_merge(outs, lses)]), bufs_p, bufs_s


def odd_mixer_prompt(proj, proj16, tables, nb, seq, g_qc, g_kc):
    mp = nb * seq
    outs, lses, bufs = [], [], []
    for gi, (window, dil) in enumerate(C_GROUPS):
        gains = jnp.stack([_head_gain(g_qc[gi], C_W), _head_gain(g_kc[gi], C_W)])
        qk16, qk32 = head_norm_rope(proj, gains, tables, width=C_W, col0=3 * gi,
                                    outs=((True, BF16), (True, F32)), name=f"hnr_dil_{dil}")
        sub = seq // dil
        o, lse = dilated_attention_prompt(qk16, proj16, nb=nb, seq=seq, dil=dil, band=window // dil, gi=gi,
                                          t=min(sub, 256 if dil == 1 else 128))
        outs.append(o)
        lses.append(lse)
        kv = jnp.concatenate([qk32[:mp, C_W:], proj[:mp, (3 * gi + 2) * C_W:(3 * gi + 3) * C_W]], axis=1)
        bufs.append(kv.reshape(nb, seq, 2, C_HEADS, HEAD_DIM)[:, seq - min(window, seq):])
    return dilation_merge(outs, lses), bufs


def kernel(x_prompt, x_sample, cache_a_kv, cache_nsa_kv, state_nsa_win, state_c_w128, state_c_w512, state_c_w2048, page_table, norm0_mix, w_in0, g_qa, g_ka, lam_q1, lam_k1, lam_q2, lam_k2, g_subln, g_qb, g_kb, pe_cmp_k, w_cmp_k1, w_cmp_k2, pe_cmp_v, w_cmp_v1, w_cmp_v2, w_out0, norm0_ffn, w_ffn_gate, w_ffn_up, w_ffn_down, norm1_mix, w_in1, g_qc, g_kc, w_out1, norm1_ffn, w_router, w_moe_gate, w_moe_up, w_moe_down):
    nb, seq, d = x_prompt.shape
    db, dt, _ = x_sample.shape
    past = page_table.shape[1] * cache_a_kv.shape[1]
    mp = nb * seq
    ms = db * dt
    pos_p = jnp.arange(seq, dtype=jnp.int32)
    pos_s = past + jnp.arange(dt, dtype=jnp.int32)
    x = jnp.concatenate([x_prompt.reshape(mp, d), x_sample.reshape(ms, d)], axis=0)

    in0_w = w_in0.shape[1]
    in0_pad = -(-in0_w // 128) * 128
    w_in0_b = jnp.pad(w_in0, ((0, 0), (0, in0_pad - in0_w))).astype(BF16)
    tables = rope_tables(jnp.concatenate([jnp.tile(pos_p, nb), jnp.tile(pos_s, db)]))
    proj0, proj0_16 = matmul_dual(rmsnorm_cast(x, norm0_mix), w_in0_b, tn=in0_pad, name="in_proj0")
    lam_init = 0.8 - 0.6 * math.exp(-0.3 * 0)
    f = lambda a: a.astype(F32)
    cmp_w = (pe_cmp_k, w_cmp_k1, w_cmp_k2, pe_cmp_v, w_cmp_v1, w_cmp_v2)
    lam_rows = jnp.zeros((8, 128), F32).at[:4, :HEAD_DIM].set(jnp.stack([f(lam_q1), f(lam_k1), f(lam_q2), f(lam_k2)]))
    cat, a_kv_p, a_kv_s, nsa_kv_p, nsa_kv_s, nsa_win_p, nsa_win_s = even_mixer(
        proj0, proj0_16, tables, nb, seq, db, dt, (cache_a_kv, cache_nsa_kv, state_nsa_win, page_table),
        (g_qa, g_ka, lam_rows, lam_init, g_subln, g_qb, g_kb, cmp_w))
    x = matmul(cat, w_out0.astype(BF16), tn=d, res=x, name="out_proj0")
    act = swiglu_gate_up(rmsnorm_cast(x, norm0_ffn), w_ffn_gate.astype(BF16), w_ffn_up.astype(BF16), tn=1408)
    x = matmul(act, w_ffn_down.astype(BF16), tn=d, res=x, name="ffn_down")

    in1_w = w_in1.shape[1]
    proj1 = matmul(rmsnorm_cast(x, norm1_mix), w_in1.astype(BF16), tn=2304, name="in_proj1")
    mix, c_p, c_s = odd_mixer(proj1, None, tables, nb, seq, db, dt,
                              (state_c_w128, state_c_w512, state_c_w2048), g_qc, g_kc)
    x = matmul(mix, w_out1.astype(BF16), tn=d, res=x, name="out_proj1")
    h16, h32 = rmsnorm_cast(x, norm1_ffn, with_f32=True)
    x = _moe(h16, h32, x, w_router, w_moe_gate.astype(BF16), w_moe_up.astype(BF16), w_moe_down.astype(BF16))

    hp = x[:mp].reshape(nb, seq, d)
    hs = x[mp:].reshape(db, dt, d)
    return (hp, hs, a_kv_p, a_kv_s, nsa_kv_p, nsa_kv_s, nsa_win_p, nsa_win_s,
            c_p[0], c_s[0], c_p[1], c_s[1], c_p[2], c_s[2])
```

```python
import functools
import math

import jax
import jax.numpy as jnp
import numpy as np
from jax import lax
from jax.experimental import pallas as pl
from jax.experimental.pallas import tpu as pltpu

F32 = jnp.float32
BF16 = jnp.bfloat16

D_MODEL = 1024
HEAD_DIM = 64
ROT_DIM = HEAD_DIM // 4
ROPE_THETA = 500000.0
NORM_EPS = 1e-6
SCALE = HEAD_DIM ** -0.5
Q_BLOCK = 128
NEG_INF = -1e30
TINY = 1e-30
A_HEADS = 4
A_VDIM = 2 * HEAD_DIM
NSA_HEADS = 8
NSA_KV_HEADS = 2
NSA_REP = NSA_HEADS // NSA_KV_HEADS
CMP_LEN = 32
CMP_STRIDE = 16
SEL_BLOCK = 64
SEL_SHIFT = 6
SEL_TOPK = 16
SEL_Q_BLOCK = 64
NSA_WINDOW = 512
FORCE_BONUS = 1e3
C_HEADS = 16
C_GROUPS = ((128, 1), (512, 4), (2048, 16))
N_C_GROUPS = len(C_GROUPS)
A_QK_W = A_HEADS * 2 * HEAD_DIM
A_V_W = A_HEADS * A_VDIM
NSA_Q_W = NSA_HEADS * HEAD_DIM
NSA_KV_W = 6 * NSA_KV_HEADS * HEAD_DIM
NSA_GATE_W = 3 * NSA_HEADS
C_W = C_HEADS * HEAD_DIM
N_EXPERTS = 8
TOP_K = 2

VMEM_LIMIT_BYTES = 56 * 1024 * 1024
TOKEN_TILE = 512


def _cparams(*sem):
    return pltpu.CompilerParams(dimension_semantics=sem, vmem_limit_bytes=VMEM_LIMIT_BYTES)


def _rmsnorm_body(x_ref, g_ref, *o_refs):
    x = x_ref[...]
    ms = jnp.mean(x * x, axis=-1, keepdims=True)
    y = x * lax.rsqrt(ms + NORM_EPS) * g_ref[...]
    for o_ref in o_refs:
        o_ref[...] = y.astype(o_ref.dtype)


def rmsnorm_cast(x, g, *, with_f32=False):
    m, d = x.shape
    tm = TOKEN_TILE
    spec = pl.BlockSpec((tm, d), lambda i: (i, 0))
    outs = pl.pallas_call(
        _rmsnorm_body,
        grid=(m // tm,),
        in_specs=[spec, pl.BlockSpec((1, d), lambda i: (0, 0))],
        out_specs=[spec, spec] if with_f32 else [spec],
        out_shape=[jax.ShapeDtypeStruct((m, d), dt) for dt in ((BF16, F32) if with_f32 else (BF16,))],
        compiler_params=_cparams("parallel"),
        name="rmsnorm",
    )(x, g.reshape(1, d))
    return tuple(outs) if with_f32 else outs[0]


def _mm_body(x_ref, w_ref, o_ref):
    o_ref[...] = jnp.dot(x_ref[...], w_ref[...], preferred_element_type=F32).astype(o_ref.dtype)


def _mm_res_body(x_ref, w_ref, r_ref, o_ref):
    acc = jnp.dot(x_ref[...], w_ref[...], preferred_element_type=F32)
    o_ref[...] = (acc + r_ref[...]).astype(o_ref.dtype)


def matmul(x, w, *, tn, res=None, out_dtype=F32, name="matmul"):
    m, k = x.shape
    n = w.shape[1]
    tm = min(TOKEN_TILE, m)
    assert m % tm == 0 and n % tn == 0
    in_specs = [pl.BlockSpec((tm, k), lambda j, i: (i, 0)), pl.BlockSpec((k, tn), lambda j, i: (0, j))]
    args = [x, w]
    body = _mm_body
    if res is not None:
        in_specs.append(pl.BlockSpec((tm, tn), lambda j, i: (i, j)))
        args.append(res)
        body = _mm_res_body
    return pl.pallas_call(
        body,
        grid=(n // tn, m // tm),
        in_specs=in_specs,
        out_specs=pl.BlockSpec((tm, tn), lambda j, i: (i, j)),
        out_shape=jax.ShapeDtypeStruct((m, n), out_dtype),
        compiler_params=_cparams("parallel", "parallel"),
        name=name,
    )(*args)


def _gate_up_body(x_ref, wg_ref, wu_ref, o_ref):
    x = x_ref[...]
    g = jnp.dot(x, wg_ref[...], preferred_element_type=F32)
    u = jnp.dot(x, wu_ref[...], preferred_element_type=F32)
    o_ref[...] = (g * jax.nn.sigmoid(g) * u).astype(o_ref.dtype)


def swiglu_gate_up(x, wg, wu, *, tn):
    m, k = x.shape
    n = wg.shape[1]
    tm = TOKEN_TILE
    return pl.pallas_call(
        _gate_up_body,
        grid=(n // tn, m // tm),
        in_specs=[pl.BlockSpec((tm, k), lambda j, i: (i, 0)),
                  pl.BlockSpec((k, tn), lambda j, i: (0, j)),
                  pl.BlockSpec((k, tn), lambda j, i: (0, j))],
        out_specs=pl.BlockSpec((tm, tn), lambda j, i: (i, j)),
        out_shape=jax.ShapeDtypeStruct((m, n), BF16),
        compiler_params=_cparams("parallel", "parallel"),
        name="swiglu_gate_up",
    )(x, wg, wu)


def _moe_gate_up_body(te_ref, x_ref, wg_ref, wu_ref, o_ref, wg16_ref, wu16_ref):
    i = pl.program_id(1)

    @pl.when((i == 0) | (te_ref[i] != te_ref[jnp.maximum(i - 1, 0)]))
    def _():
        wg16_ref[...] = wg_ref[...].astype(BF16)
        wu16_ref[...] = wu_ref[...].astype(BF16)

    x = x_ref[...].astype(BF16)
    g = jnp.dot(x, wg16_ref[...], preferred_element_type=F32)
    u = jnp.dot(x, wu16_ref[...], preferred_element_type=F32)
    o_ref[...] = (g * jax.nn.sigmoid(g) * u).astype(o_ref.dtype)


def _moe_down_body(te_ref, a_ref, wd_ref, o_ref):
    del te_ref
    o_ref[...] = jnp.dot(a_ref[...], wd_ref[...], preferred_element_type=F32)


def moe_grouped_ffn(xs, tile_expert, wg, wu, wd, *, tf):
    p, d = xs.shape
    f = wg.shape[2]
    tm = TOKEN_TILE
    nt = p // tm
    act = pl.pallas_call(
        _moe_gate_up_body,
        grid_spec=pltpu.PrefetchScalarGridSpec(
            num_scalar_prefetch=1,
            grid=(f // tf, nt),
            in_specs=[pl.BlockSpec((tm, d), lambda j, i, te: (i, 0)),
                      pl.BlockSpec((None, d, tf), lambda j, i, te: (te[i], 0, j)),
                      pl.BlockSpec((None, d, tf), lambda j, i, te: (te[i], 0, j))],
            out_specs=pl.BlockSpec((tm, tf), lambda j, i, te: (i, j)),
            scratch_shapes=[pltpu.VMEM((d, tf), BF16), pltpu.VMEM((d, tf), BF16)],
        ),
        out_shape=jax.ShapeDtypeStruct((p, f), BF16),
        compiler_params=_cparams("arbitrary", "arbitrary"),
        name="moe_gate_up",
    )(tile_expert, xs, wg, wu)
    return pl.pallas_call(
        _moe_down_body,
        grid_spec=pltpu.PrefetchScalarGridSpec(
            num_scalar_prefetch=1,
            grid=(nt,),
            in_specs=[pl.BlockSpec((tm, f), lambda i, te: (i, 0)),
                      pl.BlockSpec((None, f, d), lambda i, te: (te[i], 0, 0))],
            out_specs=pl.BlockSpec((tm, d), lambda i, te: (i, 0)),
        ),
        out_shape=jax.ShapeDtypeStruct((p, d), F32),
        compiler_params=_cparams("arbitrary"),
        name="moe_down",
    )(tile_expert, act, wd)


def _mm2_body(x_ref, w_ref, o32_ref, o16_ref):
    acc = jnp.dot(x_ref[...], w_ref[...], preferred_element_type=F32)
    o32_ref[...] = acc
    o16_ref[...] = acc.astype(BF16)


def matmul_dual(x, w, *, tn, name):
    m, k = x.shape
    n = w.shape[1]
    tm = TOKEN_TILE
    return pl.pallas_call(
        _mm2_body,
        grid=(n // tn, m // tm),
        in_specs=[pl.BlockSpec((tm, k), lambda j, i: (i, 0)), pl.BlockSpec((k, tn), lambda j, i: (0, j))],
        out_specs=[pl.BlockSpec((tm, tn), lambda j, i: (i, j)), pl.BlockSpec((tm, tn), lambda j, i: (i, j))],
        out_shape=[jax.ShapeDtypeStruct((m, n), F32), jax.ShapeDtypeStruct((m, n), BF16)],
        compiler_params=_cparams("parallel", "parallel"),
        name=name,
    )(x, w)


def rope_tables(pos):
    half = ROT_DIM // 2
    inv_freq = ROPE_THETA ** (-jnp.arange(half, dtype=F32) / half)
    ang = pos.astype(F32)[:, None] * inv_freq[None, :]
    cos, sin = jnp.cos(ang), jnp.sin(ang)
    m = pos.shape[0]
    z_half = jnp.zeros((m, half), F32)
    z_rest = jnp.zeros((m, HEAD_DIM - ROT_DIM), F32)
    c = jnp.concatenate([cos, cos, jnp.ones((m, HEAD_DIM - ROT_DIM), F32)], axis=1)
    s1 = jnp.concatenate([z_half, sin, z_rest], axis=1)
    s2 = jnp.concatenate([-sin, z_half, z_rest], axis=1)
    return tuple(jnp.tile(a, (1, 128 // HEAD_DIM)) for a in (c, s1, s2))


def _hnr_body(x_ref, g_ref, c_ref, s1_ref, s2_ref, *o_refs, width, outs):
    tm = x_ref.shape[0]
    lo = lax.broadcasted_iota(jnp.int32, (tm, 128), 1) < HEAD_DIM
    c, s1, s2 = c_ref[...], s1_ref[...], s2_ref[...]
    for j in range(width // 128):
        sl = slice(j * 128, (j + 1) * 128)
        x = x_ref[:, sl]
        x2 = x * x
        s_lo = jnp.sum(jnp.where(lo, x2, 0.0), axis=-1, keepdims=True)
        s_hi = jnp.sum(jnp.where(lo, 0.0, x2), axis=-1, keepdims=True)
        ms = jnp.where(lo, s_lo, s_hi) * (1.0 / HEAD_DIM)
        xn = x * lax.rsqrt(ms + NORM_EPS) * g_ref[:, sl]
        xr = xn * c + pltpu.roll(xn, ROT_DIM // 2, 1) * s1 + pltpu.roll(xn, 128 - ROT_DIM // 2, 1) * s2
        for (rope, _), o_ref in zip(outs, o_refs):
            o_ref[:, sl] = (xr if rope else xn).astype(o_ref.dtype)


def head_norm_rope(x, gains, tables, *, width, col0, outs, name):
    m = x.shape[0]
    ncol = gains.shape[0]
    tm = TOKEN_TILE
    tab_spec = pl.BlockSpec((tm, 128), lambda i, j: (i, 0))
    return pl.pallas_call(
        functools.partial(_hnr_body, width=width, outs=outs),
        grid=(m // tm, ncol),
        in_specs=[pl.BlockSpec((tm, width), lambda i, j: (i, col0 + j)),
                  pl.BlockSpec((None, 1, width), lambda i, j: (j, 0, 0)),
                  tab_spec, tab_spec, tab_spec],
        out_specs=[pl.BlockSpec((tm, width), lambda i, j: (i, j)) for _ in outs],
        out_shape=[jax.ShapeDtypeStruct((m, ncol * width), dt) for _, dt in outs],
        compiler_params=_cparams("parallel", "parallel"),
        name=name,
    )(x, gains.reshape(ncol, 1, width), *tables)


def _head_gain(g, width):
    return jnp.tile(g.astype(F32), width // HEAD_DIM)


def _step_tables(nq, lookback):
    qi, ki, first, last = [], [], [], []
    for q in range(nq):
        ks = list(range(q + 1)) if lookback is None else [k for k in range(q - lookback, q + 1) if k >= 0]
        for n, k in enumerate(ks):
            qi.append(q)
            ki.append(k)
            first.append(int(n == 0))
            last.append(int(n == len(ks) - 1))
    return tuple(jnp.asarray(a, jnp.int32) for a in (qi, ki, first, last))


def _pos_mask(qi, ki, t, band):
    row = lax.broadcasted_iota(jnp.int32, (t, t), 0)
    col = lax.broadcasted_iota(jnp.int32, (t, t), 1)
    d = (qi - ki) * t + row - col
    mask = d >= 0
    if band is not None:
        mask = mask & (d <= band)
    return mask


def _nt_dot(a, b):
    return lax.dot_general(a, b, (((1,), (1,)), ((), ())), preferred_element_type=F32)


def _online_update(sc, mask, v, m_ref, l_ref, acc_ref, idx):
    sc = jnp.where(mask, sc, NEG_INF)
    m_old = m_ref[idx]
    m_new = jnp.maximum(m_old, jnp.max(sc, axis=-1, keepdims=True))
    alpha = jnp.exp(m_old - m_new)
    p = jnp.where(mask, jnp.exp(sc - m_new), 0.0)
    l_ref[idx] = alpha * l_ref[idx] + jnp.sum(p, axis=-1, keepdims=True)
    acc_ref[idx] = alpha * acc_ref[idx] + jnp.dot(p.astype(BF16), v, preferred_element_type=F32)
    m_ref[idx] = m_new


def _init_state(m_ref, l_ref, acc_ref):
    m_ref[...] = jnp.full(m_ref.shape, NEG_INF, F32)
    l_ref[...] = jnp.zeros(l_ref.shape, F32)
    acc_ref[...] = jnp.zeros(acc_ref.shape, F32)


def _split_pair(q_ref, qs_ref, hb, lo):
    q = q_ref[:, hb * 128:(hb + 1) * 128].astype(F32) * SCALE
    qs_ref[2 * hb] = jnp.where(lo, q, 0.0).astype(BF16)
    qs_ref[2 * hb + 1] = jnp.where(lo, 0.0, q).astype(BF16)


def _gqa_query(q_ref, g, r, lo):
    col = g * NSA_REP + r
    blk = q_ref[:, (col // 2) * 128:(col // 2 + 1) * 128].astype(F32) * SCALE
    h = jnp.where(lo if col % 2 == 0 else jnp.logical_not(lo), blk, 0.0)
    d = h + pltpu.roll(h, HEAD_DIM, 1)
    return jnp.where(lo if g == 0 else jnp.logical_not(lo), d, 0.0).astype(BF16)


def _gqa_store(o_ref, outs, g, lo):
    keep = lo if g == 0 else jnp.logical_not(lo)
    dup = []
    for o in outs:
        z = jnp.where(keep, o, 0.0)
        dup.append(z + pltpu.roll(z, HEAD_DIM, 1))
    for pr in range(NSA_REP // 2):
        blk = g * (NSA_REP // 2) + pr
        o_ref[:, blk * 128:(blk + 1) * 128] = jnp.where(lo, dup[2 * pr], dup[2 * pr + 1]).astype(o_ref.dtype)


def _diff_body(qi_ref, ki_ref, fi_ref, la_ref, q_ref, k_ref, v_ref, lam_ref, gs_ref, o_ref,
               qs_ref, m_ref, l_ref, acc_ref, *, t, lam_init):
    s = pl.program_id(1)
    lo = lax.broadcasted_iota(jnp.int32, (t, 128), 1) < HEAD_DIM

    @pl.when(fi_ref[s] == 1)
    def _():
        for h in range(A_HEADS):
            _split_pair(q_ref, qs_ref, h, lo)
        _init_state(m_ref, l_ref, acc_ref)

    mask = _pos_mask(qi_ref[s], ki_ref[s], t, None)
    for h in range(A_HEADS):
        k = k_ref[:, h * 128:(h + 1) * 128]
        v = v_ref[:, h * 128:(h + 1) * 128]
        for var in range(2):
            _online_update(_nt_dot(qs_ref[2 * h + var], k), mask, v, m_ref, l_ref, acc_ref, 2 * h + var)

    @pl.when(la_ref[s] == 1)
    def _():
        lv = lam_ref[...]
        a = jnp.sum(lv[0:1] * lv[1:2], axis=-1, keepdims=True)
        b = jnp.sum(lv[2:3] * lv[3:4], axis=-1, keepdims=True)
        lam = jnp.exp(a) - jnp.exp(b) + lam_init
        for h in range(A_HEADS):
            o1 = acc_ref[2 * h] / jnp.maximum(l_ref[2 * h], TINY)
            o2 = acc_ref[2 * h + 1] / jnp.maximum(l_ref[2 * h + 1], TINY)
            o = o1 - lam * o2
            ms = jnp.mean(o * o, axis=-1, keepdims=True)
            y = o * lax.rsqrt(ms + NORM_EPS) * gs_ref[...] * (1.0 - lam_init)
            o_ref[:, h * 128:(h + 1) * 128] = y.astype(o_ref.dtype)


def diff_attention_prompt(qk16, v16, lam_rows, g_subln, *, nb, seq, lam_init, t=512):
    nq = seq // t
    tabs = _step_tables(nq, None)
    w = A_V_W
    qmap = lambda n, s, qi, ki, fi, la: (n * nq + qi[s], 0)
    kmap = lambda n, s, qi, ki, fi, la: (n * nq + ki[s], 1)
    vmap = lambda n, s, qi, ki, fi, la: (n * nq + ki[s], 2)
    const = lambda n, s, qi, ki, fi, la: (0, 0)
    return pl.pallas_call(
        functools.partial(_diff_body, t=t, lam_init=lam_init),
        grid_spec=pltpu.PrefetchScalarGridSpec(
            num_scalar_prefetch=4,
            grid=(nb, int(tabs[0].shape[0])),
            in_specs=[pl.BlockSpec((t, w), qmap), pl.BlockSpec((t, w), kmap), pl.BlockSpec((t, w), vmap),
                      pl.BlockSpec((8, 128), const), pl.BlockSpec((1, 128), const)],
            out_specs=pl.BlockSpec((t, w), qmap),
            scratch_shapes=[pltpu.VMEM((2 * A_HEADS, t, 128), BF16), pltpu.VMEM((2 * A_HEADS, t, 1), F32),
                            pltpu.VMEM((2 * A_HEADS, t, 1), F32), pltpu.VMEM((2 * A_HEADS, t, 128), F32)],
        ),
        out_shape=jax.ShapeDtypeStruct((nb * seq, w), BF16),
        compiler_params=_cparams("parallel", "arbitrary"),
        name="diff_attn_prompt",
    )(*tabs, qk16, qk16, v16, lam_rows, g_subln.reshape(1, A_VDIM).astype(F32))


def _dil_body(qi_ref, ki_ref, fi_ref, la_ref, q_ref, k_ref, v_ref, o_ref, lse_ref,
              qs_ref, m_ref, l_ref, acc_ref, *, t, band):
    s = pl.program_id(1)
    lo = lax.broadcasted_iota(jnp.int32, (t, 128), 1) < HEAD_DIM
    nhb = C_HEADS // 2

    @pl.when(fi_ref[s] == 1)
    def _():
        for hb in range(nhb):
            _split_pair(q_ref, qs_ref, hb, lo)
        _init_state(m_ref, l_ref, acc_ref)

    mask = _pos_mask(qi_ref[s], ki_ref[s], t, band)
    for hb in range(nhb):
        k = k_ref[:, hb * 128:(hb + 1) * 128]
        v = v_ref[:, hb * 128:(hb + 1) * 128]
        for var in range(2):
            _online_update(_nt_dot(qs_ref[2 * hb + var], k), mask, v, m_ref, l_ref, acc_ref, 2 * hb + var)

    @pl.when(la_ref[s] == 1)
    def _():
        for hb in range(nhb):
            l0 = jnp.maximum(l_ref[2 * hb], TINY)
            l1 = jnp.maximum(l_ref[2 * hb + 1], TINY)
            o = jnp.where(lo, acc_ref[2 * hb] / l0, acc_ref[2 * hb + 1] / l1)
            lse = jnp.where(lo, m_ref[2 * hb] + jnp.log(l0), m_ref[2 * hb + 1] + jnp.log(l1))
            o_ref[:, hb * 128:(hb + 1) * 128] = o.astype(o_ref.dtype)
            lse_ref[:, hb * 128:(hb + 1) * 128] = lse


def dilated_attention_prompt(qk16, v16, *, nb, seq, dil, band, gi, t):
    mp = nb * seq
    sub = seq // dil
    nq = sub // t
    tabs = _step_tables(nq, -(-band // t))
    w = C_W
    qk = qk16.reshape(qk16.shape[0] // dil, dil * 2 * w)
    vv = v16.reshape(v16.shape[0] // dil, dil * v16.shape[1])
    vcols = v16.shape[1] // w
    row = lambda b, x: (b // dil) * nq + x

    def qmap(b, s, qi, ki, fi, la):
        return (row(b, qi[s]), (b % dil) * 2)

    def kmap(b, s, qi, ki, fi, la):
        return (row(b, ki[s]), (b % dil) * 2 + 1)

    def vmap(b, s, qi, ki, fi, la):
        return (row(b, ki[s]), (b % dil) * vcols + gi * 3 + 2)

    def omap(b, s, qi, ki, fi, la):
        return (row(b, qi[s]), b % dil)

    nst = 2 * (C_HEADS // 2)
    o, lse = pl.pallas_call(
        functools.partial(_dil_body, t=t, band=band),
        grid_spec=pltpu.PrefetchScalarGridSpec(
            num_scalar_prefetch=4,
            grid=(nb * dil, int(tabs[0].shape[0])),
            in_specs=[pl.BlockSpec((t, w), qmap), pl.BlockSpec((t, w), kmap), pl.BlockSpec((t, w), vmap)],
            out_specs=[pl.BlockSpec((t, w), omap), pl.BlockSpec((t, w), omap)],
            scratch_shapes=[pltpu.VMEM((nst, t, 128), BF16), pltpu.VMEM((nst, t, 1), F32),
                            pltpu.VMEM((nst, t, 1), F32), pltpu.VMEM((nst, t, 128), F32)],
        ),
        out_shape=[jax.ShapeDtypeStruct((mp // dil, dil * w), BF16), jax.ShapeDtypeStruct((mp // dil, dil * w), F32)],
        compiler_params=_cparams("parallel", "arbitrary"),
        name=f"dilated_attn_prompt_{dil}",
    )(*tabs, qk, qk, vv)
    return o.reshape(mp, w), lse.reshape(mp, w)


def _dil_fused_body(*refs, seq):
    n_g = len(C_GROUPS)
    in_refs = refs[:3 * n_g]
    o_ref = refs[3 * n_g]
    og_refs = refs[3 * n_g + 1:3 * n_g + 1 + n_g]
    lg_refs = refs[3 * n_g + 1 + n_g:]
    for gi, (window, dil) in enumerate(C_GROUPS):
        q_ref, k_ref, v_ref = in_refs[3 * gi:3 * gi + 3]
        og_ref, lg_ref = og_refs[gi], lg_refs[gi]
        sub = seq // dil
        band = window // dil
        t = min(sub, 256 if dil == 1 else 128)
        nq = sub // t
        look = -(-band // t)
        lo = lax.broadcasted_iota(jnp.int32, (t, 128), 1) < HEAD_DIM

        def rows(rho, tile, t=t, dil=dil):
            return pl.ds(rho + dil * tile * t, t, stride=dil) if dil > 1 else pl.ds(tile * t, t)

        def residue(rho, carry, q_ref=q_ref, k_ref=k_ref, v_ref=v_ref, og_ref=og_ref, lg_ref=lg_ref,
                    t=t, nq=nq, look=look, band=band, lo=lo, rows=rows):
            for qi in range(nq):
                q = q_ref[rows(rho, qi), :] * SCALE
                qs = (jnp.where(lo, q, 0.0).astype(BF16), jnp.where(lo, 0.0, q).astype(BF16))
                m = [jnp.full((t, 1), NEG_INF, F32)] * 2
                l = [jnp.zeros((t, 1), F32)] * 2
                acc = [jnp.zeros((t, 128), F32)] * 2
                for ki in range(max(0, qi - look), qi + 1):
                    k = k_ref[rows(rho, ki), :].astype(BF16)
                    v = v_ref[rows(rho, ki), :].astype(BF16)
                    mask = _pos_mask(qi, ki, t, band)
                    for var in range(2):
                        p, alpha, m[var], l[var] = _softmax_step(_nt_dot(qs[var], k), mask, m[var], l[var])
                        acc[var] = alpha * acc[var] + jnp.dot(p.astype(BF16), v, preferred_element_type=F32)
                l = [jnp.maximum(x, TINY) for x in l]
                og_ref[rows(rho, qi), :] = jnp.where(lo, acc[0] / l[0], acc[1] / l[1])
                lg_ref[rows(rho, qi), :] = jnp.where(lo, m[0] + jnp.log(l[0]), m[1] + jnp.log(l[1]))
            return carry

        if dil == 1:
            residue(0, 0)
        else:
            lax.fori_loop(0, dil, residue, 0, unroll=2)

    chunk = 256
    for c in range(seq // chunk):
        sl = pl.ds(c * chunk, chunk)
        ls = [r[sl, :] for r in lg_refs]
        mx = functools.reduce(jnp.maximum, ls)
        es = [jnp.exp(x - mx) for x in ls]
        den = functools.reduce(lambda a, b: a + b, es)
        acc = functools.reduce(lambda a, b: a + b, [(e / den) * r[sl, :] for e, r in zip(es, og_refs)])
        o_ref[sl, :] = acc.astype(o_ref.dtype)


def dilated_attention_prompt_fused(qk32s, proj32, *, nb, seq):
    n_g = len(C_GROUPS)
    nhb = C_W // 128
    in_specs, args = [], []
    for gi in range(n_g):
        in_specs += [pl.BlockSpec((seq, 128), lambda n, hb: (n, hb)),
                     pl.BlockSpec((seq, 128), lambda n, hb: (n, nhb + hb)),
                     pl.BlockSpec((seq, 128), functools.partial(lambda n, hb, gi: (n, (3 * gi + 2) * nhb + hb), gi=gi))]
        args += [qk32s[gi], qk32s[gi], proj32]
    return pl.pallas_call(
        functools.partial(_dil_fused_body, seq=seq),
        grid=(nb, nhb),
        in_specs=in_specs,
        out_specs=pl.BlockSpec((seq, 128), lambda n, hb: (n, hb)),
        out_shape=jax.ShapeDtypeStruct((nb * seq, C_W), BF16),
        scratch_shapes=[pltpu.VMEM((seq, 128), F32) for _ in range(2 * n_g)],
        compiler_params=_cparams("parallel", "parallel"),
        name="dilated_attn_prompt",
    )(*args)


def _gqa_body(qi_ref, ki_ref, fi_ref, la_ref, q_ref, k_ref, v_ref, *rest, t, band, use_sel):
    if use_sel:
        sel_ref, o_ref, qs_ref, m_ref, l_ref, acc_ref = rest
    else:
        o_ref, qs_ref, m_ref, l_ref, acc_ref = rest
    s = pl.program_id(1)
    lo = lax.broadcasted_iota(jnp.int32, (t, 128), 1) < HEAD_DIM

    @pl.when(fi_ref[s] == 1)
    def _():
        for g in range(NSA_KV_HEADS):
            for r in range(NSA_REP):
                qs_ref[g * NSA_REP + r] = _gqa_query(q_ref, g, r, lo)
        _init_state(m_ref, l_ref, acc_ref)

    ki = ki_ref[s]
    mask = _pos_mask(qi_ref[s], ki, t, band)
    k = k_ref[...]
    v = v_ref[...]
    if use_sel:
        blk_row = lax.broadcasted_iota(jnp.int32, (128, t), 0)
        blk_col = jnp.right_shift(ki * t + lax.broadcasted_iota(jnp.int32, (128, t), 1), SEL_SHIFT)
        expand = jnp.where(blk_row == blk_col, 1.0, 0.0).astype(BF16)
    for g in range(NSA_KV_HEADS):
        mg = mask
        if use_sel:
            mg = mask & (jnp.dot(sel_ref[g], expand, preferred_element_type=F32) > 0.5)
        for r in range(NSA_REP):
            i = g * NSA_REP + r
            _online_update(_nt_dot(qs_ref[i], k), mg, v, m_ref, l_ref, acc_ref, i)

    @pl.when(la_ref[s] == 1)
    def _():
        for g in range(NSA_KV_HEADS):
            outs = [acc_ref[g * NSA_REP + r] / jnp.maximum(l_ref[g * NSA_REP + r], TINY) for r in range(NSA_REP)]
            _gqa_store(o_ref, outs, g, lo)


def nsa_branch_prompt(q16, k16, kcol, v16, vcol, sel, *, nb, seq, band, t=512, name):
    nq = seq // t
    tabs = _step_tables(nq, None if band is None else -(-band // t))
    qmap = lambda n, s, qi, ki, fi, la: (n * nq + qi[s], 0)
    kmap = lambda n, s, qi, ki, fi, la: (n * nq + ki[s], kcol)
    vmap = lambda n, s, qi, ki, fi, la: (n * nq + ki[s], vcol)
    in_specs = [pl.BlockSpec((t, NSA_Q_W), qmap), pl.BlockSpec((t, 128), kmap), pl.BlockSpec((t, 128), vmap)]
    args = [q16, k16, v16]
    if sel is not None:
        in_specs.append(pl.BlockSpec((NSA_KV_HEADS, t, 128), lambda n, s, qi, ki, fi, la: (0, n * nq + qi[s], 0)))
        args.append(sel)
    nst = NSA_HEADS
    return pl.pallas_call(
        functools.partial(_gqa_body, t=t, band=band, use_sel=sel is not None),
        grid_spec=pltpu.PrefetchScalarGridSpec(
            num_scalar_prefetch=4,
            grid=(nb, int(tabs[0].shape[0])),
            in_specs=in_specs,
            out_specs=pl.BlockSpec((t, NSA_Q_W), qmap),
            scratch_shapes=[pltpu.VMEM((nst, t, 128), BF16), pltpu.VMEM((nst, t, 1), F32),
                            pltpu.VMEM((nst, t, 1), F32), pltpu.VMEM((nst, t, 128), F32)],
        ),
        out_shape=jax.ShapeDtypeStruct((nb * seq, NSA_Q_W), BF16),
        compiler_params=_cparams("parallel", "arbitrary"),
        name=name,
    )(*tabs, *args)


def _cmp_finish_body(ab_ref, pe_ref, w1_ref, w2_ref, g_ref, o_ref, *, hid, norm):
    ab = ab_ref[...]
    pe_term = jnp.dot(pe_ref[...], w1_ref[...], preferred_element_type=F32)[0:1]
    h = ab[:, :hid] + pltpu.roll(ab[:, hid:], ab.shape[0] - 1, 0) + pe_term
    act = (h * jax.nn.sigmoid(h)).astype(BF16)
    y = jnp.dot(act, w2_ref[...], preferred_element_type=F32)
    if norm:
        y = y * lax.rsqrt(jnp.mean(y * y, axis=-1, keepdims=True) + NORM_EPS) * g_ref[...]
    o_ref[...] = y.astype(o_ref.dtype)


def _w1_ab(w1):
    half = w1.shape[0] // 2
    return jnp.concatenate([w1[:half], w1[half:]], axis=1).astype(BF16)


def compress_blocks(x_chunks, pe, w1, w2, gain):
    b, nchunk, cw = x_chunks.shape
    hid = w1.shape[1]
    ab = matmul(x_chunks.reshape(b * nchunk, cw), _w1_ab(w1), tn=2 * hid, name="compress_in")
    pe_rows = jnp.zeros((8, 2 * cw), BF16).at[0].set(pe.reshape(-1).astype(BF16))
    g = jnp.ones((1, HEAD_DIM), F32) if gain is None else gain.reshape(1, HEAD_DIM).astype(F32)
    const = lambda i: (0, 0)
    return pl.pallas_call(
        functools.partial(_cmp_finish_body, hid=hid, norm=gain is not None),
        grid=(b,),
        in_specs=[pl.BlockSpec((nchunk, 2 * hid), lambda i: (i, 0)), pl.BlockSpec((8, 2 * cw), const),
                  pl.BlockSpec((2 * cw, hid), const), pl.BlockSpec((hid, HEAD_DIM), const),
                  pl.BlockSpec((1, HEAD_DIM), const)],
        out_specs=pl.BlockSpec((None, nchunk, HEAD_DIM), lambda i: (i, 0, 0)),
        out_shape=jax.ShapeDtypeStruct((b, nchunk, HEAD_DIM), BF16),
        compiler_params=_cparams("parallel"),
        name="compress_finish",
    )(ab, pe_rows, w1.astype(BF16), w2.astype(BF16), g)


def _overlap_matrix(n_cmp, n_sel):
    c0 = np.arange(128)[:, None] * CMP_STRIDE
    s0 = np.arange(128)[None, :] * SEL_BLOCK
    ov = np.maximum(np.minimum(c0 + CMP_LEN, s0 + SEL_BLOCK) - np.maximum(c0, s0), 0) / CMP_LEN
    ov = ov * (np.arange(128)[:, None] < n_cmp) * (np.arange(128)[None, :] < n_sel)
    return jnp.asarray(ov, BF16)


def _cmp_body(q_ref, kc_ref, vc_ref, ov_ref, o_ref, sel_ref, *, t, pos0, n_cmp, n_sel):
    i = pl.program_id(1)
    lane = lax.broadcasted_iota(jnp.int32, (t, 128), 1)
    qpos = pos0 + i * t + lax.broadcasted_iota(jnp.int32, (t, 128), 0)
    lo = lane < HEAD_DIM
    vis = (lane * CMP_STRIDE + CMP_LEN - 1 <= qpos) & (lane < n_cmp)
    kc, vc, ov = kc_ref[...], vc_ref[...], ov_ref[...]
    cur = jnp.right_shift(qpos, SEL_SHIFT)
    valid = (lane <= cur) & (lane < n_sel)
    forced = (lane == 0) | (lane == cur) | (lane == cur - 1)
    for g in range(NSA_KV_HEADS):
        imp = jnp.zeros((t, 128), F32)
        outs = []
        for r in range(NSA_REP):
            sc = jnp.where(vis, _nt_dot(_gqa_query(q_ref, g, r, lo), kc), NEG_INF)
            m = jnp.max(sc, axis=-1, keepdims=True)
            e = jnp.where(vis, jnp.exp(sc - m), 0.0)
            p = (e / jnp.maximum(jnp.sum(e, axis=-1, keepdims=True), TINY)).astype(BF16)
            outs.append(jnp.dot(p, vc, preferred_element_type=F32))
            imp = imp + jnp.dot(p, ov, preferred_element_type=F32)
        _gqa_store(o_ref, outs, g, lo)
        score = jnp.where(valid, imp + jnp.where(forced, FORCE_BONUS, 0.0), NEG_INF)
        rank = jnp.zeros((t, 128), F32)
        for kk in range(n_sel):
            sk = score[:, kk:kk + 1]
            rank = rank + jnp.where((sk > score) | ((sk == score) & (lane > kk)), 1.0, 0.0)
        sel_ref[g] = jnp.where((rank < SEL_TOPK) & valid, 1.0, 0.0).astype(sel_ref.dtype)


def nsa_compressed_prompt(qn16, k_cmp, v_cmp, *, nb, seq, t=512):
    n_cmp = (seq - CMP_LEN) // CMP_STRIDE + 1
    n_sel = -(-seq // SEL_BLOCK)
    nq = seq // t
    qmap = lambda n, i: (n * nq + i, 0)
    cmap = lambda n, i: (n, 0, 0)
    return pl.pallas_call(
        functools.partial(_cmp_body, t=t, pos0=0, n_cmp=n_cmp, n_sel=n_sel),
        grid=(nb, nq),
        in_specs=[pl.BlockSpec((t, NSA_Q_W), qmap), pl.BlockSpec((None, 128, 128), cmap),
                  pl.BlockSpec((None, 128, 128), cmap), pl.BlockSpec((128, 128), lambda n, i: (0, 0))],
        out_specs=[pl.BlockSpec((t, NSA_Q_W), qmap),
                   pl.BlockSpec((NSA_KV_HEADS, t, 128), lambda n, i: (0, n * nq + i, 0))],
        out_shape=[jax.ShapeDtypeStruct((nb * seq, NSA_Q_W), BF16),
                   jax.ShapeDtypeStruct((NSA_KV_HEADS, nb * seq, 128), BF16)],
        compiler_params=_cparams("parallel", "parallel"),
        name="nsa_cmp_select_prompt",
    )(qn16, k_cmp, v_cmp, _overlap_matrix(n_cmp, n_sel))


def _gate_expand_matrices():
    lane = np.arange(128)[:, None]
    col = np.arange(NSA_Q_W)[None, :]
    return jnp.asarray(np.stack([(lane < NSA_GATE_W) & (lane % 3 == br) & (lane // 3 == col // HEAD_DIM)
                                 for br in range(3)]), BF16)


def _nsa_merge_body(gb_ref, e_ref, oc_ref, os_ref, ow_ref, o_ref):
    gates = jax.nn.sigmoid(gb_ref[...])
    hi = gates.astype(BF16)
    lo = (gates - hi.astype(F32)).astype(BF16)
    acc = jnp.zeros(o_ref.shape, F32)
    for br, b_ref in enumerate((oc_ref, os_ref, ow_ref)):
        w = jnp.dot(hi, e_ref[br], preferred_element_type=F32) + jnp.dot(lo, e_ref[br], preferred_element_type=F32)
        acc = acc + w * b_ref[...].astype(F32)
    o_ref[...] = acc.astype(o_ref.dtype)


def nsa_merge(proj32, gate_col, o_cmp, o_sel, o_win):
    m = o_cmp.shape[0]
    tm = TOKEN_TILE
    spec = pl.BlockSpec((tm, NSA_Q_W), lambda i: (i, 0))
    return pl.pallas_call(
        _nsa_merge_body,
        grid=(m // tm,),
        in_specs=[pl.BlockSpec((tm, 128), lambda i: (i, gate_col)),
                  pl.BlockSpec((3, 128, NSA_Q_W), lambda i: (0, 0, 0)), spec, spec, spec],
        out_specs=spec,
        out_shape=jax.ShapeDtypeStruct((m, NSA_Q_W), BF16),
        compiler_params=_cparams("parallel"),
        name="nsa_merge",
    )(proj32, _gate_expand_matrices(), o_cmp, o_sel, o_win)


def _dil_merge_body(o0, o1, o2, l0, l1, l2, o_ref):
    ls = [l0[...], l1[...], l2[...]]
    m = jnp.maximum(jnp.maximum(ls[0], ls[1]), ls[2])
    es = [jnp.exp(x - m) for x in ls]
    den = es[0] + es[1] + es[2]
    acc = sum((e / den) * o[...].astype(F32) for e, o in zip(es, (o0, o1, o2)))
    o_ref[...] = acc.astype(o_ref.dtype)


def dilation_merge(outs, lses):
    m, w = outs[0].shape
    tm = TOKEN_TILE
    spec = pl.BlockSpec((tm, w), lambda i: (i, 0))
    return pl.pallas_call(
        _dil_merge_body,
        grid=(m // tm,),
        in_specs=[spec] * 6,
        out_specs=spec,
        out_shape=jax.ShapeDtypeStruct((m, w), BF16),
        compiler_params=_cparams("parallel"),
        name="dilation_merge",
    )(*outs, *lses)


def _softmax_step(sc, mask, m, l):
    sc = jnp.where(mask, sc, NEG_INF)
    m_new = jnp.maximum(m, jnp.max(sc, axis=-1, keepdims=True))
    alpha = jnp.exp(m - m_new)
    p = jnp.where(mask, jnp.exp(sc - m_new), 0.0)
    return p, alpha, m_new, alpha * l + jnp.sum(p, axis=-1, keepdims=True)


def _diff_sample_body(pt_ref, *refs, npages, lam_init):
    del pt_ref
    page_refs = refs[:npages]
    q_ref, new_ref, lam_ref, gs_ref, o_ref = refs[npages:]
    rows = 2 * A_HEADS * 4
    ri = lax.broadcasted_iota(jnp.int32, (rows, 128), 0)
    lane = lax.broadcasted_iota(jnp.int32, (rows, 128), 1)
    first_variant = ri < rows // 2
    qs = jnp.where(first_variant == (lane < HEAD_DIM), q_ref[...] * SCALE, 0.0).astype(BF16)
    ncols = page_refs[0].shape[0]
    col = lax.broadcasted_iota(jnp.int32, (rows, ncols), 1)
    head = jnp.bitwise_and(jnp.right_shift(lax.broadcasted_iota(jnp.int32, (rows, ncols), 0), 2), A_HEADS - 1)
    page_mask = jnp.bitwise_and(col, 2 * A_HEADS - 1) == head
    tok = jnp.bitwise_and(ri, 3)
    new_mask = ((jnp.bitwise_and(lane, 2 * A_HEADS - 1) == jnp.bitwise_and(jnp.right_shift(ri, 2), A_HEADS - 1))
                & (jnp.right_shift(lane, 3) <= tok) & (lane < 4 * 2 * A_HEADS))
    pages = [pr[...].astype(BF16) for pr in page_refs] + [new_ref[...].astype(BF16)]
    masks = [page_mask] * npages + [new_mask]
    scs = [jnp.where(mk, _nt_dot(qs, pg), NEG_INF) for pg, mk in zip(pages, masks)]
    m = functools.reduce(jnp.maximum, [jnp.max(s, axis=-1, keepdims=True) for s in scs])
    ps = [jnp.where(mk, jnp.exp(s - m), 0.0) for s, mk in zip(scs, masks)]
    l = functools.reduce(lambda a, b: a + b, [jnp.sum(p, axis=-1, keepdims=True) for p in ps])
    acc = functools.reduce(lambda a, b: a + b, [
        jnp.dot(pltpu.roll(p, A_HEADS, 1).astype(BF16), pg, preferred_element_type=F32) for p, pg in zip(ps, pages)])

    lv = lam_ref[...]
    a = jnp.sum(lv[0:1] * lv[1:2], axis=-1, keepdims=True)
    b = jnp.sum(lv[2:3] * lv[3:4], axis=-1, keepdims=True)
    lam = jnp.exp(a) - jnp.exp(b) + lam_init
    o = acc / jnp.maximum(l, TINY)
    o = o[:rows // 2] - lam * o[rows // 2:]
    y = o * lax.rsqrt(jnp.mean(o * o, axis=-1, keepdims=True) + NORM_EPS) * gs_ref[...] * (1.0 - lam_init)
    o_ref[...] = y.astype(o_ref.dtype)


def diff_attention_sample(q_rows, new_page, cache_rows, page_table, lam_rows, g_subln, *, lam_init):
    db = q_rows.shape[0]
    npages = page_table.shape[1]
    prow = cache_rows.shape[1]
    page_specs = [pl.BlockSpec((None, prow, 128), functools.partial(lambda b, pt, p: (pt[b, p], 0, 0), p=p))
                  for p in range(npages)]
    per_b = lambda b, pt: (b, 0, 0)
    const = lambda b, pt: (0, 0)
    return pl.pallas_call(
        functools.partial(_diff_sample_body, npages=npages, lam_init=lam_init),
        grid_spec=pltpu.PrefetchScalarGridSpec(
            num_scalar_prefetch=1,
            grid=(db,),
            in_specs=page_specs + [pl.BlockSpec((None, 32, 128), per_b), pl.BlockSpec((None, 128, 128), per_b),
                                   pl.BlockSpec((8, 128), const), pl.BlockSpec((1, 128), const)],
            out_specs=pl.BlockSpec((None, 16, 128), per_b),
        ),
        out_shape=jax.ShapeDtypeStruct((db, 16, 128), F32),
        compiler_params=_cparams("parallel"),
        name="diff_attn_sample",
    )(page_table, *([cache_rows] * npages), q_rows, new_page, lam_rows, g_subln.reshape(1, A_VDIM).astype(F32))


def _place_new_columns(rolled, new_rows, t):
    sq = jnp.concatenate([new_rows, jnp.zeros_like(new_rows)], axis=1)
    new_t = pltpu.roll(jnp.transpose(sq)[:HEAD_DIM], 124, 1)
    if t > 128:
        new_t = jnp.concatenate([jnp.zeros((HEAD_DIM, t - 128), F32), new_t], axis=1)
    lane = lax.broadcasted_iota(jnp.int32, (HEAD_DIM, t), 1)
    return jnp.where(lane >= t - 4, new_t, rolled)


def _nsa_sample_body(pt_ref, *refs, npages, past, n_cmp, n_sel, hid):
    del pt_ref
    cn_refs = refs[:npages]
    ab_refs = refs[npages:2 * npages]
    (qn_ref, qr_ref, new_ref, sw_ref, pe_ref, w1k_ref, w1v_ref, w2k_ref, w2v_ref, gk_ref, ov_ref,
     oc_ref, os_ref, ow_ref, wout_ref) = refs[2 * npages:]
    rows = NSA_REP * 8
    nwin = sw_ref.shape[-1]
    lane = lax.broadcasted_iota(jnp.int32, (rows, 128), 1)
    tok = jnp.bitwise_and(lax.broadcasted_iota(jnp.int32, (rows, 128), 0), 7)
    lane8 = lane[:8]
    tok8 = tok[:8]
    vis = (lane * CMP_STRIDE + CMP_LEN - 1 <= past + tok) & (lane < n_cmp)
    cur = jnp.right_shift(past + tok8, SEL_SHIFT)
    valid = (lane8 <= cur) & (lane8 < n_sel)
    forced = (lane8 == 0) | (lane8 == cur) | (lane8 == cur - 1)
    new_mask = (lane <= tok) & (lane < 4)
    ov = ov_ref[...]
    pe_terms = [jnp.dot(pe_ref[c], w_ref[...], preferred_element_type=F32)[0:1]
                for c, w_ref in enumerate((w1k_ref, w1v_ref))]

    for g in range(NSA_KV_HEADS):
        cmp = []
        for c, w2_ref in enumerate((w2k_ref, w2v_ref)):
            a = jnp.concatenate([r[c, g, :, :hid] for r in ab_refs], axis=0)
            bb = jnp.concatenate([r[c, g, :, hid:] for r in ab_refs], axis=0)
            h = a + pltpu.roll(bb, a.shape[0] - 1, 0) + pe_terms[c]
            y = jnp.dot((h * jax.nn.sigmoid(h)).astype(BF16), w2_ref[...], preferred_element_type=F32)
            if c == 0:
                y = y * lax.rsqrt(jnp.mean(y * y, axis=-1, keepdims=True) + NORM_EPS) * gk_ref[...]
            cmp.append(y.astype(BF16))
        qn = (qn_ref[g].astype(F32) * SCALE).astype(BF16)
        qr = (qr_ref[g].astype(F32) * SCALE).astype(BF16)

        sc = jnp.where(vis, _nt_dot(qn, cmp[0]), NEG_INF)
        mx = jnp.max(sc, axis=-1, keepdims=True)
        e = jnp.where(vis, jnp.exp(sc - mx), 0.0)
        p = (e / jnp.maximum(jnp.sum(e, axis=-1, keepdims=True), TINY)).astype(BF16)
        oc_ref[g] = jnp.dot(p, cmp[1], preferred_element_type=F32)
        imp_r = jnp.dot(p, ov, preferred_element_type=F32)
        imp = imp_r[0:8] + imp_r[8:16] + imp_r[16:24] + imp_r[24:32]
        score = jnp.where(valid, imp + jnp.where(forced, FORCE_BONUS, 0.0), NEG_INF)
        rank = jnp.zeros((8, 128), F32)
        for kk in range(n_sel):
            sk = score[:, kk:kk + 1]
            rank = rank + jnp.where((sk > score) | ((sk == score) & (lane8 > kk)), 1.0, 0.0)
        sel = (rank < SEL_TOPK) & valid

        kt_all = jnp.concatenate([r[0, g] for r in cn_refs], axis=1).astype(BF16)
        vt_all = jnp.concatenate([r[1, g] for r in cn_refs], axis=1).astype(BF16)
        sel_f = jnp.where(sel, 1.0, 0.0)
        ncached = kt_all.shape[1]
        blk_of_col = jnp.right_shift(lax.broadcasted_iota(jnp.int32, (128, ncached), 1), SEL_SHIFT)
        expand = jnp.where(lax.broadcasted_iota(jnp.int32, (128, ncached), 0) == blk_of_col, 1.0, 0.0).astype(BF16)
        flags = jnp.dot(sel_f.astype(BF16), expand, preferred_element_type=F32)
        mask = jnp.concatenate([flags] * NSA_REP, axis=0) > 0.5
        last_blk = past // SEL_BLOCK
        flag_new = sel_f[:, last_blk:last_blk + 1] + jnp.zeros((8, 128), F32)
        mask_new = new_mask & (jnp.concatenate([flag_new] * NSA_REP, axis=0) > 0.5)
        s1 = jnp.where(mask, jnp.dot(qr, kt_all, preferred_element_type=F32), NEG_INF)
        s2 = jnp.where(mask_new, _nt_dot(qr, new_ref[0, g].astype(BF16)), NEG_INF)
        m = jnp.maximum(jnp.max(s1, axis=-1, keepdims=True), jnp.max(s2, axis=-1, keepdims=True))
        e1 = jnp.where(mask, jnp.exp(s1 - m), 0.0)
        e2 = jnp.where(mask_new, jnp.exp(s2 - m), 0.0)
        l = jnp.maximum(jnp.sum(e1, axis=-1, keepdims=True) + jnp.sum(e2, axis=-1, keepdims=True), TINY)
        os_ref[g] = (_nt_dot(e1.astype(BF16), vt_all)
                     + jnp.dot(e2.astype(BF16), new_ref[1, g].astype(BF16), preferred_element_type=F32)) / l

        wl = lax.broadcasted_iota(jnp.int32, (rows, nwin), 1)
        wt = jnp.bitwise_and(lax.broadcasted_iota(jnp.int32, (rows, nwin), 0), 7)
        wmask = wl >= nwin + wt - NSA_WINDOW
        kt = sw_ref[0, g]
        vt = sw_ref[1, g]
        m = jnp.full((rows, 1), NEG_INF, F32)
        l = jnp.zeros((rows, 1), F32)
        pr, alpha, m, l = _softmax_step(jnp.dot(qr, kt.astype(BF16), preferred_element_type=F32), wmask, m, l)
        acc = _nt_dot(pr.astype(BF16), vt.astype(BF16))
        pr, alpha, m, l = _softmax_step(_nt_dot(qr, new_ref[2, g].astype(BF16)), new_mask, m, l)
        acc = alpha * acc + jnp.dot(pr.astype(BF16), new_ref[3, g].astype(BF16), preferred_element_type=F32)
        ow_ref[g] = acc / jnp.maximum(l, TINY)
        wout_ref[0, g] = _place_new_columns(pltpu.roll(kt, nwin - 4, 1), new_ref[2, g], nwin)
        wout_ref[1, g] = _place_new_columns(pltpu.roll(vt, nwin - 4, 1), new_ref[3, g], nwin)


def nsa_sample(qn, qr, new_rows, cache_t, ab, win_state, page_table, cmp_w, g_kc, *, past):
    pe_k, w_k1, w_k2, pe_v, w_v1, w_v2 = cmp_w
    db = qn.shape[0]
    npages = page_table.shape[1]
    hid = w_k1.shape[1]
    nwin = win_state.shape[-1]
    n_cmp = (past + 4 - CMP_LEN) // CMP_STRIDE + 1
    n_sel = -(-(past + 4) // SEL_BLOCK)
    assert n_cmp <= npages * 8 - 1 and past % SEL_BLOCK == 0
    pe_rows = jnp.zeros((2, 8, pe_k.size), BF16).at[:, 0].set(
        jnp.stack([pe_k.reshape(-1), pe_v.reshape(-1)]).astype(BF16))
    cn_specs = [pl.BlockSpec((None, 2, NSA_KV_HEADS, HEAD_DIM, 128),
                             functools.partial(lambda b, pt, p: (pt[b, p], 1, 0, 0, 0), p=p)) for p in range(npages)]
    ab_specs = [pl.BlockSpec((2, None, NSA_KV_HEADS, 8, 2 * hid),
                             functools.partial(lambda b, pt, p: (0, pt[b, p], 0, 0, 0), p=p)) for p in range(npages)]
    b4 = lambda b, pt: (b, 0, 0, 0)
    b5 = lambda b, pt: (b, 0, 0, 0, 0)
    c2 = lambda b, pt: (0, 0)
    c3 = lambda b, pt: (0, 0, 0)
    o_spec = pl.BlockSpec((None, NSA_KV_HEADS, 32, HEAD_DIM), b4)
    o_shape = jax.ShapeDtypeStruct((db, NSA_KV_HEADS, 32, HEAD_DIM), F32)
    return pl.pallas_call(
        functools.partial(_nsa_sample_body, npages=npages, past=past, n_cmp=n_cmp, n_sel=n_sel, hid=hid),
        grid_spec=pltpu.PrefetchScalarGridSpec(
            num_scalar_prefetch=1,
            grid=(db,),
            in_specs=cn_specs + ab_specs + [
                pl.BlockSpec((None, NSA_KV_HEADS, 32, HEAD_DIM), b4), pl.BlockSpec((None, NSA_KV_HEADS, 32, HEAD_DIM), b4),
                pl.BlockSpec((None, 4, NSA_KV_HEADS, 128, HEAD_DIM), b5),
                pl.BlockSpec((None, 2, NSA_KV_HEADS, HEAD_DIM, nwin), b5),
                pl.BlockSpec((2, 8, pe_k.size), c3),
                pl.BlockSpec(w_k1.shape, c2), pl.BlockSpec(w_v1.shape, c2),
                pl.BlockSpec(w_k2.shape, c2), pl.BlockSpec(w_v2.shape, c2),
                pl.BlockSpec((1, HEAD_DIM), c2), pl.BlockSpec((128, 128), c2)],
            out_specs=[o_spec, o_spec, o_spec, pl.BlockSpec((None, 2, NSA_KV_HEADS, HEAD_DIM, nwin), b5)],
        ),
        out_shape=[o_shape, o_shape, o_shape, jax.ShapeDtypeStruct(win_state.shape, F32)],
        compiler_params=_cparams("parallel"),
        name="nsa_sample",
    )(page_table, *([cache_t] * npages), *([ab] * npages), qn, qr, new_rows, win_state, pe_rows,
      w_k1.astype(BF16), w_v1.astype(BF16), w_k2.astype(BF16), w_v2.astype(BF16),
      g_kc.reshape(1, HEAD_DIM).astype(F32), _overlap_matrix(n_cmp, n_sel))


def _dil_sample_body(q_ref, kn_ref, vn_ref, st_ref, o_ref, lse_ref, roll_ref, *, window, dil, hg):
    nbuf = st_ref.shape[-1]
    w = hg * HEAD_DIM
    r = hg * 8
    own = (jnp.right_shift(lax.broadcasted_iota(jnp.int32, (r, w), 0), 3)
           == jnp.right_shift(lax.broadcasted_iota(jnp.int32, (r, w), 1), 6))
    qbd = jnp.where(own, jnp.concatenate([q_ref[...] * SCALE] * hg, axis=0), 0.0).astype(BF16)
    kt, vt = st_ref[0], st_ref[1]
    zpad = jnp.zeros((128 - kn_ref.shape[0], w), F32)
    kn = jnp.concatenate([kn_ref[...], zpad], axis=0)
    vn = jnp.concatenate([vn_ref[...], zpad], axis=0)
    tok = jnp.bitwise_and(lax.broadcasted_iota(jnp.int32, (r, nbuf), 0), 7)
    dist = nbuf + tok - lax.broadcasted_iota(jnp.int32, (r, nbuf), 1)
    mask = (dist <= window) & (jnp.bitwise_and(dist, dil - 1) == 0)
    ncol = lax.broadcasted_iota(jnp.int32, (r, 128), 1)
    nd = jnp.bitwise_and(lax.broadcasted_iota(jnp.int32, (r, 128), 0), 7) - ncol
    nmask = (nd >= 0) & (jnp.bitwise_and(nd, dil - 1) == 0) & (ncol < 4)
    s1 = jnp.where(mask, jnp.dot(qbd, kt.astype(BF16), preferred_element_type=F32), NEG_INF)
    s2 = jnp.where(nmask, _nt_dot(qbd, kn.astype(BF16)), NEG_INF)
    m = jnp.maximum(jnp.max(s1, axis=-1, keepdims=True), jnp.max(s2, axis=-1, keepdims=True))
    e1 = jnp.where(mask, jnp.exp(s1 - m), 0.0)
    e2 = jnp.where(nmask, jnp.exp(s2 - m), 0.0)
    l = jnp.maximum(jnp.sum(e1, axis=-1, keepdims=True) + jnp.sum(e2, axis=-1, keepdims=True), TINY)
    o_all = jnp.where(own, (_nt_dot(e1.astype(BF16), vt.astype(BF16))
                            + jnp.dot(e2.astype(BF16), vn.astype(BF16), preferred_element_type=F32)) / l, 0.0)
    o = o_all[0:8]
    for h in range(1, hg):
        o = o + o_all[h * 8:(h + 1) * 8]
    o_ref[...] = o
    lse_ref[...] = m + jnp.log(l) + jnp.zeros((r, 128), F32)

    lane = lax.broadcasted_iota(jnp.int32, (w, nbuf), 1)
    for kv, (old, new) in enumerate(((kt, kn), (vt, vn))):
        new_t = jnp.concatenate([jnp.transpose(new[:, b * 128:(b + 1) * 128]) for b in range(w // 128)], axis=0)
        new_t = pltpu.roll(new_t, 124, 1)
        if nbuf > 128:
            new_t = jnp.concatenate([jnp.zeros((w, nbuf - 128), F32), new_t], axis=1)
        roll_ref[kv] = jnp.where(lane >= nbuf - 4, new_t, pltpu.roll(old, nbuf - 4, 1))


def dilated_attention_sample(q8, k_new, v_new, state_t, *, window, dil, hg=4):
    db = q8.shape[0]
    nbuf = state_t.shape[-1]
    assert dil & (dil - 1) == 0 and nbuf >= window and hg % 2 == 0
    w = hg * HEAD_DIM
    return pl.pallas_call(
        functools.partial(_dil_sample_body, window=window, dil=dil, hg=hg),
        grid=(db, C_HEADS // hg),
        in_specs=[pl.BlockSpec((None, 8, w), lambda b, j: (b, 0, j)),
                  pl.BlockSpec((None, 8, w), lambda b, j: (b, 0, j)),
                  pl.BlockSpec((None, 8, w), lambda b, j: (b, 0, j)),
                  pl.BlockSpec((None, 2, w, nbuf), lambda b, j: (b, 0, j, 0))],
        out_specs=[pl.BlockSpec((None, 8, w), lambda b, j: (b, 0, j)),
                   pl.BlockSpec((None, hg * 8, 128), lambda b, j: (b, j, 0)),
                   pl.BlockSpec((None, 2, w, nbuf), lambda b, j: (b, 0, j, 0))],
        out_shape=[jax.ShapeDtypeStruct((db, 8, C_W), F32), jax.ShapeDtypeStruct((db, C_HEADS * 8, 128), F32),
                   jax.ShapeDtypeStruct(state_t.shape, F32)],
        compiler_params=_cparams("parallel", "parallel"),
        name=f"dilated_attn_sample_{dil}",
    )(q8, k_new, v_new, state_t)


def _rms_norm(x, g):
    xf = x.astype(F32)
    y = xf * lax.rsqrt(jnp.mean(xf * xf, axis=-1, keepdims=True) + NORM_EPS)
    return (y * g.astype(F32)).astype(x.dtype)


def _partial_rope(x, pos):
    half = ROT_DIM // 2
    inv_freq = ROPE_THETA ** (-jnp.arange(half, dtype=F32) / half)
    ang = pos.astype(F32)[:, None] * inv_freq[None, :]
    shape = (1, pos.shape[0]) + (1,) * (x.ndim - 3) + (half,)
    cos = jnp.cos(ang).reshape(shape)
    sin = jnp.sin(ang).reshape(shape)
    xf = x.astype(F32)
    x1, x2 = xf[..., :half], xf[..., half:ROT_DIM]
    out = jnp.concatenate([x1 * cos - x2 * sin, x2 * cos + x1 * sin, xf[..., ROT_DIM:]], axis=-1)
    return out.astype(x.dtype)


def _masked_softmax(s, mask):
    s = jnp.where(mask, s.astype(F32), NEG_INF)
    m = jnp.max(s, axis=-1, keepdims=True)
    e = jnp.where(mask, jnp.exp(s - m), 0.0)
    l = jnp.maximum(jnp.sum(e, axis=-1, keepdims=True), TINY)
    return e / l, (m + jnp.log(l))[..., 0]


def _paged_rows(cache, page_table):
    g = cache[page_table]
    return g.reshape((g.shape[0], g.shape[1] * g.shape[2]) + g.shape[3:])


def _roll_buffer(buf, new):
    n_buf, t = buf.shape[1], new.shape[1]
    if t >= n_buf:
        return new[:, t - n_buf:]
    return jnp.concatenate([buf[:, t:], new], axis=1)


def _gather_rows(buf, new, idx):
    n_buf = buf.shape[1]
    from_buf = buf[:, np.clip(idx, 0, n_buf - 1)]
    from_new = new[:, np.clip(idx - n_buf, 0, new.shape[1] - 1)]
    sel = (idx < n_buf).reshape(idx.shape + (1,) * (buf.ndim - 2))
    return jnp.where(sel, from_buf, from_new)


def _banded_attn(q, k, v, band):
    n, L, g, r, dh = q.shape
    blk = math.gcd(L, Q_BLOCK)
    nb = L // blk
    pad = ((0, 0), (band, 0), (0, 0), (0, 0))
    idx = np.arange(nb)[:, None] * blk + np.arange(blk + band)[None, :]
    kb = jnp.pad(k, pad)[:, idx]
    vb = jnp.pad(v, pad)[:, idx]
    qb = q.reshape(n, nb, blk, g, r, dh)
    s = jnp.einsum('nbqgrd,nbkgd->nbgrqk', qb, kb, preferred_element_type=F32) * SCALE
    qpos = np.arange(nb)[:, None] * blk + np.arange(blk)[None, :]
    kpos = idx - band
    dist = qpos[:, :, None] - kpos[:, None, :]
    mask = (dist >= 0) & (dist <= band) & (kpos[:, None, :] >= 0)
    p, lse = _masked_softmax(s, mask[None, :, None, None])
    o = jnp.einsum('nbgrqk,nbkgd->nbqgrd', p, vb.astype(F32))
    return o.reshape(n, L, g, r, dh).astype(q.dtype), lse.transpose(0, 1, 4, 2, 3).reshape(n, L, g, r)


def _diff_heads(qa, ka, va, pos, g_q, g_k):
    n, t = qa.shape[:2]
    q = _partial_rope(_rms_norm(qa.reshape(n, t, A_HEADS, 2, HEAD_DIM), g_q), pos)
    k = _partial_rope(_rms_norm(ka.reshape(n, t, A_HEADS, 2, HEAD_DIM), g_k), pos)
    return q, k, va.reshape(n, t, A_HEADS, A_VDIM)


def _diff_core(q, k, v, qpos, kpos, lam):
    s = jnp.einsum('nqhmd,nkhmd->nhmqk', q, k, preferred_element_type=F32) * SCALE
    p, _ = _masked_softmax(s, (kpos[None, :] <= qpos[:, None])[None, None, None])
    a = p[:, :, 0] - lam * p[:, :, 1]
    return jnp.einsum('nhqk,nkhe->nqhe', a, v.astype(F32)).astype(v.dtype)


def _diff_attn_prompt(q, k, v, pos, lam):
    n, s = q.shape[:2]
    nb = s // Q_BLOCK
    qb = q.reshape((n, nb, Q_BLOCK) + q.shape[2:]).swapaxes(0, 1)
    ob = lax.map(lambda a: _diff_core(a[0], k, v, a[1], pos, lam), (qb, pos.reshape(nb, Q_BLOCK)))
    return ob.swapaxes(0, 1).reshape(n, s, A_HEADS, A_VDIM)


def _diff_output(o, g_sub, lam_init):
    n, t = o.shape[:2]
    return (_rms_norm(o, g_sub) * (1.0 - lam_init)).reshape(n, t, A_V_W)


def _nsa_heads(qb, kvb, gb, pos, g_q, g_k):
    n, t = qb.shape[:2]
    q = _rms_norm(qb.reshape(n, t, NSA_KV_HEADS, NSA_REP, HEAD_DIM), g_q)
    q_rot = _partial_rope(q, pos)
    kv = kvb.reshape(n, t, 6, NSA_KV_HEADS, HEAD_DIM)
    k_slc = _partial_rope(_rms_norm(kv[:, :, 2], g_k[1]), pos)
    k_win = _partial_rope(_rms_norm(kv[:, :, 4], g_k[2]), pos)
    long_rows = jnp.stack([kv[:, :, 0], kv[:, :, 1], k_slc, kv[:, :, 3]], axis=2)
    win_rows = jnp.stack([k_win, kv[:, :, 5]], axis=2)
    gates = jax.nn.sigmoid(gb.astype(F32)).reshape(n, t, NSA_KV_HEADS, NSA_REP, 3)
    return q, q_rot, long_rows, win_rows, gates


def _nsa_compress(rows, pe, w1, w2):
    n, L, g, dh = rows.shape
    n_cmp = (L - CMP_LEN) // CMP_STRIDE + 1
    idx = np.arange(n_cmp)[:, None] * CMP_STRIDE + np.arange(CMP_LEN)[None, :]
    blocks = rows[:, idx] + pe[None, None, :, None, :]
    flat = blocks.transpose(0, 1, 3, 2, 4).reshape(n, n_cmp, g, CMP_LEN * dh)
    return jax.nn.silu(flat @ w1) @ w2


def _nsa_cmp_attn(q, k_cmp, v_cmp, qpos):
    n_cmp = k_cmp.shape[1]
    end = jnp.asarray(np.arange(n_cmp) * CMP_STRIDE + CMP_LEN - 1)
    s = jnp.einsum('nqgrd,ncgd->nqgrc', q, k_cmp, preferred_element_type=F32) * SCALE
    visible = end[None, :] <= qpos[:, None]
    p, _ = _masked_softmax(s, visible[None, :, None, None, :])
    o = jnp.einsum('nqgrc,ncgd->nqgrd', p, v_cmp.astype(F32)).astype(q.dtype)
    return o, p


def _cmp_to_sel_overlap(n_cmp, n_sel):
    c0 = np.arange(n_cmp)[:, None] * CMP_STRIDE
    s0 = np.arange(n_sel)[None, :] * SEL_BLOCK
    ov = np.minimum(c0 + CMP_LEN, s0 + SEL_BLOCK) - np.maximum(c0, s0)
    return jnp.asarray(np.maximum(ov, 0) / CMP_LEN, dtype=F32)


def _nsa_select(p_cmp, qpos, n_sel):
    imp = jnp.einsum('nqgrc,cj->nqgj', p_cmp, _cmp_to_sel_overlap(p_cmp.shape[-1], n_sel))
    blk = jnp.arange(n_sel)[None, :]
    cur = (qpos // SEL_BLOCK)[:, None]
    valid = blk <= cur
    forced = (blk == 0) | (blk == cur) | (blk == cur - 1)
    score = jnp.where(valid[None, :, None], imp + jnp.where(forced, FORCE_BONUS, 0.0)[None, :, None], NEG_INF)
    _, sel = lax.top_k(score, min(SEL_TOPK, n_sel))
    return sel


def _nsa_sel_attn(q, k_blk, v_blk, sel, qpos):
    n, qc, g, r, dh = q.shape
    kk = sel.shape[-1]
    n_i = jnp.arange(n)[:, None, None, None]
    g_i = jnp.arange(g)[None, None, :, None]
    kg = k_blk[n_i, g_i, sel]
    vg = v_blk[n_i, g_i, sel]
    kpos = sel[..., None] * SEL_BLOCK + jnp.arange(SEL_BLOCK)
    visible = (kpos <= qpos[None, :, None, None, None]).reshape(n, qc, g, 1, kk * SEL_BLOCK)
    s = jnp.einsum('nqgrd,nqgkbd->nqgrkb', q, kg, preferred_element_type=F32)
    p, _ = _masked_softmax(s.reshape(n, qc, g, r, kk * SEL_BLOCK) * SCALE, visible)
    o = jnp.einsum('nqgrx,nqgxd->nqgrd', p, vg.reshape(n, qc, g, kk * SEL_BLOCK, dh).astype(F32))
    return o.astype(q.dtype)


def _nsa_long_branches(q, q_rot, long_all, qpos, g_kc, pe_k, w_k1, w_k2, pe_v, w_v1, w_v2):
    n, L, _, g, dh = long_all.shape
    k_cmp = _rms_norm(_nsa_compress(long_all[:, :, 0], pe_k, w_k1, w_k2), g_kc)
    v_cmp = _nsa_compress(long_all[:, :, 1], pe_v, w_v1, w_v2)
    o_cmp, p_cmp = _nsa_cmp_attn(q, k_cmp, v_cmp, qpos)
    n_sel = -(-L // SEL_BLOCK)
    sel = _nsa_select(p_cmp, qpos, n_sel)

    def to_blocks(x):
        x = jnp.pad(x, ((0, 0), (0, n_sel * SEL_BLOCK - L), (0, 0), (0, 0)))
        return x.reshape(n, n_sel, SEL_BLOCK, g, dh).transpose(0, 3, 1, 2, 4)

    k_blk, v_blk = to_blocks(long_all[:, :, 2]), to_blocks(long_all[:, :, 3])
    nq = q.shape[1]
    qc = math.gcd(nq, SEL_Q_BLOCK)
    nc = nq // qc

    def chunks(x):
        return x.reshape((n, nc, qc) + x.shape[2:]).swapaxes(0, 1)

    o_sel = lax.map(lambda a: _nsa_sel_attn(a[0], k_blk, v_blk, a[1], a[2]),
                    (chunks(q_rot), chunks(sel), qpos.reshape(nc, qc)))
    return o_cmp, o_sel.swapaxes(0, 1).reshape(q.shape)


def _window_attn_sample(q, k_all, v_all, n_buf, window):
    t = q.shape[1]
    dist = (n_buf + np.arange(t))[:, None] - np.arange(n_buf + t)[None, :]
    visible = (dist >= 0) & (dist <= window)
    s = jnp.einsum('ntgrd,nkgd->ntgrk', q, k_all, preferred_element_type=F32) * SCALE
    p, _ = _masked_softmax(s, visible[None, :, None, None, :])
    return jnp.einsum('ntgrk,nkgd->ntgrd', p, v_all.astype(F32)).astype(q.dtype)


def _nsa_merge(gates, o_cmp, o_sel, o_win):
    o = gates[..., 0:1] * o_cmp.astype(F32) + gates[..., 1:2] * o_sel.astype(F32) + gates[..., 2:3] * o_win.astype(F32)
    n, t = o.shape[:2]
    return o.reshape(n, t, NSA_Q_W).astype(o_cmp.dtype)


def _split_in0(proj):
    sizes = [A_QK_W, A_QK_W, A_V_W, NSA_Q_W, NSA_KV_W, NSA_GATE_W]
    return jnp.split(proj, [int(o) for o in np.cumsum(sizes)[:-1]], axis=-1)


def _even_mixer_prompt(proj, pos, mw):
    g_qa, g_ka, lam, lam_init, g_subln, g_qb, g_kb, cmp_w = mw
    n, s = proj.shape[:2]
    qa, ka, va, qb, kvb, gb = _split_in0(proj)
    q, k, v = _diff_heads(qa, ka, va, pos, g_qa, g_ka)
    o_a = _diff_output(_diff_attn_prompt(q, k, v, pos, lam), g_subln, lam_init)
    qn, qr, long_rows, win_rows, gates = _nsa_heads(qb, kvb, gb, pos, g_qb, g_kb)
    o_cmp, o_sel = _nsa_long_branches(qn, qr, long_rows, pos, g_kb[0], *cmp_w)
    o_win, _ = _banded_attn(qr, win_rows[:, :, 0], win_rows[:, :, 1], NSA_WINDOW)
    o_b = _nsa_merge(gates, o_cmp, o_sel, o_win)
    a_rows = jnp.stack([k.reshape(n, s, A_HEADS, A_VDIM), v], axis=2)
    return jnp.concatenate([o_a, o_b], axis=-1), a_rows, long_rows, win_rows[:, s - min(NSA_WINDOW, s):]


def _even_mixer_sample(proj, pos, cache_a_kv, cache_nsa_kv, state_nsa_win, page_table, mw):
    g_qa, g_ka, lam, lam_init, g_subln, g_qb, g_kb, cmp_w = mw
    n, t = proj.shape[:2]
    qa, ka, va, qb, kvb, gb = _split_in0(proj)
    q, k, v = _diff_heads(qa, ka, va, pos, g_qa, g_ka)
    a_rows = jnp.stack([k.reshape(n, t, A_HEADS, A_VDIM), v], axis=2)
    a_all = jnp.concatenate([_paged_rows(cache_a_kv, page_table), a_rows], axis=1)
    L = a_all.shape[1]
    o = _diff_core(q, a_all[:, :, 0].reshape(n, L, A_HEADS, 2, HEAD_DIM), a_all[:, :, 1], pos,
                   jnp.arange(L, dtype=jnp.int32), lam)
    o_a = _diff_output(o, g_subln, lam_init)
    qn, qr, long_rows, win_rows, gates = _nsa_heads(qb, kvb, gb, pos, g_qb, g_kb)
    long_all = jnp.concatenate([_paged_rows(cache_nsa_kv, page_table), long_rows], axis=1)
    o_cmp, o_sel = _nsa_long_branches(qn, qr, long_all, pos, g_kb[0], *cmp_w)
    n_buf = state_nsa_win.shape[1]
    win_all = jnp.concatenate([state_nsa_win, win_rows], axis=1)
    o_win = _window_attn_sample(qr, win_all[:, :, 0], win_all[:, :, 1], n_buf, NSA_WINDOW)
    o_b = _nsa_merge(gates, o_cmp, o_sel, o_win)
    return jnp.concatenate([o_a, o_b], axis=-1), a_rows, long_rows, _roll_buffer(state_nsa_win, win_rows)


def _dilated_heads(proj, pos, g_qc, g_kc):
    n, t = proj.shape[:2]
    proj = proj.reshape(n, t, N_C_GROUPS, 3, C_HEADS, HEAD_DIM)
    return [(_partial_rope(_rms_norm(proj[:, :, gi, 0], g_qc[gi]), pos),
             _partial_rope(_rms_norm(proj[:, :, gi, 1], g_kc[gi]), pos),
             proj[:, :, gi, 2]) for gi in range(N_C_GROUPS)]


def _dilated_attn_prompt(q, k, v, dil, band):
    n, S, h, dh = q.shape
    L = S // dil

    def sub(x):
        return x.reshape(n, L, dil, h, dh).transpose(0, 2, 1, 3, 4).reshape(n * dil, L, h, dh)

    o, lse = _banded_attn(sub(q)[:, :, :, None], sub(k), sub(v), band)
    o = o.reshape(n, dil, L, h, dh).transpose(0, 2, 1, 3, 4).reshape(n, S, h, dh)
    lse = lse.reshape(n, dil, L, h).transpose(0, 2, 1, 3).reshape(n, S, h)
    return o, lse


def _dilated_attn_sample(q, buf, new_rows, dil, window):
    n_buf, t = buf.shape[1], q.shape[1]
    n_keys = window // dil + 1
    idx = n_buf + np.arange(t)[:, None] - dil * np.arange(n_keys)[None, :]
    rows = _gather_rows(buf, new_rows, idx)
    s = jnp.einsum('nthd,ntkhd->nthk', q, rows[:, :, :, 0], preferred_element_type=F32) * SCALE
    p, lse = _masked_softmax(s, (idx >= 0)[None, :, None, :])
    o = jnp.einsum('nthk,ntkhd->nthd', p, rows[:, :, :, 1].astype(F32))
    return o.astype(q.dtype), lse


def _merge_dilations(outs, lses):
    w = jax.nn.softmax(jnp.stack(lses, axis=0), axis=0)
    o = jnp.einsum('gnth,gnthd->nthd', w, jnp.stack(outs, axis=0).astype(F32))
    return o.astype(outs[0].dtype)


def _odd_mixer_prompt(proj, pos, g_qc, g_kc):
    n, s = proj.shape[:2]
    outs, lses, bufs = [], [], []
    for (window, dil), (q, k, v) in zip(C_GROUPS, _dilated_heads(proj, pos, g_qc, g_kc)):
        o, lse = _dilated_attn_prompt(q, k, v, dil, window // dil)
        outs.append(o)
        lses.append(lse)
        bufs.append(jnp.stack([k, v], axis=2)[:, s - min(window, s):])
    return _merge_dilations(outs, lses).reshape(n, s, C_W), bufs


def _odd_mixer_sample(proj, pos, states, g_qc, g_kc):
    n, t = proj.shape[:2]
    outs, lses, bufs = [], [], []
    for (window, dil), (q, k, v), buf in zip(C_GROUPS, _dilated_heads(proj, pos, g_qc, g_kc), states):
        new_rows = jnp.stack([k, v], axis=2)
        o, lse = _dilated_attn_sample(q, buf, new_rows, dil, window)
        outs.append(o)
        lses.append(lse)
        bufs.append(_roll_buffer(buf, new_rows))
    return _merge_dilations(outs, lses).reshape(n, t, C_W), bufs


def _router_body(x_ref, w_ref, idx_ref, gate_ref):
    logits = jnp.dot(x_ref[...], w_ref[...], preferred_element_type=F32)
    lane = lax.broadcasted_iota(jnp.int32, logits.shape, 1).astype(F32)
    lg = jnp.where(lane < N_EXPERTS, logits, NEG_INF)
    v1 = jnp.max(lg, axis=-1, keepdims=True)
    i1 = jnp.min(jnp.where(lg == v1, lane, 128.0), axis=-1, keepdims=True)
    lg2 = jnp.where(lane == i1, NEG_INF, lg)
    v2 = jnp.max(lg2, axis=-1, keepdims=True)
    i2 = jnp.min(jnp.where(lg2 == v2, lane, 128.0), axis=-1, keepdims=True)
    e = jnp.exp(v2 - v1)
    idx_ref[...] = jnp.where(lane == 0, i1, jnp.where(lane == 1, i2, 0.0)).astype(jnp.int32)
    gate_ref[...] = jnp.where(lane == 0, 1.0 / (1.0 + e), jnp.where(lane == 1, e / (1.0 + e), 0.0))


def moe_router(h_bf16, w_router):
    m, d = h_bf16.shape
    tm = TOKEN_TILE
    w = jnp.pad(w_router, ((0, 0), (0, 128 - w_router.shape[1]))).astype(BF16)
    spec = pl.BlockSpec((tm, 128), lambda i: (i, 0))
    return pl.pallas_call(
        _router_body,
        grid=(m // tm,),
        in_specs=[pl.BlockSpec((tm, d), lambda i: (i, 0)), pl.BlockSpec((d, 128), lambda i: (0, 0))],
        out_specs=[spec, spec],
        out_shape=[jax.ShapeDtypeStruct((m, 128), jnp.int32), jax.ShapeDtypeStruct((m, 128), F32)],
        compiler_params=_cparams("parallel"),
        name="moe_router",
    )(h_bf16, w)


def _moe_combine_body(r_ref, g_ref, y0_ref, y1_ref, o_ref):
    g = g_ref[...]
    o_ref[...] = r_ref[...] + g[:, 0:1] * y0_ref[...] + g[:, 1:2] * y1_ref[...]


def moe_combine(resid, gates, y0, y1):
    m, d = resid.shape
    tm = TOKEN_TILE
    spec = pl.BlockSpec((tm, d), lambda i: (i, 0))
    return pl.pallas_call(
        _moe_combine_body,
        grid=(m // tm,),
        in_specs=[spec, pl.BlockSpec((tm, 128), lambda i: (i, 0)), spec, spec],
        out_specs=spec,
        out_shape=jax.ShapeDtypeStruct((m, d), F32),
        compiler_params=_cparams("parallel"),
        name="moe_combine",
    )(resid, gates, y0, y1)


def _moe(h_bf16, h_f32, resid, w_router, wg, wu, wd):
    m, d = h_bf16.shape
    tm = TOKEN_TILE
    idx, gates = moe_router(h_bf16, w_router)
    top_i = idx[:, :TOP_K]
    flat_e = top_i.reshape(-1)
    onehot = (flat_e[:, None] == jnp.arange(N_EXPERTS, dtype=jnp.int32)[None, :]).astype(jnp.int32)
    running = jnp.cumsum(onehot, axis=0)
    counts = running[-1]
    rank = jnp.sum(running * onehot, axis=1) - 1
    padded = ((counts + tm - 1) // tm) * tm
    pend = jnp.cumsum(padded)
    dest = (jnp.sum((pend - padded)[None, :] * onehot, axis=1) + rank).astype(jnp.int32)
    p_rows = m * TOP_K + N_EXPERTS * tm
    tile_start = jnp.arange(p_rows // tm) * tm
    tile_expert = jnp.minimum(jnp.sum(tile_start[:, None] >= pend[None, :], axis=1), N_EXPERTS - 1).astype(jnp.int32)

    slot = jnp.arange(m * TOP_K, dtype=jnp.int32)
    src_tok = jnp.zeros((p_rows,), jnp.int32).at[dest].set(slot // TOP_K)
    ys = moe_grouped_ffn(h_f32[src_tok], tile_expert, wg, wu, wd, tf=896)
    back = dest.reshape(m, TOP_K)
    return moe_combine(resid, gates, ys[back[:, 0]], ys[back[:, 1]])


COL_QA, COL_KA, COL_VA, COL_QB = 0, A_QK_W, 2 * A_QK_W, 2 * A_QK_W + A_V_W
COL_KVB = COL_QB + NSA_Q_W
COL_GATE = COL_KVB + NSA_KV_W
KVB_PAIR = NSA_KV_HEADS * HEAD_DIM


def _pad_axis(x, axis, size):
    pad = [(0, 0)] * x.ndim
    pad[axis] = (0, size - x.shape[axis])
    return jnp.pad(x, pad)


def even_mixer(proj, proj16, tables, nb, seq, db, dt, caches, mw):
    g_qa, g_ka, lam_rows, lam_init, g_subln, g_qb, g_kb, cmp_w = mw
    cache_a_kv, cache_nsa_kv, state_nsa_win, page_table = caches
    pe_k, w_k1, w_k2, pe_v, w_v1, w_v2 = cmp_w
    mp = nb * seq
    ms = db * dt
    assert dt == 4
    rope16_32 = ((True, BF16), (True, F32))
    qk_a16, qk_a32 = head_norm_rope(proj, jnp.stack([_head_gain(g_qa, A_QK_W), _head_gain(g_ka, A_QK_W)]), tables,
                                    width=A_QK_W, col0=0, outs=rope16_32, name="hnr_diff_qk")
    qn16, qr16 = head_norm_rope(proj, _head_gain(g_qb, NSA_Q_W)[None], tables, width=NSA_Q_W,
                                col0=COL_QB // NSA_Q_W, outs=((False, BF16), (True, BF16)), name="hnr_nsa_q")
    ks16, ks32 = head_norm_rope(proj, _head_gain(g_kb[1], KVB_PAIR)[None], tables, width=KVB_PAIR,
                                col0=(COL_KVB + 2 * KVB_PAIR) // KVB_PAIR, outs=rope16_32, name="hnr_nsa_kslc")
    kw16, kw32 = head_norm_rope(proj, _head_gain(g_kb[2], KVB_PAIR)[None], tables, width=KVB_PAIR,
                                col0=(COL_KVB + 4 * KVB_PAIR) // KVB_PAIR, outs=rope16_32, name="hnr_nsa_kwin")

    o_a = diff_attention_prompt(qk_a16, proj16, lam_rows, g_subln, nb=nb, seq=seq, lam_init=lam_init)

    nchunk = seq // CMP_STRIDE

    def chunks(col):
        xc = proj16[:mp, col:col + KVB_PAIR].reshape(nb, nchunk, CMP_STRIDE, NSA_KV_HEADS, HEAD_DIM)
        return xc.transpose(0, 3, 1, 2, 4).reshape(nb * NSA_KV_HEADS, nchunk, CMP_STRIDE * HEAD_DIM)

    def pair_lanes(c):
        return c.reshape(nb, NSA_KV_HEADS, nchunk, HEAD_DIM).transpose(0, 2, 1, 3).reshape(nb, nchunk, KVB_PAIR)

    k_cmp = pair_lanes(compress_blocks(chunks(COL_KVB), pe_k, w_k1, w_k2, g_kb[0]))
    v_cmp = pair_lanes(compress_blocks(chunks(COL_KVB + KVB_PAIR), pe_v, w_v1, w_v2, None))
    o_cmp, sel = nsa_compressed_prompt(qn16, k_cmp, v_cmp, nb=nb, seq=seq)
    o_sel = nsa_branch_prompt(qr16, ks16, 0, proj16, (COL_KVB + 3 * KVB_PAIR) // KVB_PAIR, sel,
                              nb=nb, seq=seq, band=None, name="nsa_sel_prompt")
    o_win = nsa_branch_prompt(qr16, kw16, 0, proj16, (COL_KVB + 5 * KVB_PAIR) // KVB_PAIR, None,
                              nb=nb, seq=seq, band=NSA_WINDOW, name="nsa_win_prompt")
    a_rows = jnp.concatenate([qk_a32[:, A_QK_W:], proj[:, COL_VA:COL_VA + A_V_W]], axis=1)
    long_rows = jnp.concatenate([proj[:, COL_KVB:COL_KVB + 2 * KVB_PAIR], ks32,
                                 proj[:, COL_KVB + 3 * KVB_PAIR:COL_KVB + 4 * KVB_PAIR]], axis=1)
    win_rows = jnp.concatenate([kw32, proj[:, COL_KVB + 5 * KVB_PAIR:COL_KVB + 6 * KVB_PAIR]], axis=1)

    past = page_table.shape[1] * cache_a_kv.shape[1]
    q_s = qk_a32[mp:, :A_QK_W].reshape(db, dt, A_HEADS, 128).transpose(0, 2, 1, 3).reshape(db, A_HEADS * dt, 128)
    new_page = _pad_axis(a_rows[mp:].reshape(db, dt * 2 * A_HEADS, 128), 1, 128)
    cache_rows = cache_a_kv.reshape(cache_a_kv.shape[0], -1, A_VDIM)
    y_s = diff_attention_sample(jnp.concatenate([q_s, q_s], axis=1), new_page, cache_rows, page_table, lam_rows,
                                g_subln, lam_init=lam_init)
    o_a_s = y_s.reshape(db, A_HEADS, dt, A_VDIM).transpose(0, 2, 1, 3).reshape(ms, A_V_W).astype(BF16)

    pool = cache_nsa_kv.shape[0]
    xc = jnp.transpose(cache_nsa_kv[:, :, :2], (2, 0, 3, 1, 4)).astype(BF16)
    xc = xc.reshape(2, pool * NSA_KV_HEADS * (cache_nsa_kv.shape[1] // CMP_STRIDE), CMP_STRIDE * HEAD_DIM)
    ab = jnp.stack([matmul(xc[c], _w1_ab(w1), tn=2 * w1.shape[1], name="compress_cache")
                    for c, w1 in enumerate((w_k1, w_v1))])
    ab = ab.reshape(2, pool, NSA_KV_HEADS, cache_nsa_kv.shape[1] // CMP_STRIDE, ab.shape[-1])

    def sample_q(q16):
        qq = q16[mp:].reshape(db, dt, NSA_KV_HEADS, NSA_REP, HEAD_DIM).transpose(0, 2, 3, 1, 4)
        return _pad_axis(qq, 3, 8).reshape(db, NSA_KV_HEADS, NSA_REP * 8, HEAD_DIM)

    def sample_kv(x):
        return x.reshape(db, dt, NSA_KV_HEADS, HEAD_DIM).transpose(0, 2, 1, 3)

    new_rows = jnp.stack([sample_kv(ks32[mp:]), sample_kv(proj[mp:, COL_KVB + 3 * KVB_PAIR:COL_KVB + 4 * KVB_PAIR]),
                          sample_kv(kw32[mp:]), sample_kv(proj[mp:, COL_KVB + 5 * KVB_PAIR:COL_KVB + 6 * KVB_PAIR])],
                         axis=1)
    o_cmp_s, o_sel_s, o_win_s, win_state = nsa_sample(
        sample_q(qn16), sample_q(qr16), _pad_axis(new_rows, 3, 128), jnp.transpose(cache_nsa_kv, (0, 2, 3, 4, 1)),
        ab, jnp.transpose(state_nsa_win, (0, 2, 3, 4, 1)), page_table, cmp_w, g_kb[0], past=past)

    def sample_o(o):
        oo = o.reshape(db, NSA_KV_HEADS, NSA_REP, 8, HEAD_DIM)[:, :, :, :dt]
        return oo.transpose(0, 3, 1, 2, 4).reshape(ms, NSA_Q_W).astype(BF16)

    o_b = nsa_merge(proj, COL_GATE // 128, jnp.concatenate([o_cmp, sample_o(o_cmp_s)]),
                    jnp.concatenate([o_sel, sample_o(o_sel_s)]), jnp.concatenate([o_win, sample_o(o_win_s)]))
    cat = jnp.concatenate([jnp.concatenate([o_a, o_a_s]), o_b], axis=1)

    keep = min(NSA_WINDOW, seq)
    return (cat,
            a_rows[:mp].reshape(nb, seq, 2, A_HEADS, A_VDIM), a_rows[mp:].reshape(db, dt, 2, A_HEADS, A_VDIM),
            long_rows[:mp].reshape(nb, seq, 4, NSA_KV_HEADS, HEAD_DIM),
            long_rows[mp:].reshape(db, dt, 4, NSA_KV_HEADS, HEAD_DIM),
            win_rows[:mp].reshape(nb, seq, 2, NSA_KV_HEADS, HEAD_DIM)[:, seq - keep:],
            jnp.transpose(win_state, (0, 4, 1, 2, 3)))


def _kv_transpose_body(k_ref, v_ref, o_ref):
    kv = pl.program_id(1)

    @pl.when(kv == 0)
    def _():
        o_ref[...] = jnp.transpose(k_ref[...])

    @pl.when(kv == 1)
    def _():
        o_ref[...] = jnp.transpose(v_ref[...])


def kv_time_minor(k_src, k_col, v_src, v_col, *, nb, seq, keep):
    ts = min(keep, 512)
    cw = 512
    row0 = seq - keep
    assert row0 % ts == 0 and seq % ts == 0 and C_W % cw == 0
    ncb = C_W // cw

    def src_map(col, which):
        def index(n, kv, i, j):
            on = 1 - kv if which == 0 else kv
            return ((n * (seq // ts) + row0 // ts + i) * on, (col * ncb + j) * on)
        return index

    return pl.pallas_call(
        _kv_transpose_body,
        grid=(nb, 2, keep // ts, ncb),
        in_specs=[pl.BlockSpec((ts, cw), src_map(k_col, 0)), pl.BlockSpec((ts, cw), src_map(v_col, 1))],
        out_specs=pl.BlockSpec((None, None, cw, ts), lambda n, kv, i, j: (n, kv, j, i)),
        out_shape=jax.ShapeDtypeStruct((nb, 2, C_W, keep), F32),
        compiler_params=_cparams("parallel", "arbitrary", "arbitrary", "arbitrary"),
        name="kv_time_minor",
    )(k_src, v_src)


def odd_mixer(proj, proj16, tables, nb, seq, db, dt, states, g_qc, g_kc):
    mp = nb * seq
    ms = db * dt
    outs, lses, bufs_p, bufs_s, qk32s = [], [], [], [], []
    for gi, ((window, dil), state) in enumerate(zip(C_GROUPS, states)):
        gains = jnp.stack([_head_gain(g_qc[gi], C_W), _head_gain(g_kc[gi], C_W)])
        (qk32,) = head_norm_rope(proj, gains, tables, width=C_W, col0=3 * gi, outs=((True, F32),),
                                 name=f"hnr_dil_{dil}")
        qk32s.append(qk32)
        keep = min(window, seq)
        kv_t = kv_time_minor(qk32, 1, proj, 3 * gi + 2, nb=nb, seq=seq, keep=keep)
        bufs_p.append(jnp.transpose(kv_t.reshape(nb, 2, C_HEADS, HEAD_DIM, keep), (0, 4, 1, 2, 3)))

        tok3 = lambda x: x.reshape(db, dt, C_W)
        nbuf = state.shape[1]
        o_s, lse_s, rolled = dilated_attention_sample(
            _pad_axis(tok3(qk32[mp:, :C_W]), 1, 8), _pad_axis(tok3(qk32[mp:, C_W:]), 1, 8),
            _pad_axis(tok3(proj[mp:, (3 * gi + 2) * C_W:(3 * gi + 3) * C_W]), 1, 8),
            jnp.transpose(state, (0, 2, 3, 4, 1)).reshape(db, 2, C_W, nbuf),
            window=window, dil=dil, hg=C_HEADS if nbuf <= 512 else 4)
        outs.append(o_s[:, :dt].reshape(ms, C_W).astype(BF16))
        lse_s = lse_s[:, :, 0].reshape(db, C_HEADS, 8)[:, :, :dt].transpose(0, 2, 1).reshape(ms, C_HEADS)
        lses.append(jnp.repeat(lse_s, HEAD_DIM, axis=1))
        bufs_s.append(jnp.transpose(rolled.reshape(db, 2, C_HEADS, HEAD_DIM, nbuf), (0, 4, 1, 2, 3)))
    y_p = dilated_attention_prompt_fused(qk32s, proj, nb=nb, seq=seq)
    return jnp.concatenate([y_p, dilation_merge(outs, lses)]), bufs_p, bufs_s


def odd_mixer_prompt(proj, proj16, tables, nb, seq, g_qc, g_kc):
    mp = nb * seq
    outs, lses, bufs = [], [], []
    for gi, (window, dil) in enumerate(C_GROUPS):
        gains = jnp.stack([_head_gain(g_qc[gi], C_W), _head_gain(g_kc[gi], C_W)])
        qk16, qk32 = head_norm_rope(proj, gains, tables, width=C_W, col0=3 * gi,
                                    outs=((True, BF16), (True, F32)), name=f"hnr_dil_{dil}")
        sub = seq // dil
        o, lse = dilated_attention_prompt(qk16, proj16, nb=nb, seq=seq, dil=dil, band=window // dil, gi=gi,
                                          t=min(sub, 256 if dil == 1 else 128))
        outs.append(o)
        lses.append(lse)
        kv = jnp.concatenate([qk32[:mp, C_W:], proj[:mp, (3 * gi + 2) * C_W:(3 * gi + 3) * C_W]], axis=1)
        bufs.append(kv.reshape(nb, seq, 2, C_HEADS, HEAD_DIM)[:, seq - min(window, seq):])
    return dilation_merge(outs, lses), bufs


def kernel(x_prompt, x_sample, cache_a_kv, cache_nsa_kv, state_nsa_win, state_c_w128, state_c_w512, state_c_w2048, page_table, norm0_mix, w_in0, g_qa, g_ka, lam_q1, lam_k1, lam_q2, lam_k2, g_subln, g_qb, g_kb, pe_cmp_k, w_cmp_k1, w_cmp_k2, pe_cmp_v, w_cmp_v1, w_cmp_v2, w_out0, norm0_ffn, w_ffn_gate, w_ffn_up, w_ffn_down, norm1_mix, w_in1, g_qc, g_kc, w_out1, norm1_ffn, w_router, w_moe_gate, w_moe_up, w_moe_down):
    nb, seq, d = x_prompt.shape
    db, dt, _ = x_sample.shape
    past = page_table.shape[1] * cache_a_kv.shape[1]
    mp = nb * seq
    ms = db * dt
    pos_p = jnp.arange(seq, dtype=jnp.int32)
    pos_s = past + jnp.arange(dt, dtype=jnp.int32)
    x = jnp.concatenate([x_prompt.reshape(mp, d), x_sample.reshape(ms, d)], axis=0)

    in0_w = w_in0.shape[1]
    in0_pad = -(-in0_w // 128) * 128
    w_in0_b = jnp.pad(w_in0, ((0, 0), (0, in0_pad - in0_w))).astype(BF16)
    tables = rope_tables(jnp.concatenate([jnp.tile(pos_p, nb), jnp.tile(pos_s, db)]))
    proj0, proj0_16 = matmul_dual(rmsnorm_cast(x, norm0_mix), w_in0_b, tn=in0_pad, name="in_proj0")
    lam_init = 0.8 - 0.6 * math.exp(-0.3 * 0)
    f = lambda a: a.astype(F32)
    cmp_w = (pe_cmp_k, w_cmp_k1, w_cmp_k2, pe_cmp_v, w_cmp_v1, w_cmp_v2)
    lam_rows = jnp.zeros((8, 128), F32).at[:4, :HEAD_DIM].set(jnp.stack([f(lam_q1), f(lam_k1), f(lam_q2), f(lam_k2)]))
    cat, a_kv_p, a_kv_s, nsa_kv_p, nsa_kv_s, nsa_win_p, nsa_win_s = even_mixer(
        proj0, proj0_16, tables, nb, seq, db, dt, (cache_a_kv, cache_nsa_kv, state_nsa_win, page_table),
        (g_qa, g_ka, lam_rows, lam_init, g_subln, g_qb, g_kb, cmp_w))
    x = matmul(cat, w_out0.astype(BF16), tn=d, res=x, name="out_proj0")
    act = swiglu_gate_up(rmsnorm_cast(x, norm0_ffn), w_ffn_gate.astype(BF16), w_ffn_up.astype(BF16), tn=1408)
    x = matmul(act, w_ffn_down.astype(BF16), tn=d, res=x, name="ffn_down")

    in1_w = w_in1.shape[1]
    proj1 = matmul(rmsnorm_cast(x, norm1_mix), w_in1.astype(BF16), tn=2304, name="in_proj1")
    mix, c_p, c_s = odd_mixer(proj1, None, tables, nb, seq, db, dt,
                              (state_c_w128, state_c_w512, state_c_w2048), g_qc, g_kc)
    x = matmul(mix, w_out1.astype(BF16), tn=d, res=x, name="out_proj1")
    h16, h32 = rmsnorm_cast(x, norm1_ffn, with_f32=True)
    x = _moe(h16, h32, x, w_router, w_moe_gate, w_moe_up, w_moe_down.astype(BF16))

    hp = x[:mp].reshape(nb, seq, d)
    hs = x[mp:].reshape(db, dt, d)
    return (hp, hs, a_kv_p, a_kv_s, nsa_kv_p, nsa_kv_s, nsa_win_p, nsa_win_s,
            c_p[0], c_s[0], c_p[1], c_s[1], c_p[2], c_s[2])
```

```python
import functools
import math

import jax
import jax.numpy as jnp
import numpy as np
from jax import lax
from jax.experimental import pallas as pl
from jax.experimental.pallas import tpu as pltpu

F32 = jnp.float32
BF16 = jnp.bfloat16

D_MODEL = 1024
HEAD_DIM = 64
ROT_DIM = HEAD_DIM // 4
ROPE_THETA = 500000.0
NORM_EPS = 1e-6
SCALE = HEAD_DIM ** -0.5
Q_BLOCK = 128
NEG_INF = -1e30
TINY = 1e-30
A_HEADS = 4
A_VDIM = 2 * HEAD_DIM
NSA_HEADS = 8
NSA_KV_HEADS = 2
NSA_REP = NSA_HEADS // NSA_KV_HEADS
CMP_LEN = 32
CMP_STRIDE = 16
SEL_BLOCK = 64
SEL_SHIFT = 6
SEL_TOPK = 16
SEL_Q_BLOCK = 64
NSA_WINDOW = 512
FORCE_BONUS = 1e3
C_HEADS = 16
C_GROUPS = ((128, 1), (512, 4), (2048, 16))
N_C_GROUPS = len(C_GROUPS)
A_QK_W = A_HEADS * 2 * HEAD_DIM
A_V_W = A_HEADS * A_VDIM
NSA_Q_W = NSA_HEADS * HEAD_DIM
NSA_KV_W = 6 * NSA_KV_HEADS * HEAD_DIM
NSA_GATE_W = 3 * NSA_HEADS
C_W = C_HEADS * HEAD_DIM
N_EXPERTS = 8
TOP_K = 2

VMEM_LIMIT_BYTES = 56 * 1024 * 1024
TOKEN_TILE = 512


def _cparams(*sem):
    return pltpu.CompilerParams(dimension_semantics=sem, vmem_limit_bytes=VMEM_LIMIT_BYTES)


def _rmsnorm_body(x_ref, g_ref, *o_refs):
    x = x_ref[...]
    ms = jnp.mean(x * x, axis=-1, keepdims=True)
    y = x * lax.rsqrt(ms + NORM_EPS) * g_ref[...]
    for o_ref in o_refs:
        o_ref[...] = y.astype(o_ref.dtype)


def rmsnorm_cast(x, g, *, with_f32=False):
    m, d = x.shape
    tm = TOKEN_TILE
    spec = pl.BlockSpec((tm, d), lambda i: (i, 0))
    outs = pl.pallas_call(
        _rmsnorm_body,
        grid=(m // tm,),
        in_specs=[spec, pl.BlockSpec((1, d), lambda i: (0, 0))],
        out_specs=[spec, spec] if with_f32 else [spec],
        out_shape=[jax.ShapeDtypeStruct((m, d), dt) for dt in ((BF16, F32) if with_f32 else (BF16,))],
        compiler_params=_cparams("parallel"),
        name="rmsnorm",
    )(x, g.reshape(1, d))
    return tuple(outs) if with_f32 else outs[0]


def _mm_body(x_ref, w_ref, o_ref):
    o_ref[...] = jnp.dot(x_ref[...], w_ref[...], preferred_element_type=F32).astype(o_ref.dtype)


def _mm_res_body(x_ref, w_ref, r_ref, o_ref):
    acc = jnp.dot(x_ref[...], w_ref[...], preferred_element_type=F32)
    o_ref[...] = (acc + r_ref[...]).astype(o_ref.dtype)


def matmul(x, w, *, tn, res=None, out_dtype=F32, name="matmul"):
    m, k = x.shape
    n = w.shape[1]
    tm = min(TOKEN_TILE, m)
    assert m % tm == 0 and n % tn == 0
    in_specs = [pl.BlockSpec((tm, k), lambda j, i: (i, 0)), pl.BlockSpec((k, tn), lambda j, i: (0, j))]
    args = [x, w]
    body = _mm_body
    if res is not None:
        in_specs.append(pl.BlockSpec((tm, tn), lambda j, i: (i, j)))
        args.append(res)
        body = _mm_res_body
    return pl.pallas_call(
        body,
        grid=(n // tn, m // tm),
        in_specs=in_specs,
        out_specs=pl.BlockSpec((tm, tn), lambda j, i: (i, j)),
        out_shape=jax.ShapeDtypeStruct((m, n), out_dtype),
        compiler_params=_cparams("parallel", "parallel"),
        name=name,
    )(*args)


def _gate_up_body(x_ref, wg_ref, wu_ref, o_ref):
    x = x_ref[...]
    g = jnp.dot(x, wg_ref[...], preferred_element_type=F32)
    u = jnp.dot(x, wu_ref[...], preferred_element_type=F32)
    o_ref[...] = (g * jax.nn.sigmoid(g) * u).astype(o_ref.dtype)


def swiglu_gate_up(x, wg, wu, *, tn):
    m, k = x.shape
    n = wg.shape[1]
    tm = TOKEN_TILE
    return pl.pallas_call(
        _gate_up_body,
        grid=(n // tn, m // tm),
        in_specs=[pl.BlockSpec((tm, k), lambda j, i: (i, 0)),
                  pl.BlockSpec((k, tn), lambda j, i: (0, j)),
                  pl.BlockSpec((k, tn), lambda j, i: (0, j))],
        out_specs=pl.BlockSpec((tm, tn), lambda j, i: (i, j)),
        out_shape=jax.ShapeDtypeStruct((m, n), BF16),
        compiler_params=_cparams("parallel", "parallel"),
        name="swiglu_gate_up",
    )(x, wg, wu)


def _moe_gate_up_body(te_ref, x_ref, wg_ref, wu_ref, o_ref, wg16_ref, wu16_ref):
    i = pl.program_id(1)

    @pl.when((i == 0) | (te_ref[i] != te_ref[jnp.maximum(i - 1, 0)]))
    def _():
        wg16_ref[...] = wg_ref[...].astype(BF16)
        wu16_ref[...] = wu_ref[...].astype(BF16)

    x = x_ref[...].astype(BF16)
    g = jnp.dot(x, wg16_ref[...], preferred_element_type=F32)
    u = jnp.dot(x, wu16_ref[...], preferred_element_type=F32)
    o_ref[...] = (g * jax.nn.sigmoid(g) * u).astype(o_ref.dtype)


def _moe_down_body(te_ref, a_ref, wd_ref, o_ref):
    del te_ref
    o_ref[...] = jnp.dot(a_ref[...], wd_ref[...], preferred_element_type=F32)


def moe_grouped_ffn(xs, tile_expert, wg, wu, wd, *, tf):
    p, d = xs.shape
    f = wg.shape[2]
    tm = TOKEN_TILE
    nt = p // tm
    act = pl.pallas_call(
        _moe_gate_up_body,
        grid_spec=pltpu.PrefetchScalarGridSpec(
            num_scalar_prefetch=1,
            grid=(f // tf, nt),
            in_specs=[pl.BlockSpec((tm, d), lambda j, i, te: (i, 0)),
                      pl.BlockSpec((None, d, tf), lambda j, i, te: (te[i], 0, j)),
                      pl.BlockSpec((None, d, tf), lambda j, i, te: (te[i], 0, j))],
            out_specs=pl.BlockSpec((tm, tf), lambda j, i, te: (i, j)),
            scratch_shapes=[pltpu.VMEM((d, tf), BF16), pltpu.VMEM((d, tf), BF16)],
        ),
        out_shape=jax.ShapeDtypeStruct((p, f), BF16),
        compiler_params=_cparams("arbitrary", "arbitrary"),
        name="moe_gate_up",
    )(tile_expert, xs, wg, wu)
    return pl.pallas_call(
        _moe_down_body,
        grid_spec=pltpu.PrefetchScalarGridSpec(
            num_scalar_prefetch=1,
            grid=(nt,),
            in_specs=[pl.BlockSpec((tm, f), lambda i, te: (i, 0)),
                      pl.BlockSpec((None, f, d), lambda i, te: (te[i], 0, 0))],
            out_specs=pl.BlockSpec((tm, d), lambda i, te: (i, 0)),
        ),
        out_shape=jax.ShapeDtypeStruct((p, d), F32),
        compiler_params=_cparams("arbitrary"),
        name="moe_down",
    )(tile_expert, act, wd)


def _mm2_body(x_ref, w_ref, o32_ref, o16_ref):
    acc = jnp.dot(x_ref[...], w_ref[...], preferred_element_type=F32)
    o32_ref[...] = acc
    o16_ref[...] = acc.astype(BF16)


def matmul_dual(x, w, *, tn, name):
    m, k = x.shape
    n = w.shape[1]
    tm = TOKEN_TILE
    return pl.pallas_call(
        _mm2_body,
        grid=(n // tn, m // tm),
        in_specs=[pl.BlockSpec((tm, k), lambda j, i: (i, 0)), pl.BlockSpec((k, tn), lambda j, i: (0, j))],
        out_specs=[pl.BlockSpec((tm, tn), lambda j, i: (i, j)), pl.BlockSpec((tm, tn), lambda j, i: (i, j))],
        out_shape=[jax.ShapeDtypeStruct((m, n), F32), jax.ShapeDtypeStruct((m, n), BF16)],
        compiler_params=_cparams("parallel", "parallel"),
        name=name,
    )(x, w)


def rope_tables(pos):
    half = ROT_DIM // 2
    inv_freq = ROPE_THETA ** (-jnp.arange(half, dtype=F32) / half)
    ang = pos.astype(F32)[:, None] * inv_freq[None, :]
    cos, sin = jnp.cos(ang), jnp.sin(ang)
    m = pos.shape[0]
    z_half = jnp.zeros((m, half), F32)
    z_rest = jnp.zeros((m, HEAD_DIM - ROT_DIM), F32)
    c = jnp.concatenate([cos, cos, jnp.ones((m, HEAD_DIM - ROT_DIM), F32)], axis=1)
    s1 = jnp.concatenate([z_half, sin, z_rest], axis=1)
    s2 = jnp.concatenate([-sin, z_half, z_rest], axis=1)
    return tuple(jnp.tile(a, (1, 128 // HEAD_DIM)) for a in (c, s1, s2))


def _hnr_body(x_ref, g_ref, c_ref, s1_ref, s2_ref, *o_refs, width, outs):
    tm = x_ref.shape[0]
    lo = lax.broadcasted_iota(jnp.int32, (tm, 128), 1) < HEAD_DIM
    c, s1, s2 = c_ref[...], s1_ref[...], s2_ref[...]
    for j in range(width // 128):
        sl = slice(j * 128, (j + 1) * 128)
        x = x_ref[:, sl]
        x2 = x * x
        s_lo = jnp.sum(jnp.where(lo, x2, 0.0), axis=-1, keepdims=True)
        s_hi = jnp.sum(jnp.where(lo, 0.0, x2), axis=-1, keepdims=True)
        ms = jnp.where(lo, s_lo, s_hi) * (1.0 / HEAD_DIM)
        xn = x * lax.rsqrt(ms + NORM_EPS) * g_ref[:, sl]
        xr = xn * c + pltpu.roll(xn, ROT_DIM // 2, 1) * s1 + pltpu.roll(xn, 128 - ROT_DIM // 2, 1) * s2
        for (rope, _), o_ref in zip(outs, o_refs):
            o_ref[:, sl] = (xr if rope else xn).astype(o_ref.dtype)


def head_norm_rope(x, gains, tables, *, width, col0, outs, name):
    m = x.shape[0]
    ncol = gains.shape[0]
    tm = TOKEN_TILE
    tab_spec = pl.BlockSpec((tm, 128), lambda i, j: (i, 0))
    return pl.pallas_call(
        functools.partial(_hnr_body, width=width, outs=outs),
        grid=(m // tm, ncol),
        in_specs=[pl.BlockSpec((tm, width), lambda i, j: (i, col0 + j)),
                  pl.BlockSpec((None, 1, width), lambda i, j: (j, 0, 0)),
                  tab_spec, tab_spec, tab_spec],
        out_specs=[pl.BlockSpec((tm, width), lambda i, j: (i, j)) for _ in outs],
        out_shape=[jax.ShapeDtypeStruct((m, ncol * width), dt) for _, dt in outs],
        compiler_params=_cparams("parallel", "parallel"),
        name=name,
    )(x, gains.reshape(ncol, 1, width), *tables)


def _head_gain(g, width):
    return jnp.tile(g.astype(F32), width // HEAD_DIM)


def _step_tables(nq, lookback):
    qi, ki, first, last = [], [], [], []
    for q in range(nq):
        ks = list(range(q + 1)) if lookback is None else [k for k in range(q - lookback, q + 1) if k >= 0]
        for n, k in enumerate(ks):
            qi.append(q)
            ki.append(k)
            first.append(int(n == 0))
            last.append(int(n == len(ks) - 1))
    return tuple(jnp.asarray(a, jnp.int32) for a in (qi, ki, first, last))


def _pos_mask(qi, ki, t, band):
    row = lax.broadcasted_iota(jnp.int32, (t, t), 0)
    col = lax.broadcasted_iota(jnp.int32, (t, t), 1)
    d = (qi - ki) * t + row - col
    mask = d >= 0
    if band is not None:
        mask = mask & (d <= band)
    return mask


def _nt_dot(a, b):
    return lax.dot_general(a, b, (((1,), (1,)), ((), ())), preferred_element_type=F32)


def _online_update(sc, mask, v, m_ref, l_ref, acc_ref, idx):
    sc = jnp.where(mask, sc, NEG_INF)
    m_old = m_ref[idx]
    m_new = jnp.maximum(m_old, jnp.max(sc, axis=-1, keepdims=True))
    alpha = jnp.exp(m_old - m_new)
    p = jnp.exp(sc - m_new)
    l_ref[idx] = alpha * l_ref[idx] + jnp.sum(p, axis=-1, keepdims=True)
    acc_ref[idx] = alpha * acc_ref[idx] + jnp.dot(p.astype(BF16), v, preferred_element_type=F32)
    m_ref[idx] = m_new


def _init_state(m_ref, l_ref, acc_ref):
    m_ref[...] = jnp.full(m_ref.shape, NEG_INF, F32)
    l_ref[...] = jnp.zeros(l_ref.shape, F32)
    acc_ref[...] = jnp.zeros(acc_ref.shape, F32)


def _split_pair(q_ref, qs_ref, hb, lo):
    q = q_ref[:, hb * 128:(hb + 1) * 128].astype(F32) * SCALE
    qs_ref[2 * hb] = jnp.where(lo, q, 0.0).astype(BF16)
    qs_ref[2 * hb + 1] = jnp.where(lo, 0.0, q).astype(BF16)


def _gqa_query(q_ref, g, r, lo):
    col = g * NSA_REP + r
    blk = q_ref[:, (col // 2) * 128:(col // 2 + 1) * 128].astype(F32) * SCALE
    h = jnp.where(lo if col % 2 == 0 else jnp.logical_not(lo), blk, 0.0)
    d = h + pltpu.roll(h, HEAD_DIM, 1)
    return jnp.where(lo if g == 0 else jnp.logical_not(lo), d, 0.0).astype(BF16)


def _gqa_store(o_ref, outs, g, lo):
    keep = lo if g == 0 else jnp.logical_not(lo)
    dup = []
    for o in outs:
        z = jnp.where(keep, o, 0.0)
        dup.append(z + pltpu.roll(z, HEAD_DIM, 1))
    for pr in range(NSA_REP // 2):
        blk = g * (NSA_REP // 2) + pr
        o_ref[:, blk * 128:(blk + 1) * 128] = jnp.where(lo, dup[2 * pr], dup[2 * pr + 1]).astype(o_ref.dtype)


def _diff_body(qi_ref, ki_ref, fi_ref, la_ref, q_ref, k_ref, v_ref, lam_ref, gs_ref, o_ref,
               qs_ref, m_ref, l_ref, acc_ref, *, t, lam_init):
    s = pl.program_id(1)
    lo = lax.broadcasted_iota(jnp.int32, (t, 128), 1) < HEAD_DIM

    @pl.when(fi_ref[s] == 1)
    def _():
        for h in range(A_HEADS):
            _split_pair(q_ref, qs_ref, h, lo)
        _init_state(m_ref, l_ref, acc_ref)

    mask = _pos_mask(qi_ref[s], ki_ref[s], t, None)
    for h in range(A_HEADS):
        k = k_ref[:, h * 128:(h + 1) * 128]
        v = v_ref[:, h * 128:(h + 1) * 128]
        for var in range(2):
            _online_update(_nt_dot(qs_ref[2 * h + var], k), mask, v, m_ref, l_ref, acc_ref, 2 * h + var)

    @pl.when(la_ref[s] == 1)
    def _():
        lv = lam_ref[...]
        a = jnp.sum(lv[0:1] * lv[1:2], axis=-1, keepdims=True)
        b = jnp.sum(lv[2:3] * lv[3:4], axis=-1, keepdims=True)
        lam = jnp.exp(a) - jnp.exp(b) + lam_init
        for h in range(A_HEADS):
            o1 = acc_ref[2 * h] / jnp.maximum(l_ref[2 * h], TINY)
            o2 = acc_ref[2 * h + 1] / jnp.maximum(l_ref[2 * h + 1], TINY)
            o = o1 - lam * o2
            ms = jnp.mean(o * o, axis=-1, keepdims=True)
            y = o * lax.rsqrt(ms + NORM_EPS) * gs_ref[...] * (1.0 - lam_init)
            o_ref[:, h * 128:(h + 1) * 128] = y.astype(o_ref.dtype)


def diff_attention_prompt(qk16, v16, lam_rows, g_subln, *, nb, seq, lam_init, t=512):
    nq = seq // t
    tabs = _step_tables(nq, None)
    w = A_V_W
    qmap = lambda n, s, qi, ki, fi, la: (n * nq + qi[s], 0)
    kmap = lambda n, s, qi, ki, fi, la: (n * nq + ki[s], 1)
    vmap = lambda n, s, qi, ki, fi, la: (n * nq + ki[s], 2)
    const = lambda n, s, qi, ki, fi, la: (0, 0)
    return pl.pallas_call(
        functools.partial(_diff_body, t=t, lam_init=lam_init),
        grid_spec=pltpu.PrefetchScalarGridSpec(
            num_scalar_prefetch=4,
            grid=(nb, int(tabs[0].shape[0])),
            in_specs=[pl.BlockSpec((t, w), qmap), pl.BlockSpec((t, w), kmap), pl.BlockSpec((t, w), vmap),
                      pl.BlockSpec((8, 128), const), pl.BlockSpec((1, 128), const)],
            out_specs=pl.BlockSpec((t, w), qmap),
            scratch_shapes=[pltpu.VMEM((2 * A_HEADS, t, 128), BF16), pltpu.VMEM((2 * A_HEADS, t, 1), F32),
                            pltpu.VMEM((2 * A_HEADS, t, 1), F32), pltpu.VMEM((2 * A_HEADS, t, 128), F32)],
        ),
        out_shape=jax.ShapeDtypeStruct((nb * seq, w), BF16),
        compiler_params=_cparams("parallel", "arbitrary"),
        name="diff_attn_prompt",
    )(*tabs, qk16, qk16, v16, lam_rows, g_subln.reshape(1, A_VDIM).astype(F32))


def _dil_body(qi_ref, ki_ref, fi_ref, la_ref, q_ref, k_ref, v_ref, o_ref, lse_ref,
              qs_ref, m_ref, l_ref, acc_ref, *, t, band):
    s = pl.program_id(1)
    lo = lax.broadcasted_iota(jnp.int32, (t, 128), 1) < HEAD_DIM
    nhb = C_HEADS // 2

    @pl.when(fi_ref[s] == 1)
    def _():
        for hb in range(nhb):
            _split_pair(q_ref, qs_ref, hb, lo)
        _init_state(m_ref, l_ref, acc_ref)

    mask = _pos_mask(qi_ref[s], ki_ref[s], t, band)
    for hb in range(nhb):
        k = k_ref[:, hb * 128:(hb + 1) * 128]
        v = v_ref[:, hb * 128:(hb + 1) * 128]
        for var in range(2):
            _online_update(_nt_dot(qs_ref[2 * hb + var], k), mask, v, m_ref, l_ref, acc_ref, 2 * hb + var)

    @pl.when(la_ref[s] == 1)
    def _():
        for hb in range(nhb):
            l0 = jnp.maximum(l_ref[2 * hb], TINY)
            l1 = jnp.maximum(l_ref[2 * hb + 1], TINY)
            o = jnp.where(lo, acc_ref[2 * hb] / l0, acc_ref[2 * hb + 1] / l1)
            lse = jnp.where(lo, m_ref[2 * hb] + jnp.log(l0), m_ref[2 * hb + 1] + jnp.log(l1))
            o_ref[:, hb * 128:(hb + 1) * 128] = o.astype(o_ref.dtype)
            lse_ref[:, hb * 128:(hb + 1) * 128] = lse


def dilated_attention_prompt(qk16, v16, *, nb, seq, dil, band, gi, t):
    mp = nb * seq
    sub = seq // dil
    nq = sub // t
    tabs = _step_tables(nq, -(-band // t))
    w = C_W
    qk = qk16.reshape(qk16.shape[0] // dil, dil * 2 * w)
    vv = v16.reshape(v16.shape[0] // dil, dil * v16.shape[1])
    vcols = v16.shape[1] // w
    row = lambda b, x: (b // dil) * nq + x

    def qmap(b, s, qi, ki, fi, la):
        return (row(b, qi[s]), (b % dil) * 2)

    def kmap(b, s, qi, ki, fi, la):
        return (row(b, ki[s]), (b % dil) * 2 + 1)

    def vmap(b, s, qi, ki, fi, la):
        return (row(b, ki[s]), (b % dil) * vcols + gi * 3 + 2)

    def omap(b, s, qi, ki, fi, la):
        return (row(b, qi[s]), b % dil)

    nst = 2 * (C_HEADS // 2)
    o, lse = pl.pallas_call(
        functools.partial(_dil_body, t=t, band=band),
        grid_spec=pltpu.PrefetchScalarGridSpec(
            num_scalar_prefetch=4,
            grid=(nb * dil, int(tabs[0].shape[0])),
            in_specs=[pl.BlockSpec((t, w), qmap), pl.BlockSpec((t, w), kmap), pl.BlockSpec((t, w), vmap)],
            out_specs=[pl.BlockSpec((t, w), omap), pl.BlockSpec((t, w), omap)],
            scratch_shapes=[pltpu.VMEM((nst, t, 128), BF16), pltpu.VMEM((nst, t, 1), F32),
                            pltpu.VMEM((nst, t, 1), F32), pltpu.VMEM((nst, t, 128), F32)],
        ),
        out_shape=[jax.ShapeDtypeStruct((mp // dil, dil * w), BF16), jax.ShapeDtypeStruct((mp // dil, dil * w), F32)],
        compiler_params=_cparams("parallel", "arbitrary"),
        name=f"dilated_attn_prompt_{dil}",
    )(*tabs, qk, qk, vv)
    return o.reshape(mp, w), lse.reshape(mp, w)


def _dil_fused_body(*refs, seq):
    n_g = len(C_GROUPS)
    in_refs = refs[:3 * n_g]
    o_ref = refs[3 * n_g]
    og_refs = refs[3 * n_g + 1:3 * n_g + 1 + n_g]
    lg_refs = refs[3 * n_g + 1 + n_g:]
    for gi, (window, dil) in enumerate(C_GROUPS):
        q_ref, k_ref, v_ref = in_refs[3 * gi:3 * gi + 3]
        og_ref, lg_ref = og_refs[gi], lg_refs[gi]
        sub = seq // dil
        band = window // dil
        t = min(sub, 256 if dil == 1 else 128)
        nq = sub // t
        look = -(-band // t)
        lo = lax.broadcasted_iota(jnp.int32, (t, 128), 1) < HEAD_DIM

        def rows(rho, tile, t=t, dil=dil):
            return pl.ds(rho + dil * tile * t, t, stride=dil) if dil > 1 else pl.ds(tile * t, t)

        def residue(rho, carry, q_ref=q_ref, k_ref=k_ref, v_ref=v_ref, og_ref=og_ref, lg_ref=lg_ref,
                    t=t, nq=nq, look=look, band=band, lo=lo, rows=rows):
            for qi in range(nq):
                q = q_ref[rows(rho, qi), :] * SCALE
                qs = (jnp.where(lo, q, 0.0).astype(BF16), jnp.where(lo, 0.0, q).astype(BF16))
                m = [jnp.full((t, 1), NEG_INF, F32)] * 2
                l = [jnp.zeros((t, 1), F32)] * 2
                acc = [jnp.zeros((t, 128), F32)] * 2
                for ki in range(max(0, qi - look), qi + 1):
                    k = k_ref[rows(rho, ki), :].astype(BF16)
                    v = v_ref[rows(rho, ki), :].astype(BF16)
                    mask = _pos_mask(qi, ki, t, band)
                    for var in range(2):
                        p, alpha, m[var], l[var] = _softmax_step(_nt_dot(qs[var], k), mask, m[var], l[var])
                        acc[var] = alpha * acc[var] + jnp.dot(p.astype(BF16), v, preferred_element_type=F32)
                l = [jnp.maximum(x, TINY) for x in l]
                og_ref[rows(rho, qi), :] = jnp.where(lo, acc[0] / l[0], acc[1] / l[1])
                lg_ref[rows(rho, qi), :] = jnp.where(lo, m[0] + jnp.log(l[0]), m[1] + jnp.log(l[1]))
            return carry

        if dil == 1:
            residue(0, 0)
        else:
            lax.fori_loop(0, dil, residue, 0, unroll=2)

    chunk = 256
    for c in range(seq // chunk):
        sl = pl.ds(c * chunk, chunk)
        ls = [r[sl, :] for r in lg_refs]
        mx = functools.reduce(jnp.maximum, ls)
        es = [jnp.exp(x - mx) for x in ls]
        den = functools.reduce(lambda a, b: a + b, es)
        acc = functools.reduce(lambda a, b: a + b, [(e / den) * r[sl, :] for e, r in zip(es, og_refs)])
        o_ref[sl, :] = acc.astype(o_ref.dtype)


def dilated_attention_prompt_fused(qk32s, proj32, *, nb, seq):
    n_g = len(C_GROUPS)
    nhb = C_W // 128
    in_specs, args = [], []
    for gi in range(n_g):
        in_specs += [pl.BlockSpec((seq, 128), lambda n, hb: (n, hb)),
                     pl.BlockSpec((seq, 128), lambda n, hb: (n, nhb + hb)),
                     pl.BlockSpec((seq, 128), functools.partial(lambda n, hb, gi: (n, (3 * gi + 2) * nhb + hb), gi=gi))]
        args += [qk32s[gi], qk32s[gi], proj32]
    return pl.pallas_call(
        functools.partial(_dil_fused_body, seq=seq),
        grid=(nb, nhb),
        in_specs=in_specs,
        out_specs=pl.BlockSpec((seq, 128), lambda n, hb: (n, hb)),
        out_shape=jax.ShapeDtypeStruct((nb * seq, C_W), BF16),
        scratch_shapes=[pltpu.VMEM((seq, 128), F32) for _ in range(2 * n_g)],
        compiler_params=_cparams("parallel", "parallel"),
        name="dilated_attn_prompt",
    )(*args)


def _gqa_body(qi_ref, ki_ref, fi_ref, la_ref, q_ref, k_ref, v_ref, *rest, t, band, use_sel):
    if use_sel:
        sel_ref, o_ref, qs_ref, m_ref, l_ref, acc_ref = rest
    else:
        o_ref, qs_ref, m_ref, l_ref, acc_ref = rest
    s = pl.program_id(1)
    lo = lax.broadcasted_iota(jnp.int32, (t, 128), 1) < HEAD_DIM

    @pl.when(fi_ref[s] == 1)
    def _():
        for g in range(NSA_KV_HEADS):
            for r in range(NSA_REP):
                qs_ref[g * NSA_REP + r] = _gqa_query(q_ref, g, r, lo)
        _init_state(m_ref, l_ref, acc_ref)

    ki = ki_ref[s]
    mask = _pos_mask(qi_ref[s], ki, t, band)
    k = k_ref[...]
    v = v_ref[...]
    if use_sel:
        blk_row = lax.broadcasted_iota(jnp.int32, (128, t), 0)
        blk_col = jnp.right_shift(ki * t + lax.broadcasted_iota(jnp.int32, (128, t), 1), SEL_SHIFT)
        expand = jnp.where(blk_row == blk_col, 1.0, 0.0).astype(BF16)
    for g in range(NSA_KV_HEADS):
        mg = mask
        if use_sel:
            mg = mask & (jnp.dot(sel_ref[g], expand, preferred_element_type=F32) > 0.5)
        for r in range(NSA_REP):
            i = g * NSA_REP + r
            _online_update(_nt_dot(qs_ref[i], k), mg, v, m_ref, l_ref, acc_ref, i)

    @pl.when(la_ref[s] == 1)
    def _():
        for g in range(NSA_KV_HEADS):
            outs = [acc_ref[g * NSA_REP + r] / jnp.maximum(l_ref[g * NSA_REP + r], TINY) for r in range(NSA_REP)]
            _gqa_store(o_ref, outs, g, lo)


def nsa_branch_prompt(q16, k16, kcol, v16, vcol, sel, *, nb, seq, band, t=512, name):
    nq = seq // t
    tabs = _step_tables(nq, None if band is None else -(-band // t))
    qmap = lambda n, s, qi, ki, fi, la: (n * nq + qi[s], 0)
    kmap = lambda n, s, qi, ki, fi, la: (n * nq + ki[s], kcol)
    vmap = lambda n, s, qi, ki, fi, la: (n * nq + ki[s], vcol)
    in_specs = [pl.BlockSpec((t, NSA_Q_W), qmap), pl.BlockSpec((t, 128), kmap), pl.BlockSpec((t, 128), vmap)]
    args = [q16, k16, v16]
    if sel is not None:
        in_specs.append(pl.BlockSpec((NSA_KV_HEADS, t, 128), lambda n, s, qi, ki, fi, la: (0, n * nq + qi[s], 0)))
        args.append(sel)
    nst = NSA_HEADS
    return pl.pallas_call(
        functools.partial(_gqa_body, t=t, band=band, use_sel=sel is not None),
        grid_spec=pltpu.PrefetchScalarGridSpec(
            num_scalar_prefetch=4,
            grid=(nb, int(tabs[0].shape[0])),
            in_specs=in_specs,
            out_specs=pl.BlockSpec((t, NSA_Q_W), qmap),
            scratch_shapes=[pltpu.VMEM((nst, t, 128), BF16), pltpu.VMEM((nst, t, 1), F32),
                            pltpu.VMEM((nst, t, 1), F32), pltpu.VMEM((nst, t, 128), F32)],
        ),
        out_shape=jax.ShapeDtypeStruct((nb * seq, NSA_Q_W), BF16),
        compiler_params=_cparams("parallel", "arbitrary"),
        name=name,
    )(*tabs, *args)


def _cmp_finish_body(ab_ref, pe_ref, w1_ref, w2_ref, g_ref, o_ref, *, hid, norm):
    ab = ab_ref[...]
    pe_term = jnp.dot(pe_ref[...], w1_ref[...], preferred_element_type=F32)[0:1]
    h = ab[:, :hid] + pltpu.roll(ab[:, hid:], ab.shape[0] - 1, 0) + pe_term
    act = (h * jax.nn.sigmoid(h)).astype(BF16)
    y = jnp.dot(act, w2_ref[...], preferred_element_type=F32)
    if norm:
        y = y * lax.rsqrt(jnp.mean(y * y, axis=-1, keepdims=True) + NORM_EPS) * g_ref[...]
    o_ref[...] = y.astype(o_ref.dtype)


def _w1_ab(w1):
    half = w1.shape[0] // 2
    return jnp.concatenate([w1[:half], w1[half:]], axis=1).astype(BF16)


def compress_blocks(x_chunks, pe, w1, w2, gain):
    b, nchunk, cw = x_chunks.shape
    hid = w1.shape[1]
    ab = matmul(x_chunks.reshape(b * nchunk, cw), _w1_ab(w1), tn=2 * hid, name="compress_in")
    pe_rows = jnp.zeros((8, 2 * cw), BF16).at[0].set(pe.reshape(-1).astype(BF16))
    g = jnp.ones((1, HEAD_DIM), F32) if gain is None else gain.reshape(1, HEAD_DIM).astype(F32)
    const = lambda i: (0, 0)
    return pl.pallas_call(
        functools.partial(_cmp_finish_body, hid=hid, norm=gain is not None),
        grid=(b,),
        in_specs=[pl.BlockSpec((nchunk, 2 * hid), lambda i: (i, 0)), pl.BlockSpec((8, 2 * cw), const),
                  pl.BlockSpec((2 * cw, hid), const), pl.BlockSpec((hid, HEAD_DIM), const),
                  pl.BlockSpec((1, HEAD_DIM), const)],
        out_specs=pl.BlockSpec((None, nchunk, HEAD_DIM), lambda i: (i, 0, 0)),
        out_shape=jax.ShapeDtypeStruct((b, nchunk, HEAD_DIM), BF16),
        compiler_params=_cparams("parallel"),
        name="compress_finish",
    )(ab, pe_rows, w1.astype(BF16), w2.astype(BF16), g)


def _overlap_matrix(n_cmp, n_sel):
    c0 = np.arange(128)[:, None] * CMP_STRIDE
    s0 = np.arange(128)[None, :] * SEL_BLOCK
    ov = np.maximum(np.minimum(c0 + CMP_LEN, s0 + SEL_BLOCK) - np.maximum(c0, s0), 0) / CMP_LEN
    ov = ov * (np.arange(128)[:, None] < n_cmp) * (np.arange(128)[None, :] < n_sel)
    return jnp.asarray(ov, BF16)


def _cmp_body(q_ref, kc_ref, vc_ref, ov_ref, o_ref, sel_ref, *, t, pos0, n_cmp, n_sel):
    i = pl.program_id(1)
    lane = lax.broadcasted_iota(jnp.int32, (t, 128), 1)
    qpos = pos0 + i * t + lax.broadcasted_iota(jnp.int32, (t, 128), 0)
    lo = lane < HEAD_DIM
    vis = (lane * CMP_STRIDE + CMP_LEN - 1 <= qpos) & (lane < n_cmp)
    kc, vc, ov = kc_ref[...], vc_ref[...], ov_ref[...]
    cur = jnp.right_shift(qpos, SEL_SHIFT)
    valid = (lane <= cur) & (lane < n_sel)
    forced = (lane == 0) | (lane == cur) | (lane == cur - 1)
    for g in range(NSA_KV_HEADS):
        imp = jnp.zeros((t, 128), F32)
        outs = []
        for r in range(NSA_REP):
            sc = jnp.where(vis, _nt_dot(_gqa_query(q_ref, g, r, lo), kc), NEG_INF)
            m = jnp.max(sc, axis=-1, keepdims=True)
            e = jnp.where(vis, jnp.exp(sc - m), 0.0)
            p = (e / jnp.maximum(jnp.sum(e, axis=-1, keepdims=True), TINY)).astype(BF16)
            outs.append(jnp.dot(p, vc, preferred_element_type=F32))
            imp = imp + jnp.dot(p, ov, preferred_element_type=F32)
        _gqa_store(o_ref, outs, g, lo)
        score = jnp.where(valid, imp + jnp.where(forced, FORCE_BONUS, 0.0), NEG_INF)
        rank = jnp.zeros((t, 128), F32)
        for kk in range(n_sel):
            sk = score[:, kk:kk + 1]
            rank = rank + jnp.where((sk > score) | ((sk == score) & (lane > kk)), 1.0, 0.0)
        sel_ref[g] = jnp.where((rank < SEL_TOPK) & valid, 1.0, 0.0).astype(sel_ref.dtype)


def nsa_compressed_prompt(qn16, k_cmp, v_cmp, *, nb, seq, t=512):
    n_cmp = (seq - CMP_LEN) // CMP_STRIDE + 1
    n_sel = -(-seq // SEL_BLOCK)
    nq = seq // t
    qmap = lambda n, i: (n * nq + i, 0)
    cmap = lambda n, i: (n, 0, 0)
    return pl.pallas_call(
        functools.partial(_cmp_body, t=t, pos0=0, n_cmp=n_cmp, n_sel=n_sel),
        grid=(nb, nq),
        in_specs=[pl.BlockSpec((t, NSA_Q_W), qmap), pl.BlockSpec((None, 128, 128), cmap),
                  pl.BlockSpec((None, 128, 128), cmap), pl.BlockSpec((128, 128), lambda n, i: (0, 0))],
        out_specs=[pl.BlockSpec((t, NSA_Q_W), qmap),
                   pl.BlockSpec((NSA_KV_HEADS, t, 128), lambda n, i: (0, n * nq + i, 0))],
        out_shape=[jax.ShapeDtypeStruct((nb * seq, NSA_Q_W), BF16),
                   jax.ShapeDtypeStruct((NSA_KV_HEADS, nb * seq, 128), BF16)],
        compiler_params=_cparams("parallel", "parallel"),
        name="nsa_cmp_select_prompt",
    )(qn16, k_cmp, v_cmp, _overlap_matrix(n_cmp, n_sel))


def _gate_expand_matrices():
    lane = np.arange(128)[:, None]
    col = np.arange(NSA_Q_W)[None, :]
    return jnp.asarray(np.stack([(lane < NSA_GATE_W) & (lane % 3 == br) & (lane // 3 == col // HEAD_DIM)
                                 for br in range(3)]), BF16)


def _nsa_merge_body(gb_ref, e_ref, oc_ref, os_ref, ow_ref, o_ref):
    gates = jax.nn.sigmoid(gb_ref[...])
    hi = gates.astype(BF16)
    lo = (gates - hi.astype(F32)).astype(BF16)
    acc = jnp.zeros(o_ref.shape, F32)
    for br, b_ref in enumerate((oc_ref, os_ref, ow_ref)):
        w = jnp.dot(hi, e_ref[br], preferred_element_type=F32) + jnp.dot(lo, e_ref[br], preferred_element_type=F32)
        acc = acc + w * b_ref[...].astype(F32)
    o_ref[...] = acc.astype(o_ref.dtype)


def nsa_merge(proj32, gate_col, o_cmp, o_sel, o_win):
    m = o_cmp.shape[0]
    tm = TOKEN_TILE
    spec = pl.BlockSpec((tm, NSA_Q_W), lambda i: (i, 0))
    return pl.pallas_call(
        _nsa_merge_body,
        grid=(m // tm,),
        in_specs=[pl.BlockSpec((tm, 128), lambda i: (i, gate_col)),
                  pl.BlockSpec((3, 128, NSA_Q_W), lambda i: (0, 0, 0)), spec, spec, spec],
        out_specs=spec,
        out_shape=jax.ShapeDtypeStruct((m, NSA_Q_W), BF16),
        compiler_params=_cparams("parallel"),
        name="nsa_merge",
    )(proj32, _gate_expand_matrices(), o_cmp, o_sel, o_win)


def _dil_merge_body(o0, o1, o2, l0, l1, l2, o_ref):
    ls = [l0[...], l1[...], l2[...]]
    m = jnp.maximum(jnp.maximum(ls[0], ls[1]), ls[2])
    es = [jnp.exp(x - m) for x in ls]
    den = es[0] + es[1] + es[2]
    acc = sum((e / den) * o[...].astype(F32) for e, o in zip(es, (o0, o1, o2)))
    o_ref[...] = acc.astype(o_ref.dtype)


def dilation_merge(outs, lses):
    m, w = outs[0].shape
    tm = TOKEN_TILE
    spec = pl.BlockSpec((tm, w), lambda i: (i, 0))
    return pl.pallas_call(
        _dil_merge_body,
        grid=(m // tm,),
        in_specs=[spec] * 6,
        out_specs=spec,
        out_shape=jax.ShapeDtypeStruct((m, w), BF16),
        compiler_params=_cparams("parallel"),
        name="dilation_merge",
    )(*outs, *lses)


def _softmax_step(sc, mask, m, l):
    sc = jnp.where(mask, sc, NEG_INF)
    m_new = jnp.maximum(m, jnp.max(sc, axis=-1, keepdims=True))
    alpha = jnp.exp(m - m_new)
    p = jnp.where(mask, jnp.exp(sc - m_new), 0.0)
    return p, alpha, m_new, alpha * l + jnp.sum(p, axis=-1, keepdims=True)


def _diff_sample_body(pt_ref, *refs, npages, lam_init):
    del pt_ref
    page_refs = refs[:npages]
    q_ref, new_ref, lam_ref, gs_ref, o_ref = refs[npages:]
    rows = 2 * A_HEADS * 4
    ri = lax.broadcasted_iota(jnp.int32, (rows, 128), 0)
    lane = lax.broadcasted_iota(jnp.int32, (rows, 128), 1)
    first_variant = ri < rows // 2
    qs = jnp.where(first_variant == (lane < HEAD_DIM), q_ref[...] * SCALE, 0.0).astype(BF16)
    ncols = page_refs[0].shape[0]
    col = lax.broadcasted_iota(jnp.int32, (rows, ncols), 1)
    head = jnp.bitwise_and(jnp.right_shift(lax.broadcasted_iota(jnp.int32, (rows, ncols), 0), 2), A_HEADS - 1)
    page_mask = jnp.bitwise_and(col, 2 * A_HEADS - 1) == head
    tok = jnp.bitwise_and(ri, 3)
    new_mask = ((jnp.bitwise_and(lane, 2 * A_HEADS - 1) == jnp.bitwise_and(jnp.right_shift(ri, 2), A_HEADS - 1))
                & (jnp.right_shift(lane, 3) <= tok) & (lane < 4 * 2 * A_HEADS))
    pages = [pr[...].astype(BF16) for pr in page_refs] + [new_ref[...].astype(BF16)]
    masks = [page_mask] * npages + [new_mask]
    scs = [jnp.where(mk, _nt_dot(qs, pg), NEG_INF) for pg, mk in zip(pages, masks)]
    m = functools.reduce(jnp.maximum, [jnp.max(s, axis=-1, keepdims=True) for s in scs])
    ps = [jnp.where(mk, jnp.exp(s - m), 0.0) for s, mk in zip(scs, masks)]
    l = functools.reduce(lambda a, b: a + b, [jnp.sum(p, axis=-1, keepdims=True) for p in ps])
    acc = functools.reduce(lambda a, b: a + b, [
        jnp.dot(pltpu.roll(p, A_HEADS, 1).astype(BF16), pg, preferred_element_type=F32) for p, pg in zip(ps, pages)])

    lv = lam_ref[...]
    a = jnp.sum(lv[0:1] * lv[1:2], axis=-1, keepdims=True)
    b = jnp.sum(lv[2:3] * lv[3:4], axis=-1, keepdims=True)
    lam = jnp.exp(a) - jnp.exp(b) + lam_init
    o = acc / jnp.maximum(l, TINY)
    o = o[:rows // 2] - lam * o[rows // 2:]
    y = o * lax.rsqrt(jnp.mean(o * o, axis=-1, keepdims=True) + NORM_EPS) * gs_ref[...] * (1.0 - lam_init)
    o_ref[...] = y.astype(o_ref.dtype)


def diff_attention_sample(q_rows, new_page, cache_rows, page_table, lam_rows, g_subln, *, lam_init):
    db = q_rows.shape[0]
    npages = page_table.shape[1]
    prow = cache_rows.shape[1]
    page_specs = [pl.BlockSpec((None, prow, 128), functools.partial(lambda b, pt, p: (pt[b, p], 0, 0), p=p))
                  for p in range(npages)]
    per_b = lambda b, pt: (b, 0, 0)
    const = lambda b, pt: (0, 0)
    return pl.pallas_call(
        functools.partial(_diff_sample_body, npages=npages, lam_init=lam_init),
        grid_spec=pltpu.PrefetchScalarGridSpec(
            num_scalar_prefetch=1,
            grid=(db,),
            in_specs=page_specs + [pl.BlockSpec((None, 32, 128), per_b), pl.BlockSpec((None, 128, 128), per_b),
                                   pl.BlockSpec((8, 128), const), pl.BlockSpec((1, 128), const)],
            out_specs=pl.BlockSpec((None, 16, 128), per_b),
        ),
        out_shape=jax.ShapeDtypeStruct((db, 16, 128), F32),
        compiler_params=_cparams("parallel"),
        name="diff_attn_sample",
    )(page_table, *([cache_rows] * npages), q_rows, new_page, lam_rows, g_subln.reshape(1, A_VDIM).astype(F32))


def _place_new_columns(rolled, new_rows, t):
    sq = jnp.concatenate([new_rows, jnp.zeros_like(new_rows)], axis=1)
    new_t = pltpu.roll(jnp.transpose(sq)[:HEAD_DIM], 124, 1)
    if t > 128:
        new_t = jnp.concatenate([jnp.zeros((HEAD_DIM, t - 128), F32), new_t], axis=1)
    lane = lax.broadcasted_iota(jnp.int32, (HEAD_DIM, t), 1)
    return jnp.where(lane >= t - 4, new_t, rolled)


def _nsa_sample_body(pt_ref, *refs, npages, past, n_cmp, n_sel, hid):
    del pt_ref
    cn_refs = refs[:npages]
    ab_refs = refs[npages:2 * npages]
    (qn_ref, qr_ref, new_ref, sw_ref, pe_ref, w1k_ref, w1v_ref, w2k_ref, w2v_ref, gk_ref, ov_ref,
     oc_ref, os_ref, ow_ref, wout_ref) = refs[2 * npages:]
    rows = NSA_REP * 8
    nwin = sw_ref.shape[-1]
    lane = lax.broadcasted_iota(jnp.int32, (rows, 128), 1)
    tok = jnp.bitwise_and(lax.broadcasted_iota(jnp.int32, (rows, 128), 0), 7)
    lane8 = lane[:8]
    tok8 = tok[:8]
    vis = (lane * CMP_STRIDE + CMP_LEN - 1 <= past + tok) & (lane < n_cmp)
    cur = jnp.right_shift(past + tok8, SEL_SHIFT)
    valid = (lane8 <= cur) & (lane8 < n_sel)
    forced = (lane8 == 0) | (lane8 == cur) | (lane8 == cur - 1)
    new_mask = (lane <= tok) & (lane < 4)
    ov = ov_ref[...]
    pe_terms = [jnp.dot(pe_ref[c], w_ref[...], preferred_element_type=F32)[0:1]
                for c, w_ref in enumerate((w1k_ref, w1v_ref))]

    for g in range(NSA_KV_HEADS):
        cmp = []
        for c, w2_ref in enumerate((w2k_ref, w2v_ref)):
            a = jnp.concatenate([r[c, g, :, :hid] for r in ab_refs], axis=0)
            bb = jnp.concatenate([r[c, g, :, hid:] for r in ab_refs], axis=0)
            h = a + pltpu.roll(bb, a.shape[0] - 1, 0) + pe_terms[c]
            y = jnp.dot((h * jax.nn.sigmoid(h)).astype(BF16), w2_ref[...], preferred_element_type=F32)
            if c == 0:
                y = y * lax.rsqrt(jnp.mean(y * y, axis=-1, keepdims=True) + NORM_EPS) * gk_ref[...]
            cmp.append(y.astype(BF16))
        qn = (qn_ref[g].astype(F32) * SCALE).astype(BF16)
        qr = (qr_ref[g].astype(F32) * SCALE).astype(BF16)

        sc = jnp.where(vis, _nt_dot(qn, cmp[0]), NEG_INF)
        mx = jnp.max(sc, axis=-1, keepdims=True)
        e = jnp.where(vis, jnp.exp(sc - mx), 0.0)
        p = (e / jnp.maximum(jnp.sum(e, axis=-1, keepdims=True), TINY)).astype(BF16)
        oc_ref[g] = jnp.dot(p, cmp[1], preferred_element_type=F32)
        imp_r = jnp.dot(p, ov, preferred_element_type=F32)
        imp = imp_r[0:8] + imp_r[8:16] + imp_r[16:24] + imp_r[24:32]
        score = jnp.where(valid, imp + jnp.where(forced, FORCE_BONUS, 0.0), NEG_INF)
        rank = jnp.zeros((8, 128), F32)
        for kk in range(n_sel):
            sk = score[:, kk:kk + 1]
            rank = rank + jnp.where((sk > score) | ((sk == score) & (lane8 > kk)), 1.0, 0.0)
        sel = (rank < SEL_TOPK) & valid

        kt_all = jnp.concatenate([r[0, g] for r in cn_refs], axis=1).astype(BF16)
        vt_all = jnp.concatenate([r[1, g] for r in cn_refs], axis=1).astype(BF16)
        sel_f = jnp.where(sel, 1.0, 0.0)
        ncached = kt_all.shape[1]
        blk_of_col = jnp.right_shift(lax.broadcasted_iota(jnp.int32, (128, ncached), 1), SEL_SHIFT)
        expand = jnp.where(lax.broadcasted_iota(jnp.int32, (128, ncached), 0) == blk_of_col, 1.0, 0.0).astype(BF16)
        flags = jnp.dot(sel_f.astype(BF16), expand, preferred_element_type=F32)
        mask = jnp.concatenate([flags] * NSA_REP, axis=0) > 0.5
        last_blk = past // SEL_BLOCK
        flag_new = sel_f[:, last_blk:last_blk + 1] + jnp.zeros((8, 128), F32)
        mask_new = new_mask & (jnp.concatenate([flag_new] * NSA_REP, axis=0) > 0.5)
        s1 = jnp.where(mask, jnp.dot(qr, kt_all, preferred_element_type=F32), NEG_INF)
        s2 = jnp.where(mask_new, _nt_dot(qr, new_ref[0, g].astype(BF16)), NEG_INF)
        m = jnp.maximum(jnp.max(s1, axis=-1, keepdims=True), jnp.max(s2, axis=-1, keepdims=True))
        e1 = jnp.where(mask, jnp.exp(s1 - m), 0.0)
        e2 = jnp.where(mask_new, jnp.exp(s2 - m), 0.0)
        l = jnp.maximum(jnp.sum(e1, axis=-1, keepdims=True) + jnp.sum(e2, axis=-1, keepdims=True), TINY)
        os_ref[g] = (_nt_dot(e1.astype(BF16), vt_all)
                     + jnp.dot(e2.astype(BF16), new_ref[1, g].astype(BF16), preferred_element_type=F32)) / l

        wl = lax.broadcasted_iota(jnp.int32, (rows, nwin), 1)
        wt = jnp.bitwise_and(lax.broadcasted_iota(jnp.int32, (rows, nwin), 0), 7)
        wmask = wl >= nwin + wt - NSA_WINDOW
        kt = sw_ref[0, g]
        vt = sw_ref[1, g]
        m = jnp.full((rows, 1), NEG_INF, F32)
        l = jnp.zeros((rows, 1), F32)
        pr, alpha, m, l = _softmax_step(jnp.dot(qr, kt.astype(BF16), preferred_element_type=F32), wmask, m, l)
        acc = _nt_dot(pr.astype(BF16), vt.astype(BF16))
        pr, alpha, m, l = _softmax_step(_nt_dot(qr, new_ref[2, g].astype(BF16)), new_mask, m, l)
        acc = alpha * acc + jnp.dot(pr.astype(BF16), new_ref[3, g].astype(BF16), preferred_element_type=F32)
        ow_ref[g] = acc / jnp.maximum(l, TINY)
        wout_ref[0, g] = _place_new_columns(pltpu.roll(kt, nwin - 4, 1), new_ref[2, g], nwin)
        wout_ref[1, g] = _place_new_columns(pltpu.roll(vt, nwin - 4, 1), new_ref[3, g], nwin)


def nsa_sample(qn, qr, new_rows, cache_t, ab, win_state, page_table, cmp_w, g_kc, *, past):
    pe_k, w_k1, w_k2, pe_v, w_v1, w_v2 = cmp_w
    db = qn.shape[0]
    npages = page_table.shape[1]
    hid = w_k1.shape[1]
    nwin = win_state.shape[-1]
    n_cmp = (past + 4 - CMP_LEN) // CMP_STRIDE + 1
    n_sel = -(-(past + 4) // SEL_BLOCK)
    assert n_cmp <= npages * 8 - 1 and past % SEL_BLOCK == 0
    pe_rows = jnp.zeros((2, 8, pe_k.size), BF16).at[:, 0].set(
        jnp.stack([pe_k.reshape(-1), pe_v.reshape(-1)]).astype(BF16))
    cn_specs = [pl.BlockSpec((None, 2, NSA_KV_HEADS, HEAD_DIM, 128),
                             functools.partial(lambda b, pt, p: (pt[b, p], 1, 0, 0, 0), p=p)) for p in range(npages)]
    ab_specs = [pl.BlockSpec((2, None, NSA_KV_HEADS, 8, 2 * hid),
                             functools.partial(lambda b, pt, p: (0, pt[b, p], 0, 0, 0), p=p)) for p in range(npages)]
    b4 = lambda b, pt: (b, 0, 0, 0)
    b5 = lambda b, pt: (b, 0, 0, 0, 0)
    c2 = lambda b, pt: (0, 0)
    c3 = lambda b, pt: (0, 0, 0)
    o_spec = pl.BlockSpec((None, NSA_KV_HEADS, 32, HEAD_DIM), b4)
    o_shape = jax.ShapeDtypeStruct((db, NSA_KV_HEADS, 32, HEAD_DIM), F32)
    return pl.pallas_call(
        functools.partial(_nsa_sample_body, npages=npages, past=past, n_cmp=n_cmp, n_sel=n_sel, hid=hid),
        grid_spec=pltpu.PrefetchScalarGridSpec(
            num_scalar_prefetch=1,
            grid=(db,),
            in_specs=cn_specs + ab_specs + [
                pl.BlockSpec((None, NSA_KV_HEADS, 32, HEAD_DIM), b4), pl.BlockSpec((None, NSA_KV_HEADS, 32, HEAD_DIM), b4),
                pl.BlockSpec((None, 4, NSA_KV_HEADS, 128, HEAD_DIM), b5),
                pl.BlockSpec((None, 2, NSA_KV_HEADS, HEAD_DIM, nwin), b5),
                pl.BlockSpec((2, 8, pe_k.size), c3),
                pl.BlockSpec(w_k1.shape, c2), pl.BlockSpec(w_v1.shape, c2),
                pl.BlockSpec(w_k2.shape, c2), pl.BlockSpec(w_v2.shape, c2),
                pl.BlockSpec((1, HEAD_DIM), c2), pl.BlockSpec((128, 128), c2)],
            out_specs=[o_spec, o_spec, o_spec, pl.BlockSpec((None, 2, NSA_KV_HEADS, HEAD_DIM, nwin), b5)],
        ),
        out_shape=[o_shape, o_shape, o_shape, jax.ShapeDtypeStruct(win_state.shape, F32)],
        compiler_params=_cparams("parallel"),
        name="nsa_sample",
    )(page_table, *([cache_t] * npages), *([ab] * npages), qn, qr, new_rows, win_state, pe_rows,
      w_k1.astype(BF16), w_v1.astype(BF16), w_k2.astype(BF16), w_v2.astype(BF16),
      g_kc.reshape(1, HEAD_DIM).astype(F32), _overlap_matrix(n_cmp, n_sel))


def _dil_sample_body(q_ref, kn_ref, vn_ref, st_ref, o_ref, lse_ref, roll_ref, *, window, dil, hg):
    nbuf = st_ref.shape[-1]
    w = hg * HEAD_DIM
    r = hg * 8
    own = (jnp.right_shift(lax.broadcasted_iota(jnp.int32, (r, w), 0), 3)
           == jnp.right_shift(lax.broadcasted_iota(jnp.int32, (r, w), 1), 6))
    qbd = jnp.where(own, jnp.concatenate([q_ref[...] * SCALE] * hg, axis=0), 0.0).astype(BF16)
    kt, vt = st_ref[0], st_ref[1]
    zpad = jnp.zeros((128 - kn_ref.shape[0], w), F32)
    kn = jnp.concatenate([kn_ref[...], zpad], axis=0)
    vn = jnp.concatenate([vn_ref[...], zpad], axis=0)
    tok = jnp.bitwise_and(lax.broadcasted_iota(jnp.int32, (r, nbuf), 0), 7)
    dist = nbuf + tok - lax.broadcasted_iota(jnp.int32, (r, nbuf), 1)
    mask = (dist <= window) & (jnp.bitwise_and(dist, dil - 1) == 0)
    ncol = lax.broadcasted_iota(jnp.int32, (r, 128), 1)
    nd = jnp.bitwise_and(lax.broadcasted_iota(jnp.int32, (r, 128), 0), 7) - ncol
    nmask = (nd >= 0) & (jnp.bitwise_and(nd, dil - 1) == 0) & (ncol < 4)
    s1 = jnp.where(mask, jnp.dot(qbd, kt.astype(BF16), preferred_element_type=F32), NEG_INF)
    s2 = jnp.where(nmask, _nt_dot(qbd, kn.astype(BF16)), NEG_INF)
    m = jnp.maximum(jnp.max(s1, axis=-1, keepdims=True), jnp.max(s2, axis=-1, keepdims=True))
    e1 = jnp.where(mask, jnp.exp(s1 - m), 0.0)
    e2 = jnp.where(nmask, jnp.exp(s2 - m), 0.0)
    l = jnp.maximum(jnp.sum(e1, axis=-1, keepdims=True) + jnp.sum(e2, axis=-1, keepdims=True), TINY)
    o_all = jnp.where(own, (_nt_dot(e1.astype(BF16), vt.astype(BF16))
                            + jnp.dot(e2.astype(BF16), vn.astype(BF16), preferred_element_type=F32)) / l, 0.0)
    o = o_all[0:8]
    for h in range(1, hg):
        o = o + o_all[h * 8:(h + 1) * 8]
    o_ref[...] = o
    lse_ref[...] = m + jnp.log(l) + jnp.zeros((r, 128), F32)

    lane = lax.broadcasted_iota(jnp.int32, (w, nbuf), 1)
    for kv, (old, new) in enumerate(((kt, kn), (vt, vn))):
        new_t = jnp.concatenate([jnp.transpose(new[:, b * 128:(b + 1) * 128]) for b in range(w // 128)], axis=0)
        new_t = pltpu.roll(new_t, 124, 1)
        if nbuf > 128:
            new_t = jnp.concatenate([jnp.zeros((w, nbuf - 128), F32), new_t], axis=1)
        roll_ref[kv] = jnp.where(lane >= nbuf - 4, new_t, pltpu.roll(old, nbuf - 4, 1))


def dilated_attention_sample(q8, k_new, v_new, state_t, *, window, dil, hg=4):
    db = q8.shape[0]
    nbuf = state_t.shape[-1]
    assert dil & (dil - 1) == 0 and nbuf >= window and hg % 2 == 0
    w = hg * HEAD_DIM
    return pl.pallas_call(
        functools.partial(_dil_sample_body, window=window, dil=dil, hg=hg),
        grid=(db, C_HEADS // hg),
        in_specs=[pl.BlockSpec((None, 8, w), lambda b, j: (b, 0, j)),
                  pl.BlockSpec((None, 8, w), lambda b, j: (b, 0, j)),
                  pl.BlockSpec((None, 8, w), lambda b, j: (b, 0, j)),
                  pl.BlockSpec((None, 2, w, nbuf), lambda b, j: (b, 0, j, 0))],
        out_specs=[pl.BlockSpec((None, 8, w), lambda b, j: (b, 0, j)),
                   pl.BlockSpec((None, hg * 8, 128), lambda b, j: (b, j, 0)),
                   pl.BlockSpec((None, 2, w, nbuf), lambda b, j: (b, 0, j, 0))],
        out_shape=[jax.ShapeDtypeStruct((db, 8, C_W), F32), jax.ShapeDtypeStruct((db, C_HEADS * 8, 128), F32),
                   jax.ShapeDtypeStruct(state_t.shape, F32)],
        compiler_params=_cparams("parallel", "parallel"),
        name=f"dilated_attn_sample_{dil}",
    )(q8, k_new, v_new, state_t)


def _rms_norm(x, g):
    xf = x.astype(F32)
    y = xf * lax.rsqrt(jnp.mean(xf * xf, axis=-1, keepdims=True) + NORM_EPS)
    return (y * g.astype(F32)).astype(x.dtype)


def _partial_rope(x, pos):
    half = ROT_DIM // 2
    inv_freq = ROPE_THETA ** (-jnp.arange(half, dtype=F32) / half)
    ang = pos.astype(F32)[:, None] * inv_freq[None, :]
    shape = (1, pos.shape[0]) + (1,) * (x.ndim - 3) + (half,)
    cos = jnp.cos(ang).reshape(shape)
    sin = jnp.sin(ang).reshape(shape)
    xf = x.astype(F32)
    x1, x2 = xf[..., :half], xf[..., half:ROT_DIM]
    out = jnp.concatenate([x1 * cos - x2 * sin, x2 * cos + x1 * sin, xf[..., ROT_DIM:]], axis=-1)
    return out.astype(x.dtype)


def _masked_softmax(s, mask):
    s = jnp.where(mask, s.astype(F32), NEG_INF)
    m = jnp.max(s, axis=-1, keepdims=True)
    e = jnp.where(mask, jnp.exp(s - m), 0.0)
    l = jnp.maximum(jnp.sum(e, axis=-1, keepdims=True), TINY)
    return e / l, (m + jnp.log(l))[..., 0]


def _paged_rows(cache, page_table):
    g = cache[page_table]
    return g.reshape((g.shape[0], g.shape[1] * g.shape[2]) + g.shape[3:])


def _roll_buffer(buf, new):
    n_buf, t = buf.shape[1], new.shape[1]
    if t >= n_buf:
        return new[:, t - n_buf:]
    return jnp.concatenate([buf[:, t:], new], axis=1)


def _gather_rows(buf, new, idx):
    n_buf = buf.shape[1]
    from_buf = buf[:, np.clip(idx, 0, n_buf - 1)]
    from_new = new[:, np.clip(idx - n_buf, 0, new.shape[1] - 1)]
    sel = (idx < n_buf).reshape(idx.shape + (1,) * (buf.ndim - 2))
    return jnp.where(sel, from_buf, from_new)


def _banded_attn(q, k, v, band):
    n, L, g, r, dh = q.shape
    blk = math.gcd(L, Q_BLOCK)
    nb = L // blk
    pad = ((0, 0), (band, 0), (0, 0), (0, 0))
    idx = np.arange(nb)[:, None] * blk + np.arange(blk + band)[None, :]
    kb = jnp.pad(k, pad)[:, idx]
    vb = jnp.pad(v, pad)[:, idx]
    qb = q.reshape(n, nb, blk, g, r, dh)
    s = jnp.einsum('nbqgrd,nbkgd->nbgrqk', qb, kb, preferred_element_type=F32) * SCALE
    qpos = np.arange(nb)[:, None] * blk + np.arange(blk)[None, :]
    kpos = idx - band
    dist = qpos[:, :, None] - kpos[:, None, :]
    mask = (dist >= 0) & (dist <= band) & (kpos[:, None, :] >= 0)
    p, lse = _masked_softmax(s, mask[None, :, None, None])
    o = jnp.einsum('nbgrqk,nbkgd->nbqgrd', p, vb.astype(F32))
    return o.reshape(n, L, g, r, dh).astype(q.dtype), lse.transpose(0, 1, 4, 2, 3).reshape(n, L, g, r)


def _diff_heads(qa, ka, va, pos, g_q, g_k):
    n, t = qa.shape[:2]
    q = _partial_rope(_rms_norm(qa.reshape(n, t, A_HEADS, 2, HEAD_DIM), g_q), pos)
    k = _partial_rope(_rms_norm(ka.reshape(n, t, A_HEADS, 2, HEAD_DIM), g_k), pos)
    return q, k, va.reshape(n, t, A_HEADS, A_VDIM)


def _diff_core(q, k, v, qpos, kpos, lam):
    s = jnp.einsum('nqhmd,nkhmd->nhmqk', q, k, preferred_element_type=F32) * SCALE
    p, _ = _masked_softmax(s, (kpos[None, :] <= qpos[:, None])[None, None, None])
    a = p[:, :, 0] - lam * p[:, :, 1]
    return jnp.einsum('nhqk,nkhe->nqhe', a, v.astype(F32)).astype(v.dtype)


def _diff_attn_prompt(q, k, v, pos, lam):
    n, s = q.shape[:2]
    nb = s // Q_BLOCK
    qb = q.reshape((n, nb, Q_BLOCK) + q.shape[2:]).swapaxes(0, 1)
    ob = lax.map(lambda a: _diff_core(a[0], k, v, a[1], pos, lam), (qb, pos.reshape(nb, Q_BLOCK)))
    return ob.swapaxes(0, 1).reshape(n, s, A_HEADS, A_VDIM)


def _diff_output(o, g_sub, lam_init):
    n, t = o.shape[:2]
    return (_rms_norm(o, g_sub) * (1.0 - lam_init)).reshape(n, t, A_V_W)


def _nsa_heads(qb, kvb, gb, pos, g_q, g_k):
    n, t = qb.shape[:2]
    q = _rms_norm(qb.reshape(n, t, NSA_KV_HEADS, NSA_REP, HEAD_DIM), g_q)
    q_rot = _partial_rope(q, pos)
    kv = kvb.reshape(n, t, 6, NSA_KV_HEADS, HEAD_DIM)
    k_slc = _partial_rope(_rms_norm(kv[:, :, 2], g_k[1]), pos)
    k_win = _partial_rope(_rms_norm(kv[:, :, 4], g_k[2]), pos)
    long_rows = jnp.stack([kv[:, :, 0], kv[:, :, 1], k_slc, kv[:, :, 3]], axis=2)
    win_rows = jnp.stack([k_win, kv[:, :, 5]], axis=2)
    gates = jax.nn.sigmoid(gb.astype(F32)).reshape(n, t, NSA_KV_HEADS, NSA_REP, 3)
    return q, q_rot, long_rows, win_rows, gates


def _nsa_compress(rows, pe, w1, w2):
    n, L, g, dh = rows.shape
    n_cmp = (L - CMP_LEN) // CMP_STRIDE + 1
    idx = np.arange(n_cmp)[:, None] * CMP_STRIDE + np.arange(CMP_LEN)[None, :]
    blocks = rows[:, idx] + pe[None, None, :, None, :]
    flat = blocks.transpose(0, 1, 3, 2, 4).reshape(n, n_cmp, g, CMP_LEN * dh)
    return jax.nn.silu(flat @ w1) @ w2


def _nsa_cmp_attn(q, k_cmp, v_cmp, qpos):
    n_cmp = k_cmp.shape[1]
    end = jnp.asarray(np.arange(n_cmp) * CMP_STRIDE + CMP_LEN - 1)
    s = jnp.einsum('nqgrd,ncgd->nqgrc', q, k_cmp, preferred_element_type=F32) * SCALE
    visible = end[None, :] <= qpos[:, None]
    p, _ = _masked_softmax(s, visible[None, :, None, None, :])
    o = jnp.einsum('nqgrc,ncgd->nqgrd', p, v_cmp.astype(F32)).astype(q.dtype)
    return o, p


def _cmp_to_sel_overlap(n_cmp, n_sel):
    c0 = np.arange(n_cmp)[:, None] * CMP_STRIDE
    s0 = np.arange(n_sel)[None, :] * SEL_BLOCK
    ov = np.minimum(c0 + CMP_LEN, s0 + SEL_BLOCK) - np.maximum(c0, s0)
    return jnp.asarray(np.maximum(ov, 0) / CMP_LEN, dtype=F32)


def _nsa_select(p_cmp, qpos, n_sel):
    imp = jnp.einsum('nqgrc,cj->nqgj', p_cmp, _cmp_to_sel_overlap(p_cmp.shape[-1], n_sel))
    blk = jnp.arange(n_sel)[None, :]
    cur = (qpos // SEL_BLOCK)[:, None]
    valid = blk <= cur
    forced = (blk == 0) | (blk == cur) | (blk == cur - 1)
    score = jnp.where(valid[None, :, None], imp + jnp.where(forced, FORCE_BONUS, 0.0)[None, :, None], NEG_INF)
    _, sel = lax.top_k(score, min(SEL_TOPK, n_sel))
    return sel


def _nsa_sel_attn(q, k_blk, v_blk, sel, qpos):
    n, qc, g, r, dh = q.shape
    kk = sel.shape[-1]
    n_i = jnp.arange(n)[:, None, None, None]
    g_i = jnp.arange(g)[None, None, :, None]
    kg = k_blk[n_i, g_i, sel]
    vg = v_blk[n_i, g_i, sel]
    kpos = sel[..., None] * SEL_BLOCK + jnp.arange(SEL_BLOCK)
    visible = (kpos <= qpos[None, :, None, None, None]).reshape(n, qc, g, 1, kk * SEL_BLOCK)
    s = jnp.einsum('nqgrd,nqgkbd->nqgrkb', q, kg, preferred_element_type=F32)
    p, _ = _masked_softmax(s.reshape(n, qc, g, r, kk * SEL_BLOCK) * SCALE, visible)
    o = jnp.einsum('nqgrx,nqgxd->nqgrd', p, vg.reshape(n, qc, g, kk * SEL_BLOCK, dh).astype(F32))
    return o.astype(q.dtype)


def _nsa_long_branches(q, q_rot, long_all, qpos, g_kc, pe_k, w_k1, w_k2, pe_v, w_v1, w_v2):
    n, L, _, g, dh = long_all.shape
    k_cmp = _rms_norm(_nsa_compress(long_all[:, :, 0], pe_k, w_k1, w_k2), g_kc)
    v_cmp = _nsa_compress(long_all[:, :, 1], pe_v, w_v1, w_v2)
    o_cmp, p_cmp = _nsa_cmp_attn(q, k_cmp, v_cmp, qpos)
    n_sel = -(-L // SEL_BLOCK)
    sel = _nsa_select(p_cmp, qpos, n_sel)

    def to_blocks(x):
        x = jnp.pad(x, ((0, 0), (0, n_sel * SEL_BLOCK - L), (0, 0), (0, 0)))
        return x.reshape(n, n_sel, SEL_BLOCK, g, dh).transpose(0, 3, 1, 2, 4)

    k_blk, v_blk = to_blocks(long_all[:, :, 2]), to_blocks(long_all[:, :, 3])
    nq = q.shape[1]
    qc = math.gcd(nq, SEL_Q_BLOCK)
    nc = nq // qc

    def chunks(x):
        return x.reshape((n, nc, qc) + x.shape[2:]).swapaxes(0, 1)

    o_sel = lax.map(lambda a: _nsa_sel_attn(a[0], k_blk, v_blk, a[1], a[2]),
                    (chunks(q_rot), chunks(sel), qpos.reshape(nc, qc)))
    return o_cmp, o_sel.swapaxes(0, 1).reshape(q.shape)


def _window_attn_sample(q, k_all, v_all, n_buf, window):
    t = q.shape[1]
    dist = (n_buf + np.arange(t))[:, None] - np.arange(n_buf + t)[None, :]
    visible = (dist >= 0) & (dist <= window)
    s = jnp.einsum('ntgrd,nkgd->ntgrk', q, k_all, preferred_element_type=F32) * SCALE
    p, _ = _masked_softmax(s, visible[None, :, None, None, :])
    return jnp.einsum('ntgrk,nkgd->ntgrd', p, v_all.astype(F32)).astype(q.dtype)


def _nsa_merge(gates, o_cmp, o_sel, o_win):
    o = gates[..., 0:1] * o_cmp.astype(F32) + gates[..., 1:2] * o_sel.astype(F32) + gates[..., 2:3] * o_win.astype(F32)
    n, t = o.shape[:2]
    return o.reshape(n, t, NSA_Q_W).astype(o_cmp.dtype)


def _split_in0(proj):
    sizes = [A_QK_W, A_QK_W, A_V_W, NSA_Q_W, NSA_KV_W, NSA_GATE_W]
    return jnp.split(proj, [int(o) for o in np.cumsum(sizes)[:-1]], axis=-1)


def _even_mixer_prompt(proj, pos, mw):
    g_qa, g_ka, lam, lam_init, g_subln, g_qb, g_kb, cmp_w = mw
    n, s = proj.shape[:2]
    qa, ka, va, qb, kvb, gb = _split_in0(proj)
    q, k, v = _diff_heads(qa, ka, va, pos, g_qa, g_ka)
    o_a = _diff_output(_diff_attn_prompt(q, k, v, pos, lam), g_subln, lam_init)
    qn, qr, long_rows, win_rows, gates = _nsa_heads(qb, kvb, gb, pos, g_qb, g_kb)
    o_cmp, o_sel = _nsa_long_branches(qn, qr, long_rows, pos, g_kb[0], *cmp_w)
    o_win, _ = _banded_attn(qr, win_rows[:, :, 0], win_rows[:, :, 1], NSA_WINDOW)
    o_b = _nsa_merge(gates, o_cmp, o_sel, o_win)
    a_rows = jnp.stack([k.reshape(n, s, A_HEADS, A_VDIM), v], axis=2)
    return jnp.concatenate([o_a, o_b], axis=-1), a_rows, long_rows, win_rows[:, s - min(NSA_WINDOW, s):]


def _even_mixer_sample(proj, pos, cache_a_kv, cache_nsa_kv, state_nsa_win, page_table, mw):
    g_qa, g_ka, lam, lam_init, g_subln, g_qb, g_kb, cmp_w = mw
    n, t = proj.shape[:2]
    qa, ka, va, qb, kvb, gb = _split_in0(proj)
    q, k, v = _diff_heads(qa, ka, va, pos, g_qa, g_ka)
    a_rows = jnp.stack([k.reshape(n, t, A_HEADS, A_VDIM), v], axis=2)
    a_all = jnp.concatenate([_paged_rows(cache_a_kv, page_table), a_rows], axis=1)
    L = a_all.shape[1]
    o = _diff_core(q, a_all[:, :, 0].reshape(n, L, A_HEADS, 2, HEAD_DIM), a_all[:, :, 1], pos,
                   jnp.arange(L, dtype=jnp.int32), lam)
    o_a = _diff_output(o, g_subln, lam_init)
    qn, qr, long_rows, win_rows, gates = _nsa_heads(qb, kvb, gb, pos, g_qb, g_kb)
    long_all = jnp.concatenate([_paged_rows(cache_nsa_kv, page_table), long_rows], axis=1)
    o_cmp, o_sel = _nsa_long_branches(qn, qr, long_all, pos, g_kb[0], *cmp_w)
    n_buf = state_nsa_win.shape[1]
    win_all = jnp.concatenate([state_nsa_win, win_rows], axis=1)
    o_win = _window_attn_sample(qr, win_all[:, :, 0], win_all[:, :, 1], n_buf, NSA_WINDOW)
    o_b = _nsa_merge(gates, o_cmp, o_sel, o_win)
    return jnp.concatenate([o_a, o_b], axis=-1), a_rows, long_rows, _roll_buffer(state_nsa_win, win_rows)


def _dilated_heads(proj, pos, g_qc, g_kc):
    n, t = proj.shape[:2]
    proj = proj.reshape(n, t, N_C_GROUPS, 3, C_HEADS, HEAD_DIM)
    return [(_partial_rope(_rms_norm(proj[:, :, gi, 0], g_qc[gi]), pos),
             _partial_rope(_rms_norm(proj[:, :, gi, 1], g_kc[gi]), pos),
             proj[:, :, gi, 2]) for gi in range(N_C_GROUPS)]


def _dilated_attn_prompt(q, k, v, dil, band):
    n, S, h, dh = q.shape
    L = S // dil

    def sub(x):
        return x.reshape(n, L, dil, h, dh).transpose(0, 2, 1, 3, 4).reshape(n * dil, L, h, dh)

    o, lse = _banded_attn(sub(q)[:, :, :, None], sub(k), sub(v), band)
    o = o.reshape(n, dil, L, h, dh).transpose(0, 2, 1, 3, 4).reshape(n, S, h, dh)
    lse = lse.reshape(n, dil, L, h).transpose(0, 2, 1, 3).reshape(n, S, h)
    return o, lse


def _dilated_attn_sample(q, buf, new_rows, dil, window):
    n_buf, t = buf.shape[1], q.shape[1]
    n_keys = window // dil + 1
    idx = n_buf + np.arange(t)[:, None] - dil * np.arange(n_keys)[None, :]
    rows = _gather_rows(buf, new_rows, idx)
    s = jnp.einsum('nthd,ntkhd->nthk', q, rows[:, :, :, 0], preferred_element_type=F32) * SCALE
    p, lse = _masked_softmax(s, (idx >= 0)[None, :, None, :])
    o = jnp.einsum('nthk,ntkhd->nthd', p, rows[:, :, :, 1].astype(F32))
    return o.astype(q.dtype), lse


def _merge_dilations(outs, lses):
    w = jax.nn.softmax(jnp.stack(lses, axis=0), axis=0)
    o = jnp.einsum('gnth,gnthd->nthd', w, jnp.stack(outs, axis=0).astype(F32))
    return o.astype(outs[0].dtype)


def _odd_mixer_prompt(proj, pos, g_qc, g_kc):
    n, s = proj.shape[:2]
    outs, lses, bufs = [], [], []
    for (window, dil), (q, k, v) in zip(C_GROUPS, _dilated_heads(proj, pos, g_qc, g_kc)):
        o, lse = _dilated_attn_prompt(q, k, v, dil, window // dil)
        outs.append(o)
        lses.append(lse)
        bufs.append(jnp.stack([k, v], axis=2)[:, s - min(window, s):])
    return _merge_dilations(outs, lses).reshape(n, s, C_W), bufs


def _odd_mixer_sample(proj, pos, states, g_qc, g_kc):
    n, t = proj.shape[:2]
    outs, lses, bufs = [], [], []
    for (window, dil), (q, k, v), buf in zip(C_GROUPS, _dilated_heads(proj, pos, g_qc, g_kc), states):
        new_rows = jnp.stack([k, v], axis=2)
        o, lse = _dilated_attn_sample(q, buf, new_rows, dil, window)
        outs.append(o)
        lses.append(lse)
        bufs.append(_roll_buffer(buf, new_rows))
    return _merge_dilations(outs, lses).reshape(n, t, C_W), bufs


def _router_body(x_ref, w_ref, idx_ref, gate_ref):
    logits = jnp.dot(x_ref[...], w_ref[...], preferred_element_type=F32)
    lane = lax.broadcasted_iota(jnp.int32, logits.shape, 1).astype(F32)
    lg = jnp.where(lane < N_EXPERTS, logits, NEG_INF)
    v1 = jnp.max(lg, axis=-1, keepdims=True)
    i1 = jnp.min(jnp.where(lg == v1, lane, 128.0), axis=-1, keepdims=True)
    lg2 = jnp.where(lane == i1, NEG_INF, lg)
    v2 = jnp.max(lg2, axis=-1, keepdims=True)
    i2 = jnp.min(jnp.where(lg2 == v2, lane, 128.0), axis=-1, keepdims=True)
    e = jnp.exp(v2 - v1)
    idx_ref[...] = jnp.where(lane == 0, i1, jnp.where(lane == 1, i2, 0.0)).astype(jnp.int32)
    gate_ref[...] = jnp.where(lane == 0, 1.0 / (1.0 + e), jnp.where(lane == 1, e / (1.0 + e), 0.0))


def moe_router(h_bf16, w_router):
    m, d = h_bf16.shape
    tm = TOKEN_TILE
    w = jnp.pad(w_router, ((0, 0), (0, 128 - w_router.shape[1]))).astype(BF16)
    spec = pl.BlockSpec((tm, 128), lambda i: (i, 0))
    return pl.pallas_call(
        _router_body,
        grid=(m // tm,),
        in_specs=[pl.BlockSpec((tm, d), lambda i: (i, 0)), pl.BlockSpec((d, 128), lambda i: (0, 0))],
        out_specs=[spec, spec],
        out_shape=[jax.ShapeDtypeStruct((m, 128), jnp.int32), jax.ShapeDtypeStruct((m, 128), F32)],
        compiler_params=_cparams("parallel"),
        name="moe_router",
    )(h_bf16, w)


def _moe_combine_body(r_ref, g_ref, y0_ref, y1_ref, o_ref):
    g = g_ref[...]
    o_ref[...] = r_ref[...] + g[:, 0:1] * y0_ref[...] + g[:, 1:2] * y1_ref[...]


def moe_combine(resid, gates, y0, y1):
    m, d = resid.shape
    tm = TOKEN_TILE
    spec = pl.BlockSpec((tm, d), lambda i: (i, 0))
    return pl.pallas_call(
        _moe_combine_body,
        grid=(m // tm,),
        in_specs=[spec, pl.BlockSpec((tm, 128), lambda i: (i, 0)), spec, spec],
        out_specs=spec,
        out_shape=jax.ShapeDtypeStruct((m, d), F32),
        compiler_params=_cparams("parallel"),
        name="moe_combine",
    )(resid, gates, y0, y1)


def _moe(h_bf16, h_f32, resid, w_router, wg, wu, wd):
    m, d = h_bf16.shape
    tm = TOKEN_TILE
    idx, gates = moe_router(h_bf16, w_router)
    top_i = idx[:, :TOP_K]
    flat_e = top_i.reshape(-1)
    onehot = (flat_e[:, None] == jnp.arange(N_EXPERTS, dtype=jnp.int32)[None, :]).astype(jnp.int32)
    running = jnp.cumsum(onehot, axis=0)
    counts = running[-1]
    rank = jnp.sum(running * onehot, axis=1) - 1
    padded = ((counts + tm - 1) // tm) * tm
    pend = jnp.cumsum(padded)
    dest = (jnp.sum((pend - padded)[None, :] * onehot, axis=1) + rank).astype(jnp.int32)
    p_rows = m * TOP_K + N_EXPERTS * tm
    tile_start = jnp.arange(p_rows // tm) * tm
    tile_expert = jnp.minimum(jnp.sum(tile_start[:, None] >= pend[None, :], axis=1), N_EXPERTS - 1).astype(jnp.int32)

    slot = jnp.arange(m * TOP_K, dtype=jnp.int32)
    src_tok = jnp.zeros((p_rows,), jnp.int32).at[dest].set(slot // TOP_K)
    ys = moe_grouped_ffn(h_f32[src_tok], tile_expert, wg, wu, wd, tf=896)
    back = dest.reshape(m, TOP_K)
    return moe_combine(resid, gates, ys[back[:, 0]], ys[back[:, 1]])


COL_QA, COL_KA, COL_VA, COL_QB = 0, A_QK_W, 2 * A_QK_W, 2 * A_QK_W + A_V_W
COL_KVB = COL_QB + NSA_Q_W
COL_GATE = COL_KVB + NSA_KV_W
KVB_PAIR = NSA_KV_HEADS * HEAD_DIM


def _pad_axis(x, axis, size):
    pad = [(0, 0)] * x.ndim
    pad[axis] = (0, size - x.shape[axis])
    return jnp.pad(x, pad)


def even_mixer(proj, proj16, tables, nb, seq, db, dt, caches, mw):
    g_qa, g_ka, lam_rows, lam_init, g_subln, g_qb, g_kb, cmp_w = mw
    cache_a_kv, cache_nsa_kv, state_nsa_win, page_table = caches
    pe_k, w_k1, w_k2, pe_v, w_v1, w_v2 = cmp_w
    mp = nb * seq
    ms = db * dt
    assert dt == 4
    rope16_32 = ((True, BF16), (True, F32))
    qk_a16, qk_a32 = head_norm_rope(proj, jnp.stack([_head_gain(g_qa, A_QK_W), _head_gain(g_ka, A_QK_W)]), tables,
                                    width=A_QK_W, col0=0, outs=rope16_32, name="hnr_diff_qk")
    qn16, qr16 = head_norm_rope(proj, _head_gain(g_qb, NSA_Q_W)[None], tables, width=NSA_Q_W,
                                col0=COL_QB // NSA_Q_W, outs=((False, BF16), (True, BF16)), name="hnr_nsa_q")
    ks16, ks32 = head_norm_rope(proj, _head_gain(g_kb[1], KVB_PAIR)[None], tables, width=KVB_PAIR,
                                col0=(COL_KVB + 2 * KVB_PAIR) // KVB_PAIR, outs=rope16_32, name="hnr_nsa_kslc")
    kw16, kw32 = head_norm_rope(proj, _head_gain(g_kb[2], KVB_PAIR)[None], tables, width=KVB_PAIR,
                                col0=(COL_KVB + 4 * KVB_PAIR) // KVB_PAIR, outs=rope16_32, name="hnr_nsa_kwin")

    o_a = diff_attention_prompt(qk_a16, proj16, lam_rows, g_subln, nb=nb, seq=seq, lam_init=lam_init)

    nchunk = seq // CMP_STRIDE

    def chunks(col):
        xc = proj16[:mp, col:col + KVB_PAIR].reshape(nb, nchunk, CMP_STRIDE, NSA_KV_HEADS, HEAD_DIM)
        return xc.transpose(0, 3, 1, 2, 4).reshape(nb * NSA_KV_HEADS, nchunk, CMP_STRIDE * HEAD_DIM)

    def pair_lanes(c):
        return c.reshape(nb, NSA_KV_HEADS, nchunk, HEAD_DIM).transpose(0, 2, 1, 3).reshape(nb, nchunk, KVB_PAIR)

    k_cmp = pair_lanes(compress_blocks(chunks(COL_KVB), pe_k, w_k1, w_k2, g_kb[0]))
    v_cmp = pair_lanes(compress_blocks(chunks(COL_KVB + KVB_PAIR), pe_v, w_v1, w_v2, None))
    o_cmp, sel = nsa_compressed_prompt(qn16, k_cmp, v_cmp, nb=nb, seq=seq)
    o_sel = nsa_branch_prompt(qr16, ks16, 0, proj16, (COL_KVB + 3 * KVB_PAIR) // KVB_PAIR, sel,
                              nb=nb, seq=seq, band=None, name="nsa_sel_prompt")
    o_win = nsa_branch_prompt(qr16, kw16, 0, proj16, (COL_KVB + 5 * KVB_PAIR) // KVB_PAIR, None,
                              nb=nb, seq=seq, band=NSA_WINDOW, name="nsa_win_prompt")
    a_rows = jnp.concatenate([qk_a32[:, A_QK_W:], proj[:, COL_VA:COL_VA + A_V_W]], axis=1)
    long_rows = jnp.concatenate([proj[mp:, COL_KVB:COL_KVB + 2 * KVB_PAIR], ks32[mp:],
                                 proj[mp:, COL_KVB + 3 * KVB_PAIR:COL_KVB + 4 * KVB_PAIR]], axis=1)

    past = page_table.shape[1] * cache_a_kv.shape[1]
    q_s = qk_a32[mp:, :A_QK_W].reshape(db, dt, A_HEADS, 128).transpose(0, 2, 1, 3).reshape(db, A_HEADS * dt, 128)
    new_page = _pad_axis(a_rows[mp:].reshape(db, dt * 2 * A_HEADS, 128), 1, 128)
    cache_rows = cache_a_kv.reshape(cache_a_kv.shape[0], -1, A_VDIM)
    y_s = diff_attention_sample(jnp.concatenate([q_s, q_s], axis=1), new_page, cache_rows, page_table, lam_rows,
                                g_subln, lam_init=lam_init)
    o_a_s = y_s.reshape(db, A_HEADS, dt, A_VDIM).transpose(0, 2, 1, 3).reshape(ms, A_V_W).astype(BF16)

    pool = cache_nsa_kv.shape[0]
    xc = jnp.transpose(cache_nsa_kv[:, :, :2], (2, 0, 3, 1, 4)).astype(BF16)
    xc = xc.reshape(2, pool * NSA_KV_HEADS * (cache_nsa_kv.shape[1] // CMP_STRIDE), CMP_STRIDE * HEAD_DIM)
    ab = jnp.stack([matmul(xc[c], _w1_ab(w1), tn=2 * w1.shape[1], name="compress_cache")
                    for c, w1 in enumerate((w_k1, w_v1))])
    ab = ab.reshape(2, pool, NSA_KV_HEADS, cache_nsa_kv.shape[1] // CMP_STRIDE, ab.shape[-1])

    def sample_q(q16):
        qq = q16[mp:].reshape(db, dt, NSA_KV_HEADS, NSA_REP, HEAD_DIM).transpose(0, 2, 3, 1, 4)
        return _pad_axis(qq, 3, 8).reshape(db, NSA_KV_HEADS, NSA_REP * 8, HEAD_DIM)

    def sample_kv(x):
        return x.reshape(db, dt, NSA_KV_HEADS, HEAD_DIM).transpose(0, 2, 1, 3)

    new_rows = jnp.stack([sample_kv(ks32[mp:]), sample_kv(proj[mp:, COL_KVB + 3 * KVB_PAIR:COL_KVB + 4 * KVB_PAIR]),
                          sample_kv(kw32[mp:]), sample_kv(proj[mp:, COL_KVB + 5 * KVB_PAIR:COL_KVB + 6 * KVB_PAIR])],
                         axis=1)
    o_cmp_s, o_sel_s, o_win_s, win_state = nsa_sample(
        sample_q(qn16), sample_q(qr16), _pad_axis(new_rows, 3, 128), jnp.transpose(cache_nsa_kv, (0, 2, 3, 4, 1)),
        ab, jnp.transpose(state_nsa_win, (0, 2, 3, 4, 1)), page_table, cmp_w, g_kb[0], past=past)

    def sample_o(o):
        oo = o.reshape(db, NSA_KV_HEADS, NSA_REP, 8, HEAD_DIM)[:, :, :, :dt]
        return oo.transpose(0, 3, 1, 2, 4).reshape(ms, NSA_Q_W).astype(BF16)

    o_b = nsa_merge(proj, COL_GATE // 128, jnp.concatenate([o_cmp, sample_o(o_cmp_s)]),
                    jnp.concatenate([o_sel, sample_o(o_sel_s)]), jnp.concatenate([o_win, sample_o(o_win_s)]))
    cat = jnp.concatenate([jnp.concatenate([o_a, o_a_s]), o_b], axis=1)

    keep = min(NSA_WINDOW, seq)
    tm_kw = dict(nb=nb, seq=seq)
    long_t = jnp.concatenate([cols_time_minor(proj, COL_KVB // (2 * KVB_PAIR), 2 * KVB_PAIR, keep=seq, **tm_kw),
                              cols_time_minor(ks32, 0, KVB_PAIR, keep=seq, **tm_kw),
                              cols_time_minor(proj, COL_KVB // KVB_PAIR + 3, KVB_PAIR, keep=seq, **tm_kw)], axis=1)
    win_t = jnp.concatenate([cols_time_minor(kw32, 0, KVB_PAIR, keep=keep, **tm_kw),
                             cols_time_minor(proj, COL_KVB // KVB_PAIR + 5, KVB_PAIR, keep=keep, **tm_kw)], axis=1)
    return (cat,
            a_rows[:mp].reshape(nb, seq, 2, A_HEADS, A_VDIM), a_rows[mp:].reshape(db, dt, 2, A_HEADS, A_VDIM),
            jnp.transpose(long_t.reshape(nb, 4, NSA_KV_HEADS, HEAD_DIM, seq), (0, 4, 1, 2, 3)),
            long_rows.reshape(db, dt, 4, NSA_KV_HEADS, HEAD_DIM),
            jnp.transpose(win_t.reshape(nb, 2, NSA_KV_HEADS, HEAD_DIM, keep), (0, 4, 1, 2, 3)),
            jnp.transpose(win_state, (0, 4, 1, 2, 3)))


def _transpose_body(x_ref, o_ref):
    o_ref[...] = jnp.transpose(x_ref[...])


def cols_time_minor(src, col_block, width, *, nb, seq, keep):
    ts = min(keep, 512)
    row0 = seq - keep
    assert row0 % ts == 0 and width % 128 == 0
    return pl.pallas_call(
        _transpose_body,
        grid=(nb, keep // ts),
        in_specs=[pl.BlockSpec((ts, width), lambda n, i: (n * (seq // ts) + row0 // ts + i, col_block))],
        out_specs=pl.BlockSpec((None, width, ts), lambda n, i: (n, 0, i)),
        out_shape=jax.ShapeDtypeStruct((nb, width, keep), F32),
        compiler_params=_cparams("parallel", "parallel"),
        name="cols_time_minor",
    )(src)


def _kv_transpose_body(k_ref, v_ref, o_ref):
    kv = pl.program_id(1)

    @pl.when(kv == 0)
    def _():
        o_ref[...] = jnp.transpose(k_ref[...])

    @pl.when(kv == 1)
    def _():
        o_ref[...] = jnp.transpose(v_ref[...])


def kv_time_minor(k_src, k_col, v_src, v_col, *, nb, seq, keep):
    ts = min(keep, 512)
    cw = 512
    row0 = seq - keep
    assert row0 % ts == 0 and seq % ts == 0 and C_W % cw == 0
    ncb = C_W // cw

    def src_map(col, which):
        def index(n, kv, i, j):
            on = 1 - kv if which == 0 else kv
            return ((n * (seq // ts) + row0 // ts + i) * on, (col * ncb + j) * on)
        return index

    return pl.pallas_call(
        _kv_transpose_body,
        grid=(nb, 2, keep // ts, ncb),
        in_specs=[pl.BlockSpec((ts, cw), src_map(k_col, 0)), pl.BlockSpec((ts, cw), src_map(v_col, 1))],
        out_specs=pl.BlockSpec((None, None, cw, ts), lambda n, kv, i, j: (n, kv, j, i)),
        out_shape=jax.ShapeDtypeStruct((nb, 2, C_W, keep), F32),
        compiler_params=_cparams("parallel", "arbitrary", "arbitrary", "arbitrary"),
        name="kv_time_minor",
    )(k_src, v_src)


def odd_mixer(proj, proj16, tables, nb, seq, db, dt, states, g_qc, g_kc):
    mp = nb * seq
    ms = db * dt
    outs, lses, bufs_p, bufs_s, qk32s = [], [], [], [], []
    for gi, ((window, dil), state) in enumerate(zip(C_GROUPS, states)):
        gains = jnp.stack([_head_gain(g_qc[gi], C_W), _head_gain(g_kc[gi], C_W)])
        (qk32,) = head_norm_rope(proj, gains, tables, width=C_W, col0=3 * gi, outs=((True, F32),),
                                 name=f"hnr_dil_{dil}")
        qk32s.append(qk32)
        keep = min(window, seq)
        kv_t = kv_time_minor(qk32, 1, proj, 3 * gi + 2, nb=nb, seq=seq, keep=keep)
        bufs_p.append(jnp.transpose(kv_t.reshape(nb, 2, C_HEADS, HEAD_DIM, keep), (0, 4, 1, 2, 3)))

        tok3 = lambda x: x.reshape(db, dt, C_W)
        nbuf = state.shape[1]
        o_s, lse_s, rolled = dilated_attention_sample(
            _pad_axis(tok3(qk32[mp:, :C_W]), 1, 8), _pad_axis(tok3(qk32[mp:, C_W:]), 1, 8),
            _pad_axis(tok3(proj[mp:, (3 * gi + 2) * C_W:(3 * gi + 3) * C_W]), 1, 8),
            jnp.transpose(state, (0, 2, 3, 4, 1)).reshape(db, 2, C_W, nbuf),
            window=window, dil=dil, hg=C_HEADS if nbuf <= 512 else 4)
        outs.append(o_s[:, :dt].reshape(ms, C_W).astype(BF16))
        lse_s = lse_s[:, :, 0].reshape(db, C_HEADS, 8)[:, :, :dt].transpose(0, 2, 1).reshape(ms, C_HEADS)
        lses.append(jnp.repeat(lse_s, HEAD_DIM, axis=1))
        bufs_s.append(jnp.transpose(rolled.reshape(db, 2, C_HEADS, HEAD_DIM, nbuf), (0, 4, 1, 2, 3)))
    y_p = dilated_attention_prompt_fused(qk32s, proj, nb=nb, seq=seq)
    return jnp.concatenate([y_p, dilation_merge(outs, lses)]), bufs_p, bufs_s


def odd_mixer_prompt(proj, proj16, tables, nb, seq, g_qc, g_kc):
    mp = nb * seq
    outs, lses, bufs = [], [], []
    for gi, (window, dil) in enumerate(C_GROUPS):
        gains = jnp.stack([_head_gain(g_qc[gi], C_W), _head_gain(g_kc[gi], C_W)])
        qk16, qk32 = head_norm_rope(proj, gains, tables, width=C_W, col0=3 * gi,
                                    outs=((True, BF16), (True, F32)), name=f"hnr_dil_{dil}")
        sub = seq // dil
        o, lse = dilated_attention_prompt(qk16, proj16, nb=nb, seq=seq, dil=dil, band=window // dil, gi=gi,
                                          t=min(sub, 256 if dil == 1 else 128))
        outs.append(o)
        lses.append(lse)
        kv = jnp.concatenate([qk32[:mp, C_W:], proj[:mp, (3 * gi + 2) * C_W:(3 * gi + 3) * C_W]], axis=1)
        bufs.append(kv.reshape(nb, seq, 2, C_HEADS, HEAD_DIM)[:, seq - min(window, seq):])
    return dilation_merge(outs, lses), bufs


def kernel(x_prompt, x_sample, cache_a_kv, cache_nsa_kv, state_nsa_win, state_c_w128, state_c_w512, state_c_w2048, page_table, norm0_mix, w_in0, g_qa, g_ka, lam_q1, lam_k1, lam_q2, lam_k2, g_subln, g_qb, g_kb, pe_cmp_k, w_cmp_k1, w_cmp_k2, pe_cmp_v, w_cmp_v1, w_cmp_v2, w_out0, norm0_ffn, w_ffn_gate, w_ffn_up, w_ffn_down, norm1_mix, w_in1, g_qc, g_kc, w_out1, norm1_ffn, w_router, w_moe_gate, w_moe_up, w_moe_down):
    nb, seq, d = x_prompt.shape
    db, dt, _ = x_sample.shape
    past = page_table.shape[1] * cache_a_kv.shape[1]
    mp = nb * seq
    ms = db * dt
    pos_p = jnp.arange(seq, dtype=jnp.int32)
    pos_s = past + jnp.arange(dt, dtype=jnp.int32)
    x = jnp.concatenate([x_prompt.reshape(mp, d), x_sample.reshape(ms, d)], axis=0)

    in0_w = w_in0.shape[1]
    in0_pad = -(-in0_w // 128) * 128
    w_in0_b = jnp.pad(w_in0, ((0, 0), (0, in0_pad - in0_w))).astype(BF16)
    tables = rope_tables(jnp.concatenate([jnp.tile(pos_p, nb), jnp.tile(pos_s, db)]))
    proj0, proj0_16 = matmul_dual(rmsnorm_cast(x, norm0_mix), w_in0_b, tn=in0_pad, name="in_proj0")
    lam_init = 0.8 - 0.6 * math.exp(-0.3 * 0)
    f = lambda a: a.astype(F32)
    cmp_w = (pe_cmp_k, w_cmp_k1, w_cmp_k2, pe_cmp_v, w_cmp_v1, w_cmp_v2)
    lam_rows = jnp.zeros((8, 128), F32).at[:4, :HEAD_DIM].set(jnp.stack([f(lam_q1), f(lam_k1), f(lam_q2), f(lam_k2)]))
    cat, a_kv_p, a_kv_s, nsa_kv_p, nsa_kv_s, nsa_win_p, nsa_win_s = even_mixer(
        proj0, proj0_16, tables, nb, seq, db, dt, (cache_a_kv, cache_nsa_kv, state_nsa_win, page_table),
        (g_qa, g_ka, lam_rows, lam_init, g_subln, g_qb, g_kb, cmp_w))
    x = matmul(cat, w_out0.astype(BF16), tn=d, res=x, name="out_proj0")
    act = swiglu_gate_up(rmsnorm_cast(x, norm0_ffn), w_ffn_gate.astype(BF16), w_ffn_up.astype(BF16), tn=1408)
    x = matmul(act, w_ffn_down.astype(BF16), tn=d, res=x, name="ffn_down")

    in1_w = w_in1.shape[1]
    proj1 = matmul(rmsnorm_cast(x, norm1_mix), w_in1.astype(BF16), tn=2304, name="in_proj1")
    mix, c_p, c_s = odd_mixer(proj1, None, tables, nb, seq, db, dt,
                              (state_c_w128, state_c_w512, state_c_w2048), g_qc, g_kc)
    x = matmul(mix, w_out1.astype(BF16), tn=d, res=x, name="out_proj1")
    h16, h32 = rmsnorm_cast(x, norm1_ffn, with_f32=True)
    x = _moe(h16, h32, x, w_router, w_moe_gate, w_moe_up, w_moe_down.astype(BF16))

    hp = x[:mp].reshape(nb, seq, d)
    hs = x[mp:].reshape(db, dt, d)
    return (hp, hs, a_kv_p, a_kv_s, nsa_kv_p, nsa_kv_s, nsa_win_p, nsa_win_s,
            c_p[0], c_s[0], c_p[1], c_s[1], c_p[2], c_s[2])
```
